```python
import math
import jax, jax.numpy as jnp
from jax import lax
import numpy as np

D_MODEL = 2048
BATCH = 8
SEQ = 8192
DEPTH = 4

N_MIXERS = 3
N_SB_LAYERS = (DEPTH + 2) // 3
N_CF_LAYERS = (DEPTH + 1) // 3
N_M2_LAYERS = DEPTH // 3
SB_HEADS = 16
SB_HEAD_DIM = D_MODEL // SB_HEADS
SB_BLOCK = 128
CF_KERNEL = 31
M2_EXPAND = 2
M2_D_INNER = M2_EXPAND * D_MODEL
M2_HEAD_DIM = 64
M2_HEADS = M2_D_INNER // M2_HEAD_DIM
M2_GROUPS = 8
M2_HEADS_PER_GROUP = M2_HEADS // M2_GROUPS
M2_STATE = 128
M2_CONV = 4
M2_CHUNK = 128
M2_CONV_DIM = M2_D_INNER + 2 * M2_GROUPS * M2_STATE
M2_IN_DIM = 2 * M2_D_INNER + 2 * M2_GROUPS * M2_STATE + M2_HEADS
MLP_HIDDEN = 4 * D_MODEL
RMS_EPS = 1e-6
LN_EPS = 1e-5

kernel_name = "hybrid_sb_conformer_ssd_trunk"


def rms_norm(x, w, eps=RMS_EPS):
    xf = x.astype(jnp.float32)
    y = xf * lax.rsqrt(jnp.mean(xf * xf, axis=-1, keepdims=True) + eps)
    return (y * w.astype(jnp.float32)).astype(x.dtype)


def layer_norm(x, w, b, eps=LN_EPS):
    xf = x.astype(jnp.float32)
    mu = jnp.mean(xf, axis=-1, keepdims=True)
    xc = xf - mu
    y = xc * lax.rsqrt(jnp.mean(xc * xc, axis=-1, keepdims=True) + eps)
    return (y * w.astype(jnp.float32) + b.astype(jnp.float32)).astype(x.dtype)


def causal_depthwise_conv(x, w, b):
    k_width, ch = w.shape
    y = lax.conv_general_dilated(
        x, w[:, None, :].astype(x.dtype), window_strides=(1,),
        padding=[(k_width - 1, 0)], dimension_numbers=('NWC', 'WIO', 'NWC'),
        feature_group_count=ch)
    return y + b.astype(x.dtype)


def stick_breaking_mixer(h, w_qkv, q_norm_w, k_norm_w, w_o):
    b, l, _ = h.shape
    qkv = (h @ w_qkv).reshape(b, l, 3, SB_HEADS, SB_HEAD_DIM)
    q = rms_norm(qkv[:, :, 0], q_norm_w).transpose(0, 2, 1, 3)
    k = rms_norm(qkv[:, :, 1], k_norm_w).transpose(0, 2, 1, 3)
    v = qkv[:, :, 2].transpose(0, 2, 1, 3)
    n_blk = l // SB_BLOCK
    q_blocks = q.reshape(b, SB_HEADS, n_blk, SB_BLOCK, SB_HEAD_DIM).transpose(2, 0, 1, 3, 4)
    key_pos = jnp.arange(l)
    scale = 1.0 / math.sqrt(SB_HEAD_DIM)

    def block(args):
        i, qi = args
        z = jnp.einsum('bhqd,bhkd->bhqk', qi, k).astype(jnp.float32) * scale
        q_pos = i * SB_BLOCK + jnp.arange(SB_BLOCK)
        strict = key_pos[None, :] < q_pos[:, None]
        log_keep = jnp.where(strict, -jax.nn.softplus(z), 0.0)
        later = lax.cumsum(log_keep, axis=3, reverse=True) - log_keep
        att = jnp.where(strict, jnp.exp(jax.nn.log_sigmoid(z) + later), 0.0)
        return jnp.einsum('bhqk,bhkd->bhqd', att.astype(v.dtype), v)

    o = lax.map(block, (jnp.arange(n_blk), q_blocks))
    o = o.transpose(1, 0, 3, 2, 4).reshape(b, l, D_MODEL)
    return o @ w_o


def conformer_conv_mixer(h, w_in, b_in, dw_w, dw_b, ln_w, ln_b, w_out, b_out):
    u = h @ w_in + b_in
    val, gate = jnp.split(u, 2, axis=-1)
    u = val * jax.nn.sigmoid(gate)
    u = causal_depthwise_conv(u, dw_w, dw_b)
    u = jax.nn.silu(layer_norm(u, ln_w, ln_b))
    return u @ w_out + b_out


def ssd_chunked(xs, dt, a, bm, cm):
    b, l, g, r, p = xs.shape
    n = bm.shape[-1]
    qn = M2_CHUNK
    c = l // qn
    xdt = (xs * dt[..., None]).reshape(b, c, qn, g, r, p)
    bc = bm.reshape(b, c, qn, g, n)
    cc = cm.reshape(b, c, qn, g, n)
    a_cum = jnp.cumsum((dt * a).reshape(b, c, qn, g, r), axis=2)
    causal = jnp.tril(jnp.ones((qn, qn), dtype=bool))[None, None, :, :, None, None]
    seg = a_cum[:, :, :, None] - a_cum[:, :, None, :]
    decay_in = jnp.exp(jnp.where(causal, seg, -jnp.inf))
    cb = jnp.einsum('bcqgn,bcsgn->bcqsg', cc, bc)
    y_diag = jnp.einsum('bcqsg,bcqsgr,bcsgrp->bcqgrp', cb, decay_in, xdt)
    decay_out = jnp.exp(a_cum[:, :, -1:] - a_cum)
    states = jnp.einsum('bcsgn,bcsgr,bcsgrp->bcgrpn', bc, decay_out, xdt)
    chunk_decay = jnp.exp(a_cum[:, :, -1])

    def step(h_state, inp):
        st, dec = inp
        return h_state * dec[..., None, None] + st, h_state

    h0 = jnp.zeros((b, g, r, p, n), jnp.float32)
    _, prev = lax.scan(step, h0, (jnp.moveaxis(states, 1, 0), jnp.moveaxis(chunk_decay, 1, 0)))
    prev = jnp.moveaxis(prev, 0, 1)
    y_off = jnp.einsum('bcqgn,bcgrpn,bcqgr->bcqgrp', cc, prev, jnp.exp(a_cum))
    return (y_diag + y_off).reshape(b, l, g, r, p)


def mamba2_mixer(h, w_in, conv_w, conv_b, dt_bias, a_log, d_skip, norm_w, w_out):
    b, l, _ = h.shape
    f32 = jnp.float32
    zxbcdt = h @ w_in
    z, xbc, dt = jnp.split(zxbcdt, [M2_D_INNER, M2_D_INNER + M2_CONV_DIM], axis=-1)
    xbc = jax.nn.silu(causal_depthwise_conv(xbc, conv_w, conv_b))
    xs, bm, cm = jnp.split(xbc, [M2_D_INNER, M2_D_INNER + M2_GROUPS * M2_STATE], axis=-1)
    xs = xs.reshape(b, l, M2_GROUPS, M2_HEADS_PER_GROUP, M2_HEAD_DIM).astype(f32)
    bm = bm.reshape(b, l, M2_GROUPS, M2_STATE).astype(f32)
    cm = cm.reshape(b, l, M2_GROUPS, M2_STATE).astype(f32)
    dt = jax.nn.softplus(dt.astype(f32) + dt_bias.astype(f32)).reshape(b, l, M2_GROUPS, M2_HEADS_PER_GROUP)
    a = -jnp.exp(a_log.astype(f32)).reshape(M2_GROUPS, M2_HEADS_PER_GROUP)
    d = d_skip.astype(f32).reshape(M2_GROUPS, M2_HEADS_PER_GROUP)[..., None]
    y = ssd_chunked(xs, dt, a, bm, cm) + d * xs
    group_w = M2_HEADS_PER_GROUP * M2_HEAD_DIM
    y = y.reshape(b, l, M2_GROUPS, group_w) * jax.nn.silu(z.astype(f32)).reshape(b, l, M2_GROUPS, group_w)
    y = rms_norm(y, norm_w.reshape(M2_GROUPS, group_w), eps=1e-5)
    return y.reshape(b, l, M2_D_INNER).astype(h.dtype) @ w_out


def squared_relu_mlp(h, w_up, w_down):
    return jnp.square(jax.nn.relu(h @ w_up)) @ w_down


def _fwd_setup_inputs(seed: int = 0) -> dict:
    key = jax.random.key(seed)
    ks = iter(jax.random.split(key, 32))
    f32 = jnp.float32

    def nrm(shape, fan_in):
        return jax.random.normal(next(ks), shape, f32) * (fan_in ** -0.5)

    def gain(shape):
        return 1.0 + 0.02 * jax.random.normal(next(ks), shape, f32)

    def bias(shape):
        return 0.01 * jax.random.normal(next(ks), shape, f32)

    x = jax.random.normal(next(ks), (BATCH, SEQ, D_MODEL), f32)
    dt0 = jnp.exp(jax.random.uniform(next(ks), (N_M2_LAYERS, M2_HEADS), f32,
                                     minval=math.log(1e-3), maxval=math.log(1e-1)))
    return {
        "x": x,
        "norm_mix_w": gain((DEPTH, D_MODEL)),
        "norm_mlp_w": gain((DEPTH, D_MODEL)),
        "sb_w_qkv": nrm((N_SB_LAYERS, D_MODEL, 3 * D_MODEL), D_MODEL),
        "sb_q_norm_w": gain((N_SB_LAYERS, SB_HEAD_DIM)),
        "sb_k_norm_w": gain((N_SB_LAYERS, SB_HEAD_DIM)),
        "sb_w_o": nrm((N_SB_LAYERS, D_MODEL, D_MODEL), D_MODEL),
        "cf_w_in": nrm((N_CF_LAYERS, D_MODEL, 2 * D_MODEL), D_MODEL),
        "cf_b_in": bias((N_CF_LAYERS, 2 * D_MODEL)),
        "cf_dw_w": nrm((N_CF_LAYERS, CF_KERNEL, D_MODEL), CF_KERNEL),
        "cf_dw_b": bias((N_CF_LAYERS, D_MODEL)),
        "cf_ln_w": gain((N_CF_LAYERS, D_MODEL)),
        "cf_ln_b": bias((N_CF_LAYERS, D_MODEL)),
        "cf_w_out": nrm((N_CF_LAYERS, D_MODEL, D_MODEL), D_MODEL),
        "cf_b_out": bias((N_CF_LAYERS, D_MODEL)),
        "m2_w_in": nrm((N_M2_LAYERS, D_MODEL, M2_IN_DIM), D_MODEL),
        "m2_conv_w": nrm((N_M2_LAYERS, M2_CONV, M2_CONV_DIM), M2_CONV),
        "m2_conv_b": bias((N_M2_LAYERS, M2_CONV_DIM)),
        "m2_dt_bias": dt0 + jnp.log(-jnp.expm1(-dt0)),
        "m2_a_log": jnp.log(jax.random.uniform(next(ks), (N_M2_LAYERS, M2_HEADS), f32, minval=1.0, maxval=16.0)),
        "m2_d": gain((N_M2_LAYERS, M2_HEADS)),
        "m2_norm_w": gain((N_M2_LAYERS, M2_D_INNER)),
        "m2_w_out": nrm((N_M2_LAYERS, M2_D_INNER, D_MODEL), M2_D_INNER),
        "mlp_w_up": nrm((DEPTH, D_MODEL, MLP_HIDDEN), D_MODEL),
        "mlp_w_down": nrm((DEPTH, MLP_HIDDEN, D_MODEL), MLP_HIDDEN),
    }


def _fwd_reference(x, norm_mix_w, norm_mlp_w, sb_w_qkv, sb_q_norm_w, sb_k_norm_w, sb_w_o,
              cf_w_in, cf_b_in, cf_dw_w, cf_dw_b, cf_ln_w, cf_ln_b, cf_w_out, cf_b_out,
              m2_w_in, m2_conv_w, m2_conv_b, m2_dt_bias, m2_a_log, m2_d, m2_norm_w, m2_w_out,
              mlp_w_up, mlp_w_down):
    for i in range(DEPTH):
        kind, j = i % N_MIXERS, i // N_MIXERS
        h = rms_norm(x, norm_mix_w[i])
        if kind == 0:
            mix = stick_breaking_mixer(h, sb_w_qkv[j], sb_q_norm_w[j], sb_k_norm_w[j], sb_w_o[j])
        elif kind == 1:
            mix = conformer_conv_mixer(h, cf_w_in[j], cf_b_in[j], cf_dw_w[j], cf_dw_b[j],
                                       cf_ln_w[j], cf_ln_b[j], cf_w_out[j], cf_b_out[j])
        else:
            mix = mamba2_mixer(h, m2_w_in[j], m2_conv_w[j], m2_conv_b[j], m2_dt_bias[j],
                               m2_a_log[j], m2_d[j], m2_norm_w[j], m2_w_out[j])
        x = x + mix.astype(x.dtype)
        h = rms_norm(x, norm_mlp_w[i])
        x = x + squared_relu_mlp(h, mlp_w_up[i], mlp_w_down[i]).astype(x.dtype)
    return x


import jax as _jax
import jax.numpy as _jnp

TWIN_FORMAT = 'train_step'
FWD_PARAMS = ['x', 'norm_mix_w', 'norm_mlp_w', 'sb_w_qkv', 'sb_q_norm_w', 'sb_k_norm_w', 'sb_w_o', 'cf_w_in', 'cf_b_in', 'cf_dw_w', 'cf_dw_b', 'cf_ln_w', 'cf_ln_b', 'cf_w_out', 'cf_b_out', 'm2_w_in', 'm2_conv_w', 'm2_conv_b', 'm2_dt_bias', 'm2_a_log', 'm2_d', 'm2_norm_w', 'm2_w_out', 'mlp_w_up', 'mlp_w_down']
TWIN_WEIGHTS = ['norm_mix_w', 'norm_mlp_w', 'sb_w_qkv', 'sb_q_norm_w', 'sb_k_norm_w', 'sb_w_o', 'cf_w_in', 'cf_b_in', 'cf_dw_w', 'cf_dw_b', 'cf_ln_w', 'cf_ln_b', 'cf_w_out', 'cf_b_out', 'm2_w_in', 'm2_conv_w', 'm2_conv_b', 'm2_dt_bias', 'm2_a_log', 'm2_d', 'm2_norm_w', 'm2_w_out', 'mlp_w_up', 'mlp_w_down']
TWIN_DIFF_INPUT = 'x'
TWIN_INPUTS = ['x', 'norm_mix_w', 'norm_mlp_w', 'sb_w_qkv', 'sb_q_norm_w', 'sb_k_norm_w', 'sb_w_o', 'cf_w_in', 'cf_b_in', 'cf_dw_w', 'cf_dw_b', 'cf_ln_w', 'cf_ln_b', 'cf_w_out', 'cf_b_out', 'm2_w_in', 'm2_conv_w', 'm2_conv_b', 'm2_dt_bias', 'm2_a_log', 'm2_d', 'm2_norm_w', 'm2_w_out', 'mlp_w_up', 'mlp_w_down', 'loss_target', 'm_norm_mix_w', 'm_norm_mlp_w', 'm_sb_w_qkv', 'm_sb_q_norm_w', 'm_sb_k_norm_w', 'm_sb_w_o', 'm_cf_w_in', 'm_cf_b_in', 'm_cf_dw_w', 'm_cf_dw_b', 'm_cf_ln_w', 'm_cf_ln_b', 'm_cf_w_out', 'm_cf_b_out', 'm_m2_w_in', 'm_m2_conv_w', 'm_m2_conv_b', 'm_m2_dt_bias', 'm_m2_a_log', 'm_m2_d', 'm_m2_norm_w', 'm_m2_w_out', 'm_mlp_w_up', 'm_mlp_w_down', 'v_norm_mix_w', 'v_norm_mlp_w', 'v_sb_w_qkv', 'v_sb_q_norm_w', 'v_sb_k_norm_w', 'v_sb_w_o', 'v_cf_w_in', 'v_cf_b_in', 'v_cf_dw_w', 'v_cf_dw_b', 'v_cf_ln_w', 'v_cf_ln_b', 'v_cf_w_out', 'v_cf_b_out', 'v_m2_w_in', 'v_m2_conv_w', 'v_m2_conv_b', 'v_m2_dt_bias', 'v_m2_a_log', 'v_m2_d', 'v_m2_norm_w', 'v_m2_w_out', 'v_mlp_w_up', 'v_mlp_w_down']
TWIN_OUTPUTS = ['loss', 'grad_x', 'grad_norm_mix_w', 'grad_norm_mlp_w', 'grad_sb_w_qkv', 'grad_sb_q_norm_w', 'grad_sb_k_norm_w', 'grad_sb_w_o', 'grad_cf_w_in', 'grad_cf_b_in', 'grad_cf_dw_w', 'grad_cf_dw_b', 'grad_cf_ln_w', 'grad_cf_ln_b', 'grad_cf_w_out', 'grad_cf_b_out', 'grad_m2_w_in', 'grad_m2_conv_w', 'grad_m2_conv_b', 'grad_m2_dt_bias', 'grad_m2_a_log', 'grad_m2_d', 'grad_m2_norm_w', 'grad_m2_w_out', 'grad_mlp_w_up', 'grad_mlp_w_down', 'delta_norm_mix_w', 'delta_norm_mlp_w', 'delta_sb_w_qkv', 'delta_sb_q_norm_w', 'delta_sb_k_norm_w', 'delta_sb_w_o', 'delta_cf_w_in', 'delta_cf_b_in', 'delta_cf_dw_w', 'delta_cf_dw_b', 'delta_cf_ln_w', 'delta_cf_ln_b', 'delta_cf_w_out', 'delta_cf_b_out', 'delta_m2_w_in', 'delta_m2_conv_w', 'delta_m2_conv_b', 'delta_m2_dt_bias', 'delta_m2_a_log', 'delta_m2_d', 'delta_m2_norm_w', 'delta_m2_w_out', 'delta_mlp_w_up', 'delta_mlp_w_down', 'new_m_norm_mix_w', 'new_m_norm_mlp_w', 'new_m_sb_w_qkv', 'new_m_sb_q_norm_w', 'new_m_sb_k_norm_w', 'new_m_sb_w_o', 'new_m_cf_w_in', 'new_m_cf_b_in', 'new_m_cf_dw_w', 'new_m_cf_dw_b', 'new_m_cf_ln_w', 'new_m_cf_ln_b', 'new_m_cf_w_out', 'new_m_cf_b_out', 'new_m_m2_w_in', 'new_m_m2_conv_w', 'new_m_m2_conv_b', 'new_m_m2_dt_bias', 'new_m_m2_a_log', 'new_m_m2_d', 'new_m_m2_norm_w', 'new_m_m2_w_out', 'new_m_mlp_w_up', 'new_m_mlp_w_down', 'new_v_norm_mix_w', 'new_v_norm_mlp_w', 'new_v_sb_w_qkv', 'new_v_sb_q_norm_w', 'new_v_sb_k_norm_w', 'new_v_sb_w_o', 'new_v_cf_w_in', 'new_v_cf_b_in', 'new_v_cf_dw_w', 'new_v_cf_dw_b', 'new_v_cf_ln_w', 'new_v_cf_ln_b', 'new_v_cf_w_out', 'new_v_cf_b_out', 'new_v_m2_w_in', 'new_v_m2_conv_w', 'new_v_m2_conv_b', 'new_v_m2_dt_bias', 'new_v_m2_a_log', 'new_v_m2_d', 'new_v_m2_norm_w', 'new_v_m2_w_out', 'new_v_mlp_w_up', 'new_v_mlp_w_down']
TWIN_LEAF_KINDS = {'loss': 'loss', 'grad_x': 'grad_x', 'grad_norm_mix_w': 'grad_w', 'grad_norm_mlp_w': 'grad_w', 'grad_sb_w_qkv': 'grad_w', 'grad_sb_q_norm_w': 'grad_w', 'grad_sb_k_norm_w': 'grad_w', 'grad_sb_w_o': 'grad_w', 'grad_cf_w_in': 'grad_w', 'grad_cf_b_in': 'grad_w', 'grad_cf_dw_w': 'grad_w', 'grad_cf_dw_b': 'grad_w', 'grad_cf_ln_w': 'grad_w', 'grad_cf_ln_b': 'grad_w', 'grad_cf_w_out': 'grad_w', 'grad_cf_b_out': 'grad_w', 'grad_m2_w_in': 'grad_w', 'grad_m2_conv_w': 'grad_w', 'grad_m2_conv_b': 'grad_w', 'grad_m2_dt_bias': 'grad_w', 'grad_m2_a_log': 'grad_w', 'grad_m2_d': 'grad_w', 'grad_m2_norm_w': 'grad_w', 'grad_m2_w_out': 'grad_w', 'grad_mlp_w_up': 'grad_w', 'grad_mlp_w_down': 'grad_w', 'delta_norm_mix_w': 'delta_w', 'delta_norm_mlp_w': 'delta_w', 'delta_sb_w_qkv': 'delta_w', 'delta_sb_q_norm_w': 'delta_w', 'delta_sb_k_norm_w': 'delta_w', 'delta_sb_w_o': 'delta_w', 'delta_cf_w_in': 'delta_w', 'delta_cf_b_in': 'delta_w', 'delta_cf_dw_w': 'delta_w', 'delta_cf_dw_b': 'delta_w', 'delta_cf_ln_w': 'delta_w', 'delta_cf_ln_b': 'delta_w', 'delta_cf_w_out': 'delta_w', 'delta_cf_b_out': 'delta_w', 'delta_m2_w_in': 'delta_w', 'delta_m2_conv_w': 'delta_w', 'delta_m2_conv_b': 'delta_w', 'delta_m2_dt_bias': 'delta_w', 'delta_m2_a_log': 'delta_w', 'delta_m2_d': 'delta_w', 'delta_m2_norm_w': 'delta_w', 'delta_m2_w_out': 'delta_w', 'delta_mlp_w_up': 'delta_w', 'delta_mlp_w_down': 'delta_w', 'new_m_norm_mix_w': 'new_m', 'new_m_norm_mlp_w': 'new_m', 'new_m_sb_w_qkv': 'new_m', 'new_m_sb_q_norm_w': 'new_m', 'new_m_sb_k_norm_w': 'new_m', 'new_m_sb_w_o': 'new_m', 'new_m_cf_w_in': 'new_m', 'new_m_cf_b_in': 'new_m', 'new_m_cf_dw_w': 'new_m', 'new_m_cf_dw_b': 'new_m', 'new_m_cf_ln_w': 'new_m', 'new_m_cf_ln_b': 'new_m', 'new_m_cf_w_out': 'new_m', 'new_m_cf_b_out': 'new_m', 'new_m_m2_w_in': 'new_m', 'new_m_m2_conv_w': 'new_m', 'new_m_m2_conv_b': 'new_m', 'new_m_m2_dt_bias': 'new_m', 'new_m_m2_a_log': 'new_m', 'new_m_m2_d': 'new_m', 'new_m_m2_norm_w': 'new_m', 'new_m_m2_w_out': 'new_m', 'new_m_mlp_w_up': 'new_m', 'new_m_mlp_w_down': 'new_m', 'new_v_norm_mix_w': 'new_v', 'new_v_norm_mlp_w': 'new_v', 'new_v_sb_w_qkv': 'new_v', 'new_v_sb_q_norm_w': 'new_v', 'new_v_sb_k_norm_w': 'new_v', 'new_v_sb_w_o': 'new_v', 'new_v_cf_w_in': 'new_v', 'new_v_cf_b_in': 'new_v', 'new_v_cf_dw_w': 'new_v', 'new_v_cf_dw_b': 'new_v', 'new_v_cf_ln_w': 'new_v', 'new_v_cf_ln_b': 'new_v', 'new_v_cf_w_out': 'new_v', 'new_v_cf_b_out': 'new_v', 'new_v_m2_w_in': 'new_v', 'new_v_m2_conv_w': 'new_v', 'new_v_m2_conv_b': 'new_v', 'new_v_m2_dt_bias': 'new_v', 'new_v_m2_a_log': 'new_v', 'new_v_m2_d': 'new_v', 'new_v_m2_norm_w': 'new_v', 'new_v_m2_w_out': 'new_v', 'new_v_mlp_w_up': 'new_v', 'new_v_mlp_w_down': 'new_v'}


def _forward(args):
    return _fwd_reference(*[args[k] for k in FWD_PARAMS])


def _output_shape():
    def fwd():
        inp = _fwd_setup_inputs(0)
        return _fwd_reference(*[inp[k] for k in FWD_PARAMS])
    out = _jax.eval_shape(fwd)
    return out.shape, out.dtype

N_MICROBATCH = 1
ADAM_LR = 0.001
ADAM_B1 = 0.9
ADAM_B2 = 0.999
ADAM_EPS = 1e-08
ADAM_WD = 0.01
ADAM_STEP = 10
PER_EXAMPLE_BATCH_AXIS = {'x': 0, 'loss_target': 0}
SHARED_INPUTS = []
_WEIGHT_DTYPES = {'norm_mix_w': _jnp.float32, 'norm_mlp_w': _jnp.float32, 'sb_w_qkv': _jnp.float32, 'sb_q_norm_w': _jnp.float32, 'sb_k_norm_w': _jnp.float32, 'sb_w_o': _jnp.float32, 'cf_w_in': _jnp.float32, 'cf_b_in': _jnp.float32, 'cf_dw_w': _jnp.float32, 'cf_dw_b': _jnp.float32, 'cf_ln_w': _jnp.float32, 'cf_ln_b': _jnp.float32, 'cf_w_out': _jnp.float32, 'cf_b_out': _jnp.float32, 'm2_w_in': _jnp.float32, 'm2_conv_w': _jnp.float32, 'm2_conv_b': _jnp.float32, 'm2_dt_bias': _jnp.float32, 'm2_a_log': _jnp.float32, 'm2_d': _jnp.float32, 'm2_norm_w': _jnp.float32, 'm2_w_out': _jnp.float32, 'mlp_w_up': _jnp.float32, 'mlp_w_down': _jnp.float32}
MOMENT_SCALE = {'norm_mix_w': 1.632359e+01, 'norm_mlp_w': 9.825101e+01, 'sb_w_qkv': 7.156256e+00, 'sb_q_norm_w': 2.852778e+01, 'sb_k_norm_w': 2.851183e+01, 'sb_w_o': 1.147923e+01, 'cf_w_in': 6.349930e+00, 'cf_b_in': 2.115182e+01, 'cf_dw_w': 1.004943e+01, 'cf_dw_b': 5.662554e+01, 'cf_ln_w': 2.889203e+01, 'cf_ln_b': 3.352469e+01, 'cf_w_out': 1.628008e+01, 'cf_b_out': 6.343443e+01, 'm2_w_in': 6.135069e+00, 'm2_conv_w': 7.612047e+00, 'm2_conv_b': 1.565627e+01, 'm2_dt_bias': 4.310959e+00, 'm2_a_log': 2.453831e+01, 'm2_d': 3.546582e+01, 'm2_norm_w': 3.897275e+01, 'm2_w_out': 1.663383e+01, 'mlp_w_up': 7.884163e+00, 'mlp_w_down': 2.957798e+01}


def _to_microbatches(a, axis):
    t = _jnp.moveaxis(a, axis, 0)
    t = t.reshape((N_MICROBATCH, t.shape[0] // N_MICROBATCH) + t.shape[1:])
    return _jnp.moveaxis(t, 1, axis + 1)


def setup_inputs(seed: int = 0) -> dict:
    inp = _fwd_setup_inputs(seed)
    key = _jax.random.fold_in(_jax.random.key(seed), 7919)
    shape, _ = _output_shape()
    out = dict(inp)
    out["loss_target"] = _jax.random.normal(_jax.random.fold_in(key, 0), shape, _jnp.float32)
    for i, name in enumerate(TWIN_WEIGHTS):
        w = inp[name].astype(_jnp.float32)
        if MOMENT_SCALE is None:
            s = _jnp.sqrt(_jnp.mean(_jnp.square(w)) + 1e-30)
        else:
            s = MOMENT_SCALE[name]
        km, kv = _jax.random.split(_jax.random.fold_in(key, i + 1))
        out[name] = w
        out["m_" + name] = s * _jax.random.normal(km, w.shape, _jnp.float32)
        out["v_" + name] = (s * s) * _jax.random.uniform(kv, w.shape, _jnp.float32, 0.5, 1.5)
    if N_MICROBATCH > 1:
        for name, axis in PER_EXAMPLE_BATCH_AXIS.items():
            out[name] = _to_microbatches(out[name], axis)
    return {'x': out['x'], 'norm_mix_w': out['norm_mix_w'], 'norm_mlp_w': out['norm_mlp_w'], 'sb_w_qkv': out['sb_w_qkv'], 'sb_q_norm_w': out['sb_q_norm_w'], 'sb_k_norm_w': out['sb_k_norm_w'], 'sb_w_o': out['sb_w_o'], 'cf_w_in': out['cf_w_in'], 'cf_b_in': out['cf_b_in'], 'cf_dw_w': out['cf_dw_w'], 'cf_dw_b': out['cf_dw_b'], 'cf_ln_w': out['cf_ln_w'], 'cf_ln_b': out['cf_ln_b'], 'cf_w_out': out['cf_w_out'], 'cf_b_out': out['cf_b_out'], 'm2_w_in': out['m2_w_in'], 'm2_conv_w': out['m2_conv_w'], 'm2_conv_b': out['m2_conv_b'], 'm2_dt_bias': out['m2_dt_bias'], 'm2_a_log': out['m2_a_log'], 'm2_d': out['m2_d'], 'm2_norm_w': out['m2_norm_w'], 'm2_w_out': out['m2_w_out'], 'mlp_w_up': out['mlp_w_up'], 'mlp_w_down': out['mlp_w_down'], 'loss_target': out['loss_target'], 'm_norm_mix_w': out['m_norm_mix_w'], 'm_norm_mlp_w': out['m_norm_mlp_w'], 'm_sb_w_qkv': out['m_sb_w_qkv'], 'm_sb_q_norm_w': out['m_sb_q_norm_w'], 'm_sb_k_norm_w': out['m_sb_k_norm_w'], 'm_sb_w_o': out['m_sb_w_o'], 'm_cf_w_in': out['m_cf_w_in'], 'm_cf_b_in': out['m_cf_b_in'], 'm_cf_dw_w': out['m_cf_dw_w'], 'm_cf_dw_b': out['m_cf_dw_b'], 'm_cf_ln_w': out['m_cf_ln_w'], 'm_cf_ln_b': out['m_cf_ln_b'], 'm_cf_w_out': out['m_cf_w_out'], 'm_cf_b_out': out['m_cf_b_out'], 'm_m2_w_in': out['m_m2_w_in'], 'm_m2_conv_w': out['m_m2_conv_w'], 'm_m2_conv_b': out['m_m2_conv_b'], 'm_m2_dt_bias': out['m_m2_dt_bias'], 'm_m2_a_log': out['m_m2_a_log'], 'm_m2_d': out['m_m2_d'], 'm_m2_norm_w': out['m_m2_norm_w'], 'm_m2_w_out': out['m_m2_w_out'], 'm_mlp_w_up': out['m_mlp_w_up'], 'm_mlp_w_down': out['m_mlp_w_down'], 'v_norm_mix_w': out['v_norm_mix_w'], 'v_norm_mlp_w': out['v_norm_mlp_w'], 'v_sb_w_qkv': out['v_sb_w_qkv'], 'v_sb_q_norm_w': out['v_sb_q_norm_w'], 'v_sb_k_norm_w': out['v_sb_k_norm_w'], 'v_sb_w_o': out['v_sb_w_o'], 'v_cf_w_in': out['v_cf_w_in'], 'v_cf_b_in': out['v_cf_b_in'], 'v_cf_dw_w': out['v_cf_dw_w'], 'v_cf_dw_b': out['v_cf_dw_b'], 'v_cf_ln_w': out['v_cf_ln_w'], 'v_cf_ln_b': out['v_cf_ln_b'], 'v_cf_w_out': out['v_cf_w_out'], 'v_cf_b_out': out['v_cf_b_out'], 'v_m2_w_in': out['v_m2_w_in'], 'v_m2_conv_w': out['v_m2_conv_w'], 'v_m2_conv_b': out['v_m2_conv_b'], 'v_m2_dt_bias': out['v_m2_dt_bias'], 'v_m2_a_log': out['v_m2_a_log'], 'v_m2_d': out['v_m2_d'], 'v_m2_norm_w': out['v_m2_norm_w'], 'v_m2_w_out': out['v_m2_w_out'], 'v_mlp_w_up': out['v_mlp_w_up'], 'v_mlp_w_down': out['v_mlp_w_down']}


def _loss(weights, diff, rest, loss_target):
    with _jax.named_scope("forward"):
        args = {**rest, TWIN_DIFF_INPUT: diff, **{k: w.astype(_WEIGHT_DTYPES[k]) for k, w in weights.items()}}
        y = _forward(args)
    with _jax.named_scope("loss_head"):
        err = _jnp.square(y.astype(_jnp.float32) - loss_target)
        return 0.5 * _jnp.sum(_jnp.mean(err, axis=-1)) if err.ndim else 0.5 * err


def _adamw(w, g, m, v):
    m = ADAM_B1 * m + (1.0 - ADAM_B1) * g
    v = ADAM_B2 * v + (1.0 - ADAM_B2) * _jnp.square(g)
    m_hat = m / (1.0 - ADAM_B1 ** ADAM_STEP)
    v_hat = v / (1.0 - ADAM_B2 ** ADAM_STEP)
    delta = -ADAM_LR * (m_hat / (_jnp.sqrt(v_hat) + ADAM_EPS) + ADAM_WD * w)
    return delta, m, v


def reference(x, norm_mix_w, norm_mlp_w, sb_w_qkv, sb_q_norm_w, sb_k_norm_w, sb_w_o, cf_w_in, cf_b_in, cf_dw_w, cf_dw_b, cf_ln_w, cf_ln_b, cf_w_out, cf_b_out, m2_w_in, m2_conv_w, m2_conv_b, m2_dt_bias, m2_a_log, m2_d, m2_norm_w, m2_w_out, mlp_w_up, mlp_w_down, loss_target, m_norm_mix_w, m_norm_mlp_w, m_sb_w_qkv, m_sb_q_norm_w, m_sb_k_norm_w, m_sb_w_o, m_cf_w_in, m_cf_b_in, m_cf_dw_w, m_cf_dw_b, m_cf_ln_w, m_cf_ln_b, m_cf_w_out, m_cf_b_out, m_m2_w_in, m_m2_conv_w, m_m2_conv_b, m_m2_dt_bias, m_m2_a_log, m_m2_d, m_m2_norm_w, m_m2_w_out, m_mlp_w_up, m_mlp_w_down, v_norm_mix_w, v_norm_mlp_w, v_sb_w_qkv, v_sb_q_norm_w, v_sb_k_norm_w, v_sb_w_o, v_cf_w_in, v_cf_b_in, v_cf_dw_w, v_cf_dw_b, v_cf_ln_w, v_cf_ln_b, v_cf_w_out, v_cf_b_out, v_m2_w_in, v_m2_conv_w, v_m2_conv_b, v_m2_dt_bias, v_m2_a_log, v_m2_d, v_m2_norm_w, v_m2_w_out, v_mlp_w_up, v_mlp_w_down):
    given = dict(x=x, norm_mix_w=norm_mix_w, norm_mlp_w=norm_mlp_w, sb_w_qkv=sb_w_qkv, sb_q_norm_w=sb_q_norm_w, sb_k_norm_w=sb_k_norm_w, sb_w_o=sb_w_o, cf_w_in=cf_w_in, cf_b_in=cf_b_in, cf_dw_w=cf_dw_w, cf_dw_b=cf_dw_b, cf_ln_w=cf_ln_w, cf_ln_b=cf_ln_b, cf_w_out=cf_w_out, cf_b_out=cf_b_out, m2_w_in=m2_w_in, m2_conv_w=m2_conv_w, m2_conv_b=m2_conv_b, m2_dt_bias=m2_dt_bias, m2_a_log=m2_a_log, m2_d=m2_d, m2_norm_w=m2_norm_w, m2_w_out=m2_w_out, mlp_w_up=mlp_w_up, mlp_w_down=mlp_w_down, loss_target=loss_target, m_norm_mix_w=m_norm_mix_w, m_norm_mlp_w=m_norm_mlp_w, m_sb_w_qkv=m_sb_w_qkv, m_sb_q_norm_w=m_sb_q_norm_w, m_sb_k_norm_w=m_sb_k_norm_w, m_sb_w_o=m_sb_w_o, m_cf_w_in=m_cf_w_in, m_cf_b_in=m_cf_b_in, m_cf_dw_w=m_cf_dw_w, m_cf_dw_b=m_cf_dw_b, m_cf_ln_w=m_cf_ln_w, m_cf_ln_b=m_cf_ln_b, m_cf_w_out=m_cf_w_out, m_cf_b_out=m_cf_b_out, m_m2_w_in=m_m2_w_in, m_m2_conv_w=m_m2_conv_w, m_m2_conv_b=m_m2_conv_b, m_m2_dt_bias=m_m2_dt_bias, m_m2_a_log=m_m2_a_log, m_m2_d=m_m2_d, m_m2_norm_w=m_m2_norm_w, m_m2_w_out=m_m2_w_out, m_mlp_w_up=m_mlp_w_up, m_mlp_w_down=m_mlp_w_down, v_norm_mix_w=v_norm_mix_w, v_norm_mlp_w=v_norm_mlp_w, v_sb_w_qkv=v_sb_w_qkv, v_sb_q_norm_w=v_sb_q_norm_w, v_sb_k_norm_w=v_sb_k_norm_w, v_sb_w_o=v_sb_w_o, v_cf_w_in=v_cf_w_in, v_cf_b_in=v_cf_b_in, v_cf_dw_w=v_cf_dw_w, v_cf_dw_b=v_cf_dw_b, v_cf_ln_w=v_cf_ln_w, v_cf_ln_b=v_cf_ln_b, v_cf_w_out=v_cf_w_out, v_cf_b_out=v_cf_b_out, v_m2_w_in=v_m2_w_in, v_m2_conv_w=v_m2_conv_w, v_m2_conv_b=v_m2_conv_b, v_m2_dt_bias=v_m2_dt_bias, v_m2_a_log=v_m2_a_log, v_m2_d=v_m2_d, v_m2_norm_w=v_m2_norm_w, v_m2_w_out=v_m2_w_out, v_mlp_w_up=v_mlp_w_up, v_mlp_w_down=v_mlp_w_down)
    weights = {n: given[n] for n in TWIN_WEIGHTS}
    shared = {n: given[n] for n in SHARED_INPUTS}
    per_example = {n: given[n] for n in ['x']}
    grad_fn = _jax.value_and_grad(_loss, argnums=(0, 1))

    def one_microbatch(ex, loss_target):
        ex = dict(ex)
        diff = ex.pop(TWIN_DIFF_INPUT)
        return grad_fn(weights, diff, {**shared, **ex}, loss_target)

    if N_MICROBATCH == 1:
        loss, (grad_w, grad_x) = one_microbatch(per_example, given["loss_target"])
    else:
        def body(carry, xs):
            loss_sum, grad_sum = carry
            l_k, (gw_k, gx_k) = one_microbatch(xs[0], xs[1])
            with _jax.named_scope("update"):
                return (loss_sum + l_k, _jax.tree.map(_jnp.add, grad_sum, gw_k)), gx_k

        init = (_jnp.zeros((), _jnp.float32), _jax.tree.map(_jnp.zeros_like, weights))
        (loss, grad_w), grad_x = _jax.lax.scan(body, init, (per_example, given["loss_target"]))
    with _jax.named_scope("update"):
        delta_w, new_m, new_v = {}, {}, {}
        for n in TWIN_WEIGHTS:
            delta_w[n], new_m[n], new_v[n] = _adamw(weights[n], grad_w[n], given["m_" + n], given["v_" + n])
    return (loss, grad_x, *[grad_w[n] for n in TWIN_WEIGHTS], *[delta_w[n] for n in TWIN_WEIGHTS],
            *[new_m[n] for n in TWIN_WEIGHTS], *[new_v[n] for n in TWIN_WEIGHTS])
```

```python
import math

import jax
import jax.numpy as jnp
from jax import lax
from jax.experimental import pallas as pl
from jax.experimental.pallas import tpu as pltpu

F32 = jnp.float32
BF16 = jnp.bfloat16
MESH_ID = pl.DeviceIdType.MESH
HIGHEST = lax.Precision.HIGHEST

SB_HEAD_DIM = 128
M2_HEAD_DIM = 64
M2_STATE = 128
M2_GROUPS = 8
M2_CHUNK = 128
RMS_EPS = 1e-6
LN_EPS = 1e-5
M2_NORM_EPS = 1e-5
ADAM_LR = 0.001
ADAM_B1 = 0.9
ADAM_B2 = 0.999
ADAM_EPS = 1e-08
ADAM_WD = 0.01
ADAM_STEP = 10

N_XY = 4
N_DEV = 8

V7X_VMEM_BYTES = 64 * 2**20
VMEM_LIMIT = (V7X_VMEM_BYTES * 3) // 4
LANES = 128
ROW_BLOCK_BYTES = 6 * 2**20
PACK_W = 1024
PACK_ROWS = 256

W_NAMES = ['norm_mix_w', 'norm_mlp_w', 'sb_w_qkv', 'sb_q_norm_w', 'sb_k_norm_w', 'sb_w_o', 'cf_w_in', 'cf_b_in',
           'cf_dw_w', 'cf_dw_b', 'cf_ln_w', 'cf_ln_b', 'cf_w_out', 'cf_b_out', 'm2_w_in', 'm2_conv_w', 'm2_conv_b',
           'm2_dt_bias', 'm2_a_log', 'm2_d', 'm2_norm_w', 'm2_w_out', 'mlp_w_up', 'mlp_w_down']
BULK = {'sb_w_qkv': 2, 'sb_w_o': 1, 'cf_w_in': 2, 'cf_w_out': 1, 'm2_w_in': 2, 'm2_w_out': 1, 'mlp_w_up': 2,
        'mlp_w_down': 1}
SMALL = {'cf_dw_w': 2, 'm2_conv_w': 2, 'm2_conv_b': 1, 'm2_norm_w': 1}
REPL = ['norm_mix_w', 'norm_mlp_w', 'sb_q_norm_w', 'sb_k_norm_w', 'cf_b_in', 'cf_dw_b', 'cf_ln_w', 'cf_ln_b',
        'cf_b_out', 'm2_dt_bias', 'm2_a_log', 'm2_d']


def _tile(n, prefs):
    for p in prefs:
        if n % p == 0:
            return p
    return n


def _cparams(n):
    return pltpu.CompilerParams(dimension_semantics=("arbitrary",) * n, vmem_limit_bytes=VMEM_LIMIT)


def _silu(v):
    return v / (1.0 + jnp.exp(-v))


def _softplus(v):
    return jnp.maximum(v, 0.0) + jnp.log(1.0 + jnp.exp(-jnp.abs(v)))


def _mm(a, b, mode, *, extras=(), epilogue=None, out_dtypes=(F32,), name):
    if mode == "tn":
        kdim, m = a.shape
    else:
        m, kdim = a.shape
    n = b.shape[0] if mode == "nt" else b.shape[1]
    tm = _tile(m, (512, 256, 128))
    tn = _tile(n, (1024, 512, 256, 128))
    tk = _tile(kdim, (1024, 512, 256, 128))
    nk = kdim // tk
    dims = {"nn": ((1,), (0,)), "nt": ((1,), (1,)), "tn": ((0,), (0,))}[mode]
    if mode == "tn":
        a_spec = pl.BlockSpec((tk, tm), lambda i, j, k: (k, i))
    else:
        a_spec = pl.BlockSpec((tm, tk), lambda i, j, k: (i, k))
    if mode == "nt":
        b_spec = pl.BlockSpec((tn, tk), lambda i, j, k: (j, k))
    else:
        b_spec = pl.BlockSpec((tk, tn), lambda i, j, k: (k, j))
    ex_specs = [pl.BlockSpec((1, tn), lambda i, j, k: (0, j)) if e.shape[0] == 1
                else pl.BlockSpec((tm, tn), lambda i, j, k: (i, j)) for e in extras]
    n_ex, n_out = len(extras), len(out_dtypes)

    def body(*refs):
        a_ref, b_ref = refs[:2]
        ex = refs[2:2 + n_ex]
        outs = refs[2 + n_ex:2 + n_ex + n_out]
        acc = refs[-1]
        k = pl.program_id(2)

        @pl.when(k == 0)
        def _():
            acc[...] = jnp.zeros_like(acc)

        acc[...] += lax.dot_general(a_ref[...].astype(BF16), b_ref[...].astype(BF16), (dims, ((), ())),
                                    preferred_element_type=F32)

        @pl.when(k == nk - 1)
        def _():
            res = acc[...]
            vals = epilogue(res, *[e[...] for e in ex]) if epilogue is not None else (res,)
            for o, v in zip(outs, vals):
                o[...] = v.astype(o.dtype)

    outs = pl.pallas_call(
        body, grid=(m // tm, n // tn, nk), in_specs=[a_spec, b_spec, *ex_specs],
        out_specs=[pl.BlockSpec((tm, tn), lambda i, j, k: (i, j))] * n_out,
        out_shape=[jax.ShapeDtypeStruct((m, n), d) for d in out_dtypes],
        scratch_shapes=[pltpu.VMEM((tm, tn), F32)], compiler_params=_cparams(3), name=name)(a, b, *extras)
    return outs[0] if n_out == 1 else outs


def _ep_add(acc, r):
    return (acc + r,)


def _ep_bias_add(acc, b, r):
    return (acc + b + r,)


def _ep_relu2(acc):
    return acc, jnp.square(jnp.maximum(acc, 0.0))


def _ep_relu2_bwd(acc, u):
    return (acc * (2.0 * jnp.maximum(u, 0.0)),)


def _rowwise(fn, rows, consts, outs, *, ncb=1, name):
    nrow = rows[0][0].shape[0]
    row_bytes = sum(w * arr.dtype.itemsize for arr, w, _ in rows)
    row_bytes += sum(o[2] * jnp.dtype(o[3]).itemsize for o in outs if o[0] == 'row')
    t = nrow
    for cand in (1024, 512, 256, 128, 64, 32, 16):
        if nrow % cand == 0:
            t = cand
            if cand * row_bytes <= ROW_BLOCK_BYTES:
                break
    in_specs = [pl.BlockSpec((t, w), (lambda j, i, off=off: (i, off + j))) for _, w, off in rows]
    for arr, per_col in consts:
        r, wc = arr.shape
        if per_col:
            in_specs.append(pl.BlockSpec((r, wc // ncb), lambda j, i: (0, j)))
        else:
            in_specs.append(pl.BlockSpec((r, wc), lambda j, i: (0, 0)))
    out_specs, out_shape = [], []
    for o in outs:
        if o[0] == 'row':
            out_specs.append(pl.BlockSpec((t, o[2]), lambda j, i: (i, j)))
            out_shape.append(jax.ShapeDtypeStruct((nrow, o[1]), o[3]))
        else:
            r, wt = o[1]
            if o[2]:
                out_specs.append(pl.BlockSpec((r, wt // ncb), lambda j, i: (0, j)))
            else:
                out_specs.append(pl.BlockSpec((r, wt), lambda j, i: (0, 0)))
            out_shape.append(jax.ShapeDtypeStruct((r, wt), F32))
    nin = len(rows) + len(consts)

    def body(*refs):
        j, i = pl.program_id(0), pl.program_id(1)
        vals = fn(*[r[...] for r in refs[:nin]])

        def store(spec, ref, v):
            if spec[0] == 'row':
                ref[...] = v.astype(ref.dtype)
            else:
                first = (i == 0) if spec[2] else ((i == 0) & (j == 0))

                @pl.when(first)
                def _():
                    ref[...] = jnp.zeros_like(ref)

                ref[...] += v

        for spec, ref, v in zip(outs, refs[nin:], vals):
            store(spec, ref, v)

    res = pl.pallas_call(body, grid=(ncb, nrow // t), in_specs=in_specs, out_specs=out_specs, out_shape=out_shape,
                         compiler_params=_cparams(2), name=name)(*[r[0] for r in rows], *[c[0] for c in consts])
    return res


def _rms_f(eps):
    def f(x, w):
        return x * lax.rsqrt(jnp.mean(x * x, axis=-1, keepdims=True) + eps) * w
    return f


def _rms_fwd(x, w, name):
    d = x.shape[1]
    f = _rms_f(RMS_EPS)
    return _rowwise(lambda xv, wv: (f(xv, wv),), [(x, d, 0)], [(w, False)], [('row', d, d, BF16)], name=name)[0]


def _rms_bwd(x, w, dh, dres, name):
    d = x.shape[1]
    f = _rms_f(RMS_EPS)

    def fn(xv, dhv, drv, wv):
        _, vjp = jax.vjp(f, xv, wv)
        dx, dw = vjp(dhv)
        return dx + drv, dw

    return _rowwise(fn, [(x, d, 0), (dh, d, 0), (dres, d, 0)], [(w, False)],
                    [('row', d, d, F32), ('acc', (1, d), False)], name=name)


def _colsum(x, name):
    w = x.shape[1]
    return _rowwise(lambda v: (jnp.sum(v, axis=0, keepdims=True),), [(x, w, 0)], [], [('acc', (1, w), False)],
                    name=name)[0]


def _loss(y, tgt, name):
    d = y.shape[1]

    def fn(yv, tv):
        e = yv - tv
        s = jnp.sum(jnp.sum(e * e, axis=1, keepdims=True), axis=0, keepdims=True) * (0.5 / d)
        return e * (1.0 / d), s + jnp.zeros((1, LANES), F32)

    return _rowwise(fn, [(y, d, 0), (tgt, d, 0)], [], [('row', d, d, F32), ('acc', (1, LANES), False)], name=name)


def _conv_tiles(x, w):
    nrow, ch = x.shape
    kw = w.shape[0]
    halo = 8 * ((kw - 1 + 7) // 8)
    t = _tile(nrow, (128, 64, 32))
    tc = _tile(ch, (512, 256, 128))
    assert t % halo == 0 and nrow % t == 0
    return nrow, ch, kw, halo, t, tc


def _dwconv_fwd(x, w, b, name):
    nrow, ch, kw, halo, t, tc = _conv_tiles(x, w)
    per = t // halo

    def body(x_ref, halo_ref, w_ref, b_ref, y_ref, buf):
        i = pl.program_id(1)
        buf[0:halo, :] = jnp.where(i == 0, 0.0, halo_ref[...])
        buf[halo:halo + t, :] = x_ref[...]
        acc = jnp.zeros((t, tc), F32) + b_ref[...]
        for k in range(kw):
            s = halo - (kw - 1) + k
            acc = acc + w_ref[k:k + 1, :] * buf[s:s + t, :]
        y_ref[...] = acc

    return pl.pallas_call(
        body, grid=(ch // tc, nrow // t),
        in_specs=[pl.BlockSpec((t, tc), lambda j, i: (i, j)),
                  pl.BlockSpec((halo, tc), lambda j, i: (jnp.maximum(i * per - 1, 0), j)),
                  pl.BlockSpec((kw, tc), lambda j, i: (0, j)), pl.BlockSpec((1, tc), lambda j, i: (0, j))],
        out_specs=pl.BlockSpec((t, tc), lambda j, i: (i, j)), out_shape=jax.ShapeDtypeStruct((nrow, ch), F32),
        scratch_shapes=[pltpu.VMEM((t + halo, tc), F32)], compiler_params=_cparams(2), name=name)(x, x, w, b)


def _dwconv_bwd(x, dy, w, name):
    nrow, ch, kw, halo, t, tc = _conv_tiles(x, w)
    per = t // halo
    nt = nrow // t
    nhalo = nrow // halo

    def body(x_ref, xh_ref, dy_ref, dyh_ref, w_ref, dx_ref, dw_ref, db_ref, xbuf, dybuf):
        i = pl.program_id(1)
        xbuf[0:halo, :] = jnp.where(i == 0, 0.0, xh_ref[...])
        xbuf[halo:halo + t, :] = x_ref[...]
        dyc = dy_ref[...]
        dybuf[0:t, :] = dyc
        dybuf[t:t + halo, :] = jnp.where(i == nt - 1, 0.0, dyh_ref[...])
        acc = jnp.zeros((t, tc), F32)
        for k in range(kw):
            s = kw - 1 - k
            acc = acc + w_ref[k:k + 1, :] * dybuf[s:s + t, :]
        dx_ref[...] = acc

        @pl.when(i == 0)
        def _():
            dw_ref[...] = jnp.zeros_like(dw_ref)
            db_ref[...] = jnp.zeros_like(db_ref)

        for k in range(kw):
            s = kw - 1 - k
            dw_ref[k:k + 1, :] += jnp.sum(dyc * xbuf[halo - s:halo - s + t, :], axis=0, keepdims=True)
        db_ref[...] += jnp.sum(dyc, axis=0, keepdims=True)

    return pl.pallas_call(
        body, grid=(ch // tc, nt),
        in_specs=[pl.BlockSpec((t, tc), lambda j, i: (i, j)),
                  pl.BlockSpec((halo, tc), lambda j, i: (jnp.maximum(i * per - 1, 0), j)),
                  pl.BlockSpec((t, tc), lambda j, i: (i, j)),
                  pl.BlockSpec((halo, tc), lambda j, i: (jnp.minimum((i + 1) * per, nhalo - 1), j)),
                  pl.BlockSpec((kw, tc), lambda j, i: (0, j))],
        out_specs=[pl.BlockSpec((t, tc), lambda j, i: (i, j)), pl.BlockSpec((kw, tc), lambda j, i: (0, j)),
                   pl.BlockSpec((1, tc), lambda j, i: (0, j))],
        out_shape=[jax.ShapeDtypeStruct((nrow, ch), F32), jax.ShapeDtypeStruct((kw, ch), F32),
                   jax.ShapeDtypeStruct((1, ch), F32)],
        scratch_shapes=[pltpu.VMEM((t + halo, tc), F32), pltpu.VMEM((t + halo, tc), F32)],
        compiler_params=_cparams(2), name=name)(x, x, dy, dy, w)


SB_TK = 128


def _dot_split2(x, u):
    hi = x.astype(BF16)
    lo = (x - hi.astype(F32)).astype(BF16)
    return jnp.dot(hi, u, preferred_element_type=F32) + jnp.dot(lo, u, preferred_element_type=F32)


def _sb_scores(q, k, strict, scale):
    z = lax.dot_general(q, k, (((1,), (1,)), ((), ())), preferred_element_type=F32) * scale
    soft = jnp.log(1.0 + jnp.exp(-jnp.abs(z)))
    ls = jnp.minimum(z, 0.0) - soft
    lk = jnp.where(strict, jnp.minimum(-z, 0.0) - soft, 0.0)
    return ls, lk


def _sb_dims(qn):
    nrow, d = qn.shape
    hd = SB_HEAD_DIM
    tq = _tile(nrow, (256, 128))
    assert nrow % SB_TK == 0 and nrow // SB_TK <= LANES and tq % SB_TK == 0
    return nrow, d, hd, d // hd, tq


def _sb_attn_fwd(qn, kn, qkv, name):
    nrow, d, hd, nh, tq = _sb_dims(qn)
    tk = SB_TK
    scale = 1.0 / math.sqrt(hd)

    def body(q_ref, k_ref, v_ref, o_ref, rs_ref, r_acc, o_acc):
        i = pl.program_id(1)
        q = q_ref[...]
        nkb = (i + 1) * (tq // tk)
        rowi = i * tq + lax.broadcasted_iota(jnp.int32, (tq, tk), 0)
        coli = lax.broadcasted_iota(jnp.int32, (tq, tk), 1)
        lane = lax.broadcasted_iota(jnp.int32, (tq, LANES), 1)
        after = (lax.broadcasted_iota(jnp.int32, (tk, tk), 0) > lax.broadcasted_iota(jnp.int32, (tk, tk), 1)
                 ).astype(BF16)

        r_acc[...] = jnp.zeros_like(r_acc)
        o_acc[...] = jnp.zeros_like(o_acc)
        rs_ref[...] = jnp.zeros_like(rs_ref)

        def step(n, carry):
            j = nkb - 1 - n
            start = pl.multiple_of(j * tk, tk)
            k = k_ref[pl.ds(start, tk), :]
            v = v_ref[pl.ds(start, tk), :].astype(BF16)
            strict = (coli + j * tk) < rowi
            ls, lk = _sb_scores(q, k, strict, scale)
            r = r_acc[...]
            later = _dot_split2(lk, after) + r
            att = jnp.where(strict, jnp.exp(ls + later), 0.0)
            o_acc[...] += jnp.dot(att.astype(BF16), v, preferred_element_type=F32)
            rs_ref[...] = jnp.where(lane == j, r, rs_ref[...])
            r_acc[...] = r + jnp.sum(lk, axis=1, keepdims=True)
            return carry

        lax.fori_loop(0, nkb, step, 0)
        o_ref[...] = o_acc[...].astype(o_ref.dtype)

    return pl.pallas_call(
        body, grid=(nh, nrow // tq),
        in_specs=[pl.BlockSpec((tq, hd), lambda h, i: (i, h)), pl.BlockSpec((nrow, hd), lambda h, i: (0, h)),
                  pl.BlockSpec((nrow, hd), lambda h, i: (0, 2 * nh + h))],
        out_specs=[pl.BlockSpec((tq, hd), lambda h, i: (i, h)), pl.BlockSpec((None, tq, LANES), lambda h, i: (h, i, 0))],
        out_shape=[jax.ShapeDtypeStruct((nrow, d), BF16), jax.ShapeDtypeStruct((nh, nrow, LANES), F32)],
        scratch_shapes=[pltpu.VMEM((tq, LANES), F32), pltpu.VMEM((tq, hd), F32)],
        compiler_params=_cparams(2), name=name)(qn, kn, qkv)


def _sb_attn_bwd(qn, kn, qkv, do, rsave, name):
    nrow, d, hd, nh, tq = _sb_dims(qn)
    tk = SB_TK
    scale = 1.0 / math.sqrt(hd)

    def body(q_ref, k_ref, v_ref, do_ref, rs_ref, dq_ref, dk_ref, dv_ref, pg_acc):
        i = pl.program_id(1)

        @pl.when(i == 0)
        def _():
            dk_ref[...] = jnp.zeros_like(dk_ref)
            dv_ref[...] = jnp.zeros_like(dv_ref)

        pg_acc[...] = jnp.zeros_like(pg_acc)
        dq_ref[...] = jnp.zeros_like(dq_ref)
        q = q_ref[...]
        dob = do_ref[...].astype(BF16)
        nkb = (i + 1) * (tq // tk)
        rowi = i * tq + lax.broadcasted_iota(jnp.int32, (tq, tk), 0)
        coli = lax.broadcasted_iota(jnp.int32, (tq, tk), 1)
        lane = lax.broadcasted_iota(jnp.int32, (tq, LANES), 1)
        r0 = lax.broadcasted_iota(jnp.int32, (tk, tk), 0)
        c0 = lax.broadcasted_iota(jnp.int32, (tk, tk), 1)
        after = (r0 > c0).astype(BF16)
        before = (r0 < c0).astype(BF16)

        def step(j, carry):
            start = pl.multiple_of(j * tk, tk)
            k = k_ref[pl.ds(start, tk), :]
            v = v_ref[pl.ds(start, tk), :].astype(BF16)
            strict = (coli + j * tk) < rowi
            ls, lk = _sb_scores(q, k, strict, scale)
            pg = pg_acc[...]
            rj = jnp.sum(jnp.where(lane == j, rs_ref[...], 0.0), axis=1, keepdims=True)
            later = _dot_split2(lk, after) + rj
            att = jnp.where(strict, jnp.exp(ls + later), 0.0)
            datt = lax.dot_general(dob, v, (((1,), (1,)), ((), ())), preferred_element_type=F32)
            g = datt * att
            dlk = _dot_split2(g, before) + pg
            sig = jnp.exp(ls)
            dz = (jnp.where(strict, g * (1.0 - sig) - dlk * sig, 0.0) * scale).astype(BF16)
            dq_ref[...] += jnp.dot(dz, k, preferred_element_type=F32)
            dk_ref[pl.ds(start, tk), :] += lax.dot_general(dz, q, (((0,), (0,)), ((), ())),
                                                           preferred_element_type=F32)
            dv_ref[pl.ds(start, tk), :] += lax.dot_general(att.astype(BF16), dob, (((0,), (0,)), ((), ())),
                                                           preferred_element_type=F32)
            pg_acc[...] = pg + jnp.sum(g, axis=1, keepdims=True)
            return carry

        lax.fori_loop(0, nkb, step, 0)

    return pl.pallas_call(
        body, grid=(nh, nrow // tq),
        in_specs=[pl.BlockSpec((tq, hd), lambda h, i: (i, h)), pl.BlockSpec((nrow, hd), lambda h, i: (0, h)),
                  pl.BlockSpec((nrow, hd), lambda h, i: (0, 2 * nh + h)), pl.BlockSpec((tq, hd), lambda h, i: (i, h)),
                  pl.BlockSpec((None, tq, LANES), lambda h, i: (h, i, 0))],
        out_specs=[pl.BlockSpec((tq, hd), lambda h, i: (i, h)), pl.BlockSpec((nrow, hd), lambda h, i: (0, h)),
                   pl.BlockSpec((nrow, hd), lambda h, i: (0, h))],
        out_shape=[jax.ShapeDtypeStruct((nrow, d), F32)] * 3, scratch_shapes=[pltpu.VMEM((tq, LANES), F32)],
        compiler_params=_cparams(2), name=name)(qn, kn, qkv, do, rsave)


def _ssd_chunk_fn(g, nheads_g, q):
    p = M2_HEAD_DIM
    npair = nheads_g // 2

    def f(xp, bp, cp, dtr, hp, dtb, alog, dsk):
        b = _silu(bp).astype(BF16)
        c = _silu(cp).astype(BF16)
        dt = _softplus(dtr + dtb)
        dta = dt * (-jnp.exp(alog))
        ri = lax.broadcasted_iota(jnp.int32, (q, q), 0)
        ci = lax.broadcasted_iota(jnp.int32, (q, q), 1)
        causal = ri >= ci
        tri = causal.astype(F32)
        acol = jnp.dot(tri, dta, precision=HIGHEST, preferred_element_type=F32)
        arow = lax.dot_general(dta, tri, (((0,), (1,)), ((), ())), precision=HIGHEST,
                               preferred_element_type=F32)
        alast = jnp.sum(dta, axis=0, keepdims=True)
        cb = lax.dot_general(c, b, (((1,), (1,)), ((), ())), preferred_element_type=F32)
        lane = lax.broadcasted_iota(jnp.int32, (1, LANES), 1)
        sub = lax.broadcasted_iota(jnp.int32, (LANES, 1), 0)
        lane_half = lane < p
        sub_half = sub < p

        def col(mat, h):
            return jnp.sum(jnp.where(lane == h, mat, 0.0), axis=1, keepdims=True)

        def row(mat, h):
            return jnp.sum(jnp.where(sub == h, mat, 0.0), axis=0, keepdims=True)

        def mix(v0, v1):
            return jnp.where(lane_half, v0, v1)

        ys, hns = [], []
        for jp in range(npair):
            h0 = g * nheads_g + 2 * jp
            h1 = h0 + 1
            x = _silu(xp[jp])
            ac0, ac1 = col(acol, h0), col(acol, h1)
            xdt = x * mix(col(dt, h0), col(dt, h1))
            xdtb = xdt.astype(BF16)
            yd = []
            for h, ac in ((h0, ac0), (h1, ac1)):
                dec = jnp.exp(jnp.where(causal, ac - row(arow, h), -jnp.inf))
                yd.append(jnp.dot((cb * dec).astype(BF16), xdtb, preferred_element_type=F32))
            hpj = hp[jp]
            yo = lax.dot_general(c, hpj.astype(BF16), (((1,), (1,)), ((), ())), preferred_element_type=F32)
            y = mix(yd[0], yd[1]) + yo * jnp.exp(mix(ac0, ac1)) + mix(col(dsk, h0), col(dsk, h1)) * x
            al0, al1 = col(alast, h0), col(alast, h1)
            dout = jnp.exp(mix(al0 - ac0, al1 - ac1))
            st = lax.dot_general((xdt * dout).astype(BF16), b, (((0,), (0,)), ((), ())), preferred_element_type=F32)
            hns.append(hpj * jnp.exp(jnp.where(sub_half, al0, al1)) + st)
            ys.append(y)
        return ys, hns

    return f


def _ssd_dims(pre, dtr):
    nrow = pre.shape[0]
    ng, n, p, q = M2_GROUPS, M2_STATE, M2_HEAD_DIM, M2_CHUNK
    d_inner = pre.shape[1] - 2 * ng * n
    gw = d_inner // ng
    nhg = gw // p
    assert nhg % 2 == 0 and gw % LANES == 0 and n == LANES and ng * nhg <= LANES and dtr.shape[1] == LANES
    return nrow, ng, n, q, d_inner, gw, nhg, nrow // q


def _ssd_fwd(pre, dtr, dtb, alog, dsk, name):
    nrow, ng, n, q, d_inner, gw, nhg, nc = _ssd_dims(pre, dtr)
    boff = d_inner // n
    npair = nhg // 2

    def body(x_ref, b_ref, c_ref, dt_ref, dtb_ref, al_ref, ds_ref, y_ref, hs_ref, state):
        ci, g = pl.program_id(0), pl.program_id(1)

        @pl.when(ci == 0)
        def _():
            state[g] = jnp.zeros((gw, n), F32)

        hs_ref[...] = state[g]
        f = _ssd_chunk_fn(g, nhg, q)
        xp = [x_ref[:, LANES * j:LANES * (j + 1)] for j in range(npair)]
        hp = [state[g, LANES * j:LANES * (j + 1), :] for j in range(npair)]
        ys, hns = f(xp, b_ref[...], c_ref[...], dt_ref[...], hp, dtb_ref[...], al_ref[...], ds_ref[...])
        for j in range(npair):
            y_ref[:, LANES * j:LANES * (j + 1)] = ys[j]
            state[g, LANES * j:LANES * (j + 1), :] = hns[j]

    par = pl.BlockSpec((1, LANES), lambda ci, g: (0, 0))
    return pl.pallas_call(
        body, grid=(nc, ng),
        in_specs=[pl.BlockSpec((q, gw), lambda ci, g: (ci, g)), pl.BlockSpec((q, n), lambda ci, g: (ci, boff + g)),
                  pl.BlockSpec((q, n), lambda ci, g: (ci, boff + ng + g)), pl.BlockSpec((q, LANES), lambda ci, g: (ci, 0)),
                  par, par, par],
        out_specs=[pl.BlockSpec((q, gw), lambda ci, g: (ci, g)),
                   pl.BlockSpec((None, None, gw, n), lambda ci, g: (ci, g, 0, 0))],
        out_shape=[jax.ShapeDtypeStruct((nrow, d_inner), F32), jax.ShapeDtypeStruct((nc, ng, gw, n), F32)],
        scratch_shapes=[pltpu.VMEM((ng, gw, n), F32)], compiler_params=_cparams(2), name=name)(
            pre, pre, pre, dtr, dtb, alog, dsk)


def _ssd_bwd(pre, dtr, hs, dy, dtb, alog, dsk, name):
    nrow, ng, n, q, d_inner, gw, nhg, nc = _ssd_dims(pre, dtr)
    boff = d_inner // n
    npair = nhg // 2

    def body(x_ref, b_ref, c_ref, dt_ref, hs_ref, dy_ref, dtb_ref, al_ref, ds_ref,
             dx_ref, db_ref, dc_ref, ddt_ref, ddtb_ref, dal_ref, dds_ref, dstate):
        ci, g = pl.program_id(0), pl.program_id(1)

        @pl.when(ci == 0)
        def _():
            dstate[g] = jnp.zeros((gw, n), F32)

        @pl.when((ci == 0) & (g == 0))
        def _():
            ddtb_ref[...] = jnp.zeros_like(ddtb_ref)
            dal_ref[...] = jnp.zeros_like(dal_ref)
            dds_ref[...] = jnp.zeros_like(dds_ref)

        @pl.when(g == 0)
        def _():
            ddt_ref[...] = jnp.zeros_like(ddt_ref)

        f = _ssd_chunk_fn(g, nhg, q)
        xp = [x_ref[:, LANES * j:LANES * (j + 1)] for j in range(npair)]
        hp = [hs_ref[LANES * j:LANES * (j + 1), :] for j in range(npair)]
        _, vjp = jax.vjp(f, xp, b_ref[...], c_ref[...], dt_ref[...], hp, dtb_ref[...], al_ref[...], ds_ref[...])
        dys = [dy_ref[:, LANES * j:LANES * (j + 1)] for j in range(npair)]
        dhn = [dstate[g, LANES * j:LANES * (j + 1), :] for j in range(npair)]
        dxp, db, dc, ddt, dhp, ddtb, dal, dds = vjp((dys, dhn))
        for j in range(npair):
            dx_ref[:, LANES * j:LANES * (j + 1)] = dxp[j]
            dstate[g, LANES * j:LANES * (j + 1), :] = dhp[j]
        db_ref[...] = db
        dc_ref[...] = dc
        ddt_ref[...] += ddt
        ddtb_ref[...] += ddtb
        dal_ref[...] += dal
        dds_ref[...] += dds

    par = pl.BlockSpec((1, LANES), lambda ci, g: (0, 0))
    last = nc - 1
    return pl.pallas_call(
        body, grid=(nc, ng),
        in_specs=[pl.BlockSpec((q, gw), lambda ci, g: (last - ci, g)),
                  pl.BlockSpec((q, n), lambda ci, g: (last - ci, boff + g)),
                  pl.BlockSpec((q, n), lambda ci, g: (last - ci, boff + ng + g)),
                  pl.BlockSpec((q, LANES), lambda ci, g: (last - ci, 0)),
                  pl.BlockSpec((None, None, gw, n), lambda ci, g: (last - ci, g, 0, 0)),
                  pl.BlockSpec((q, gw), lambda ci, g: (last - ci, g)), par, par, par],
        out_specs=[pl.BlockSpec((q, gw), lambda ci, g: (last - ci, g)), pl.BlockSpec((q, n), lambda ci, g: (last - ci, g)),
                   pl.BlockSpec((q, n), lambda ci, g: (last - ci, g)), pl.BlockSpec((q, LANES), lambda ci, g: (last - ci, 0)),
                   par, par, par],
        out_shape=[jax.ShapeDtypeStruct((nrow, d_inner), F32), jax.ShapeDtypeStruct((nrow, ng * n), F32),
                   jax.ShapeDtypeStruct((nrow, ng * n), F32), jax.ShapeDtypeStruct((nrow, LANES), F32),
                   jax.ShapeDtypeStruct((1, LANES), F32), jax.ShapeDtypeStruct((1, LANES), F32),
                   jax.ShapeDtypeStruct((1, LANES), F32)],
        scratch_shapes=[pltpu.VMEM((ng, gw, n), F32)], compiler_params=_cparams(2), name=name)(
            pre, pre, pre, dtr, hs, dy, dtb, alog, dsk)


HBM_SPEC = pl.BlockSpec(memory_space=pl.ANY)


def _xy_peers(mx, my):
    return [(1 - mx, my), (mx, 1 - my), (1 - mx, 1 - my)]


def _remote(src, dst, send_sems, recv_sems, k, dev):
    return pltpu.make_async_remote_copy(src_ref=src, dst_ref=dst, send_sem=send_sems.at[k], recv_sem=recv_sems.at[k],
                                        device_id=dev, device_id_type=MESH_ID)


def _all_gather_xy(buf, name):
    def body(src, out, send_sems, recv_sems, local_sem):
        mx, my, mc = lax.axis_index("x"), lax.axis_index("y"), lax.axis_index("c")
        me = 2 * mx + my
        peers = _xy_peers(mx, my)
        local = pltpu.make_async_copy(src, out.at[me], local_sem)
        local.start()
        sends = [_remote(src, out.at[me], send_sems, recv_sems, k, (px, py, mc)) for k, (px, py) in enumerate(peers)]
        for cp in sends:
            cp.start()
        for k, (px, py) in enumerate(peers):
            _remote(src, out.at[2 * px + py], send_sems, recv_sems, k, (px, py, mc)).wait_recv()
        for cp in sends:
            cp.wait_send()
        local.wait()

    return pl.pallas_call(
        body, out_shape=jax.ShapeDtypeStruct((N_XY,) + buf.shape, buf.dtype), in_specs=[HBM_SPEC], out_specs=HBM_SPEC,
        scratch_shapes=[pltpu.SemaphoreType.DMA((N_XY - 1,)), pltpu.SemaphoreType.DMA((N_XY - 1,)),
                        pltpu.SemaphoreType.DMA], name=name)(buf)


def _exchange_xy(parts, name):
    def body(src, out, send_sems, recv_sems, local_sem):
        mx, my, mc = lax.axis_index("x"), lax.axis_index("y"), lax.axis_index("c")
        me = 2 * mx + my
        peers = _xy_peers(mx, my)
        local = pltpu.make_async_copy(src.at[me], out.at[me], local_sem)
        local.start()
        sends = [_remote(src.at[2 * px + py], out.at[me], send_sems, recv_sems, k, (px, py, mc))
                 for k, (px, py) in enumerate(peers)]
        for cp in sends:
            cp.start()
        for k, (px, py) in enumerate(peers):
            _remote(src.at[me], out.at[2 * px + py], send_sems, recv_sems, k, (px, py, mc)).wait_recv()
        for cp in sends:
            cp.wait_send()
        local.wait()

    return pl.pallas_call(
        body, out_shape=jax.ShapeDtypeStruct(parts.shape, parts.dtype), in_specs=[HBM_SPEC], out_specs=HBM_SPEC,
        scratch_shapes=[pltpu.SemaphoreType.DMA((N_XY - 1,)), pltpu.SemaphoreType.DMA((N_XY - 1,)),
                        pltpu.SemaphoreType.DMA], name=name)(parts)


def _swap_c(buf, name):
    def body(src, out, send_sems, recv_sems):
        mx, my, mc = lax.axis_index("x"), lax.axis_index("y"), lax.axis_index("c")
        cp = _remote(src, out, send_sems, recv_sems, 0, (mx, my, 1 - mc))
        cp.start()
        cp.wait()

    return pl.pallas_call(
        body, out_shape=jax.ShapeDtypeStruct(buf.shape, buf.dtype), in_specs=[HBM_SPEC], out_specs=HBM_SPEC,
        scratch_shapes=[pltpu.SemaphoreType.DMA((1,)), pltpu.SemaphoreType.DMA((1,))], name=name)(buf)


def _all_gather_all(buf, name):
    flips = [(fx, fy, fc) for fx in (0, 1) for fy in (0, 1) for fc in (0, 1) if fx or fy or fc]

    def body(src, out, send_sems, recv_sems, local_sem):
        mx, my, mc = lax.axis_index("x"), lax.axis_index("y"), lax.axis_index("c")
        me = 4 * mx + 2 * my + mc
        peers = [(1 - mx if fx else mx, 1 - my if fy else my, 1 - mc if fc else mc) for fx, fy, fc in flips]
        local = pltpu.make_async_copy(src, out.at[me], local_sem)
        local.start()
        sends = [_remote(src, out.at[me], send_sems, recv_sems, k, dev) for k, dev in enumerate(peers)]
        for cp in sends:
            cp.start()
        for k, (px, py, pc) in enumerate(peers):
            _remote(src, out.at[4 * px + 2 * py + pc], send_sems, recv_sems, k, (px, py, pc)).wait_recv()
        for cp in sends:
            cp.wait_send()
        local.wait()

    return pl.pallas_call(
        body, out_shape=jax.ShapeDtypeStruct((N_DEV,) + buf.shape, buf.dtype), in_specs=[HBM_SPEC], out_specs=HBM_SPEC,
        scratch_shapes=[pltpu.SemaphoreType.DMA((N_DEV - 1,)), pltpu.SemaphoreType.DMA((N_DEV - 1,)),
                        pltpu.SemaphoreType.DMA], name=name)(buf)


def _flat_tile(nrow, width, narrays):
    t = nrow
    for cand in (512, 256, 128, 64, 32, 16, 8):
        if nrow % cand == 0:
            t = cand
            if cand * width * 4 * narrays <= ROW_BLOCK_BYTES:
                break
    return t


def _sum_parts(parts, name):
    kparts, nrow, width = parts.shape
    t = _flat_tile(nrow, width, kparts + 1)

    def body(p_ref, o_ref):
        s = p_ref[0].astype(F32)
        for k in range(1, kparts):
            s = s + p_ref[k].astype(F32)
        o_ref[...] = s

    return pl.pallas_call(
        body, grid=(nrow // t,), in_specs=[pl.BlockSpec((kparts, t, width), lambda i: (0, i, 0))],
        out_specs=pl.BlockSpec((t, width), lambda i: (i, 0)), out_shape=jax.ShapeDtypeStruct((nrow, width), F32),
        compiler_params=_cparams(1), name=name)(parts)


def _adamw(w, m, v, ga, gb, name):
    nrow, width = w.shape
    t = _flat_tile(nrow, width, 9)
    c1 = 1.0 - ADAM_B1 ** ADAM_STEP
    c2 = 1.0 - ADAM_B2 ** ADAM_STEP

    def body(w_ref, m_ref, v_ref, ga_ref, gb_ref, g_ref, d_ref, mo_ref, vo_ref):
        g = ga_ref[...] + gb_ref[...]
        mn = ADAM_B1 * m_ref[...] + (1.0 - ADAM_B1) * g
        vn = ADAM_B2 * v_ref[...] + (1.0 - ADAM_B2) * jnp.square(g)
        g_ref[...] = g
        mo_ref[...] = mn
        vo_ref[...] = vn
        d_ref[...] = -ADAM_LR * ((mn / c1) / (jnp.sqrt(vn / c2) + ADAM_EPS) + ADAM_WD * w_ref[...])

    spec = pl.BlockSpec((t, width), lambda i: (i, 0))
    return pl.pallas_call(
        body, grid=(nrow // t,), in_specs=[spec] * 5, out_specs=[spec] * 4,
        out_shape=[jax.ShapeDtypeStruct((nrow, width), F32)] * 4, compiler_params=_cparams(1), name=name)(
            w, m, v, ga, gb)


def _pack(arrs, dtype, width):
    flat = jnp.concatenate([a.astype(dtype).reshape(-1) for a in arrs])
    quantum = PACK_ROWS * width
    pad = (-flat.shape[0]) % quantum
    if pad:
        flat = jnp.concatenate([flat, jnp.zeros((pad,), dtype)])
    return flat.reshape(-1, width)


def _unpack(buf, shapes):
    flat = buf.reshape(-1)
    out, off = [], 0
    for s in shapes:
        size = math.prod(s)
        out.append(flat[off:off + size].reshape(s))
        off += size
    return out


def _qk_norm_fwd(qkv, w, off, nh, tag):
    f = _rms_f(RMS_EPS)
    return _rowwise(lambda xv, wv: (f(xv, wv),), [(qkv, SB_HEAD_DIM, off)], [(w, False)],
                    [('row', nh * SB_HEAD_DIM, SB_HEAD_DIM, BF16)], ncb=nh, name=tag)[0]


def _qk_norm_bwd(qkv, w, off, dn, nh, tag):
    f = _rms_f(RMS_EPS)

    def fn(xv, dv, wv):
        _, vjp = jax.vjp(f, xv, wv)
        return vjp(dv)

    return _rowwise(fn, [(qkv, SB_HEAD_DIM, off), (dn, SB_HEAD_DIM, 0)], [(w, False)],
                    [('row', nh * SB_HEAD_DIM, SB_HEAD_DIM, F32), ('acc', (1, SB_HEAD_DIM), False)], ncb=nh, name=tag)


def _sb_forward(h, xin, wt, tag):
    d = xin.shape[1]
    nh = d // SB_HEAD_DIM
    qkv = _mm(h, wt['w_qkv'], "nn", name=tag + "_qkv")
    qn = _qk_norm_fwd(qkv, wt['q_norm_w'], 0, nh, tag + "_qnorm")
    kn = _qk_norm_fwd(qkv, wt['k_norm_w'], nh, nh, tag + "_knorm")
    o, rsave = _sb_attn_fwd(qn, kn, qkv, tag + "_attn")
    xm = _mm(o, wt['w_o'], "nn", extras=(xin,), epilogue=_ep_add, name=tag + "_out")
    return xm, dict(qkv=qkv, qn=qn, kn=kn, o=o, rsave=rsave)


def _sb_backward(h, dx, wt, sv, tag):
    nh = dx.shape[1] // SB_HEAD_DIM
    do = _mm(dx, wt['w_o'], "nt", name=tag + "_do")
    g_wo = _mm(sv['o'], dx, "tn", name=tag + "_dwo")
    dqn, dkn, dv = _sb_attn_bwd(sv['qn'], sv['kn'], sv['qkv'], do, sv['rsave'], tag + "_attn_bwd")
    dq, g_qw = _qk_norm_bwd(sv['qkv'], wt['q_norm_w'], 0, dqn, nh, tag + "_qnorm_bwd")
    dk, g_kw = _qk_norm_bwd(sv['qkv'], wt['k_norm_w'], nh, dkn, nh, tag + "_knorm_bwd")
    dqkv = jnp.concatenate([dq, dk, dv], axis=1)
    g_wqkv = _mm(h, dqkv, "tn", name=tag + "_dwqkv")
    dh = _mm(dqkv, wt['w_qkv'], "nt", name=tag + "_dh")
    return dh, dict(w_qkv=g_wqkv, w_o=g_wo, q_norm_w=g_qw, k_norm_w=g_kw)


def _ln_silu_f(x, w, b):
    mu = jnp.mean(x, axis=-1, keepdims=True)
    xc = x - mu
    return _silu(xc * lax.rsqrt(jnp.mean(xc * xc, axis=-1, keepdims=True) + LN_EPS) * w + b)


def _glu_bwd_fn(val, gate, dg):
    sg = 1.0 / (1.0 + jnp.exp(-gate))
    return (jnp.concatenate([dg * sg, dg * val * sg * (1.0 - sg)], axis=1),)


def _cf_forward(h, xin, wt, tag):
    d = xin.shape[1]
    u = _mm(h, wt['w_in'], "nn", extras=(wt['b_in'],), epilogue=_ep_add, name=tag + "_in")
    gl = _rowwise(lambda val, gate: (val / (1.0 + jnp.exp(-gate)),), [(u, d, 0), (u, d, 1)], [],
                  [('row', d, d, F32)], name=tag + "_glu")[0]
    cv = _dwconv_fwd(gl, wt['dw_w'], wt['dw_b'], tag + "_conv")
    s = _rowwise(lambda x, w, b: (_ln_silu_f(x, w, b),), [(cv, d, 0)], [(wt['ln_w'], False), (wt['ln_b'], False)],
                 [('row', d, d, BF16)], name=tag + "_ln")[0]
    xm = _mm(s, wt['w_out'], "nn", extras=(wt['b_out'], xin), epilogue=_ep_bias_add, name=tag + "_out")
    return xm, dict(u=u, gl=gl, cv=cv, s=s)


def _cf_backward(h, dx, wt, sv, tag):
    d = dx.shape[1]
    ds = _mm(dx, wt['w_out'], "nt", name=tag + "_ds")
    g_wout = _mm(sv['s'], dx, "tn", name=tag + "_dwout")
    g_bout = _colsum(dx, tag + "_dbout")

    def ln_bwd(x, dsv, w, b):
        _, vjp = jax.vjp(_ln_silu_f, x, w, b)
        return vjp(dsv)

    dcv, g_lnw, g_lnb = _rowwise(ln_bwd, [(sv['cv'], d, 0), (ds, d, 0)], [(wt['ln_w'], False), (wt['ln_b'], False)],
                                 [('row', d, d, F32), ('acc', (1, d), False), ('acc', (1, d), False)],
                                 name=tag + "_ln_bwd")
    dgl, g_dww, g_dwb = _dwconv_bwd(sv['gl'], dcv, wt['dw_w'], tag + "_conv_bwd")
    du = _rowwise(_glu_bwd_fn, [(sv['u'], d, 0), (sv['u'], d, 1), (dgl, d, 0)], [], [('row', 2 * d, 2 * d, F32)],
                  name=tag + "_glu_bwd")[0]
    g_bin = _colsum(du, tag + "_dbin")
    g_win = _mm(h, du, "tn", name=tag + "_dwin")
    dh = _mm(du, wt['w_in'], "nt", name=tag + "_dh")
    return dh, dict(w_in=g_win, b_in=g_bin, dw_w=g_dww, dw_b=g_dwb, ln_w=g_lnw, ln_b=g_lnb, w_out=g_wout, b_out=g_bout)


def _gated_norm_f(y, z, w):
    t = y * _silu(z)
    return t * lax.rsqrt(jnp.mean(t * t, axis=-1, keepdims=True) + M2_NORM_EPS) * w


def _m2_forward(h, xin, wt, tag):
    z = _mm(h, wt['w_z'], "nn", name=tag + "_z")
    xbc = _mm(h, wt['w_xbc'], "nn", name=tag + "_xbc")
    dtr = _mm(h, wt['w_dt'], "nn", name=tag + "_dt")
    pre = _dwconv_fwd(xbc, wt['conv_w'], wt['conv_b'], tag + "_conv")
    y, hs = _ssd_fwd(pre, dtr, wt['dt_bias'], wt['a_log'], wt['d'], tag + "_ssd")
    d_inner = y.shape[1]
    gw = d_inner // M2_GROUPS
    yn = _rowwise(lambda yv, zv, w: (_gated_norm_f(yv, zv, w),), [(y, gw, 0), (z, gw, 0)], [(wt['norm_w'], True)],
                  [('row', d_inner, gw, BF16)], ncb=M2_GROUPS, name=tag + "_gnorm")[0]
    xm = _mm(yn, wt['w_out'], "nn", extras=(xin,), epilogue=_ep_add, name=tag + "_out")
    return xm, dict(z=z, xbc=xbc, dtr=dtr, pre=pre, y=y, hs=hs, yn=yn)


def _m2_backward(h, dx, wt, sv, tag):
    dyn = _mm(dx, wt['w_out'], "nt", name=tag + "_dyn")
    g_wout = _mm(sv['yn'], dx, "tn", name=tag + "_dwout")
    d_inner = sv['y'].shape[1]
    gw = d_inner // M2_GROUPS

    def gn_bwd(yv, zv, dv, w):
        _, vjp = jax.vjp(_gated_norm_f, yv, zv, w)
        return vjp(dv)

    dy, dz, g_nw = _rowwise(gn_bwd, [(sv['y'], gw, 0), (sv['z'], gw, 0), (dyn, gw, 0)], [(wt['norm_w'], True)],
                            [('row', d_inner, gw, F32), ('row', d_inner, gw, F32), ('acc', (1, d_inner), True)],
                            ncb=M2_GROUPS, name=tag + "_gnorm_bwd")
    dxp, db, dc, ddt, g_dtb, g_alog, g_d = _ssd_bwd(sv['pre'], sv['dtr'], sv['hs'], dy, wt['dt_bias'], wt['a_log'],
                                                   wt['d'], tag + "_ssd_bwd")
    dpre = jnp.concatenate([dxp, db, dc], axis=1)
    dxbc, g_cw, g_cb = _dwconv_bwd(sv['xbc'], dpre, wt['conv_w'], tag + "_conv_bwd")
    g_wz = _mm(h, dz, "tn", name=tag + "_dwz")
    g_wxbc = _mm(h, dxbc, "tn", name=tag + "_dwxbc")
    g_wdt = _mm(h, ddt, "tn", name=tag + "_dwdt")
    dh = _mm(dz, wt['w_z'], "nt", name=tag + "_dh_z")
    dh = _mm(dxbc, wt['w_xbc'], "nt", extras=(dh,), epilogue=_ep_add, name=tag + "_dh_xbc")
    dh = _mm(ddt, wt['w_dt'], "nt", extras=(dh,), epilogue=_ep_add, name=tag + "_dh_dt")
    return dh, dict(w_z=g_wz, w_xbc=g_wxbc, w_dt=g_wdt, conv_w=g_cw, conv_b=g_cb, dt_bias=g_dtb, a_log=g_alog, d=g_d,
                    norm_w=g_nw, w_out=g_wout)


def _pad_lanes(v):
    return jnp.pad(v.reshape(1, -1), ((0, 0), (0, LANES - v.shape[0])))


def kernel(x, norm_mix_w, norm_mlp_w, sb_w_qkv, sb_q_norm_w, sb_k_norm_w, sb_w_o, cf_w_in, cf_b_in, cf_dw_w, cf_dw_b, cf_ln_w, cf_ln_b, cf_w_out, cf_b_out, m2_w_in, m2_conv_w, m2_conv_b, m2_dt_bias, m2_a_log, m2_d, m2_norm_w, m2_w_out, mlp_w_up, mlp_w_down, loss_target, m_norm_mix_w, m_norm_mlp_w, m_sb_w_qkv, m_sb_q_norm_w, m_sb_k_norm_w, m_sb_w_o, m_cf_w_in, m_cf_b_in, m_cf_dw_w, m_cf_dw_b, m_cf_ln_w, m_cf_ln_b, m_cf_w_out, m_cf_b_out, m_m2_w_in, m_m2_conv_w, m_m2_conv_b, m_m2_dt_bias, m_m2_a_log, m_m2_d, m_m2_norm_w, m_m2_w_out, m_mlp_w_up, m_mlp_w_down, v_norm_mix_w, v_norm_mlp_w, v_sb_w_qkv, v_sb_q_norm_w, v_sb_k_norm_w, v_sb_w_o, v_cf_w_in, v_cf_b_in, v_cf_dw_w, v_cf_dw_b, v_cf_ln_w, v_cf_ln_b, v_cf_w_out, v_cf_b_out, v_m2_w_in, v_m2_conv_w, v_m2_conv_b, v_m2_dt_bias, v_m2_a_log, v_m2_d, v_m2_norm_w, v_m2_w_out, v_mlp_w_up, v_mlp_w_down):
    given = dict(locals())
    xl = x[0]
    tgt = loss_target[0]
    d = xl.shape[1]
    depth = norm_mix_w.shape[0]
    bulk_names, small_names = list(BULK), list(SMALL)

    gath_b = _all_gather_xy(_pack([given[n] for n in bulk_names], BF16, PACK_W), "comm_gather_bulk")
    gath_s = _all_gather_xy(_pack([given[n] for n in small_names], F32, PACK_W), "comm_gather_small")
    full = {}
    for names, gath, axes in ((bulk_names, gath_b, BULK), (small_names, gath_s, SMALL)):
        pieces = [_unpack(gath[s], [given[n].shape for n in names]) for s in range(N_XY)]
        for idx, n in enumerate(names):
            full[n] = jnp.concatenate([pieces[s][idx] for s in range(N_XY)], axis=axes[n])

    def row(v):
        return v.reshape(1, -1)

    d_inner = full['m2_w_out'].shape[1]
    conv_dim = full['m2_conv_w'].shape[2]
    nheads = m2_dt_bias.shape[1]

    def layer_weights(i):
        kind, j = i % 3, i // 3
        if kind == 0:
            return dict(w_qkv=full['sb_w_qkv'][j], w_o=full['sb_w_o'][j], q_norm_w=row(sb_q_norm_w[j]),
                        k_norm_w=row(sb_k_norm_w[j]))
        if kind == 1:
            return dict(w_in=full['cf_w_in'][j], b_in=row(cf_b_in[j]), dw_w=full['cf_dw_w'][j], dw_b=row(cf_dw_b[j]),
                        ln_w=row(cf_ln_w[j]), ln_b=row(cf_ln_b[j]), w_out=full['cf_w_out'][j], b_out=row(cf_b_out[j]))
        w_in = full['m2_w_in'][j]
        w_dt = jnp.pad(w_in[:, d_inner + conv_dim:], ((0, 0), (0, LANES - nheads)))
        return dict(w_z=w_in[:, :d_inner], w_xbc=w_in[:, d_inner:d_inner + conv_dim], w_dt=w_dt,
                    conv_w=full['m2_conv_w'][j], conv_b=row(full['m2_conv_b'][j]), dt_bias=_pad_lanes(m2_dt_bias[j]),
                    a_log=_pad_lanes(m2_a_log[j]), d=_pad_lanes(m2_d[j]), norm_w=row(full['m2_norm_w'][j]),
                    w_out=full['m2_w_out'][j])

    fwd = (_sb_forward, _cf_forward, _m2_forward)
    bwd = (_sb_backward, _cf_backward, _m2_backward)

    saved = []
    xc = xl
    for i in range(depth):
        wt = layer_weights(i)
        h = _rms_fwd(xc, row(norm_mix_w[i]), f"l{i}_norm_mix")
        xm, sv = fwd[i % 3](h, xc, wt, f"l{i}_mix")
        h2 = _rms_fwd(xm, row(norm_mlp_w[i]), f"l{i}_norm_mlp")
        u, a = _mm(h2, full['mlp_w_up'][i], "nn", epilogue=_ep_relu2, out_dtypes=(F32, BF16), name=f"l{i}_up")
        xn = _mm(a, full['mlp_w_down'][i], "nn", extras=(xm,), epilogue=_ep_add, name=f"l{i}_down")
        saved.append(dict(wt=wt, x_in=xc, h=h, mix=sv, x_mid=xm, h2=h2, u=u, a=a))
        xc = xn
    dx, loss_acc = _loss(xc, tgt, "loss")
    loss = lax.psum(loss_acc[0, 0], ("x", "y", "c"))

    grads = {n: [None] * given[n].shape[0] for n in W_NAMES}
    for i in reversed(range(depth)):
        sv = saved[i]
        kind, j = i % 3, i // 3
        du = _mm(dx, full['mlp_w_down'][i], "nt", extras=(sv['u'],), epilogue=_ep_relu2_bwd, out_dtypes=(BF16,),
                 name=f"l{i}_du")
        grads['mlp_w_down'][i] = _mm(sv['a'], dx, "tn", name=f"l{i}_dwdown")
        grads['mlp_w_up'][i] = _mm(sv['h2'], du, "tn", name=f"l{i}_dwup")
        dh2 = _mm(du, full['mlp_w_up'][i], "nt", name=f"l{i}_dh2")
        dxm, g_n2 = _rms_bwd(sv['x_mid'], row(norm_mlp_w[i]), dh2, dx, f"l{i}_norm_mlp_bwd")
        grads['norm_mlp_w'][i] = g_n2[0]
        dh, gw = bwd[kind](sv['h'], dxm, sv['wt'], sv['mix'], f"l{i}_mix")
        dx, g_n1 = _rms_bwd(sv['x_in'], row(norm_mix_w[i]), dh, dxm, f"l{i}_norm_mix_bwd")
        grads['norm_mix_w'][i] = g_n1[0]
        if kind == 0:
            grads['sb_w_qkv'][j], grads['sb_w_o'][j] = gw['w_qkv'], gw['w_o']
            grads['sb_q_norm_w'][j], grads['sb_k_norm_w'][j] = gw['q_norm_w'][0], gw['k_norm_w'][0]
        elif kind == 1:
            for n in ('w_in', 'dw_w', 'w_out'):
                grads['cf_' + n][j] = gw[n]
            for n in ('b_in', 'dw_b', 'ln_w', 'ln_b', 'b_out'):
                grads['cf_' + n][j] = gw[n][0]
        else:
            grads['m2_w_in'][j] = jnp.concatenate([gw['w_z'], gw['w_xbc'], gw['w_dt'][:, :nheads]], axis=1)
            grads['m2_conv_w'][j], grads['m2_w_out'][j] = gw['conv_w'], gw['w_out']
            grads['m2_conv_b'][j], grads['m2_norm_w'][j] = gw['conv_b'][0], gw['norm_w'][0]
            for n in ('dt_bias', 'a_log', 'd'):
                grads['m2_' + n][j] = gw[n][0, :nheads]
    grad_x = dx[None]
    gfull = {n: jnp.stack(grads[n]) for n in W_NAMES}

    outs = {}
    for names, axes, dtype, tag in ((bulk_names, BULK, BF16, "bulk"), (small_names, SMALL, F32, "small")):
        split = {n: jnp.split(gfull[n], N_XY, axis=axes[n]) for n in names}
        parts = jnp.stack([_pack([split[n][s] for n in names], dtype, PACK_W) for s in range(N_XY)])
        mine = _sum_parts(_exchange_xy(parts, f"comm_exchange_{tag}"), f"sum_{tag}")
        other = _swap_c(mine, f"comm_swap_{tag}")
        res = _adamw(*[_pack([given[p + n] for n in names], F32, PACK_W) for p in ("", "m_", "v_")], mine, other,
                     f"adamw_{tag}")
        shapes = [given[n].shape for n in names]
        for kind, buf in zip(("grad", "delta", "new_m", "new_v"), res):
            for n, arr in zip(names, _unpack(buf, shapes)):
                outs[kind, n] = arr
    rparts = _all_gather_all(_pack([gfull[n] for n in REPL], F32, LANES), "comm_gather_repl")
    nsplit = N_DEV // 2
    res = _adamw(*[_pack([given[p + n] for n in REPL], F32, LANES) for p in ("", "m_", "v_")],
                 _sum_parts(rparts[:nsplit], "sum_repl_a"), _sum_parts(rparts[nsplit:], "sum_repl_b"), "adamw_repl")
    shapes = [given[n].shape for n in REPL]
    for kind, buf in zip(("grad", "delta", "new_m", "new_v"), res):
        for n, arr in zip(REPL, _unpack(buf, shapes)):
            outs[kind, n] = arr

    return (loss, grad_x, *[outs[kind, n] for kind in ("grad", "delta", "new_m", "new_v") for n in W_NAMES])
```

```python
import math

import jax
import jax.numpy as jnp
from jax import lax
from jax.experimental import pallas as pl
from jax.experimental.pallas import tpu as pltpu

F32 = jnp.float32
BF16 = jnp.bfloat16
MESH_ID = pl.DeviceIdType.MESH
HIGHEST = lax.Precision.HIGHEST

SB_HEAD_DIM = 128
M2_HEAD_DIM = 64
M2_STATE = 128
M2_GROUPS = 8
M2_CHUNK = 128
RMS_EPS = 1e-6
LN_EPS = 1e-5
M2_NORM_EPS = 1e-5
ADAM_LR = 0.001
ADAM_B1 = 0.9
ADAM_B2 = 0.999
ADAM_EPS = 1e-08
ADAM_WD = 0.01
ADAM_STEP = 10

N_XY = 4
N_DEV = 8

V7X_VMEM_BYTES = 64 * 2**20
VMEM_LIMIT = (V7X_VMEM_BYTES * 3) // 4
LANES = 128
ROW_BLOCK_BYTES = 6 * 2**20
MM_VMEM_BUDGET = (VMEM_LIMIT * 3) // 4
PACK_W = 1024
PACK_ROWS = 256

W_NAMES = ['norm_mix_w', 'norm_mlp_w', 'sb_w_qkv', 'sb_q_norm_w', 'sb_k_norm_w', 'sb_w_o', 'cf_w_in', 'cf_b_in',
           'cf_dw_w', 'cf_dw_b', 'cf_ln_w', 'cf_ln_b', 'cf_w_out', 'cf_b_out', 'm2_w_in', 'm2_conv_w', 'm2_conv_b',
           'm2_dt_bias', 'm2_a_log', 'm2_d', 'm2_norm_w', 'm2_w_out', 'mlp_w_up', 'mlp_w_down']
BULK = {'sb_w_qkv': 2, 'sb_w_o': 1, 'cf_w_in': 2, 'cf_w_out': 1, 'm2_w_in': 2, 'm2_w_out': 1, 'mlp_w_up': 2,
        'mlp_w_down': 1}
SMALL = {'cf_dw_w': 2, 'm2_conv_w': 2, 'm2_conv_b': 1, 'm2_norm_w': 1}
REPL = ['norm_mix_w', 'norm_mlp_w', 'sb_q_norm_w', 'sb_k_norm_w', 'cf_b_in', 'cf_dw_b', 'cf_ln_w', 'cf_ln_b',
        'cf_b_out', 'm2_dt_bias', 'm2_a_log', 'm2_d']


def _tile(n, prefs):
    for p in prefs:
        if n % p == 0:
            return p
    return n


def _cparams(n):
    return pltpu.CompilerParams(dimension_semantics=("arbitrary",) * n, vmem_limit_bytes=VMEM_LIMIT)


def _silu(v):
    return v / (1.0 + jnp.exp(-v))


def _softplus(v):
    return jnp.maximum(v, 0.0) + jnp.log(1.0 + jnp.exp(-jnp.abs(v)))


def _mm(a, b, mode, *, extras=(), epilogue=None, out_dtypes=(F32,), name):
    if mode == "tn":
        kdim, m = a.shape
    else:
        m, kdim = a.shape
    n = b.shape[0] if mode == "nt" else b.shape[1]
    tm = _tile(m, (1024, 512, 256, 128))
    tn = _tile(n, (1024, 512, 256, 128))
    acc_in_out = out_dtypes[0] == F32
    fixed = sum(tm * tn * jnp.dtype(dt).itemsize for dt in out_dtypes)
    fixed += sum((1 if e.shape[0] == 1 else tm) * tn * e.dtype.itemsize for e in extras)

    def vmem_bytes(t):
        ab = tm * t * a.dtype.itemsize + t * tn * b.dtype.itemsize
        casts = (tm * t * 2 if a.dtype != BF16 else 0) + (t * tn * 2 if b.dtype != BF16 else 0)
        return 2 * (ab + fixed) + (0 if acc_in_out else tm * tn * 4) + tm * tn * 4 + casts

    tk = kdim
    for cand in (2048, 1024, 512, 256, 128):
        if kdim % cand == 0:
            tk = cand
            if vmem_bytes(cand) <= MM_VMEM_BUDGET:
                break
    nk = kdim // tk
    dims = {"nn": ((1,), (0,)), "nt": ((1,), (1,)), "tn": ((0,), (0,))}[mode]
    if mode == "tn":
        a_spec = pl.BlockSpec((tk, tm), lambda i, j, k: (k, i))
    else:
        a_spec = pl.BlockSpec((tm, tk), lambda i, j, k: (i, k))
    if mode == "nt":
        b_spec = pl.BlockSpec((tn, tk), lambda i, j, k: (j, k))
    else:
        b_spec = pl.BlockSpec((tk, tn), lambda i, j, k: (k, j))
    ex_specs = [pl.BlockSpec((1, tn), lambda i, j, k: (0, j)) if e.shape[0] == 1
                else pl.BlockSpec((tm, tn), lambda i, j, k: (i, j)) for e in extras]
    n_ex, n_out = len(extras), len(out_dtypes)

    def body(*refs):
        a_ref, b_ref = refs[:2]
        ex = refs[2:2 + n_ex]
        outs = refs[2 + n_ex:2 + n_ex + n_out]
        acc = outs[0] if acc_in_out else refs[-1]
        k = pl.program_id(2)
        prod = lax.dot_general(a_ref[...].astype(BF16), b_ref[...].astype(BF16), (dims, ((), ())),
                               preferred_element_type=F32)

        def finish(res):
            vals = epilogue(res, *[e[...] for e in ex]) if epilogue is not None else (res,)
            for o, v in zip(outs, vals):
                o[...] = v.astype(o.dtype)

        if nk == 1:
            finish(prod)
        else:
            @pl.when(k == 0)
            def _():
                acc[...] = prod

            @pl.when((k > 0) & (k < nk - 1))
            def _():
                acc[...] += prod

            @pl.when(k == nk - 1)
            def _():
                finish(acc[...] + prod)

    scratch = [] if (acc_in_out or nk == 1) else [pltpu.VMEM((tm, tn), F32)]
    outs = pl.pallas_call(
        body, grid=(m // tm, n // tn, nk), in_specs=[a_spec, b_spec, *ex_specs],
        out_specs=[pl.BlockSpec((tm, tn), lambda i, j, k: (i, j))] * n_out,
        out_shape=[jax.ShapeDtypeStruct((m, n), d) for d in out_dtypes],
        scratch_shapes=scratch, compiler_params=_cparams(3), name=name)(a, b, *extras)
    return outs[0] if n_out == 1 else outs


def _ep_add(acc, r):
    return (acc + r,)


def _ep_bias_add(acc, b, r):
    return (acc + b + r,)


def _ep_relu2(acc):
    return acc, jnp.square(jnp.maximum(acc, 0.0))


def _ep_relu2_bwd(acc, u):
    return (acc * (2.0 * jnp.maximum(u, 0.0)),)


def _rowwise(fn, rows, consts, outs, *, ncb=1, name):
    nrow = rows[0][0].shape[0]
    row_bytes = sum(w * arr.dtype.itemsize for arr, w, _ in rows)
    row_bytes += sum(o[2] * jnp.dtype(o[3]).itemsize for o in outs if o[0] == 'row')
    t = nrow
    for cand in (1024, 512, 256, 128, 64, 32, 16):
        if nrow % cand == 0:
            t = cand
            if cand * row_bytes <= ROW_BLOCK_BYTES:
                break
    in_specs = [pl.BlockSpec((t, w), (lambda j, i, off=off: (i, off + j))) for _, w, off in rows]
    for arr, per_col in consts:
        r, wc = arr.shape
        if per_col:
            in_specs.append(pl.BlockSpec((r, wc // ncb), lambda j, i: (0, j)))
        else:
            in_specs.append(pl.BlockSpec((r, wc), lambda j, i: (0, 0)))
    out_specs, out_shape = [], []
    for o in outs:
        if o[0] == 'row':
            out_specs.append(pl.BlockSpec((t, o[2]), lambda j, i: (i, j)))
            out_shape.append(jax.ShapeDtypeStruct((nrow, o[1]), o[3]))
        else:
            r, wt = o[1]
            if o[2]:
                out_specs.append(pl.BlockSpec((r, wt // ncb), lambda j, i: (0, j)))
            else:
                out_specs.append(pl.BlockSpec((r, wt), lambda j, i: (0, 0)))
            out_shape.append(jax.ShapeDtypeStruct((r, wt), F32))
    nin = len(rows) + len(consts)

    def body(*refs):
        j, i = pl.program_id(0), pl.program_id(1)
        vals = fn(*[r[...] for r in refs[:nin]])

        def store(spec, ref, v):
            if spec[0] == 'row':
                ref[...] = v.astype(ref.dtype)
            else:
                first = (i == 0) if spec[2] else ((i == 0) & (j == 0))

                @pl.when(first)
                def _():
                    ref[...] = jnp.zeros_like(ref)

                ref[...] += v

        for spec, ref, v in zip(outs, refs[nin:], vals):
            store(spec, ref, v)

    res = pl.pallas_call(body, grid=(ncb, nrow // t), in_specs=in_specs, out_specs=out_specs, out_shape=out_shape,
                         compiler_params=_cparams(2), name=name)(*[r[0] for r in rows], *[c[0] for c in consts])
    return res


def _rms_f(eps):
    def f(x, w):
        return x * lax.rsqrt(jnp.mean(x * x, axis=-1, keepdims=True) + eps) * w
    return f


def _rms_fwd(x, w, name):
    d = x.shape[1]
    f = _rms_f(RMS_EPS)
    return _rowwise(lambda xv, wv: (f(xv, wv),), [(x, d, 0)], [(w, False)], [('row', d, d, BF16)], name=name)[0]


def _rms_bwd(x, w, dh, dres, name):
    d = x.shape[1]
    f = _rms_f(RMS_EPS)

    def fn(xv, dhv, drv, wv):
        _, vjp = jax.vjp(f, xv, wv)
        dx, dw = vjp(dhv)
        return dx + drv, dx + drv, dw

    return _rowwise(fn, [(x, d, 0), (dh, d, 0), (dres, d, 0)], [(w, False)],
                    [('row', d, d, F32), ('row', d, d, BF16), ('acc', (1, d), False)], name=name)


def _colsum(x, name):
    w = x.shape[1]
    return _rowwise(lambda v: (jnp.sum(v, axis=0, keepdims=True),), [(x, w, 0)], [], [('acc', (1, w), False)],
                    name=name)[0]


def _loss(y, tgt, name):
    d = y.shape[1]

    def fn(yv, tv):
        e = yv - tv
        s = jnp.sum(jnp.sum(e * e, axis=1, keepdims=True), axis=0, keepdims=True) * (0.5 / d)
        return e * (1.0 / d), e * (1.0 / d), s + jnp.zeros((1, LANES), F32)

    return _rowwise(fn, [(y, d, 0), (tgt, d, 0)], [],
                    [('row', d, d, F32), ('row', d, d, BF16), ('acc', (1, LANES), False)], name=name)


def _conv_tiles(x, w):
    nrow, ch = x.shape
    kw = w.shape[0]
    halo = 8 * ((kw - 1 + 7) // 8)
    t = _tile(nrow, (128, 64, 32))
    tc = _tile(ch, (512, 256, 128))
    assert t % halo == 0 and nrow % t == 0
    return nrow, ch, kw, halo, t, tc


def _dwconv_fwd(x, w, b, name):
    nrow, ch, kw, halo, t, tc = _conv_tiles(x, w)
    per = t // halo

    def body(x_ref, halo_ref, w_ref, b_ref, y_ref, buf):
        i = pl.program_id(1)
        buf[0:halo, :] = jnp.where(i == 0, 0.0, halo_ref[...])
        buf[halo:halo + t, :] = x_ref[...]
        acc = jnp.zeros((t, tc), F32) + b_ref[...]
        for k in range(kw):
            s = halo - (kw - 1) + k
            acc = acc + w_ref[k:k + 1, :] * buf[s:s + t, :]
        y_ref[...] = acc

    return pl.pallas_call(
        body, grid=(ch // tc, nrow // t),
        in_specs=[pl.BlockSpec((t, tc), lambda j, i: (i, j)),
                  pl.BlockSpec((halo, tc), lambda j, i: (jnp.maximum(i * per - 1, 0), j)),
                  pl.BlockSpec((kw, tc), lambda j, i: (0, j)), pl.BlockSpec((1, tc), lambda j, i: (0, j))],
        out_specs=pl.BlockSpec((t, tc), lambda j, i: (i, j)), out_shape=jax.ShapeDtypeStruct((nrow, ch), F32),
        scratch_shapes=[pltpu.VMEM((t + halo, tc), F32)], compiler_params=_cparams(2), name=name)(x, x, w, b)


def _dwconv_bwd(x, dy, w, name):
    nrow, ch, kw, halo, t, tc = _conv_tiles(x, w)
    per = t // halo
    nt = nrow // t
    nhalo = nrow // halo

    def body(x_ref, xh_ref, dy_ref, dyh_ref, w_ref, dx_ref, dw_ref, db_ref, xbuf, dybuf):
        i = pl.program_id(1)
        xbuf[0:halo, :] = jnp.where(i == 0, 0.0, xh_ref[...])
        xbuf[halo:halo + t, :] = x_ref[...]
        dyc = dy_ref[...]
        dybuf[0:t, :] = dyc
        dybuf[t:t + halo, :] = jnp.where(i == nt - 1, 0.0, dyh_ref[...])
        acc = jnp.zeros((t, tc), F32)
        for k in range(kw):
            s = kw - 1 - k
            acc = acc + w_ref[k:k + 1, :] * dybuf[s:s + t, :]
        dx_ref[...] = acc

        @pl.when(i == 0)
        def _():
            dw_ref[...] = jnp.zeros_like(dw_ref)
            db_ref[...] = jnp.zeros_like(db_ref)

        for k in range(kw):
            s = kw - 1 - k
            dw_ref[k:k + 1, :] += jnp.sum(dyc * xbuf[halo - s:halo - s + t, :], axis=0, keepdims=True)
        db_ref[...] += jnp.sum(dyc, axis=0, keepdims=True)

    return pl.pallas_call(
        body, grid=(ch // tc, nt),
        in_specs=[pl.BlockSpec((t, tc), lambda j, i: (i, j)),
                  pl.BlockSpec((halo, tc), lambda j, i: (jnp.maximum(i * per - 1, 0), j)),
                  pl.BlockSpec((t, tc), lambda j, i: (i, j)),
                  pl.BlockSpec((halo, tc), lambda j, i: (jnp.minimum((i + 1) * per, nhalo - 1), j)),
                  pl.BlockSpec((kw, tc), lambda j, i: (0, j))],
        out_specs=[pl.BlockSpec((t, tc), lambda j, i: (i, j)), pl.BlockSpec((kw, tc), lambda j, i: (0, j)),
                   pl.BlockSpec((1, tc), lambda j, i: (0, j))],
        out_shape=[jax.ShapeDtypeStruct((nrow, ch), F32), jax.ShapeDtypeStruct((kw, ch), F32),
                   jax.ShapeDtypeStruct((1, ch), F32)],
        scratch_shapes=[pltpu.VMEM((t + halo, tc), F32), pltpu.VMEM((t + halo, tc), F32)],
        compiler_params=_cparams(2), name=name)(x, x, dy, dy, w)


SB_SUB = 128


def _dot_split2(x, u):
    hi = x.astype(BF16)
    lo = (x - hi.astype(F32)).astype(BF16)
    return jnp.dot(hi, u, preferred_element_type=F32) + jnp.dot(lo, u, preferred_element_type=F32)


def _sb_scores(q, k, strict, scale):
    z = lax.dot_general(q, k, (((1,), (1,)), ((), ())), preferred_element_type=F32) * scale
    soft = jnp.log(1.0 + jnp.exp(-jnp.abs(z)))
    ls = jnp.minimum(z, 0.0) - soft
    lk = jnp.minimum(-z, 0.0) - soft
    if strict is not None:
        lk = jnp.where(strict, lk, 0.0)
    return ls, lk


def _sb_running(x, tri, start, reverse):
    nsub = x.shape[1] // SB_SUB
    parts = [x[:, SB_SUB * b:SB_SUB * (b + 1)] for b in range(nsub)]
    out = [None] * nsub
    run = start
    for b in (reversed(range(nsub)) if reverse else range(nsub)):
        out[b] = _dot_split2(parts[b], tri) + run
        run = run + jnp.sum(parts[b], axis=1, keepdims=True)
    return (jnp.concatenate(out, axis=1) if nsub > 1 else out[0]), run


def _sb_dims(qn):
    nrow, d = qn.shape
    hd = SB_HEAD_DIM
    t = _tile(nrow, (512, 256, 128))
    assert nrow % t == 0 and nrow // t <= LANES
    return nrow, d, hd, d // hd, t


def _sb_masks(t):
    strict = lax.broadcasted_iota(jnp.int32, (t, t), 1) < lax.broadcasted_iota(jnp.int32, (t, t), 0)
    r0 = lax.broadcasted_iota(jnp.int32, (SB_SUB, SB_SUB), 0)
    c0 = lax.broadcasted_iota(jnp.int32, (SB_SUB, SB_SUB), 1)
    return strict, (r0 > c0).astype(BF16), (r0 < c0).astype(BF16)


def _sb_attn_fwd(qn, kn, qkv, name):
    nrow, d, hd, nh, t = _sb_dims(qn)
    scale = 1.0 / math.sqrt(hd)

    def body(q_ref, k_ref, v_ref, o_ref, rs_ref, r_acc, o_acc):
        i = pl.program_id(1)
        q = q_ref[...]
        strict, after, _ = _sb_masks(t)
        lane = lax.broadcasted_iota(jnp.int32, (t, LANES), 1)
        r_acc[...] = jnp.zeros_like(r_acc)
        o_acc[...] = jnp.zeros_like(o_acc)
        rs_ref[...] = jnp.zeros_like(rs_ref)

        def tile(j, mask):
            start = pl.multiple_of(j * t, t)
            k = k_ref[pl.ds(start, t), :]
            v = v_ref[pl.ds(start, t), :].astype(BF16)
            ls, lk = _sb_scores(q, k, mask, scale)
            r = r_acc[...]
            later, r_next = _sb_running(lk, after, r, True)
            att = jnp.exp(ls + later)
            if mask is not None:
                att = jnp.where(mask, att, 0.0)
            o_acc[...] += jnp.dot(att.astype(BF16), v, preferred_element_type=F32)
            rs_ref[...] = jnp.where(lane == j, r, rs_ref[...])
            r_acc[...] = r_next

        tile(i, strict)

        def step(n, carry):
            tile(i - 1 - n, None)
            return carry

        lax.fori_loop(0, i, step, 0)
        o_ref[...] = o_acc[...].astype(o_ref.dtype)

    return pl.pallas_call(
        body, grid=(nh, nrow // t),
        in_specs=[pl.BlockSpec((t, hd), lambda h, i: (i, h)), pl.BlockSpec((nrow, hd), lambda h, i: (0, h)),
                  pl.BlockSpec((nrow, hd), lambda h, i: (0, 2 * nh + h))],
        out_specs=[pl.BlockSpec((t, hd), lambda h, i: (i, h)), pl.BlockSpec((None, t, LANES), lambda h, i: (h, i, 0))],
        out_shape=[jax.ShapeDtypeStruct((nrow, d), BF16), jax.ShapeDtypeStruct((nh, nrow, LANES), F32)],
        scratch_shapes=[pltpu.VMEM((t, LANES), F32), pltpu.VMEM((t, hd), F32)],
        compiler_params=_cparams(2), name=name)(qn, kn, qkv)


def _sb_attn_bwd(qn, kn, qkv, do, rsave, name):
    nrow, d, hd, nh, t = _sb_dims(qn)
    scale = 1.0 / math.sqrt(hd)

    def body(q_ref, k_ref, v_ref, do_ref, rs_ref, dq_ref, dk_ref, dv_ref, pg_acc):
        i = pl.program_id(1)

        @pl.when(i == 0)
        def _():
            dk_ref[...] = jnp.zeros_like(dk_ref)
            dv_ref[...] = jnp.zeros_like(dv_ref)

        pg_acc[...] = jnp.zeros_like(pg_acc)
        dq_ref[...] = jnp.zeros_like(dq_ref)
        q = q_ref[...]
        dob = do_ref[...].astype(BF16)
        strict, after, before = _sb_masks(t)
        lane = lax.broadcasted_iota(jnp.int32, (t, LANES), 1)

        def tile(j, mask):
            start = pl.multiple_of(j * t, t)
            k = k_ref[pl.ds(start, t), :]
            v = v_ref[pl.ds(start, t), :].astype(BF16)
            ls, lk = _sb_scores(q, k, mask, scale)
            rj = jnp.sum(jnp.where(lane == j, rs_ref[...], 0.0), axis=1, keepdims=True)
            later, _ = _sb_running(lk, after, jnp.broadcast_to(rj, (t, LANES)), True)
            att = jnp.exp(ls + later)
            if mask is not None:
                att = jnp.where(mask, att, 0.0)
            datt = lax.dot_general(dob, v, (((1,), (1,)), ((), ())), preferred_element_type=F32)
            g = datt * att
            dlk, pg_next = _sb_running(g, before, pg_acc[...], False)
            sig = jnp.exp(ls)
            dz = g * (1.0 - sig) - dlk * sig
            if mask is not None:
                dz = jnp.where(mask, dz, 0.0)
            dz = (dz * scale).astype(BF16)
            dq_ref[...] += jnp.dot(dz, k, preferred_element_type=F32)
            dk_ref[pl.ds(start, t), :] += lax.dot_general(dz, q, (((0,), (0,)), ((), ())), preferred_element_type=F32)
            dv_ref[pl.ds(start, t), :] += lax.dot_general(att.astype(BF16), dob, (((0,), (0,)), ((), ())),
                                                          preferred_element_type=F32)
            pg_acc[...] = pg_next

        def step(j, carry):
            tile(j, None)
            return carry

        lax.fori_loop(0, i, step, 0)
        tile(i, strict)

    return pl.pallas_call(
        body, grid=(nh, nrow // t),
        in_specs=[pl.BlockSpec((t, hd), lambda h, i: (i, h)), pl.BlockSpec((nrow, hd), lambda h, i: (0, h)),
                  pl.BlockSpec((nrow, hd), lambda h, i: (0, 2 * nh + h)), pl.BlockSpec((t, hd), lambda h, i: (i, h)),
                  pl.BlockSpec((None, t, LANES), lambda h, i: (h, i, 0))],
        out_specs=[pl.BlockSpec((t, hd), lambda h, i: (i, h)), pl.BlockSpec((nrow, hd), lambda h, i: (0, h)),
                   pl.BlockSpec((nrow, hd), lambda h, i: (0, h))],
        out_shape=[jax.ShapeDtypeStruct((nrow, d), F32)] * 3, scratch_shapes=[pltpu.VMEM((t, LANES), F32)],
        compiler_params=_cparams(2), name=name)(qn, kn, qkv, do, rsave)


def _ssd_chunk_fn(g, nheads_g, q):
    p = M2_HEAD_DIM
    npair = nheads_g // 2

    def f(xp, bp, cp, dtr, hp, dtb, alog, dsk):
        b = _silu(bp).astype(BF16)
        c = _silu(cp).astype(BF16)
        dt = _softplus(dtr + dtb)
        dta = dt * (-jnp.exp(alog))
        ri = lax.broadcasted_iota(jnp.int32, (q, q), 0)
        ci = lax.broadcasted_iota(jnp.int32, (q, q), 1)
        causal = ri >= ci
        tri = causal.astype(F32)
        acol = jnp.dot(tri, dta, precision=HIGHEST, preferred_element_type=F32)
        arow = lax.dot_general(dta, tri, (((0,), (1,)), ((), ())), precision=HIGHEST,
                               preferred_element_type=F32)
        alast = jnp.sum(dta, axis=0, keepdims=True)
        cb = lax.dot_general(c, b, (((1,), (1,)), ((), ())), preferred_element_type=F32)
        lane = lax.broadcasted_iota(jnp.int32, (1, LANES), 1)
        sub = lax.broadcasted_iota(jnp.int32, (LANES, 1), 0)
        lane_half = lane < p
        sub_half = sub < p

        def col(mat, h):
            return jnp.sum(jnp.where(lane == h, mat, 0.0), axis=1, keepdims=True)

        def row(mat, h):
            return jnp.sum(jnp.where(sub == h, mat, 0.0), axis=0, keepdims=True)

        def mix(v0, v1):
            return jnp.where(lane_half, v0, v1)

        ys, hns = [], []
        for jp in range(npair):
            h0 = g * nheads_g + 2 * jp
            h1 = h0 + 1
            x = _silu(xp[jp])
            ac0, ac1 = col(acol, h0), col(acol, h1)
            xdt = x * mix(col(dt, h0), col(dt, h1))
            xdtb = xdt.astype(BF16)
            yd = []
            for h, ac in ((h0, ac0), (h1, ac1)):
                dec = jnp.exp(jnp.where(causal, ac - row(arow, h), -jnp.inf))
                yd.append(jnp.dot((cb * dec).astype(BF16), xdtb, preferred_element_type=F32))
            hpj = hp[jp]
            yo = lax.dot_general(c, hpj.astype(BF16), (((1,), (1,)), ((), ())), preferred_element_type=F32)
            y = mix(yd[0], yd[1]) + yo * jnp.exp(mix(ac0, ac1)) + mix(col(dsk, h0), col(dsk, h1)) * x
            al0, al1 = col(alast, h0), col(alast, h1)
            dout = jnp.exp(mix(al0 - ac0, al1 - ac1))
            st = lax.dot_general((xdt * dout).astype(BF16), b, (((0,), (0,)), ((), ())), preferred_element_type=F32)
            hns.append(hpj * jnp.exp(jnp.where(sub_half, al0, al1)) + st)
            ys.append(y)
        return ys, hns

    return f


def _ssd_dims(pre, dtr):
    nrow = pre.shape[0]
    ng, n, p, q = M2_GROUPS, M2_STATE, M2_HEAD_DIM, M2_CHUNK
    d_inner = pre.shape[1] - 2 * ng * n
    gw = d_inner // ng
    nhg = gw // p
    assert nhg % 2 == 0 and gw % LANES == 0 and n == LANES and ng * nhg <= LANES and dtr.shape[1] == LANES
    return nrow, ng, n, q, d_inner, gw, nhg, nrow // q


def _ssd_fwd(pre, dtr, dtb, alog, dsk, name):
    nrow, ng, n, q, d_inner, gw, nhg, nc = _ssd_dims(pre, dtr)
    boff = d_inner // n
    npair = nhg // 2

    def body(x_ref, b_ref, c_ref, dt_ref, dtb_ref, al_ref, ds_ref, y_ref, hs_ref, state):
        ci, g = pl.program_id(0), pl.program_id(1)

        @pl.when(ci == 0)
        def _():
            state[g] = jnp.zeros((gw, n), F32)

        hs_ref[...] = state[g]
        f = _ssd_chunk_fn(g, nhg, q)
        xp = [x_ref[:, LANES * j:LANES * (j + 1)] for j in range(npair)]
        hp = [state[g, LANES * j:LANES * (j + 1), :] for j in range(npair)]
        ys, hns = f(xp, b_ref[...], c_ref[...], dt_ref[...], hp, dtb_ref[...], al_ref[...], ds_ref[...])
        for j in range(npair):
            y_ref[:, LANES * j:LANES * (j + 1)] = ys[j]
            state[g, LANES * j:LANES * (j + 1), :] = hns[j]

    par = pl.BlockSpec((1, LANES), lambda ci, g: (0, 0))
    return pl.pallas_call(
        body, grid=(nc, ng),
        in_specs=[pl.BlockSpec((q, gw), lambda ci, g: (ci, g)), pl.BlockSpec((q, n), lambda ci, g: (ci, boff + g)),
                  pl.BlockSpec((q, n), lambda ci, g: (ci, boff + ng + g)), pl.BlockSpec((q, LANES), lambda ci, g: (ci, 0)),
                  par, par, par],
        out_specs=[pl.BlockSpec((q, gw), lambda ci, g: (ci, g)),
                   pl.BlockSpec((None, None, gw, n), lambda ci, g: (ci, g, 0, 0))],
        out_shape=[jax.ShapeDtypeStruct((nrow, d_inner), F32), jax.ShapeDtypeStruct((nc, ng, gw, n), F32)],
        scratch_shapes=[pltpu.VMEM((ng, gw, n), F32)], compiler_params=_cparams(2), name=name)(
            pre, pre, pre, dtr, dtb, alog, dsk)


def _ssd_bwd(pre, dtr, hs, dy, dtb, alog, dsk, name):
    nrow, ng, n, q, d_inner, gw, nhg, nc = _ssd_dims(pre, dtr)
    boff = d_inner // n
    npair = nhg // 2

    def body(x_ref, b_ref, c_ref, dt_ref, hs_ref, dy_ref, dtb_ref, al_ref, ds_ref,
             dx_ref, db_ref, dc_ref, ddt_ref, ddtb_ref, dal_ref, dds_ref, dstate):
        ci, g = pl.program_id(0), pl.program_id(1)

        @pl.when(ci == 0)
        def _():
            dstate[g] = jnp.zeros((gw, n), F32)

        @pl.when((ci == 0) & (g == 0))
        def _():
            ddtb_ref[...] = jnp.zeros_like(ddtb_ref)
            dal_ref[...] = jnp.zeros_like(dal_ref)
            dds_ref[...] = jnp.zeros_like(dds_ref)

        @pl.when(g == 0)
        def _():
            ddt_ref[...] = jnp.zeros_like(ddt_ref)

        f = _ssd_chunk_fn(g, nhg, q)
        xp = [x_ref[:, LANES * j:LANES * (j + 1)] for j in range(npair)]
        hp = [hs_ref[LANES * j:LANES * (j + 1), :] for j in range(npair)]
        _, vjp = jax.vjp(f, xp, b_ref[...], c_ref[...], dt_ref[...], hp, dtb_ref[...], al_ref[...], ds_ref[...])
        dys = [dy_ref[:, LANES * j:LANES * (j + 1)] for j in range(npair)]
        dhn = [dstate[g, LANES * j:LANES * (j + 1), :] for j in range(npair)]
        dxp, db, dc, ddt, dhp, ddtb, dal, dds = vjp((dys, dhn))
        for j in range(npair):
            dx_ref[:, LANES * j:LANES * (j + 1)] = dxp[j]
            dstate[g, LANES * j:LANES * (j + 1), :] = dhp[j]
        db_ref[...] = db
        dc_ref[...] = dc
        ddt_ref[...] += ddt
        ddtb_ref[...] += ddtb
        dal_ref[...] += dal
        dds_ref[...] += dds

    par = pl.BlockSpec((1, LANES), lambda ci, g: (0, 0))
    last = nc - 1
    return pl.pallas_call(
        body, grid=(nc, ng),
        in_specs=[pl.BlockSpec((q, gw), lambda ci, g: (last - ci, g)),
                  pl.BlockSpec((q, n), lambda ci, g: (last - ci, boff + g)),
                  pl.BlockSpec((q, n), lambda ci, g: (last - ci, boff + ng + g)),
                  pl.BlockSpec((q, LANES), lambda ci, g: (last - ci, 0)),
                  pl.BlockSpec((None, None, gw, n), lambda ci, g: (last - ci, g, 0, 0)),
                  pl.BlockSpec((q, gw), lambda ci, g: (last - ci, g)), par, par, par],
        out_specs=[pl.BlockSpec((q, gw), lambda ci, g: (last - ci, g)), pl.BlockSpec((q, n), lambda ci, g: (last - ci, g)),
                   pl.BlockSpec((q, n), lambda ci, g: (last - ci, g)), pl.BlockSpec((q, LANES), lambda ci, g: (last - ci, 0)),
                   par, par, par],
        out_shape=[jax.ShapeDtypeStruct((nrow, d_inner), F32), jax.ShapeDtypeStruct((nrow, ng * n), F32),
                   jax.ShapeDtypeStruct((nrow, ng * n), F32), jax.ShapeDtypeStruct((nrow, LANES), F32),
                   jax.ShapeDtypeStruct((1, LANES), F32), jax.ShapeDtypeStruct((1, LANES), F32),
                   jax.ShapeDtypeStruct((1, LANES), F32)],
        scratch_shapes=[pltpu.VMEM((ng, gw, n), F32)], compiler_params=_cparams(2), name=name)(
            pre, pre, pre, dtr, hs, dy, dtb, alog, dsk)


HBM_SPEC = pl.BlockSpec(memory_space=pl.ANY)


def _xy_peers(mx, my):
    return [(1 - mx, my), (mx, 1 - my), (1 - mx, 1 - my)]


def _remote(src, dst, send_sems, recv_sems, k, dev):
    return pltpu.make_async_remote_copy(src_ref=src, dst_ref=dst, send_sem=send_sems.at[k], recv_sem=recv_sems.at[k],
                                        device_id=dev, device_id_type=MESH_ID)


def _all_gather_xy(buf, name):
    half = buf.shape[0] // 2
    npeer = N_XY - 1

    def body(src, out, send_sems, recv_sems, local_sem):
        mx, my, mc = lax.axis_index("x"), lax.axis_index("y"), lax.axis_index("c")
        me = 2 * mx + my
        sibling = (mx, my, 1 - mc)
        peers = _xy_peers(mx, my)
        mine, theirs = pl.ds(mc * half, half), pl.ds((1 - mc) * half, half)
        local = pltpu.make_async_copy(src, out.at[me], local_sem)
        local.start()
        sends = [_remote(src.at[mine], out.at[me, mine], send_sems, recv_sems, k, (px, py, mc))
                 for k, (px, py) in enumerate(peers)]
        for cp in sends:
            cp.start()
        for k, (px, py) in enumerate(peers):
            landed = out.at[2 * px + py, mine]
            _remote(src.at[mine], landed, send_sems, recv_sems, k, (px, py, mc)).wait_recv()
            cp = _remote(landed, landed, send_sems, recv_sems, npeer + k, sibling)
            cp.start()
            sends.append(cp)
        for k, (px, py) in enumerate(peers):
            passed = out.at[2 * px + py, theirs]
            _remote(passed, passed, send_sems, recv_sems, npeer + k, sibling).wait_recv()
        for cp in sends:
            cp.wait_send()
        local.wait()

    return pl.pallas_call(
        body, out_shape=jax.ShapeDtypeStruct((N_XY,) + buf.shape, buf.dtype), in_specs=[HBM_SPEC], out_specs=HBM_SPEC,
        scratch_shapes=[pltpu.SemaphoreType.DMA((2 * npeer,)), pltpu.SemaphoreType.DMA((2 * npeer,)),
                        pltpu.SemaphoreType.DMA], name=name)(buf)


def _exchange_xy(parts, name):
    def body(src, out, send_sems, recv_sems, local_sem):
        mx, my, mc = lax.axis_index("x"), lax.axis_index("y"), lax.axis_index("c")
        me = 2 * mx + my
        peers = _xy_peers(mx, my)
        local = pltpu.make_async_copy(src.at[me], out.at[me], local_sem)
        local.start()
        sends = [_remote(src.at[2 * px + py], out.at[me], send_sems, recv_sems, k, (px, py, mc))
                 for k, (px, py) in enumerate(peers)]
        for cp in sends:
            cp.start()
        for k, (px, py) in enumerate(peers):
            _remote(src.at[me], out.at[2 * px + py], send_sems, recv_sems, k, (px, py, mc)).wait_recv()
        for cp in sends:
            cp.wait_send()
        local.wait()

    return pl.pallas_call(
        body, out_shape=jax.ShapeDtypeStruct(parts.shape, parts.dtype), in_specs=[HBM_SPEC], out_specs=HBM_SPEC,
        scratch_shapes=[pltpu.SemaphoreType.DMA((N_XY - 1,)), pltpu.SemaphoreType.DMA((N_XY - 1,)),
                        pltpu.SemaphoreType.DMA], name=name)(parts)


def _pair_split(parts, name):
    nparts, nrow, width = parts.shape
    half = nrow // 2

    def body(src, kept, got, send_sems, recv_sems, local_sem):
        mx, my, mc = lax.axis_index("x"), lax.axis_index("y"), lax.axis_index("c")
        local = pltpu.make_async_copy(src.at[:, pl.ds(mc * half, half), :], kept, local_sem)
        local.start()
        cp = _remote(src.at[:, pl.ds((1 - mc) * half, half), :], got, send_sems, recv_sems, 0, (mx, my, 1 - mc))
        cp.start()
        cp.wait()
        local.wait()

    shape = jax.ShapeDtypeStruct((nparts, half, width), parts.dtype)
    return pl.pallas_call(
        body, out_shape=[shape, shape], in_specs=[HBM_SPEC], out_specs=[HBM_SPEC, HBM_SPEC],
        scratch_shapes=[pltpu.SemaphoreType.DMA((1,)), pltpu.SemaphoreType.DMA((1,)), pltpu.SemaphoreType.DMA],
        name=name)(parts)


def _pair_join(mine, name):
    half, width = mine.shape

    def body(src, out, send_sems, recv_sems, local_sem):
        mx, my, mc = lax.axis_index("x"), lax.axis_index("y"), lax.axis_index("c")
        here = out.at[pl.ds(mc * half, half), :]
        local = pltpu.make_async_copy(src, here, local_sem)
        local.start()
        cp = _remote(src, here, send_sems, recv_sems, 0, (mx, my, 1 - mc))
        cp.start()
        cp.wait_send()
        there = out.at[pl.ds((1 - mc) * half, half), :]
        _remote(src, there, send_sems, recv_sems, 0, (mx, my, 1 - mc)).wait_recv()
        local.wait()

    return pl.pallas_call(
        body, out_shape=jax.ShapeDtypeStruct((2 * half, width), mine.dtype), in_specs=[HBM_SPEC], out_specs=HBM_SPEC,
        scratch_shapes=[pltpu.SemaphoreType.DMA((1,)), pltpu.SemaphoreType.DMA((1,)), pltpu.SemaphoreType.DMA],
        name=name)(mine)


def _all_gather_all(buf, name):
    flips = [(fx, fy, fc) for fx in (0, 1) for fy in (0, 1) for fc in (0, 1) if fx or fy or fc]

    def body(src, out, send_sems, recv_sems, local_sem):
        mx, my, mc = lax.axis_index("x"), lax.axis_index("y"), lax.axis_index("c")
        me = 4 * mx + 2 * my + mc
        peers = [(1 - mx if fx else mx, 1 - my if fy else my, 1 - mc if fc else mc) for fx, fy, fc in flips]
        local = pltpu.make_async_copy(src, out.at[me], local_sem)
        local.start()
        sends = [_remote(src, out.at[me], send_sems, recv_sems, k, dev) for k, dev in enumerate(peers)]
        for cp in sends:
            cp.start()
        for k, (px, py, pc) in enumerate(peers):
            _remote(src, out.at[4 * px + 2 * py + pc], send_sems, recv_sems, k, (px, py, pc)).wait_recv()
        for cp in sends:
            cp.wait_send()
        local.wait()

    return pl.pallas_call(
        body, out_shape=jax.ShapeDtypeStruct((N_DEV,) + buf.shape, buf.dtype), in_specs=[HBM_SPEC], out_specs=HBM_SPEC,
        scratch_shapes=[pltpu.SemaphoreType.DMA((N_DEV - 1,)), pltpu.SemaphoreType.DMA((N_DEV - 1,)),
                        pltpu.SemaphoreType.DMA], name=name)(buf)


def _flat_tile(nrow, width, narrays):
    t = nrow
    for cand in (512, 256, 128, 64, 32, 16, 8):
        if nrow % cand == 0:
            t = cand
            if cand * width * 4 * narrays <= ROW_BLOCK_BYTES:
                break
    return t


def _sum_parts(parts, name):
    kparts, nrow, width = parts.shape
    t = _flat_tile(nrow, width, kparts + 1)

    def body(p_ref, o_ref):
        s = p_ref[0].astype(F32)
        for k in range(1, kparts):
            s = s + p_ref[k].astype(F32)
        o_ref[...] = s

    return pl.pallas_call(
        body, grid=(nrow // t,), in_specs=[pl.BlockSpec((kparts, t, width), lambda i: (0, i, 0))],
        out_specs=pl.BlockSpec((t, width), lambda i: (i, 0)), out_shape=jax.ShapeDtypeStruct((nrow, width), F32),
        compiler_params=_cparams(1), name=name)(parts)


def _add2(a, b, name):
    nparts, nrow, width = a.shape
    t = _flat_tile(nrow, width, 3)

    def body(a_ref, b_ref, o_ref):
        o_ref[...] = (a_ref[...].astype(F32) + b_ref[...].astype(F32)).astype(o_ref.dtype)

    spec = pl.BlockSpec((None, t, width), lambda p, i: (p, i, 0))
    return pl.pallas_call(body, grid=(nparts, nrow // t), in_specs=[spec, spec], out_specs=spec,
                          out_shape=jax.ShapeDtypeStruct(a.shape, a.dtype), compiler_params=_cparams(2), name=name)(a, b)


def _adamw(w, m, v, ga, gb, name):
    nrow, width = w.shape
    t = _flat_tile(nrow, width, 9)
    c1 = 1.0 - ADAM_B1 ** ADAM_STEP
    c2 = 1.0 - ADAM_B2 ** ADAM_STEP
    grads = [ga] if gb is None else [ga, gb]

    def body(w_ref, m_ref, v_ref, *refs):
        g_ref, d_ref, mo_ref, vo_ref = refs[len(grads):]
        g = refs[0][...] if gb is None else refs[0][...] + refs[1][...]
        mn = ADAM_B1 * m_ref[...] + (1.0 - ADAM_B1) * g
        vn = ADAM_B2 * v_ref[...] + (1.0 - ADAM_B2) * jnp.square(g)
        g_ref[...] = g
        mo_ref[...] = mn
        vo_ref[...] = vn
        d_ref[...] = -ADAM_LR * ((mn / c1) / (jnp.sqrt(vn / c2) + ADAM_EPS) + ADAM_WD * w_ref[...])

    spec = pl.BlockSpec((t, width), lambda i: (i, 0))
    return pl.pallas_call(
        body, grid=(nrow // t,), in_specs=[spec] * (3 + len(grads)), out_specs=[spec] * 4,
        out_shape=[jax.ShapeDtypeStruct((nrow, width), F32)] * 4, compiler_params=_cparams(1), name=name)(
            w, m, v, *grads)


def _pack(arrs, dtype, width):
    flat = jnp.concatenate([a.astype(dtype).reshape(-1) for a in arrs])
    quantum = PACK_ROWS * width
    pad = (-flat.shape[0]) % quantum
    if pad:
        flat = jnp.concatenate([flat, jnp.zeros((pad,), dtype)])
    return flat.reshape(-1, width)


def _unpack(buf, shapes):
    flat = buf.reshape(-1)
    out, off = [], 0
    for s in shapes:
        size = math.prod(s)
        out.append(flat[off:off + size].reshape(s))
        off += size
    return out


def _qk_norm_fwd(qkv, w, off, nh, tag):
    f = _rms_f(RMS_EPS)
    return _rowwise(lambda xv, wv: (f(xv, wv),), [(qkv, SB_HEAD_DIM, off)], [(w, False)],
                    [('row', nh * SB_HEAD_DIM, SB_HEAD_DIM, BF16)], ncb=nh, name=tag)[0]


def _qk_norm_bwd(qkv, w, off, dn, nh, tag):
    f = _rms_f(RMS_EPS)

    def fn(xv, dv, wv):
        _, vjp = jax.vjp(f, xv, wv)
        return vjp(dv)

    return _rowwise(fn, [(qkv, SB_HEAD_DIM, off), (dn, SB_HEAD_DIM, 0)], [(w, False)],
                    [('row', nh * SB_HEAD_DIM, SB_HEAD_DIM, F32), ('acc', (1, SB_HEAD_DIM), False)], ncb=nh, name=tag)


def _sb_forward(h, xin, wt, tag):
    d = xin.shape[1]
    nh = d // SB_HEAD_DIM
    qkv = _mm(h, wt['w_qkv'], "nn", name=tag + "_qkv")
    qn = _qk_norm_fwd(qkv, wt['q_norm_w'], 0, nh, tag + "_qnorm")
    kn = _qk_norm_fwd(qkv, wt['k_norm_w'], nh, nh, tag + "_knorm")
    o, rsave = _sb_attn_fwd(qn, kn, qkv, tag + "_attn")
    xm = _mm(o, wt['w_o'], "nn", extras=(xin,), epilogue=_ep_add, name=tag + "_out")
    return xm, dict(qkv=qkv, qn=qn, kn=kn, o=o, rsave=rsave)


def _sb_backward(h, dx, dxb, wt, sv, tag):
    nh = dx.shape[1] // SB_HEAD_DIM
    do = _mm(dxb, wt['w_o'], "nt", name=tag + "_do")
    g_wo = _mm(sv['o'], dxb, "tn", name=tag + "_dwo")
    dqn, dkn, dv = _sb_attn_bwd(sv['qn'], sv['kn'], sv['qkv'], do, sv['rsave'], tag + "_attn_bwd")
    dq, g_qw = _qk_norm_bwd(sv['qkv'], wt['q_norm_w'], 0, dqn, nh, tag + "_qnorm_bwd")
    dk, g_kw = _qk_norm_bwd(sv['qkv'], wt['k_norm_w'], nh, dkn, nh, tag + "_knorm_bwd")
    dqkv = jnp.concatenate([dq, dk, dv], axis=1)
    g_wqkv = _mm(h, dqkv, "tn", name=tag + "_dwqkv")
    dh = _mm(dqkv, wt['w_qkv'], "nt", name=tag + "_dh")
    return dh, dict(w_qkv=g_wqkv, w_o=g_wo, q_norm_w=g_qw, k_norm_w=g_kw)


def _ln_silu_f(x, w, b):
    mu = jnp.mean(x, axis=-1, keepdims=True)
    xc = x - mu
    return _silu(xc * lax.rsqrt(jnp.mean(xc * xc, axis=-1, keepdims=True) + LN_EPS) * w + b)


def _glu_bwd_fn(val, gate, dg):
    sg = 1.0 / (1.0 + jnp.exp(-gate))
    return (jnp.concatenate([dg * sg, dg * val * sg * (1.0 - sg)], axis=1),)


def _cf_forward(h, xin, wt, tag):
    d = xin.shape[1]
    u = _mm(h, wt['w_in'], "nn", extras=(wt['b_in'],), epilogue=_ep_add, name=tag + "_in")
    gl = _rowwise(lambda val, gate: (val / (1.0 + jnp.exp(-gate)),), [(u, d, 0), (u, d, 1)], [],
                  [('row', d, d, F32)], name=tag + "_glu")[0]
    cv = _dwconv_fwd(gl, wt['dw_w'], wt['dw_b'], tag + "_conv")
    s = _rowwise(lambda x, w, b: (_ln_silu_f(x, w, b),), [(cv, d, 0)], [(wt['ln_w'], False), (wt['ln_b'], False)],
                 [('row', d, d, BF16)], name=tag + "_ln")[0]
    xm = _mm(s, wt['w_out'], "nn", extras=(wt['b_out'], xin), epilogue=_ep_bias_add, name=tag + "_out")
    return xm, dict(u=u, gl=gl, cv=cv, s=s)


def _cf_backward(h, dx, dxb, wt, sv, tag):
    d = dx.shape[1]
    ds = _mm(dxb, wt['w_out'], "nt", name=tag + "_ds")
    g_wout = _mm(sv['s'], dxb, "tn", name=tag + "_dwout")
    g_bout = _colsum(dx, tag + "_dbout")

    def ln_bwd(x, dsv, w, b):
        _, vjp = jax.vjp(_ln_silu_f, x, w, b)
        return vjp(dsv)

    dcv, g_lnw, g_lnb = _rowwise(ln_bwd, [(sv['cv'], d, 0), (ds, d, 0)], [(wt['ln_w'], False), (wt['ln_b'], False)],
                                 [('row', d, d, F32), ('acc', (1, d), False), ('acc', (1, d), False)],
                                 name=tag + "_ln_bwd")
    dgl, g_dww, g_dwb = _dwconv_bwd(sv['gl'], dcv, wt['dw_w'], tag + "_conv_bwd")
    du = _rowwise(_glu_bwd_fn, [(sv['u'], d, 0), (sv['u'], d, 1), (dgl, d, 0)], [], [('row', 2 * d, 2 * d, F32)],
                  name=tag + "_glu_bwd")[0]
    g_bin = _colsum(du, tag + "_dbin")
    g_win = _mm(h, du, "tn", name=tag + "_dwin")
    dh = _mm(du, wt['w_in'], "nt", name=tag + "_dh")
    return dh, dict(w_in=g_win, b_in=g_bin, dw_w=g_dww, dw_b=g_dwb, ln_w=g_lnw, ln_b=g_lnb, w_out=g_wout, b_out=g_bout)


def _gated_norm_f(y, z, w):
    t = y * _silu(z)
    return t * lax.rsqrt(jnp.mean(t * t, axis=-1, keepdims=True) + M2_NORM_EPS) * w


def _m2_forward(h, xin, wt, tag):
    z = _mm(h, wt['w_z'], "nn", name=tag + "_z")
    xbc = _mm(h, wt['w_xbc'], "nn", name=tag + "_xbc")
    dtr = _mm(h, wt['w_dt'], "nn", name=tag + "_dt")
    pre = _dwconv_fwd(xbc, wt['conv_w'], wt['conv_b'], tag + "_conv")
    y, hs = _ssd_fwd(pre, dtr, wt['dt_bias'], wt['a_log'], wt['d'], tag + "_ssd")
    d_inner = y.shape[1]
    gw = d_inner // M2_GROUPS
    yn = _rowwise(lambda yv, zv, w: (_gated_norm_f(yv, zv, w),), [(y, gw, 0), (z, gw, 0)], [(wt['norm_w'], True)],
                  [('row', d_inner, gw, BF16)], ncb=M2_GROUPS, name=tag + "_gnorm")[0]
    xm = _mm(yn, wt['w_out'], "nn", extras=(xin,), epilogue=_ep_add, name=tag + "_out")
    return xm, dict(z=z, xbc=xbc, dtr=dtr, pre=pre, y=y, hs=hs, yn=yn)


def _m2_backward(h, dx, dxb, wt, sv, tag):
    dyn = _mm(dxb, wt['w_out'], "nt", name=tag + "_dyn")
    g_wout = _mm(sv['yn'], dxb, "tn", name=tag + "_dwout")
    d_inner = sv['y'].shape[1]
    gw = d_inner // M2_GROUPS

    def gn_bwd(yv, zv, dv, w):
        _, vjp = jax.vjp(_gated_norm_f, yv, zv, w)
        return vjp(dv)

    dy, dz, g_nw = _rowwise(gn_bwd, [(sv['y'], gw, 0), (sv['z'], gw, 0), (dyn, gw, 0)], [(wt['norm_w'], True)],
                            [('row', d_inner, gw, F32), ('row', d_inner, gw, F32), ('acc', (1, d_inner), True)],
                            ncb=M2_GROUPS, name=tag + "_gnorm_bwd")
    dxp, db, dc, ddt, g_dtb, g_alog, g_d = _ssd_bwd(sv['pre'], sv['dtr'], sv['hs'], dy, wt['dt_bias'], wt['a_log'],
                                                   wt['d'], tag + "_ssd_bwd")
    dpre = jnp.concatenate([dxp, db, dc], axis=1)
    dxbc, g_cw, g_cb = _dwconv_bwd(sv['xbc'], dpre, wt['conv_w'], tag + "_conv_bwd")
    g_wz = _mm(h, dz, "tn", name=tag + "_dwz")
    g_wxbc = _mm(h, dxbc, "tn", name=tag + "_dwxbc")
    g_wdt = _mm(h, ddt, "tn", name=tag + "_dwdt")
    dh = _mm(dz, wt['w_z'], "nt", name=tag + "_dh_z")
    dh = _mm(dxbc, wt['w_xbc'], "nt", extras=(dh,), epilogue=_ep_add, name=tag + "_dh_xbc")
    dh = _mm(ddt, wt['w_dt'], "nt", extras=(dh,), epilogue=_ep_add, name=tag + "_dh_dt")
    return dh, dict(w_z=g_wz, w_xbc=g_wxbc, w_dt=g_wdt, conv_w=g_cw, conv_b=g_cb, dt_bias=g_dtb, a_log=g_alog, d=g_d,
                    norm_w=g_nw, w_out=g_wout)


def _pad_lanes(v):
    return jnp.pad(v.reshape(1, -1), ((0, 0), (0, LANES - v.shape[0])))


def kernel(x, norm_mix_w, norm_mlp_w, sb_w_qkv, sb_q_norm_w, sb_k_norm_w, sb_w_o, cf_w_in, cf_b_in, cf_dw_w, cf_dw_b, cf_ln_w, cf_ln_b, cf_w_out, cf_b_out, m2_w_in, m2_conv_w, m2_conv_b, m2_dt_bias, m2_a_log, m2_d, m2_norm_w, m2_w_out, mlp_w_up, mlp_w_down, loss_target, m_norm_mix_w, m_norm_mlp_w, m_sb_w_qkv, m_sb_q_norm_w, m_sb_k_norm_w, m_sb_w_o, m_cf_w_in, m_cf_b_in, m_cf_dw_w, m_cf_dw_b, m_cf_ln_w, m_cf_ln_b, m_cf_w_out, m_cf_b_out, m_m2_w_in, m_m2_conv_w, m_m2_conv_b, m_m2_dt_bias, m_m2_a_log, m_m2_d, m_m2_norm_w, m_m2_w_out, m_mlp_w_up, m_mlp_w_down, v_norm_mix_w, v_norm_mlp_w, v_sb_w_qkv, v_sb_q_norm_w, v_sb_k_norm_w, v_sb_w_o, v_cf_w_in, v_cf_b_in, v_cf_dw_w, v_cf_dw_b, v_cf_ln_w, v_cf_ln_b, v_cf_w_out, v_cf_b_out, v_m2_w_in, v_m2_conv_w, v_m2_conv_b, v_m2_dt_bias, v_m2_a_log, v_m2_d, v_m2_norm_w, v_m2_w_out, v_mlp_w_up, v_mlp_w_down):
    given = dict(locals())
    xl = x[0]
    tgt = loss_target[0]
    d = xl.shape[1]
    depth = norm_mix_w.shape[0]
    bulk_names, small_names = list(BULK), list(SMALL)

    gath_b = _all_gather_xy(_pack([given[n] for n in bulk_names], BF16, PACK_W), "comm_gather_bulk")
    gath_s = _all_gather_xy(_pack([given[n] for n in small_names], F32, PACK_W), "comm_gather_small")
    full = {}
    for names, gath, axes in ((bulk_names, gath_b, BULK), (small_names, gath_s, SMALL)):
        pieces = [_unpack(gath[s], [given[n].shape for n in names]) for s in range(N_XY)]
        for idx, n in enumerate(names):
            full[n] = jnp.concatenate([pieces[s][idx] for s in range(N_XY)], axis=axes[n])

    def row(v):
        return v.reshape(1, -1)

    d_inner = full['m2_w_out'].shape[1]
    conv_dim = full['m2_conv_w'].shape[2]
    nheads = m2_dt_bias.shape[1]

    def layer_weights(i):
        kind, j = i % 3, i // 3
        if kind == 0:
            return dict(w_qkv=full['sb_w_qkv'][j], w_o=full['sb_w_o'][j], q_norm_w=row(sb_q_norm_w[j]),
                        k_norm_w=row(sb_k_norm_w[j]))
        if kind == 1:
            return dict(w_in=full['cf_w_in'][j], b_in=row(cf_b_in[j]), dw_w=full['cf_dw_w'][j], dw_b=row(cf_dw_b[j]),
                        ln_w=row(cf_ln_w[j]), ln_b=row(cf_ln_b[j]), w_out=full['cf_w_out'][j], b_out=row(cf_b_out[j]))
        w_in = full['m2_w_in'][j]
        w_dt = jnp.pad(w_in[:, d_inner + conv_dim:], ((0, 0), (0, LANES - nheads)))
        return dict(w_z=w_in[:, :d_inner], w_xbc=w_in[:, d_inner:d_inner + conv_dim], w_dt=w_dt,
                    conv_w=full['m2_conv_w'][j], conv_b=row(full['m2_conv_b'][j]), dt_bias=_pad_lanes(m2_dt_bias[j]),
                    a_log=_pad_lanes(m2_a_log[j]), d=_pad_lanes(m2_d[j]), norm_w=row(full['m2_norm_w'][j]),
                    w_out=full['m2_w_out'][j])

    fwd = (_sb_forward, _cf_forward, _m2_forward)
    bwd = (_sb_backward, _cf_backward, _m2_backward)

    saved = []
    xc = xl
    for i in range(depth):
        wt = layer_weights(i)
        h = _rms_fwd(xc, row(norm_mix_w[i]), f"l{i}_norm_mix")
        xm, sv = fwd[i % 3](h, xc, wt, f"l{i}_mix")
        h2 = _rms_fwd(xm, row(norm_mlp_w[i]), f"l{i}_norm_mlp")
        u, a = _mm(h2, full['mlp_w_up'][i], "nn", epilogue=_ep_relu2, out_dtypes=(F32, BF16), name=f"l{i}_up")
        xn = _mm(a, full['mlp_w_down'][i], "nn", extras=(xm,), epilogue=_ep_add, name=f"l{i}_down")
        saved.append(dict(wt=wt, x_in=xc, h=h, mix=sv, x_mid=xm, h2=h2, u=u, a=a))
        xc = xn
    dx, dxb, loss_acc = _loss(xc, tgt, "loss")
    loss = lax.psum(loss_acc[0, 0], ("x", "y", "c"))

    grads = {n: [None] * given[n].shape[0] for n in W_NAMES}
    for i in reversed(range(depth)):
        sv = saved[i]
        kind, j = i % 3, i // 3
        du = _mm(dxb, full['mlp_w_down'][i], "nt", extras=(sv['u'],), epilogue=_ep_relu2_bwd, out_dtypes=(BF16,),
                 name=f"l{i}_du")
        grads['mlp_w_down'][i] = _mm(sv['a'], dxb, "tn", name=f"l{i}_dwdown")
        grads['mlp_w_up'][i] = _mm(sv['h2'], du, "tn", name=f"l{i}_dwup")
        dh2 = _mm(du, full['mlp_w_up'][i], "nt", name=f"l{i}_dh2")
        dxm, dxmb, g_n2 = _rms_bwd(sv['x_mid'], row(norm_mlp_w[i]), dh2, dx, f"l{i}_norm_mlp_bwd")
        grads['norm_mlp_w'][i] = g_n2[0]
        dh, gw = bwd[kind](sv['h'], dxm, dxmb, sv['wt'], sv['mix'], f"l{i}_mix")
        dx, dxb, g_n1 = _rms_bwd(sv['x_in'], row(norm_mix_w[i]), dh, dxm, f"l{i}_norm_mix_bwd")
        grads['norm_mix_w'][i] = g_n1[0]
        if kind == 0:
            grads['sb_w_qkv'][j], grads['sb_w_o'][j] = gw['w_qkv'], gw['w_o']
            grads['sb_q_norm_w'][j], grads['sb_k_norm_w'][j] = gw['q_norm_w'][0], gw['k_norm_w'][0]
        elif kind == 1:
            for n in ('w_in', 'dw_w', 'w_out'):
                grads['cf_' + n][j] = gw[n]
            for n in ('b_in', 'dw_b', 'ln_w', 'ln_b', 'b_out'):
                grads['cf_' + n][j] = gw[n][0]
        else:
            grads['m2_w_in'][j] = jnp.concatenate([gw['w_z'], gw['w_xbc'], gw['w_dt'][:, :nheads]], axis=1)
            grads['m2_conv_w'][j], grads['m2_w_out'][j] = gw['conv_w'], gw['w_out']
            grads['m2_conv_b'][j], grads['m2_norm_w'][j] = gw['conv_b'][0], gw['norm_w'][0]
            for n in ('dt_bias', 'a_log', 'd'):
                grads['m2_' + n][j] = gw[n][0, :nheads]
    grad_x = dx[None]
    gfull = {n: jnp.stack(grads[n]) for n in W_NAMES}

    outs = {}
    for names, axes, dtype, tag in ((bulk_names, BULK, BF16, "bulk"), (small_names, SMALL, F32, "small")):
        split = {n: jnp.split(gfull[n], N_XY, axis=axes[n]) for n in names}
        parts = jnp.stack([_pack([split[n][s] for n in names], dtype, PACK_W) for s in range(N_XY)])
        kept, got = _pair_split(parts, f"comm_split_{tag}")
        pair = _add2(kept, got, f"sum_pair_{tag}")
        mine = _sum_parts(_exchange_xy(pair, f"comm_exchange_{tag}"), f"sum_{tag}")
        gsum = _pair_join(mine, f"comm_join_{tag}")
        res = _adamw(*[_pack([given[p + n] for n in names], F32, PACK_W) for p in ("", "m_", "v_")], gsum, None,
                     f"adamw_{tag}")
        shapes = [given[n].shape for n in names]
        for kind, buf in zip(("grad", "delta", "new_m", "new_v"), res):
            for n, arr in zip(names, _unpack(buf, shapes)):
                outs[kind, n] = arr
    rparts = _all_gather_all(_pack([gfull[n] for n in REPL], F32, LANES), "comm_gather_repl")
    nsplit = N_DEV // 2
    res = _adamw(*[_pack([given[p + n] for n in REPL], F32, LANES) for p in ("", "m_", "v_")],
                 _sum_parts(rparts[:nsplit], "sum_repl_a"), _sum_parts(rparts[nsplit:], "sum_repl_b"), "adamw_repl")
    shapes = [given[n].shape for n in REPL]
    for kind, buf in zip(("grad", "delta", "new_m", "new_v"), res):
        for n, arr in zip(REPL, _unpack(buf, shapes)):
            outs[kind, n] = arr

    return (loss, grad_x, *[outs[kind, n] for kind in ("grad", "delta", "new_m", "new_v") for n in W_NAMES])
```

```python
import math

import jax
import jax.numpy as jnp
from jax import lax
from jax.experimental import pallas as pl
from jax.experimental.pallas import tpu as pltpu

F32 = jnp.float32
BF16 = jnp.bfloat16
MESH_ID = pl.DeviceIdType.MESH
HIGHEST = lax.Precision.HIGHEST

SB_HEAD_DIM = 128
M2_HEAD_DIM = 64
M2_STATE = 128
M2_GROUPS = 8
M2_CHUNK = 128
RMS_EPS = 1e-6
LN_EPS = 1e-5
M2_NORM_EPS = 1e-5
ADAM_LR = 0.001
ADAM_B1 = 0.9
ADAM_B2 = 0.999
ADAM_EPS = 1e-08
ADAM_WD = 0.01
ADAM_STEP = 10

N_XY = 4
N_DEV = 8

V7X_VMEM_BYTES = 64 * 2**20
VMEM_LIMIT = (V7X_VMEM_BYTES * 3) // 4
LANES = 128
ROW_BLOCK_BYTES = 6 * 2**20
MM_VMEM_BUDGET = (VMEM_LIMIT * 3) // 4
PACK_W = 1024
PACK_ROWS = 1024

W_NAMES = ['norm_mix_w', 'norm_mlp_w', 'sb_w_qkv', 'sb_q_norm_w', 'sb_k_norm_w', 'sb_w_o', 'cf_w_in', 'cf_b_in',
           'cf_dw_w', 'cf_dw_b', 'cf_ln_w', 'cf_ln_b', 'cf_w_out', 'cf_b_out', 'm2_w_in', 'm2_conv_w', 'm2_conv_b',
           'm2_dt_bias', 'm2_a_log', 'm2_d', 'm2_norm_w', 'm2_w_out', 'mlp_w_up', 'mlp_w_down']
BULK = {'sb_w_qkv': 2, 'sb_w_o': 1, 'cf_w_in': 2, 'cf_w_out': 1, 'm2_w_in': 2, 'm2_w_out': 1, 'mlp_w_up': 2,
        'mlp_w_down': 1}
SMALL = {'cf_dw_w': 2, 'm2_conv_w': 2, 'm2_conv_b': 1, 'm2_norm_w': 1}
REPL = ['norm_mix_w', 'norm_mlp_w', 'sb_q_norm_w', 'sb_k_norm_w', 'cf_b_in', 'cf_dw_b', 'cf_ln_w', 'cf_ln_b',
        'cf_b_out', 'm2_dt_bias', 'm2_a_log', 'm2_d']


def _tile(n, prefs):
    for p in prefs:
        if n % p == 0:
            return p
    return n


def _cparams(n):
    return pltpu.CompilerParams(dimension_semantics=("arbitrary",) * n, vmem_limit_bytes=VMEM_LIMIT)


def _silu(v):
    return v / (1.0 + jnp.exp(-v))


def _softplus(v):
    return jnp.maximum(v, 0.0) + jnp.log(1.0 + jnp.exp(-jnp.abs(v)))


def _mm(a, b, mode, *, extras=(), epilogue=None, out_dtypes=(F32,), name):
    if mode == "tn":
        kdim, m = a.shape
    else:
        m, kdim = a.shape
    n = b.shape[0] if mode == "nt" else b.shape[1]
    tm = _tile(m, (1024, 512, 256, 128))
    tn = _tile(n, (1024, 512, 256, 128))
    acc_in_out = out_dtypes[0] == F32
    fixed = sum(tm * tn * jnp.dtype(dt).itemsize for dt in out_dtypes)
    fixed += sum((1 if e.shape[0] == 1 else tm) * tn * e.dtype.itemsize for e in extras)

    def vmem_bytes(t):
        ab = tm * t * a.dtype.itemsize + t * tn * b.dtype.itemsize
        casts = (tm * t * 2 if a.dtype != BF16 else 0) + (t * tn * 2 if b.dtype != BF16 else 0)
        return 2 * (ab + fixed) + (0 if acc_in_out else tm * tn * 4) + tm * tn * 4 + casts

    tk = kdim
    for cand in (2048, 1024, 512, 256, 128):
        if kdim % cand == 0:
            tk = cand
            if vmem_bytes(cand) <= MM_VMEM_BUDGET:
                break
    nk = kdim // tk
    dims = {"nn": ((1,), (0,)), "nt": ((1,), (1,)), "tn": ((0,), (0,))}[mode]
    if mode == "tn":
        a_spec = pl.BlockSpec((tk, tm), lambda i, j, k: (k, i))
    else:
        a_spec = pl.BlockSpec((tm, tk), lambda i, j, k: (i, k))
    if mode == "nt":
        b_spec = pl.BlockSpec((tn, tk), lambda i, j, k: (j, k))
    else:
        b_spec = pl.BlockSpec((tk, tn), lambda i, j, k: (k, j))
    ex_specs = [pl.BlockSpec((1, tn), lambda i, j, k: (0, j)) if e.shape[0] == 1
                else pl.BlockSpec((tm, tn), lambda i, j, k: (i, j)) for e in extras]
    n_ex, n_out = len(extras), len(out_dtypes)

    def body(*refs):
        a_ref, b_ref = refs[:2]
        ex = refs[2:2 + n_ex]
        outs = refs[2 + n_ex:2 + n_ex + n_out]
        acc = outs[0] if acc_in_out else refs[-1]
        k = pl.program_id(2)
        prod = lax.dot_general(a_ref[...].astype(BF16), b_ref[...].astype(BF16), (dims, ((), ())),
                               preferred_element_type=F32)

        def finish(res):
            vals = epilogue(res, *[e[...] for e in ex]) if epilogue is not None else (res,)
            for o, v in zip(outs, vals):
                o[...] = v.astype(o.dtype)

        if nk == 1:
            finish(prod)
        else:
            @pl.when(k == 0)
            def _():
                acc[...] = prod

            @pl.when((k > 0) & (k < nk - 1))
            def _():
                acc[...] += prod

            @pl.when(k == nk - 1)
            def _():
                finish(acc[...] + prod)

    scratch = [] if (acc_in_out or nk == 1) else [pltpu.VMEM((tm, tn), F32)]
    outs = pl.pallas_call(
        body, grid=(m // tm, n // tn, nk), in_specs=[a_spec, b_spec, *ex_specs],
        out_specs=[pl.BlockSpec((tm, tn), lambda i, j, k: (i, j))] * n_out,
        out_shape=[jax.ShapeDtypeStruct((m, n), d) for d in out_dtypes],
        scratch_shapes=scratch, compiler_params=_cparams(3), name=name)(a, b, *extras)
    return outs[0] if n_out == 1 else outs


def _ep_add(acc, r):
    return (acc + r,)


def _ep_bias_add(acc, b, r):
    return (acc + b + r,)


def _ep_relu2(acc):
    return acc, jnp.square(jnp.maximum(acc, 0.0))


def _ep_relu2_bwd(acc, u):
    return (acc * (2.0 * jnp.maximum(u, 0.0)),)


def _rowwise(fn, rows, consts, outs, *, ncb=1, name):
    nrow = rows[0][0].shape[0]
    row_bytes = sum(w * arr.dtype.itemsize for arr, w, _ in rows)
    row_bytes += sum(o[2] * jnp.dtype(o[3]).itemsize for o in outs if o[0] == 'row')
    t = nrow
    for cand in (1024, 512, 256, 128, 64, 32, 16):
        if nrow % cand == 0:
            t = cand
            if cand * row_bytes <= ROW_BLOCK_BYTES:
                break
    in_specs = [pl.BlockSpec((t, w), (lambda j, i, off=off: (i, off + j))) for _, w, off in rows]
    for arr, per_col in consts:
        r, wc = arr.shape
        if per_col:
            in_specs.append(pl.BlockSpec((r, wc // ncb), lambda j, i: (0, j)))
        else:
            in_specs.append(pl.BlockSpec((r, wc), lambda j, i: (0, 0)))
    out_specs, out_shape = [], []
    for o in outs:
        if o[0] == 'row':
            out_specs.append(pl.BlockSpec((t, o[2]), lambda j, i: (i, j)))
            out_shape.append(jax.ShapeDtypeStruct((nrow, o[1]), o[3]))
        else:
            r, wt = o[1]
            if o[2]:
                out_specs.append(pl.BlockSpec((r, wt // ncb), lambda j, i: (0, j)))
            else:
                out_specs.append(pl.BlockSpec((r, wt), lambda j, i: (0, 0)))
            out_shape.append(jax.ShapeDtypeStruct((r, wt), F32))
    nin = len(rows) + len(consts)

    def body(*refs):
        j, i = pl.program_id(0), pl.program_id(1)
        vals = fn(*[r[...] for r in refs[:nin]])

        def store(spec, ref, v):
            if spec[0] == 'row':
                ref[...] = v.astype(ref.dtype)
            else:
                first = (i == 0) if spec[2] else ((i == 0) & (j == 0))

                @pl.when(first)
                def _():
                    ref[...] = jnp.zeros_like(ref)

                ref[...] += v

        for spec, ref, v in zip(outs, refs[nin:], vals):
            store(spec, ref, v)

    res = pl.pallas_call(body, grid=(ncb, nrow // t), in_specs=in_specs, out_specs=out_specs, out_shape=out_shape,
                         compiler_params=_cparams(2), name=name)(*[r[0] for r in rows], *[c[0] for c in consts])
    return res


def _rms_f(eps):
    def f(x, w):
        return x * lax.rsqrt(jnp.mean(x * x, axis=-1, keepdims=True) + eps) * w
    return f


def _rms_fwd(x, w, name):
    d = x.shape[1]
    f = _rms_f(RMS_EPS)
    return _rowwise(lambda xv, wv: (f(xv, wv),), [(x, d, 0)], [(w, False)], [('row', d, d, BF16)], name=name)[0]


def _rms_bwd(x, w, dh, dres, name):
    d = x.shape[1]
    f = _rms_f(RMS_EPS)

    def fn(xv, dhv, drv, wv):
        _, vjp = jax.vjp(f, xv, wv)
        dx, dw = vjp(dhv)
        return dx + drv, dx + drv, dw

    return _rowwise(fn, [(x, d, 0), (dh, d, 0), (dres, d, 0)], [(w, False)],
                    [('row', d, d, F32), ('row', d, d, BF16), ('acc', (1, d), False)], name=name)


def _colsum(x, name):
    w = x.shape[1]
    return _rowwise(lambda v: (jnp.sum(v, axis=0, keepdims=True),), [(x, w, 0)], [], [('acc', (1, w), False)],
                    name=name)[0]


def _loss(y, tgt, name):
    d = y.shape[1]

    def fn(yv, tv):
        e = yv - tv
        s = jnp.sum(jnp.sum(e * e, axis=1, keepdims=True), axis=0, keepdims=True) * (0.5 / d)
        return e * (1.0 / d), e * (1.0 / d), s + jnp.zeros((1, LANES), F32)

    return _rowwise(fn, [(y, d, 0), (tgt, d, 0)], [],
                    [('row', d, d, F32), ('row', d, d, BF16), ('acc', (1, LANES), False)], name=name)


def _conv_tiles(x, w):
    nrow, ch = x.shape
    kw = w.shape[0]
    halo = 8 * ((kw - 1 + 7) // 8)
    t = _tile(nrow, (256, 128, 64, 32))
    tc = _tile(ch, (512, 256, 128))
    assert t % halo == 0 and nrow % t == 0
    return nrow, ch, kw, halo, t, tc


def _dwconv_fwd(x, w, b, name):
    nrow, ch, kw, halo, t, tc = _conv_tiles(x, w)
    per = t // halo

    def body(x_ref, halo_ref, w_ref, b_ref, y_ref, buf):
        i = pl.program_id(1)
        buf[0:halo, :] = jnp.where(i == 0, 0.0, halo_ref[...])
        buf[halo:halo + t, :] = x_ref[...]
        acc = jnp.zeros((t, tc), F32) + b_ref[...]
        for k in range(kw):
            s = halo - (kw - 1) + k
            acc = acc + w_ref[k:k + 1, :] * buf[s:s + t, :]
        y_ref[...] = acc

    return pl.pallas_call(
        body, grid=(ch // tc, nrow // t),
        in_specs=[pl.BlockSpec((t, tc), lambda j, i: (i, j)),
                  pl.BlockSpec((halo, tc), lambda j, i: (jnp.maximum(i * per - 1, 0), j)),
                  pl.BlockSpec((kw, tc), lambda j, i: (0, j)), pl.BlockSpec((1, tc), lambda j, i: (0, j))],
        out_specs=pl.BlockSpec((t, tc), lambda j, i: (i, j)), out_shape=jax.ShapeDtypeStruct((nrow, ch), F32),
        scratch_shapes=[pltpu.VMEM((t + halo, tc), F32)], compiler_params=_cparams(2), name=name)(x, x, w, b)


def _dwconv_bwd(x, dy, w, name):
    nrow, ch, kw, halo, t, tc = _conv_tiles(x, w)
    per = t // halo
    nt = nrow // t
    nhalo = nrow // halo

    def body(x_ref, xh_ref, dy_ref, dyh_ref, w_ref, dx_ref, dw_ref, db_ref, xbuf, dybuf):
        i = pl.program_id(1)
        xbuf[0:halo, :] = jnp.where(i == 0, 0.0, xh_ref[...])
        xbuf[halo:halo + t, :] = x_ref[...]
        dyc = dy_ref[...]
        dybuf[0:t, :] = dyc
        dybuf[t:t + halo, :] = jnp.where(i == nt - 1, 0.0, dyh_ref[...])
        acc = jnp.zeros((t, tc), F32)
        for k in range(kw):
            s = kw - 1 - k
            acc = acc + w_ref[k:k + 1, :] * dybuf[s:s + t, :]
        dx_ref[...] = acc

        @pl.when(i == 0)
        def _():
            dw_ref[...] = jnp.zeros_like(dw_ref)
            db_ref[...] = jnp.zeros_like(db_ref)

        for k in range(kw):
            s = kw - 1 - k
            dw_ref[k:k + 1, :] += jnp.sum(dyc * xbuf[halo - s:halo - s + t, :], axis=0, keepdims=True)
        db_ref[...] += jnp.sum(dyc, axis=0, keepdims=True)

    return pl.pallas_call(
        body, grid=(ch // tc, nt),
        in_specs=[pl.BlockSpec((t, tc), lambda j, i: (i, j)),
                  pl.BlockSpec((halo, tc), lambda j, i: (jnp.maximum(i * per - 1, 0), j)),
                  pl.BlockSpec((t, tc), lambda j, i: (i, j)),
                  pl.BlockSpec((halo, tc), lambda j, i: (jnp.minimum((i + 1) * per, nhalo - 1), j)),
                  pl.BlockSpec((kw, tc), lambda j, i: (0, j))],
        out_specs=[pl.BlockSpec((t, tc), lambda j, i: (i, j)), pl.BlockSpec((kw, tc), lambda j, i: (0, j)),
                   pl.BlockSpec((1, tc), lambda j, i: (0, j))],
        out_shape=[jax.ShapeDtypeStruct((nrow, ch), F32), jax.ShapeDtypeStruct((kw, ch), F32),
                   jax.ShapeDtypeStruct((1, ch), F32)],
        scratch_shapes=[pltpu.VMEM((t + halo, tc), F32), pltpu.VMEM((t + halo, tc), F32)],
        compiler_params=_cparams(2), name=name)(x, x, dy, dy, w)


SB_SUB = 128


def _dot_split2(x, u):
    hi = x.astype(BF16)
    lo = (x - hi.astype(F32)).astype(BF16)
    return jnp.dot(hi, u, preferred_element_type=F32) + jnp.dot(lo, u, preferred_element_type=F32)


def _sb_scores(q, k, strict, scale):
    z = lax.dot_general(q, k, (((1,), (1,)), ((), ())), preferred_element_type=F32) * scale
    soft = jnp.log(1.0 + jnp.exp(-jnp.abs(z)))
    ls = jnp.minimum(z, 0.0) - soft
    lk = ls - z
    if strict is not None:
        lk = jnp.where(strict, lk, 0.0)
    return ls, lk


def _sb_running(x, tri, start, reverse):
    nsub = x.shape[1] // SB_SUB
    parts = [x[:, SB_SUB * b:SB_SUB * (b + 1)] for b in range(nsub)]
    out = [None] * nsub
    run = start
    for b in (reversed(range(nsub)) if reverse else range(nsub)):
        out[b] = _dot_split2(parts[b], tri) + run
        run = run + jnp.sum(parts[b], axis=1, keepdims=True)
    return (jnp.concatenate(out, axis=1) if nsub > 1 else out[0]), run


def _sb_dims(qn):
    nrow, d = qn.shape
    hd = SB_HEAD_DIM
    t = _tile(nrow, (512, 256, 128))
    assert nrow % t == 0 and nrow // t <= LANES
    return nrow, d, hd, d // hd, t


def _sb_masks(t):
    strict = lax.broadcasted_iota(jnp.int32, (t, t), 1) < lax.broadcasted_iota(jnp.int32, (t, t), 0)
    r0 = lax.broadcasted_iota(jnp.int32, (SB_SUB, SB_SUB), 0)
    c0 = lax.broadcasted_iota(jnp.int32, (SB_SUB, SB_SUB), 1)
    return strict, (r0 > c0).astype(BF16), (r0 < c0).astype(BF16)


def _sb_attn_fwd(qn, kn, qkv, name):
    nrow, d, hd, nh, t = _sb_dims(qn)
    scale = 1.0 / math.sqrt(hd)

    def body(q_ref, k_ref, v_ref, o_ref, rs_ref, r_acc, o_acc):
        i = pl.program_id(1)
        q = q_ref[...]
        strict, after, _ = _sb_masks(t)
        lane = lax.broadcasted_iota(jnp.int32, (t, LANES), 1)
        r_acc[...] = jnp.zeros_like(r_acc)
        o_acc[...] = jnp.zeros_like(o_acc)
        rs_ref[...] = jnp.zeros_like(rs_ref)

        def tile(j, mask):
            start = pl.multiple_of(j * t, t)
            k = k_ref[pl.ds(start, t), :]
            v = v_ref[pl.ds(start, t), :].astype(BF16)
            ls, lk = _sb_scores(q, k, mask, scale)
            r = r_acc[...]
            later, r_next = _sb_running(lk, after, r, True)
            att = jnp.exp(ls + later)
            if mask is not None:
                att = jnp.where(mask, att, 0.0)
            o_acc[...] += jnp.dot(att.astype(BF16), v, preferred_element_type=F32)
            rs_ref[...] = jnp.where(lane == j, r, rs_ref[...])
            r_acc[...] = r_next

        tile(i, strict)

        def step(n, carry):
            tile(i - 1 - n, None)
            return carry

        lax.fori_loop(0, i, step, 0)
        o_ref[...] = o_acc[...].astype(o_ref.dtype)

    return pl.pallas_call(
        body, grid=(nh, nrow // t),
        in_specs=[pl.BlockSpec((t, hd), lambda h, i: (i, h)), pl.BlockSpec((nrow, hd), lambda h, i: (0, h)),
                  pl.BlockSpec((nrow, hd), lambda h, i: (0, 2 * nh + h))],
        out_specs=[pl.BlockSpec((t, hd), lambda h, i: (i, h)), pl.BlockSpec((None, t, LANES), lambda h, i: (h, i, 0))],
        out_shape=[jax.ShapeDtypeStruct((nrow, d), BF16), jax.ShapeDtypeStruct((nh, nrow, LANES), F32)],
        scratch_shapes=[pltpu.VMEM((t, LANES), F32), pltpu.VMEM((t, hd), F32)],
        compiler_params=_cparams(2), name=name)(qn, kn, qkv)


def _sb_attn_bwd(qn, kn, qkv, do, rsave, name):
    nrow, d, hd, nh, t = _sb_dims(qn)
    scale = 1.0 / math.sqrt(hd)

    def body(q_ref, k_ref, v_ref, do_ref, rs_ref, dq_ref, dk_ref, dv_ref, pg_acc):
        i = pl.program_id(1)

        @pl.when(i == 0)
        def _():
            dk_ref[...] = jnp.zeros_like(dk_ref)
            dv_ref[...] = jnp.zeros_like(dv_ref)

        pg_acc[...] = jnp.zeros_like(pg_acc)
        dq_ref[...] = jnp.zeros_like(dq_ref)
        q = q_ref[...]
        dob = do_ref[...].astype(BF16)
        strict, after, before = _sb_masks(t)
        lane = lax.broadcasted_iota(jnp.int32, (t, LANES), 1)

        def tile(j, mask):
            start = pl.multiple_of(j * t, t)
            k = k_ref[pl.ds(start, t), :]
            v = v_ref[pl.ds(start, t), :].astype(BF16)
            ls, lk = _sb_scores(q, k, mask, scale)
            rj = jnp.sum(jnp.where(lane == j, rs_ref[...], 0.0), axis=1, keepdims=True)
            later, _ = _sb_running(lk, after, jnp.broadcast_to(rj, (t, LANES)), True)
            att = jnp.exp(ls + later)
            if mask is not None:
                att = jnp.where(mask, att, 0.0)
            datt = lax.dot_general(dob, v, (((1,), (1,)), ((), ())), preferred_element_type=F32)
            g = datt * att
            dlk, pg_next = _sb_running(g, before, pg_acc[...], False)
            sig = jnp.exp(ls)
            dz = g * (1.0 - sig) - dlk * sig
            if mask is not None:
                dz = jnp.where(mask, dz, 0.0)
            dz = (dz * scale).astype(BF16)
            dq_ref[...] += jnp.dot(dz, k, preferred_element_type=F32)
            dk_ref[pl.ds(start, t), :] += lax.dot_general(dz, q, (((0,), (0,)), ((), ())), preferred_element_type=F32)
            dv_ref[pl.ds(start, t), :] += lax.dot_general(att.astype(BF16), dob, (((0,), (0,)), ((), ())),
                                                          preferred_element_type=F32)
            pg_acc[...] = pg_next

        def step(j, carry):
            tile(j, None)
            return carry

        lax.fori_loop(0, i, step, 0)
        tile(i, strict)

    return pl.pallas_call(
        body, grid=(nh, nrow // t),
        in_specs=[pl.BlockSpec((t, hd), lambda h, i: (i, h)), pl.BlockSpec((nrow, hd), lambda h, i: (0, h)),
                  pl.BlockSpec((nrow, hd), lambda h, i: (0, 2 * nh + h)), pl.BlockSpec((t, hd), lambda h, i: (i, h)),
                  pl.BlockSpec((None, t, LANES), lambda h, i: (h, i, 0))],
        out_specs=[pl.BlockSpec((t, hd), lambda h, i: (i, h)), pl.BlockSpec((nrow, hd), lambda h, i: (0, h)),
                   pl.BlockSpec((nrow, hd), lambda h, i: (0, h))],
        out_shape=[jax.ShapeDtypeStruct((nrow, d), F32)] * 3, scratch_shapes=[pltpu.VMEM((t, LANES), F32)],
        compiler_params=_cparams(2), name=name)(qn, kn, qkv, do, rsave)


def _ssd_chunk_fn(g, nheads_g, q):
    p = M2_HEAD_DIM
    npair = nheads_g // 2

    def f(xp, bp, cp, dtr, hp, dtb, alog, dsk):
        b = _silu(bp).astype(BF16)
        c = _silu(cp).astype(BF16)
        dt = _softplus(dtr + dtb)
        dta = dt * (-jnp.exp(alog))
        ri = lax.broadcasted_iota(jnp.int32, (q, q), 0)
        ci = lax.broadcasted_iota(jnp.int32, (q, q), 1)
        causal = ri >= ci
        tri = causal.astype(F32)
        acol = jnp.dot(tri, dta, precision=HIGHEST, preferred_element_type=F32)
        arow = lax.dot_general(dta, tri, (((0,), (1,)), ((), ())), precision=HIGHEST,
                               preferred_element_type=F32)
        alast = jnp.sum(dta, axis=0, keepdims=True)
        cb = lax.dot_general(c, b, (((1,), (1,)), ((), ())), preferred_element_type=F32)
        lane = lax.broadcasted_iota(jnp.int32, (1, LANES), 1)
        sub = lax.broadcasted_iota(jnp.int32, (LANES, 1), 0)
        lane_half = lane < p
        sub_half = sub < p

        def col(mat, h):
            return jnp.sum(jnp.where(lane == h, mat, 0.0), axis=1, keepdims=True)

        def row(mat, h):
            return jnp.sum(jnp.where(sub == h, mat, 0.0), axis=0, keepdims=True)

        def mix(v0, v1):
            return jnp.where(lane_half, v0, v1)

        ys, hns = [], []
        for jp in range(npair):
            h0 = g * nheads_g + 2 * jp
            h1 = h0 + 1
            x = _silu(xp[jp])
            ac0, ac1 = col(acol, h0), col(acol, h1)
            xdt = x * mix(col(dt, h0), col(dt, h1))
            xdtb = xdt.astype(BF16)
            yd = []
            for h, ac in ((h0, ac0), (h1, ac1)):
                dec = jnp.exp(jnp.where(causal, ac - row(arow, h), -jnp.inf))
                yd.append(jnp.dot((cb * dec).astype(BF16), xdtb, preferred_element_type=F32))
            hpj = hp[jp]
            yo = lax.dot_general(c, hpj.astype(BF16), (((1,), (1,)), ((), ())), preferred_element_type=F32)
            y = mix(yd[0], yd[1]) + yo * jnp.exp(mix(ac0, ac1)) + mix(col(dsk, h0), col(dsk, h1)) * x
            al0, al1 = col(alast, h0), col(alast, h1)
            dout = jnp.exp(mix(al0 - ac0, al1 - ac1))
            st = lax.dot_general((xdt * dout).astype(BF16), b, (((0,), (0,)), ((), ())), preferred_element_type=F32)
            hns.append(hpj * jnp.exp(jnp.where(sub_half, al0, al1)) + st)
            ys.append(y)
        return ys, hns

    return f


def _ssd_dims(pre, dtr):
    nrow = pre.shape[0]
    ng, n, p, q = M2_GROUPS, M2_STATE, M2_HEAD_DIM, M2_CHUNK
    d_inner = pre.shape[1] - 2 * ng * n
    gw = d_inner // ng
    nhg = gw // p
    assert nhg % 2 == 0 and gw % LANES == 0 and n == LANES and ng * nhg <= LANES and dtr.shape[1] == LANES
    return nrow, ng, n, q, d_inner, gw, nhg, nrow // q


def _ssd_fwd(pre, dtr, dtb, alog, dsk, name):
    nrow, ng, n, q, d_inner, gw, nhg, nc = _ssd_dims(pre, dtr)
    boff = d_inner // n
    npair = nhg // 2

    def body(x_ref, b_ref, c_ref, dt_ref, dtb_ref, al_ref, ds_ref, y_ref, hs_ref, state):
        ci, g = pl.program_id(0), pl.program_id(1)

        @pl.when(ci == 0)
        def _():
            state[g] = jnp.zeros((gw, n), F32)

        hs_ref[...] = state[g]
        f = _ssd_chunk_fn(g, nhg, q)
        xp = [x_ref[:, LANES * j:LANES * (j + 1)] for j in range(npair)]
        hp = [state[g, LANES * j:LANES * (j + 1), :] for j in range(npair)]
        ys, hns = f(xp, b_ref[...], c_ref[...], dt_ref[...], hp, dtb_ref[...], al_ref[...], ds_ref[...])
        for j in range(npair):
            y_ref[:, LANES * j:LANES * (j + 1)] = ys[j]
            state[g, LANES * j:LANES * (j + 1), :] = hns[j]

    par = pl.BlockSpec((1, LANES), lambda ci, g: (0, 0))
    return pl.pallas_call(
        body, grid=(nc, ng),
        in_specs=[pl.BlockSpec((q, gw), lambda ci, g: (ci, g)), pl.BlockSpec((q, n), lambda ci, g: (ci, boff + g)),
                  pl.BlockSpec((q, n), lambda ci, g: (ci, boff + ng + g)), pl.BlockSpec((q, LANES), lambda ci, g: (ci, 0)),
                  par, par, par],
        out_specs=[pl.BlockSpec((q, gw), lambda ci, g: (ci, g)),
                   pl.BlockSpec((None, None, gw, n), lambda ci, g: (ci, g, 0, 0))],
        out_shape=[jax.ShapeDtypeStruct((nrow, d_inner), F32), jax.ShapeDtypeStruct((nc, ng, gw, n), F32)],
        scratch_shapes=[pltpu.VMEM((ng, gw, n), F32)], compiler_params=_cparams(2), name=name)(
            pre, pre, pre, dtr, dtb, alog, dsk)


def _ssd_bwd(pre, dtr, hs, dy, dtb, alog, dsk, name):
    nrow, ng, n, q, d_inner, gw, nhg, nc = _ssd_dims(pre, dtr)
    boff = d_inner // n
    npair = nhg // 2

    def body(x_ref, b_ref, c_ref, dt_ref, hs_ref, dy_ref, dtb_ref, al_ref, ds_ref,
             dx_ref, db_ref, dc_ref, ddt_ref, ddtb_ref, dal_ref, dds_ref, dstate):
        ci, g = pl.program_id(0), pl.program_id(1)

        @pl.when(ci == 0)
        def _():
            dstate[g] = jnp.zeros((gw, n), F32)

        @pl.when((ci == 0) & (g == 0))
        def _():
            ddtb_ref[...] = jnp.zeros_like(ddtb_ref)
            dal_ref[...] = jnp.zeros_like(dal_ref)
            dds_ref[...] = jnp.zeros_like(dds_ref)

        @pl.when(g == 0)
        def _():
            ddt_ref[...] = jnp.zeros_like(ddt_ref)

        f = _ssd_chunk_fn(g, nhg, q)
        xp = [x_ref[:, LANES * j:LANES * (j + 1)] for j in range(npair)]
        hp = [hs_ref[LANES * j:LANES * (j + 1), :] for j in range(npair)]
        _, vjp = jax.vjp(f, xp, b_ref[...], c_ref[...], dt_ref[...], hp, dtb_ref[...], al_ref[...], ds_ref[...])
        dys = [dy_ref[:, LANES * j:LANES * (j + 1)] for j in range(npair)]
        dhn = [dstate[g, LANES * j:LANES * (j + 1), :] for j in range(npair)]
        dxp, db, dc, ddt, dhp, ddtb, dal, dds = vjp((dys, dhn))
        for j in range(npair):
            dx_ref[:, LANES * j:LANES * (j + 1)] = dxp[j]
            dstate[g, LANES * j:LANES * (j + 1), :] = dhp[j]
        db_ref[...] = db
        dc_ref[...] = dc
        ddt_ref[...] += ddt
        ddtb_ref[...] += ddtb
        dal_ref[...] += dal
        dds_ref[...] += dds

    par = pl.BlockSpec((1, LANES), lambda ci, g: (0, 0))
    last = nc - 1
    return pl.pallas_call(
        body, grid=(nc, ng),
        in_specs=[pl.BlockSpec((q, gw), lambda ci, g: (last - ci, g)),
                  pl.BlockSpec((q, n), lambda ci, g: (last - ci, boff + g)),
                  pl.BlockSpec((q, n), lambda ci, g: (last - ci, boff + ng + g)),
                  pl.BlockSpec((q, LANES), lambda ci, g: (last - ci, 0)),
                  pl.BlockSpec((None, None, gw, n), lambda ci, g: (last - ci, g, 0, 0)),
                  pl.BlockSpec((q, gw), lambda ci, g: (last - ci, g)), par, par, par],
        out_specs=[pl.BlockSpec((q, gw), lambda ci, g: (last - ci, g)), pl.BlockSpec((q, n), lambda ci, g: (last - ci, g)),
                   pl.BlockSpec((q, n), lambda ci, g: (last - ci, g)), pl.BlockSpec((q, LANES), lambda ci, g: (last - ci, 0)),
                   par, par, par],
        out_shape=[jax.ShapeDtypeStruct((nrow, d_inner), F32), jax.ShapeDtypeStruct((nrow, ng * n), F32),
                   jax.ShapeDtypeStruct((nrow, ng * n), F32), jax.ShapeDtypeStruct((nrow, LANES), F32),
                   jax.ShapeDtypeStruct((1, LANES), F32), jax.ShapeDtypeStruct((1, LANES), F32),
                   jax.ShapeDtypeStruct((1, LANES), F32)],
        scratch_shapes=[pltpu.VMEM((ng, gw, n), F32)], compiler_params=_cparams(2), name=name)(
            pre, pre, pre, dtr, hs, dy, dtb, alog, dsk)


HBM_SPEC = pl.BlockSpec(memory_space=pl.ANY)


def _xy_peers(mx, my):
    return [(1 - mx, my), (mx, 1 - my), (1 - mx, 1 - my)]


def _remote(src, dst, send_sems, recv_sems, k, dev):
    return pltpu.make_async_remote_copy(src_ref=src, dst_ref=dst, send_sem=send_sems.at[k], recv_sem=recv_sems.at[k],
                                        device_id=dev, device_id_type=MESH_ID)


def _all_gather_xy(buf, name):
    npeer = N_XY - 1

    def body(src, out, send_sems, recv_sems, local_sem):
        mx, my, mc = lax.axis_index("x"), lax.axis_index("y"), lax.axis_index("c")
        me = 2 * mx + my
        sibling = (mx, my, 1 - mc)
        peers = _xy_peers(mx, my)
        mine, theirs = mc, 1 - mc
        local = pltpu.make_async_copy(src, out.at[me], local_sem)
        local.start()
        sends = [_remote(src.at[mine], out.at[me, mine], send_sems, recv_sems, k, (px, py, mc))
                 for k, (px, py) in enumerate(peers)]
        for cp in sends:
            cp.start()
        for k, (px, py) in enumerate(peers):
            landed = out.at[2 * px + py, mine]
            _remote(src.at[mine], landed, send_sems, recv_sems, k, (px, py, mc)).wait_recv()
            cp = _remote(landed, landed, send_sems, recv_sems, npeer + k, sibling)
            cp.start()
            sends.append(cp)
        for k, (px, py) in enumerate(peers):
            passed = out.at[2 * px + py, theirs]
            _remote(passed, passed, send_sems, recv_sems, npeer + k, sibling).wait_recv()
        for cp in sends:
            cp.wait_send()
        local.wait()

    return pl.pallas_call(
        body, out_shape=jax.ShapeDtypeStruct((N_XY,) + buf.shape, buf.dtype), in_specs=[HBM_SPEC], out_specs=HBM_SPEC,
        scratch_shapes=[pltpu.SemaphoreType.DMA((2 * npeer,)), pltpu.SemaphoreType.DMA((2 * npeer,)),
                        pltpu.SemaphoreType.DMA], name=name)(buf)


def _exchange_xy(parts, name):
    def body(src, out, send_sems, recv_sems, local_sem):
        mx, my, mc = lax.axis_index("x"), lax.axis_index("y"), lax.axis_index("c")
        me = 2 * mx + my
        peers = _xy_peers(mx, my)
        local = pltpu.make_async_copy(src.at[me], out.at[me], local_sem)
        local.start()
        sends = [_remote(src.at[2 * px + py], out.at[me], send_sems, recv_sems, k, (px, py, mc))
                 for k, (px, py) in enumerate(peers)]
        for cp in sends:
            cp.start()
        for k, (px, py) in enumerate(peers):
            _remote(src.at[me], out.at[2 * px + py], send_sems, recv_sems, k, (px, py, mc)).wait_recv()
        for cp in sends:
            cp.wait_send()
        local.wait()

    return pl.pallas_call(
        body, out_shape=jax.ShapeDtypeStruct(parts.shape, parts.dtype), in_specs=[HBM_SPEC], out_specs=HBM_SPEC,
        scratch_shapes=[pltpu.SemaphoreType.DMA((N_XY - 1,)), pltpu.SemaphoreType.DMA((N_XY - 1,)),
                        pltpu.SemaphoreType.DMA], name=name)(parts)


def _pair_split(parts, name):
    nparts, _, nrow, width = parts.shape

    def body(src, kept, got, send_sems, recv_sems, local_sems):
        mx, my, mc = lax.axis_index("x"), lax.axis_index("y"), lax.axis_index("c")
        copies = []
        for s in range(nparts):
            local = pltpu.make_async_copy(src.at[s, mc], kept.at[s], local_sems.at[s])
            local.start()
            cp = _remote(src.at[s, 1 - mc], got.at[s], send_sems, recv_sems, s, (mx, my, 1 - mc))
            cp.start()
            copies += [local, cp]
        for cp in copies:
            cp.wait()

    shape = jax.ShapeDtypeStruct((nparts, nrow, width), parts.dtype)
    return pl.pallas_call(
        body, out_shape=[shape, shape], in_specs=[HBM_SPEC], out_specs=[HBM_SPEC, HBM_SPEC],
        scratch_shapes=[pltpu.SemaphoreType.DMA((nparts,)), pltpu.SemaphoreType.DMA((nparts,)),
                        pltpu.SemaphoreType.DMA((nparts,))], name=name)(parts)


def _pair_join(mine, name):
    def body(src, out, send_sems, recv_sems, local_sem):
        mx, my, mc = lax.axis_index("x"), lax.axis_index("y"), lax.axis_index("c")
        local = pltpu.make_async_copy(src, out.at[mc], local_sem)
        local.start()
        cp = _remote(src, out.at[mc], send_sems, recv_sems, 0, (mx, my, 1 - mc))
        cp.start()
        cp.wait_send()
        _remote(src, out.at[1 - mc], send_sems, recv_sems, 0, (mx, my, 1 - mc)).wait_recv()
        local.wait()

    return pl.pallas_call(
        body, out_shape=jax.ShapeDtypeStruct((2,) + mine.shape, mine.dtype), in_specs=[HBM_SPEC], out_specs=HBM_SPEC,
        scratch_shapes=[pltpu.SemaphoreType.DMA((1,)), pltpu.SemaphoreType.DMA((1,)), pltpu.SemaphoreType.DMA],
        name=name)(mine)


def _all_gather_all(buf, name):
    flips = [(fx, fy, fc) for fx in (0, 1) for fy in (0, 1) for fc in (0, 1) if fx or fy or fc]

    def body(src, out, send_sems, recv_sems, local_sem):
        mx, my, mc = lax.axis_index("x"), lax.axis_index("y"), lax.axis_index("c")
        me = 4 * mx + 2 * my + mc
        peers = [(1 - mx if fx else mx, 1 - my if fy else my, 1 - mc if fc else mc) for fx, fy, fc in flips]
        local = pltpu.make_async_copy(src, out.at[me], local_sem)
        local.start()
        sends = [_remote(src, out.at[me], send_sems, recv_sems, k, dev) for k, dev in enumerate(peers)]
        for cp in sends:
            cp.start()
        for k, (px, py, pc) in enumerate(peers):
            _remote(src, out.at[4 * px + 2 * py + pc], send_sems, recv_sems, k, (px, py, pc)).wait_recv()
        for cp in sends:
            cp.wait_send()
        local.wait()

    return pl.pallas_call(
        body, out_shape=jax.ShapeDtypeStruct((N_DEV,) + buf.shape, buf.dtype), in_specs=[HBM_SPEC], out_specs=HBM_SPEC,
        scratch_shapes=[pltpu.SemaphoreType.DMA((N_DEV - 1,)), pltpu.SemaphoreType.DMA((N_DEV - 1,)),
                        pltpu.SemaphoreType.DMA], name=name)(buf)


def _flat_tile(nrow, width, narrays):
    t = nrow
    for cand in (512, 256, 128, 64, 32, 16, 8):
        if nrow % cand == 0:
            t = cand
            if cand * width * 4 * narrays <= ROW_BLOCK_BYTES:
                break
    return t


def _sum_parts(parts, name):
    kparts, nrow, width = parts.shape
    t = _flat_tile(nrow, width, kparts + 1)

    def body(p_ref, o_ref):
        s = p_ref[0].astype(F32)
        for k in range(1, kparts):
            s = s + p_ref[k].astype(F32)
        o_ref[...] = s

    return pl.pallas_call(
        body, grid=(nrow // t,), in_specs=[pl.BlockSpec((kparts, t, width), lambda i: (0, i, 0))],
        out_specs=pl.BlockSpec((t, width), lambda i: (i, 0)), out_shape=jax.ShapeDtypeStruct((nrow, width), F32),
        compiler_params=_cparams(1), name=name)(parts)


def _add2(a, b, name):
    nparts, nrow, width = a.shape
    t = _flat_tile(nrow, width, 3)

    def body(a_ref, b_ref, o_ref):
        o_ref[...] = (a_ref[...].astype(F32) + b_ref[...].astype(F32)).astype(o_ref.dtype)

    spec = pl.BlockSpec((None, t, width), lambda p, i: (p, i, 0))
    return pl.pallas_call(body, grid=(nparts, nrow // t), in_specs=[spec, spec], out_specs=spec,
                          out_shape=jax.ShapeDtypeStruct(a.shape, a.dtype), compiler_params=_cparams(2), name=name)(a, b)


def _adamw(w, m, v, ga, gb, name):
    nrow, width = w.shape
    t = _flat_tile(nrow, width, 9)
    c1 = 1.0 - ADAM_B1 ** ADAM_STEP
    c2 = 1.0 - ADAM_B2 ** ADAM_STEP
    grads = [ga] if gb is None else [ga, gb]
    nout = 3 if gb is None else 4

    def body(w_ref, m_ref, v_ref, *refs):
        d_ref, mo_ref, vo_ref = refs[-3:]
        g = refs[0][...]
        if gb is not None:
            g = g + refs[1][...]
            refs[2][...] = g
        mn = ADAM_B1 * m_ref[...] + (1.0 - ADAM_B1) * g
        vn = ADAM_B2 * v_ref[...] + (1.0 - ADAM_B2) * jnp.square(g)
        mo_ref[...] = mn
        vo_ref[...] = vn
        d_ref[...] = -ADAM_LR * ((mn / c1) / (jnp.sqrt(vn / c2) + ADAM_EPS) + ADAM_WD * w_ref[...])

    spec = pl.BlockSpec((t, width), lambda i: (i, 0))
    return pl.pallas_call(
        body, grid=(nrow // t,), in_specs=[spec] * (3 + len(grads)), out_specs=[spec] * nout,
        out_shape=[jax.ShapeDtypeStruct((nrow, width), F32)] * nout, compiler_params=_cparams(1), name=name)(
            w, m, v, *grads)


def _pack(arrs, dtype, width):
    flat = jnp.concatenate([a.astype(dtype).reshape(-1) for a in arrs])
    quantum = PACK_ROWS * width
    pad = (-flat.shape[0]) % quantum
    if pad:
        flat = jnp.concatenate([flat, jnp.zeros((pad,), dtype)])
    return flat.reshape(-1, width)


def _unpack(buf, shapes):
    flat = buf.reshape(-1)
    out, off = [], 0
    for s in shapes:
        size = math.prod(s)
        out.append(flat[off:off + size].reshape(s))
        off += size
    return out


def _qk_norm_fwd(qkv, w, off, nh, tag):
    f = _rms_f(RMS_EPS)
    return _rowwise(lambda xv, wv: (f(xv, wv),), [(qkv, SB_HEAD_DIM, off)], [(w, False)],
                    [('row', nh * SB_HEAD_DIM, SB_HEAD_DIM, BF16)], ncb=nh, name=tag)[0]


def _qk_norm_bwd(qkv, w, off, dn, nh, tag):
    f = _rms_f(RMS_EPS)

    def fn(xv, dv, wv):
        _, vjp = jax.vjp(f, xv, wv)
        return vjp(dv)

    return _rowwise(fn, [(qkv, SB_HEAD_DIM, off), (dn, SB_HEAD_DIM, 0)], [(w, False)],
                    [('row', nh * SB_HEAD_DIM, SB_HEAD_DIM, F32), ('acc', (1, SB_HEAD_DIM), False)], ncb=nh, name=tag)


def _sb_forward(h, xin, wt, tag):
    d = xin.shape[1]
    nh = d // SB_HEAD_DIM
    qkv = _mm(h, wt['w_qkv'], "nn", name=tag + "_qkv")
    qn = _qk_norm_fwd(qkv, wt['q_norm_w'], 0, nh, tag + "_qnorm")
    kn = _qk_norm_fwd(qkv, wt['k_norm_w'], nh, nh, tag + "_knorm")
    o, rsave = _sb_attn_fwd(qn, kn, qkv, tag + "_attn")
    xm = _mm(o, wt['w_o'], "nn", extras=(xin,), epilogue=_ep_add, name=tag + "_out")
    return xm, dict(qkv=qkv, qn=qn, kn=kn, o=o, rsave=rsave)


def _sb_backward(h, dx, dxb, wt, sv, tag):
    nh = dx.shape[1] // SB_HEAD_DIM
    do = _mm(dxb, wt['w_o'], "nt", name=tag + "_do")
    g_wo = _mm(sv['o'], dxb, "tn", name=tag + "_dwo")
    dqn, dkn, dv = _sb_attn_bwd(sv['qn'], sv['kn'], sv['qkv'], do, sv['rsave'], tag + "_attn_bwd")
    dq, g_qw = _qk_norm_bwd(sv['qkv'], wt['q_norm_w'], 0, dqn, nh, tag + "_qnorm_bwd")
    dk, g_kw = _qk_norm_bwd(sv['qkv'], wt['k_norm_w'], nh, dkn, nh, tag + "_knorm_bwd")
    dqkv = jnp.concatenate([dq, dk, dv], axis=1)
    g_wqkv = _mm(h, dqkv, "tn", name=tag + "_dwqkv")
    dh = _mm(dqkv, wt['w_qkv'], "nt", name=tag + "_dh")
    return dh, dict(w_qkv=g_wqkv, w_o=g_wo, q_norm_w=g_qw, k_norm_w=g_kw)


def _ln_silu_f(x, w, b):
    mu = jnp.mean(x, axis=-1, keepdims=True)
    xc = x - mu
    return _silu(xc * lax.rsqrt(jnp.mean(xc * xc, axis=-1, keepdims=True) + LN_EPS) * w + b)


def _glu_bwd_fn(val, gate, dg):
    sg = 1.0 / (1.0 + jnp.exp(-gate))
    return (jnp.concatenate([dg * sg, dg * val * sg * (1.0 - sg)], axis=1),)


def _cf_forward(h, xin, wt, tag):
    d = xin.shape[1]
    u = _mm(h, wt['w_in'], "nn", extras=(wt['b_in'],), epilogue=_ep_add, name=tag + "_in")
    gl = _rowwise(lambda val, gate: (val / (1.0 + jnp.exp(-gate)),), [(u, d, 0), (u, d, 1)], [],
                  [('row', d, d, F32)], name=tag + "_glu")[0]
    cv = _dwconv_fwd(gl, wt['dw_w'], wt['dw_b'], tag + "_conv")
    s = _rowwise(lambda x, w, b: (_ln_silu_f(x, w, b),), [(cv, d, 0)], [(wt['ln_w'], False), (wt['ln_b'], False)],
                 [('row', d, d, BF16)], name=tag + "_ln")[0]
    xm = _mm(s, wt['w_out'], "nn", extras=(wt['b_out'], xin), epilogue=_ep_bias_add, name=tag + "_out")
    return xm, dict(u=u, gl=gl, cv=cv, s=s)


def _cf_backward(h, dx, dxb, wt, sv, tag):
    d = dx.shape[1]
    ds = _mm(dxb, wt['w_out'], "nt", name=tag + "_ds")
    g_wout = _mm(sv['s'], dxb, "tn", name=tag + "_dwout")
    g_bout = _colsum(dx, tag + "_dbout")

    def ln_bwd(x, dsv, w, b):
        _, vjp = jax.vjp(_ln_silu_f, x, w, b)
        return vjp(dsv)

    dcv, g_lnw, g_lnb = _rowwise(ln_bwd, [(sv['cv'], d, 0), (ds, d, 0)], [(wt['ln_w'], False), (wt['ln_b'], False)],
                                 [('row', d, d, F32), ('acc', (1, d), False), ('acc', (1, d), False)],
                                 name=tag + "_ln_bwd")
    dgl, g_dww, g_dwb = _dwconv_bwd(sv['gl'], dcv, wt['dw_w'], tag + "_conv_bwd")
    du = _rowwise(_glu_bwd_fn, [(sv['u'], d, 0), (sv['u'], d, 1), (dgl, d, 0)], [], [('row', 2 * d, 2 * d, F32)],
                  name=tag + "_glu_bwd")[0]
    g_bin = _colsum(du, tag + "_dbin")
    g_win = _mm(h, du, "tn", name=tag + "_dwin")
    dh = _mm(du, wt['w_in'], "nt", name=tag + "_dh")
    return dh, dict(w_in=g_win, b_in=g_bin, dw_w=g_dww, dw_b=g_dwb, ln_w=g_lnw, ln_b=g_lnb, w_out=g_wout, b_out=g_bout)


def _gated_norm_f(y, z, w):
    t = y * _silu(z)
    return t * lax.rsqrt(jnp.mean(t * t, axis=-1, keepdims=True) + M2_NORM_EPS) * w


def _m2_forward(h, xin, wt, tag):
    z = _mm(h, wt['w_z'], "nn", name=tag + "_z")
    xbc = _mm(h, wt['w_xbc'], "nn", name=tag + "_xbc")
    dtr = _mm(h, wt['w_dt'], "nn", name=tag + "_dt")
    pre = _dwconv_fwd(xbc, wt['conv_w'], wt['conv_b'], tag + "_conv")
    y, hs = _ssd_fwd(pre, dtr, wt['dt_bias'], wt['a_log'], wt['d'], tag + "_ssd")
    d_inner = y.shape[1]
    gw = d_inner // M2_GROUPS
    yn = _rowwise(lambda yv, zv, w: (_gated_norm_f(yv, zv, w),), [(y, gw, 0), (z, gw, 0)], [(wt['norm_w'], True)],
                  [('row', d_inner, gw, BF16)], ncb=M2_GROUPS, name=tag + "_gnorm")[0]
    xm = _mm(yn, wt['w_out'], "nn", extras=(xin,), epilogue=_ep_add, name=tag + "_out")
    return xm, dict(z=z, xbc=xbc, dtr=dtr, pre=pre, y=y, hs=hs, yn=yn)


def _m2_backward(h, dx, dxb, wt, sv, tag):
    dyn = _mm(dxb, wt['w_out'], "nt", name=tag + "_dyn")
    g_wout = _mm(sv['yn'], dxb, "tn", name=tag + "_dwout")
    d_inner = sv['y'].shape[1]
    gw = d_inner // M2_GROUPS

    def gn_bwd(yv, zv, dv, w):
        _, vjp = jax.vjp(_gated_norm_f, yv, zv, w)
        return vjp(dv)

    dy, dz, g_nw = _rowwise(gn_bwd, [(sv['y'], gw, 0), (sv['z'], gw, 0), (dyn, gw, 0)], [(wt['norm_w'], True)],
                            [('row', d_inner, gw, F32), ('row', d_inner, gw, F32), ('acc', (1, d_inner), True)],
                            ncb=M2_GROUPS, name=tag + "_gnorm_bwd")
    dxp, db, dc, ddt, g_dtb, g_alog, g_d = _ssd_bwd(sv['pre'], sv['dtr'], sv['hs'], dy, wt['dt_bias'], wt['a_log'],
                                                   wt['d'], tag + "_ssd_bwd")
    dpre = jnp.concatenate([dxp, db, dc], axis=1)
    dxbc, g_cw, g_cb = _dwconv_bwd(sv['xbc'], dpre, wt['conv_w'], tag + "_conv_bwd")
    g_wz = _mm(h, dz, "tn", name=tag + "_dwz")
    g_wxbc = _mm(h, dxbc, "tn", name=tag + "_dwxbc")
    g_wdt = _mm(h, ddt, "tn", name=tag + "_dwdt")
    dh = _mm(dz, wt['w_z'], "nt", name=tag + "_dh_z")
    dh = _mm(dxbc, wt['w_xbc'], "nt", extras=(dh,), epilogue=_ep_add, name=tag + "_dh_xbc")
    dh = _mm(ddt, wt['w_dt'], "nt", extras=(dh,), epilogue=_ep_add, name=tag + "_dh_dt")
    return dh, dict(w_z=g_wz, w_xbc=g_wxbc, w_dt=g_wdt, conv_w=g_cw, conv_b=g_cb, dt_bias=g_dtb, a_log=g_alog, d=g_d,
                    norm_w=g_nw, w_out=g_wout)


def _pad_lanes(v):
    return jnp.pad(v.reshape(1, -1), ((0, 0), (0, LANES - v.shape[0])))


def kernel(x, norm_mix_w, norm_mlp_w, sb_w_qkv, sb_q_norm_w, sb_k_norm_w, sb_w_o, cf_w_in, cf_b_in, cf_dw_w, cf_dw_b, cf_ln_w, cf_ln_b, cf_w_out, cf_b_out, m2_w_in, m2_conv_w, m2_conv_b, m2_dt_bias, m2_a_log, m2_d, m2_norm_w, m2_w_out, mlp_w_up, mlp_w_down, loss_target, m_norm_mix_w, m_norm_mlp_w, m_sb_w_qkv, m_sb_q_norm_w, m_sb_k_norm_w, m_sb_w_o, m_cf_w_in, m_cf_b_in, m_cf_dw_w, m_cf_dw_b, m_cf_ln_w, m_cf_ln_b, m_cf_w_out, m_cf_b_out, m_m2_w_in, m_m2_conv_w, m_m2_conv_b, m_m2_dt_bias, m_m2_a_log, m_m2_d, m_m2_norm_w, m_m2_w_out, m_mlp_w_up, m_mlp_w_down, v_norm_mix_w, v_norm_mlp_w, v_sb_w_qkv, v_sb_q_norm_w, v_sb_k_norm_w, v_sb_w_o, v_cf_w_in, v_cf_b_in, v_cf_dw_w, v_cf_dw_b, v_cf_ln_w, v_cf_ln_b, v_cf_w_out, v_cf_b_out, v_m2_w_in, v_m2_conv_w, v_m2_conv_b, v_m2_dt_bias, v_m2_a_log, v_m2_d, v_m2_norm_w, v_m2_w_out, v_mlp_w_up, v_mlp_w_down):
    given = dict(locals())
    xl = x[0]
    tgt = loss_target[0]
    d = xl.shape[1]
    depth = norm_mix_w.shape[0]
    bulk_names, small_names = list(BULK), list(SMALL)

    def halves(buf):
        return buf.reshape(2, -1, buf.shape[-1])

    gath_b = _all_gather_xy(halves(_pack([given[n] for n in bulk_names], BF16, PACK_W)), "comm_gather_bulk")
    gath_s = _all_gather_xy(halves(_pack([given[n] for n in small_names], F32, PACK_W)), "comm_gather_small")
    full = {}
    for names, gath, axes in ((bulk_names, gath_b, BULK), (small_names, gath_s, SMALL)):
        pieces = [_unpack(gath[s], [given[n].shape for n in names]) for s in range(N_XY)]
        for idx, n in enumerate(names):
            full[n] = jnp.concatenate([pieces[s][idx] for s in range(N_XY)], axis=axes[n])

    def row(v):
        return v.reshape(1, -1)

    d_inner = full['m2_w_out'].shape[1]
    conv_dim = full['m2_conv_w'].shape[2]
    nheads = m2_dt_bias.shape[1]

    def layer_weights(i):
        kind, j = i % 3, i // 3
        if kind == 0:
            return dict(w_qkv=full['sb_w_qkv'][j], w_o=full['sb_w_o'][j], q_norm_w=row(sb_q_norm_w[j]),
                        k_norm_w=row(sb_k_norm_w[j]))
        if kind == 1:
            return dict(w_in=full['cf_w_in'][j], b_in=row(cf_b_in[j]), dw_w=full['cf_dw_w'][j], dw_b=row(cf_dw_b[j]),
                        ln_w=row(cf_ln_w[j]), ln_b=row(cf_ln_b[j]), w_out=full['cf_w_out'][j], b_out=row(cf_b_out[j]))
        w_in = full['m2_w_in'][j]
        w_dt = jnp.pad(w_in[:, d_inner + conv_dim:], ((0, 0), (0, LANES - nheads)))
        return dict(w_z=w_in[:, :d_inner], w_xbc=w_in[:, d_inner:d_inner + conv_dim], w_dt=w_dt,
                    conv_w=full['m2_conv_w'][j], conv_b=row(full['m2_conv_b'][j]), dt_bias=_pad_lanes(m2_dt_bias[j]),
                    a_log=_pad_lanes(m2_a_log[j]), d=_pad_lanes(m2_d[j]), norm_w=row(full['m2_norm_w'][j]),
                    w_out=full['m2_w_out'][j])

    fwd = (_sb_forward, _cf_forward, _m2_forward)
    bwd = (_sb_backward, _cf_backward, _m2_backward)

    saved = []
    xc = xl
    for i in range(depth):
        wt = layer_weights(i)
        h = _rms_fwd(xc, row(norm_mix_w[i]), f"l{i}_norm_mix")
        xm, sv = fwd[i % 3](h, xc, wt, f"l{i}_mix")
        h2 = _rms_fwd(xm, row(norm_mlp_w[i]), f"l{i}_norm_mlp")
        u, a = _mm(h2, full['mlp_w_up'][i], "nn", epilogue=_ep_relu2, out_dtypes=(F32, BF16), name=f"l{i}_up")
        xn = _mm(a, full['mlp_w_down'][i], "nn", extras=(xm,), epilogue=_ep_add, name=f"l{i}_down")
        saved.append(dict(wt=wt, x_in=xc, h=h, mix=sv, x_mid=xm, h2=h2, u=u, a=a))
        xc = xn
    dx, dxb, loss_acc = _loss(xc, tgt, "loss")
    loss = lax.psum(loss_acc[0, 0], ("x", "y", "c"))

    grads = {n: [None] * given[n].shape[0] for n in W_NAMES}
    for i in reversed(range(depth)):
        sv = saved[i]
        kind, j = i % 3, i // 3
        du = _mm(dxb, full['mlp_w_down'][i], "nt", extras=(sv['u'],), epilogue=_ep_relu2_bwd, out_dtypes=(BF16,),
                 name=f"l{i}_du")
        grads['mlp_w_down'][i] = _mm(sv['a'], dxb, "tn", name=f"l{i}_dwdown")
        grads['mlp_w_up'][i] = _mm(sv['h2'], du, "tn", name=f"l{i}_dwup")
        dh2 = _mm(du, full['mlp_w_up'][i], "nt", name=f"l{i}_dh2")
        dxm, dxmb, g_n2 = _rms_bwd(sv['x_mid'], row(norm_mlp_w[i]), dh2, dx, f"l{i}_norm_mlp_bwd")
        grads['norm_mlp_w'][i] = g_n2[0]
        dh, gw = bwd[kind](sv['h'], dxm, dxmb, sv['wt'], sv['mix'], f"l{i}_mix")
        dx, dxb, g_n1 = _rms_bwd(sv['x_in'], row(norm_mix_w[i]), dh, dxm, f"l{i}_norm_mix_bwd")
        grads['norm_mix_w'][i] = g_n1[0]
        if kind == 0:
            grads['sb_w_qkv'][j], grads['sb_w_o'][j] = gw['w_qkv'], gw['w_o']
            grads['sb_q_norm_w'][j], grads['sb_k_norm_w'][j] = gw['q_norm_w'][0], gw['k_norm_w'][0]
        elif kind == 1:
            for n in ('w_in', 'dw_w', 'w_out'):
                grads['cf_' + n][j] = gw[n]
            for n in ('b_in', 'dw_b', 'ln_w', 'ln_b', 'b_out'):
                grads['cf_' + n][j] = gw[n][0]
        else:
            grads['m2_w_in'][j] = jnp.concatenate([gw['w_z'], gw['w_xbc'], gw['w_dt'][:, :nheads]], axis=1)
            grads['m2_conv_w'][j], grads['m2_w_out'][j] = gw['conv_w'], gw['w_out']
            grads['m2_conv_b'][j], grads['m2_norm_w'][j] = gw['conv_b'][0], gw['norm_w'][0]
            for n in ('dt_bias', 'a_log', 'd'):
                grads['m2_' + n][j] = gw[n][0, :nheads]
    grad_x = dx[None]
    gfull = {n: jnp.stack(grads[n]) for n in W_NAMES}

    outs = {}
    for names, axes, dtype, tag in ((bulk_names, BULK, BF16, "bulk"), (small_names, SMALL, F32, "small")):
        split = {n: jnp.split(gfull[n], N_XY, axis=axes[n]) for n in names}
        parts = jnp.stack([halves(_pack([split[n][s] for n in names], dtype, PACK_W)) for s in range(N_XY)])
        kept, got = _pair_split(parts, f"comm_split_{tag}")
        pair = _add2(kept, got, f"sum_pair_{tag}")
        mine = _sum_parts(_exchange_xy(pair, f"comm_exchange_{tag}"), f"sum_{tag}")
        gsum = _pair_join(mine, f"comm_join_{tag}")
        for n, g in zip(names, _unpack(gsum, [given[n].shape for n in names])):
            shape = given[n].shape
            res = _adamw(*[given[p + n].reshape(-1, shape[-1]) for p in ("", "m_", "v_")], g.reshape(-1, shape[-1]),
                         None, f"adamw_{n}")
            outs["grad", n] = g
            for kind, arr in zip(("delta", "new_m", "new_v"), res):
                outs[kind, n] = arr.reshape(shape)
    rparts = _all_gather_all(_pack([gfull[n] for n in REPL], F32, LANES), "comm_gather_repl")
    nsplit = N_DEV // 2
    res = _adamw(*[_pack([given[p + n] for n in REPL], F32, LANES) for p in ("", "m_", "v_")],
                 _sum_parts(rparts[:nsplit], "sum_repl_a"), _sum_parts(rparts[nsplit:], "sum_repl_b"), "adamw_repl")
    shapes = [given[n].shape for n in REPL]
    for kind, buf in zip(("grad", "delta", "new_m", "new_v"), res):
        for n, arr in zip(REPL, _unpack(buf, shapes)):
            outs[kind, n] = arr

    return (loss, grad_x, *[outs[kind, n] for kind in ("grad", "delta", "new_m", "new_v") for n in W_NAMES])
```

```python
import math

import jax
import jax.numpy as jnp
from jax import lax
from jax.experimental import pallas as pl
from jax.experimental.pallas import tpu as pltpu

F32 = jnp.float32
BF16 = jnp.bfloat16
MESH_ID = pl.DeviceIdType.MESH
HIGHEST = lax.Precision.HIGHEST

SB_HEAD_DIM = 128
M2_HEAD_DIM = 64
M2_STATE = 128
M2_GROUPS = 8
M2_CHUNK = 128
RMS_EPS = 1e-6
LN_EPS = 1e-5
M2_NORM_EPS = 1e-5
ADAM_LR = 0.001
ADAM_B1 = 0.9
ADAM_B2 = 0.999
ADAM_EPS = 1e-08
ADAM_WD = 0.01
ADAM_STEP = 10

N_XY = 4
N_DEV = 8

V7X_VMEM_BYTES = 64 * 2**20
VMEM_LIMIT = (V7X_VMEM_BYTES * 3) // 4
LANES = 128
ROW_BLOCK_BYTES = 6 * 2**20
MM_VMEM_BUDGET = (VMEM_LIMIT * 3) // 4
PACK_W = 1024
PACK_ROWS = 1024

W_NAMES = ['norm_mix_w', 'norm_mlp_w', 'sb_w_qkv', 'sb_q_norm_w', 'sb_k_norm_w', 'sb_w_o', 'cf_w_in', 'cf_b_in',
           'cf_dw_w', 'cf_dw_b', 'cf_ln_w', 'cf_ln_b', 'cf_w_out', 'cf_b_out', 'm2_w_in', 'm2_conv_w', 'm2_conv_b',
           'm2_dt_bias', 'm2_a_log', 'm2_d', 'm2_norm_w', 'm2_w_out', 'mlp_w_up', 'mlp_w_down']
BULK = {'sb_w_qkv': 2, 'sb_w_o': 1, 'cf_w_in': 2, 'cf_w_out': 1, 'm2_w_in': 2, 'm2_w_out': 1, 'mlp_w_up': 2,
        'mlp_w_down': 1}
SMALL = {'cf_dw_w': 2, 'm2_conv_w': 2, 'm2_conv_b': 1, 'm2_norm_w': 1}
REPL = ['norm_mix_w', 'norm_mlp_w', 'sb_q_norm_w', 'sb_k_norm_w', 'cf_b_in', 'cf_dw_b', 'cf_ln_w', 'cf_ln_b',
        'cf_b_out', 'm2_dt_bias', 'm2_a_log', 'm2_d']


def _tile(n, prefs):
    for p in prefs:
        if n % p == 0:
            return p
    return n


def _cparams(n):
    return pltpu.CompilerParams(dimension_semantics=("arbitrary",) * n, vmem_limit_bytes=VMEM_LIMIT)


def _silu(v):
    return v / (1.0 + jnp.exp(-v))


def _softplus(v):
    return jnp.maximum(v, 0.0) + jnp.log(1.0 + jnp.exp(-jnp.abs(v)))


def _mm(a, b, mode, *, extras=(), epilogue=None, out_dtypes=(F32,), name):
    if mode == "tn":
        kdim, m = a.shape
    else:
        m, kdim = a.shape
    n = b.shape[0] if mode == "nt" else b.shape[1]
    tm = _tile(m, (1024, 512, 256, 128))
    tn = _tile(n, (1024, 512, 256, 128))
    acc_in_out = out_dtypes[0] == F32
    fixed = sum(tm * tn * jnp.dtype(dt).itemsize for dt in out_dtypes)
    fixed += sum((1 if e.shape[0] == 1 else tm) * tn * e.dtype.itemsize for e in extras)

    def vmem_bytes(t):
        ab = tm * t * a.dtype.itemsize + t * tn * b.dtype.itemsize
        casts = (tm * t * 2 if a.dtype != BF16 else 0) + (t * tn * 2 if b.dtype != BF16 else 0)
        return 2 * (ab + fixed) + (0 if acc_in_out else tm * tn * 4) + tm * tn * 4 + casts

    tk = kdim
    for cand in (2048, 1024, 512, 256, 128):
        if kdim % cand == 0:
            tk = cand
            if vmem_bytes(cand) <= MM_VMEM_BUDGET:
                break
    nk = kdim // tk
    dims = {"nn": ((1,), (0,)), "nt": ((1,), (1,)), "tn": ((0,), (0,))}[mode]
    if mode == "tn":
        a_spec = pl.BlockSpec((tk, tm), lambda i, j, k: (k, i))
    else:
        a_spec = pl.BlockSpec((tm, tk), lambda i, j, k: (i, k))
    if mode == "nt":
        b_spec = pl.BlockSpec((tn, tk), lambda i, j, k: (j, k))
    else:
        b_spec = pl.BlockSpec((tk, tn), lambda i, j, k: (k, j))
    ex_specs = [pl.BlockSpec((1, tn), lambda i, j, k: (0, j)) if e.shape[0] == 1
                else pl.BlockSpec((tm, tn), lambda i, j, k: (i, j)) for e in extras]
    n_ex, n_out = len(extras), len(out_dtypes)

    def body(*refs):
        a_ref, b_ref = refs[:2]
        ex = refs[2:2 + n_ex]
        outs = refs[2 + n_ex:2 + n_ex + n_out]
        acc = outs[0] if acc_in_out else refs[-1]
        k = pl.program_id(2)
        prod = lax.dot_general(a_ref[...].astype(BF16), b_ref[...].astype(BF16), (dims, ((), ())),
                               preferred_element_type=F32)

        def finish(res):
            vals = epilogue(res, *[e[...] for e in ex]) if epilogue is not None else (res,)
            for o, v in zip(outs, vals):
                o[...] = v.astype(o.dtype)

        if nk == 1:
            finish(prod)
        else:
            @pl.when(k == 0)
            def _():
                acc[...] = prod

            @pl.when((k > 0) & (k < nk - 1))
            def _():
                acc[...] += prod

            @pl.when(k == nk - 1)
            def _():
                finish(acc[...] + prod)

    scratch = [] if (acc_in_out or nk == 1) else [pltpu.VMEM((tm, tn), F32)]
    outs = pl.pallas_call(
        body, grid=(m // tm, n // tn, nk), in_specs=[a_spec, b_spec, *ex_specs],
        out_specs=[pl.BlockSpec((tm, tn), lambda i, j, k: (i, j))] * n_out,
        out_shape=[jax.ShapeDtypeStruct((m, n), d) for d in out_dtypes],
        scratch_shapes=scratch, compiler_params=_cparams(3), name=name)(a, b, *extras)
    return outs[0] if n_out == 1 else outs


def _ep_add(acc, r):
    return (acc + r,)


def _ep_bias_add(acc, b, r):
    return (acc + b + r,)


def _ep_relu2(acc):
    return acc, jnp.square(jnp.maximum(acc, 0.0))


def _ep_relu2_bwd(acc, u):
    return (acc * (2.0 * jnp.maximum(u, 0.0)),)


def _rowwise(fn, rows, consts, outs, *, ncb=1, name):
    nrow = rows[0][0].shape[0]
    row_bytes = sum(w * arr.dtype.itemsize for arr, w, _ in rows)
    row_bytes += sum(o[2] * jnp.dtype(o[3]).itemsize for o in outs if o[0] == 'row')
    t = nrow
    for cand in (1024, 512, 256, 128, 64, 32, 16):
        if nrow % cand == 0:
            t = cand
            if cand * row_bytes <= ROW_BLOCK_BYTES:
                break
    in_specs = [pl.BlockSpec((t, w), (lambda j, i, off=off: (i, off + j))) for _, w, off in rows]
    for arr, per_col in consts:
        r, wc = arr.shape
        if per_col:
            in_specs.append(pl.BlockSpec((r, wc // ncb), lambda j, i: (0, j)))
        else:
            in_specs.append(pl.BlockSpec((r, wc), lambda j, i: (0, 0)))
    out_specs, out_shape = [], []
    for o in outs:
        if o[0] == 'row':
            out_specs.append(pl.BlockSpec((t, o[2]), lambda j, i: (i, j)))
            out_shape.append(jax.ShapeDtypeStruct((nrow, o[1]), o[3]))
        else:
            r, wt = o[1]
            if o[2]:
                out_specs.append(pl.BlockSpec((r, wt // ncb), lambda j, i: (0, j)))
            else:
                out_specs.append(pl.BlockSpec((r, wt), lambda j, i: (0, 0)))
            out_shape.append(jax.ShapeDtypeStruct((r, wt), F32))
    nin = len(rows) + len(consts)

    def body(*refs):
        j, i = pl.program_id(0), pl.program_id(1)
        vals = fn(*[r[...] for r in refs[:nin]])

        def store(spec, ref, v):
            if spec[0] == 'row':
                ref[...] = v.astype(ref.dtype)
            else:
                first = (i == 0) if spec[2] else ((i == 0) & (j == 0))

                @pl.when(first)
                def _():
                    ref[...] = jnp.zeros_like(ref)

                ref[...] += v

        for spec, ref, v in zip(outs, refs[nin:], vals):
            store(spec, ref, v)

    res = pl.pallas_call(body, grid=(ncb, nrow // t), in_specs=in_specs, out_specs=out_specs, out_shape=out_shape,
                         compiler_params=_cparams(2), name=name)(*[r[0] for r in rows], *[c[0] for c in consts])
    return res


def _rms_f(eps):
    def f(x, w):
        return x * lax.rsqrt(jnp.mean(x * x, axis=-1, keepdims=True) + eps) * w
    return f


def _rms_fwd(x, w, name):
    d = x.shape[1]
    f = _rms_f(RMS_EPS)
    return _rowwise(lambda xv, wv: (f(xv, wv),), [(x, d, 0)], [(w, False)], [('row', d, d, BF16)], name=name)[0]


def _rms_bwd(x, w, dh, dres, name):
    d = x.shape[1]
    f = _rms_f(RMS_EPS)

    def fn(xv, dhv, drv, wv):
        _, vjp = jax.vjp(f, xv, wv)
        dx, dw = vjp(dhv)
        return dx + drv, dx + drv, dw

    return _rowwise(fn, [(x, d, 0), (dh, d, 0), (dres, d, 0)], [(w, False)],
                    [('row', d, d, F32), ('row', d, d, BF16), ('acc', (1, d), False)], name=name)


def _colsum(x, name):
    w = x.shape[1]
    return _rowwise(lambda v: (jnp.sum(v, axis=0, keepdims=True),), [(x, w, 0)], [], [('acc', (1, w), False)],
                    name=name)[0]


def _loss(y, tgt, name):
    d = y.shape[1]

    def fn(yv, tv):
        e = yv - tv
        s = jnp.sum(jnp.sum(e * e, axis=1, keepdims=True), axis=0, keepdims=True) * (0.5 / d)
        return e * (1.0 / d), e * (1.0 / d), s + jnp.zeros((1, LANES), F32)

    return _rowwise(fn, [(y, d, 0), (tgt, d, 0)], [],
                    [('row', d, d, F32), ('row', d, d, BF16), ('acc', (1, LANES), False)], name=name)


def _conv_tiles(x, w):
    nrow, ch = x.shape
    kw = w.shape[0]
    halo = 8 * ((kw - 1 + 7) // 8)
    t = _tile(nrow, (256, 128, 64, 32))
    tc = _tile(ch, (512, 256, 128))
    assert t % halo == 0 and nrow % t == 0
    return nrow, ch, kw, halo, t, tc


def _dwconv_fwd(x, w, b, name):
    nrow, ch, kw, halo, t, tc = _conv_tiles(x, w)
    per = t // halo

    def body(x_ref, halo_ref, w_ref, b_ref, y_ref, buf):
        i = pl.program_id(1)
        buf[0:halo, :] = jnp.where(i == 0, 0.0, halo_ref[...])
        buf[halo:halo + t, :] = x_ref[...]
        acc = jnp.zeros((t, tc), F32) + b_ref[...]
        for k in range(kw):
            s = halo - (kw - 1) + k
            acc = acc + w_ref[k:k + 1, :] * buf[s:s + t, :]
        y_ref[...] = acc

    return pl.pallas_call(
        body, grid=(ch // tc, nrow // t),
        in_specs=[pl.BlockSpec((t, tc), lambda j, i: (i, j)),
                  pl.BlockSpec((halo, tc), lambda j, i: (jnp.maximum(i * per - 1, 0), j)),
                  pl.BlockSpec((kw, tc), lambda j, i: (0, j)), pl.BlockSpec((1, tc), lambda j, i: (0, j))],
        out_specs=pl.BlockSpec((t, tc), lambda j, i: (i, j)), out_shape=jax.ShapeDtypeStruct((nrow, ch), F32),
        scratch_shapes=[pltpu.VMEM((t + halo, tc), F32)], compiler_params=_cparams(2), name=name)(x, x, w, b)


def _dwconv_bwd(x, dy, w, name):
    nrow, ch, kw, halo, t, tc = _conv_tiles(x, w)
    per = t // halo
    nt = nrow // t
    nhalo = nrow // halo

    def body(x_ref, xh_ref, dy_ref, dyh_ref, w_ref, dx_ref, dw_ref, db_ref, xbuf, dybuf):
        i = pl.program_id(1)
        xbuf[0:halo, :] = jnp.where(i == 0, 0.0, xh_ref[...])
        xbuf[halo:halo + t, :] = x_ref[...]
        dyc = dy_ref[...]
        dybuf[0:t, :] = dyc
        dybuf[t:t + halo, :] = jnp.where(i == nt - 1, 0.0, dyh_ref[...])
        acc = jnp.zeros((t, tc), F32)
        for k in range(kw):
            s = kw - 1 - k
            acc = acc + w_ref[k:k + 1, :] * dybuf[s:s + t, :]
        dx_ref[...] = acc

        @pl.when(i == 0)
        def _():
            dw_ref[...] = jnp.zeros_like(dw_ref)
            db_ref[...] = jnp.zeros_like(db_ref)

        for k in range(kw):
            s = kw - 1 - k
            dw_ref[k:k + 1, :] += jnp.sum(dyc * xbuf[halo - s:halo - s + t, :], axis=0, keepdims=True)
        db_ref[...] += jnp.sum(dyc, axis=0, keepdims=True)

    return pl.pallas_call(
        body, grid=(ch // tc, nt),
        in_specs=[pl.BlockSpec((t, tc), lambda j, i: (i, j)),
                  pl.BlockSpec((halo, tc), lambda j, i: (jnp.maximum(i * per - 1, 0), j)),
                  pl.BlockSpec((t, tc), lambda j, i: (i, j)),
                  pl.BlockSpec((halo, tc), lambda j, i: (jnp.minimum((i + 1) * per, nhalo - 1), j)),
                  pl.BlockSpec((kw, tc), lambda j, i: (0, j))],
        out_specs=[pl.BlockSpec((t, tc), lambda j, i: (i, j)), pl.BlockSpec((kw, tc), lambda j, i: (0, j)),
                   pl.BlockSpec((1, tc), lambda j, i: (0, j))],
        out_shape=[jax.ShapeDtypeStruct((nrow, ch), F32), jax.ShapeDtypeStruct((kw, ch), F32),
                   jax.ShapeDtypeStruct((1, ch), F32)],
        scratch_shapes=[pltpu.VMEM((t + halo, tc), F32), pltpu.VMEM((t + halo, tc), F32)],
        compiler_params=_cparams(2), name=name)(x, x, dy, dy, w)


SB_SUB = 128


def _dot_split2(x, u):
    hi = x.astype(BF16)
    lo = (x - hi.astype(F32)).astype(BF16)
    return jnp.dot(hi, u, preferred_element_type=F32) + jnp.dot(lo, u, preferred_element_type=F32)


def _sb_scores(q, k, strict, scale):
    z = lax.dot_general(q, k, (((1,), (1,)), ((), ())), preferred_element_type=F32) * scale
    soft = jnp.log(1.0 + jnp.exp(-jnp.abs(z)))
    ls = jnp.minimum(z, 0.0) - soft
    lk = ls - z
    if strict is not None:
        lk = jnp.where(strict, lk, 0.0)
    return ls, lk


def _sb_running(x, tri, start, reverse):
    nsub = x.shape[1] // SB_SUB
    parts = [x[:, SB_SUB * b:SB_SUB * (b + 1)] for b in range(nsub)]
    out = [None] * nsub
    run = start
    for b in (reversed(range(nsub)) if reverse else range(nsub)):
        out[b] = _dot_split2(parts[b], tri) + run
        run = run + jnp.sum(parts[b], axis=1, keepdims=True)
    return (jnp.concatenate(out, axis=1) if nsub > 1 else out[0]), run


def _sb_dims(qn):
    nrow, d = qn.shape
    hd = SB_HEAD_DIM
    t = _tile(nrow, (512, 256, 128))
    assert nrow % t == 0 and nrow // t <= LANES
    return nrow, d, hd, d // hd, t


def _sb_masks(t):
    strict = lax.broadcasted_iota(jnp.int32, (t, t), 1) < lax.broadcasted_iota(jnp.int32, (t, t), 0)
    r0 = lax.broadcasted_iota(jnp.int32, (SB_SUB, SB_SUB), 0)
    c0 = lax.broadcasted_iota(jnp.int32, (SB_SUB, SB_SUB), 1)
    return strict, (r0 > c0).astype(BF16), (r0 < c0).astype(BF16)


def _sb_attn_fwd(qn, kn, qkv, name):
    nrow, d, hd, nh, t = _sb_dims(qn)
    scale = 1.0 / math.sqrt(hd)

    def body(q_ref, k_ref, v_ref, o_ref, rs_ref, r_acc, o_acc):
        i = pl.program_id(1)
        q = q_ref[...]
        strict, after, _ = _sb_masks(t)
        lane = lax.broadcasted_iota(jnp.int32, (t, LANES), 1)
        r_acc[...] = jnp.zeros_like(r_acc)
        o_acc[...] = jnp.zeros_like(o_acc)
        rs_ref[...] = jnp.zeros_like(rs_ref)

        def tile(j, mask):
            start = pl.multiple_of(j * t, t)
            k = k_ref[pl.ds(start, t), :]
            v = v_ref[pl.ds(start, t), :].astype(BF16)
            ls, lk = _sb_scores(q, k, mask, scale)
            r = r_acc[...]
            later, r_next = _sb_running(lk, after, r, True)
            att = jnp.exp(ls + later)
            if mask is not None:
                att = jnp.where(mask, att, 0.0)
            o_acc[...] += jnp.dot(att.astype(BF16), v, preferred_element_type=F32)
            rs_ref[...] = jnp.where(lane == j, r, rs_ref[...])
            r_acc[...] = r_next

        tile(i, strict)

        def step(n, carry):
            tile(i - 1 - n, None)
            return carry

        lax.fori_loop(0, i, step, 0)
        o_ref[...] = o_acc[...].astype(o_ref.dtype)

    return pl.pallas_call(
        body, grid=(nh, nrow // t),
        in_specs=[pl.BlockSpec((t, hd), lambda h, i: (i, h)), pl.BlockSpec((nrow, hd), lambda h, i: (0, h)),
                  pl.BlockSpec((nrow, hd), lambda h, i: (0, 2 * nh + h))],
        out_specs=[pl.BlockSpec((t, hd), lambda h, i: (i, h)), pl.BlockSpec((None, t, LANES), lambda h, i: (h, i, 0))],
        out_shape=[jax.ShapeDtypeStruct((nrow, d), BF16), jax.ShapeDtypeStruct((nh, nrow, LANES), F32)],
        scratch_shapes=[pltpu.VMEM((t, LANES), F32), pltpu.VMEM((t, hd), F32)],
        compiler_params=_cparams(2), name=name)(qn, kn, qkv)


def _sb_attn_bwd(qn, kn, qkv, do, rsave, name):
    nrow, d, hd, nh, t = _sb_dims(qn)
    scale = 1.0 / math.sqrt(hd)

    def body(q_ref, k_ref, v_ref, do_ref, rs_ref, dq_ref, dk_ref, dv_ref, pg_acc):
        i = pl.program_id(1)

        @pl.when(i == 0)
        def _():
            dk_ref[...] = jnp.zeros_like(dk_ref)
            dv_ref[...] = jnp.zeros_like(dv_ref)

        pg_acc[...] = jnp.zeros_like(pg_acc)
        dq_ref[...] = jnp.zeros_like(dq_ref)
        q = q_ref[...]
        dob = do_ref[...].astype(BF16)
        strict, after, before = _sb_masks(t)
        lane = lax.broadcasted_iota(jnp.int32, (t, LANES), 1)

        def tile(j, mask):
            start = pl.multiple_of(j * t, t)
            k = k_ref[pl.ds(start, t), :]
            v = v_ref[pl.ds(start, t), :].astype(BF16)
            ls, lk = _sb_scores(q, k, mask, scale)
            rj = jnp.sum(jnp.where(lane == j, rs_ref[...], 0.0), axis=1, keepdims=True)
            later, _ = _sb_running(lk, after, jnp.broadcast_to(rj, (t, LANES)), True)
            att = jnp.exp(ls + later)
            if mask is not None:
                att = jnp.where(mask, att, 0.0)
            datt = lax.dot_general(dob, v, (((1,), (1,)), ((), ())), preferred_element_type=F32)
            g = datt * att
            dlk, pg_next = _sb_running(g, before, pg_acc[...], False)
            sig = jnp.exp(ls)
            dz = g * (1.0 - sig) - dlk * sig
            if mask is not None:
                dz = jnp.where(mask, dz, 0.0)
            dz = (dz * scale).astype(BF16)
            dq_ref[...] += jnp.dot(dz, k, preferred_element_type=F32)
            dk_ref[pl.ds(start, t), :] += lax.dot_general(dz, q, (((0,), (0,)), ((), ())), preferred_element_type=F32)
            dv_ref[pl.ds(start, t), :] += lax.dot_general(att.astype(BF16), dob, (((0,), (0,)), ((), ())),
                                                          preferred_element_type=F32)
            pg_acc[...] = pg_next

        def step(j, carry):
            tile(j, None)
            return carry

        lax.fori_loop(0, i, step, 0)
        tile(i, strict)

    return pl.pallas_call(
        body, grid=(nh, nrow // t),
        in_specs=[pl.BlockSpec((t, hd), lambda h, i: (i, h)), pl.BlockSpec((nrow, hd), lambda h, i: (0, h)),
                  pl.BlockSpec((nrow, hd), lambda h, i: (0, 2 * nh + h)), pl.BlockSpec((t, hd), lambda h, i: (i, h)),
                  pl.BlockSpec((None, t, LANES), lambda h, i: (h, i, 0))],
        out_specs=[pl.BlockSpec((t, hd), lambda h, i: (i, h)), pl.BlockSpec((nrow, hd), lambda h, i: (0, h)),
                   pl.BlockSpec((nrow, hd), lambda h, i: (0, h))],
        out_shape=[jax.ShapeDtypeStruct((nrow, d), F32)] * 3, scratch_shapes=[pltpu.VMEM((t, LANES), F32)],
        compiler_params=_cparams(2), name=name)(qn, kn, qkv, do, rsave)


def _ssd_chunk_fn(g, nheads_g, q):
    p = M2_HEAD_DIM
    npair = nheads_g // 2

    def f(xp, bp, cp, dtr, hp, dtb, alog, dsk):
        b = _silu(bp).astype(BF16)
        c = _silu(cp).astype(BF16)
        dt = _softplus(dtr + dtb)
        dta = dt * (-jnp.exp(alog))
        ri = lax.broadcasted_iota(jnp.int32, (q, q), 0)
        ci = lax.broadcasted_iota(jnp.int32, (q, q), 1)
        causal = ri >= ci
        tri = causal.astype(F32)
        acol = jnp.dot(tri, dta, precision=HIGHEST, preferred_element_type=F32)
        arow = lax.dot_general(dta, tri, (((0,), (1,)), ((), ())), precision=HIGHEST,
                               preferred_element_type=F32)
        alast = jnp.sum(dta, axis=0, keepdims=True)
        cb = lax.dot_general(c, b, (((1,), (1,)), ((), ())), preferred_element_type=F32)
        lane = lax.broadcasted_iota(jnp.int32, (1, LANES), 1)
        sub = lax.broadcasted_iota(jnp.int32, (LANES, 1), 0)
        lane_half = lane < p
        sub_half = sub < p

        def col(mat, h):
            return jnp.sum(jnp.where(lane == h, mat, 0.0), axis=1, keepdims=True)

        def row(mat, h):
            return jnp.sum(jnp.where(sub == h, mat, 0.0), axis=0, keepdims=True)

        def mix(v0, v1):
            return jnp.where(lane_half, v0, v1)

        ys, hns = [], []
        for jp in range(npair):
            h0 = g * nheads_g + 2 * jp
            h1 = h0 + 1
            x = _silu(xp[jp])
            ac0, ac1 = col(acol, h0), col(acol, h1)
            xdt = x * mix(col(dt, h0), col(dt, h1))
            xdtb = xdt.astype(BF16)
            yd = []
            for h, ac in ((h0, ac0), (h1, ac1)):
                dec = jnp.exp(jnp.where(causal, ac - row(arow, h), -jnp.inf))
                yd.append(jnp.dot((cb * dec).astype(BF16), xdtb, preferred_element_type=F32))
            hpj = hp[jp]
            yo = lax.dot_general(c, hpj.astype(BF16), (((1,), (1,)), ((), ())), preferred_element_type=F32)
            y = mix(yd[0], yd[1]) + yo * jnp.exp(mix(ac0, ac1)) + mix(col(dsk, h0), col(dsk, h1)) * x
            al0, al1 = col(alast, h0), col(alast, h1)
            dout = jnp.exp(mix(al0 - ac0, al1 - ac1))
            st = lax.dot_general((xdt * dout).astype(BF16), b, (((0,), (0,)), ((), ())), preferred_element_type=F32)
            hns.append(hpj * jnp.exp(jnp.where(sub_half, al0, al1)) + st)
            ys.append(y)
        return ys, hns

    return f


def _ssd_dims(pre, dtr):
    nrow = pre.shape[0]
    ng, n, p, q = M2_GROUPS, M2_STATE, M2_HEAD_DIM, M2_CHUNK
    d_inner = pre.shape[1] - 2 * ng * n
    gw = d_inner // ng
    nhg = gw // p
    assert nhg % 2 == 0 and gw % LANES == 0 and n == LANES and ng * nhg <= LANES and dtr.shape[1] == LANES
    return nrow, ng, n, q, d_inner, gw, nhg, nrow // q


def _ssd_fwd(pre, dtr, dtb, alog, dsk, name):
    nrow, ng, n, q, d_inner, gw, nhg, nc = _ssd_dims(pre, dtr)
    boff = d_inner // n
    npair = nhg // 2

    def body(x_ref, b_ref, c_ref, dt_ref, dtb_ref, al_ref, ds_ref, y_ref, hs_ref, state):
        ci, g = pl.program_id(0), pl.program_id(1)

        @pl.when(ci == 0)
        def _():
            state[g] = jnp.zeros((gw, n), F32)

        hs_ref[...] = state[g]
        f = _ssd_chunk_fn(g, nhg, q)
        xp = [x_ref[:, LANES * j:LANES * (j + 1)] for j in range(npair)]
        hp = [state[g, LANES * j:LANES * (j + 1), :] for j in range(npair)]
        ys, hns = f(xp, b_ref[...], c_ref[...], dt_ref[...], hp, dtb_ref[...], al_ref[...], ds_ref[...])
        for j in range(npair):
            y_ref[:, LANES * j:LANES * (j + 1)] = ys[j]
            state[g, LANES * j:LANES * (j + 1), :] = hns[j]

    par = pl.BlockSpec((1, LANES), lambda ci, g: (0, 0))
    return pl.pallas_call(
        body, grid=(nc, ng),
        in_specs=[pl.BlockSpec((q, gw), lambda ci, g: (ci, g)), pl.BlockSpec((q, n), lambda ci, g: (ci, boff + g)),
                  pl.BlockSpec((q, n), lambda ci, g: (ci, boff + ng + g)), pl.BlockSpec((q, LANES), lambda ci, g: (ci, 0)),
                  par, par, par],
        out_specs=[pl.BlockSpec((q, gw), lambda ci, g: (ci, g)),
                   pl.BlockSpec((None, None, gw, n), lambda ci, g: (ci, g, 0, 0))],
        out_shape=[jax.ShapeDtypeStruct((nrow, d_inner), F32), jax.ShapeDtypeStruct((nc, ng, gw, n), F32)],
        scratch_shapes=[pltpu.VMEM((ng, gw, n), F32)], compiler_params=_cparams(2), name=name)(
            pre, pre, pre, dtr, dtb, alog, dsk)


def _ssd_bwd(pre, dtr, hs, dy, dtb, alog, dsk, name):
    nrow, ng, n, q, d_inner, gw, nhg, nc = _ssd_dims(pre, dtr)
    boff = d_inner // n
    npair = nhg // 2

    def body(x_ref, b_ref, c_ref, dt_ref, hs_ref, dy_ref, dtb_ref, al_ref, ds_ref,
             dx_ref, db_ref, dc_ref, ddt_ref, ddtb_ref, dal_ref, dds_ref, dstate):
        ci, g = pl.program_id(0), pl.program_id(1)

        @pl.when(ci == 0)
        def _():
            dstate[g] = jnp.zeros((gw, n), F32)

        @pl.when((ci == 0) & (g == 0))
        def _():
            ddtb_ref[...] = jnp.zeros_like(ddtb_ref)
            dal_ref[...] = jnp.zeros_like(dal_ref)
            dds_ref[...] = jnp.zeros_like(dds_ref)

        @pl.when(g == 0)
        def _():
            ddt_ref[...] = jnp.zeros_like(ddt_ref)

        f = _ssd_chunk_fn(g, nhg, q)
        xp = [x_ref[:, LANES * j:LANES * (j + 1)] for j in range(npair)]
        hp = [hs_ref[LANES * j:LANES * (j + 1), :] for j in range(npair)]
        _, vjp = jax.vjp(f, xp, b_ref[...], c_ref[...], dt_ref[...], hp, dtb_ref[...], al_ref[...], ds_ref[...])
        dys = [dy_ref[:, LANES * j:LANES * (j + 1)] for j in range(npair)]
        dhn = [dstate[g, LANES * j:LANES * (j + 1), :] for j in range(npair)]
        dxp, db, dc, ddt, dhp, ddtb, dal, dds = vjp((dys, dhn))
        for j in range(npair):
            dx_ref[:, LANES * j:LANES * (j + 1)] = dxp[j]
            dstate[g, LANES * j:LANES * (j + 1), :] = dhp[j]
        db_ref[...] = db
        dc_ref[...] = dc
        ddt_ref[...] += ddt
        ddtb_ref[...] += ddtb
        dal_ref[...] += dal
        dds_ref[...] += dds

    par = pl.BlockSpec((1, LANES), lambda ci, g: (0, 0))
    last = nc - 1
    return pl.pallas_call(
        body, grid=(nc, ng),
        in_specs=[pl.BlockSpec((q, gw), lambda ci, g: (last - ci, g)),
                  pl.BlockSpec((q, n), lambda ci, g: (last - ci, boff + g)),
                  pl.BlockSpec((q, n), lambda ci, g: (last - ci, boff + ng + g)),
                  pl.BlockSpec((q, LANES), lambda ci, g: (last - ci, 0)),
                  pl.BlockSpec((None, None, gw, n), lambda ci, g: (last - ci, g, 0, 0)),
                  pl.BlockSpec((q, gw), lambda ci, g: (last - ci, g)), par, par, par],
        out_specs=[pl.BlockSpec((q, gw), lambda ci, g: (last - ci, g)), pl.BlockSpec((q, n), lambda ci, g: (last - ci, g)),
                   pl.BlockSpec((q, n), lambda ci, g: (last - ci, g)), pl.BlockSpec((q, LANES), lambda ci, g: (last - ci, 0)),
                   par, par, par],
        out_shape=[jax.ShapeDtypeStruct((nrow, d_inner), F32), jax.ShapeDtypeStruct((nrow, ng * n), F32),
                   jax.ShapeDtypeStruct((nrow, ng * n), F32), jax.ShapeDtypeStruct((nrow, LANES), F32),
                   jax.ShapeDtypeStruct((1, LANES), F32), jax.ShapeDtypeStruct((1, LANES), F32),
                   jax.ShapeDtypeStruct((1, LANES), F32)],
        scratch_shapes=[pltpu.VMEM((ng, gw, n), F32)], compiler_params=_cparams(2), name=name)(
            pre, pre, pre, dtr, hs, dy, dtb, alog, dsk)


HBM_SPEC = pl.BlockSpec(memory_space=pl.ANY)


def _xy_peers(mx, my):
    return [(1 - mx, my), (mx, 1 - my), (1 - mx, 1 - my)]


def _remote(src, dst, send_sems, recv_sems, k, dev):
    return pltpu.make_async_remote_copy(src_ref=src, dst_ref=dst, send_sem=send_sems.at[k], recv_sem=recv_sems.at[k],
                                        device_id=dev, device_id_type=MESH_ID)


def _all_gather_xy(buf, name):
    npeer = N_XY - 1

    def body(src, out, send_sems, recv_sems):
        mx, my, mc = lax.axis_index("x"), lax.axis_index("y"), lax.axis_index("c")
        me = 2 * mx + my
        sibling = (mx, my, 1 - mc)
        peers = _xy_peers(mx, my)
        mine, theirs = mc, 1 - mc
        sends = [_remote(src.at[mine], out.at[me, mine], send_sems, recv_sems, k, (px, py, mc))
                 for k, (px, py) in enumerate(peers)]
        for cp in sends:
            cp.start()
        for k, (px, py) in enumerate(peers):
            landed = out.at[2 * px + py, mine]
            _remote(src.at[mine], landed, send_sems, recv_sems, k, (px, py, mc)).wait_recv()
            cp = _remote(landed, landed, send_sems, recv_sems, npeer + k, sibling)
            cp.start()
            sends.append(cp)
        for k, (px, py) in enumerate(peers):
            passed = out.at[2 * px + py, theirs]
            _remote(passed, passed, send_sems, recv_sems, npeer + k, sibling).wait_recv()
        for cp in sends:
            cp.wait_send()

    out = pl.pallas_call(
        body, out_shape=jax.ShapeDtypeStruct((N_XY,) + buf.shape, buf.dtype), in_specs=[HBM_SPEC], out_specs=HBM_SPEC,
        scratch_shapes=[pltpu.SemaphoreType.DMA((2 * npeer,)), pltpu.SemaphoreType.DMA((2 * npeer,))], name=name)(buf)
    return lax.dynamic_update_index_in_dim(out, buf, _my_shard(), 0)


def _my_shard():
    return 2 * lax.axis_index("x") + lax.axis_index("y")


def _exchange_xy(parts, name):
    def body(src, out, send_sems, recv_sems):
        mx, my, mc = lax.axis_index("x"), lax.axis_index("y"), lax.axis_index("c")
        me = 2 * mx + my
        peers = _xy_peers(mx, my)
        sends = [_remote(src.at[2 * px + py], out.at[me], send_sems, recv_sems, k, (px, py, mc))
                 for k, (px, py) in enumerate(peers)]
        for cp in sends:
            cp.start()
        for k, (px, py) in enumerate(peers):
            _remote(src.at[me], out.at[2 * px + py], send_sems, recv_sems, k, (px, py, mc)).wait_recv()
        for cp in sends:
            cp.wait_send()

    out = pl.pallas_call(
        body, out_shape=jax.ShapeDtypeStruct(parts.shape, parts.dtype), in_specs=[HBM_SPEC], out_specs=HBM_SPEC,
        scratch_shapes=[pltpu.SemaphoreType.DMA((N_XY - 1,)), pltpu.SemaphoreType.DMA((N_XY - 1,))], name=name)(parts)
    me = _my_shard()
    return lax.dynamic_update_index_in_dim(out, lax.dynamic_index_in_dim(parts, me, 0, keepdims=False), me, 0)


def _pair_split(parts, name):
    nparts, _, nrow, width = parts.shape

    def body(src, got, send_sems, recv_sems):
        mx, my, mc = lax.axis_index("x"), lax.axis_index("y"), lax.axis_index("c")
        copies = [_remote(src.at[s, 1 - mc], got.at[s], send_sems, recv_sems, s, (mx, my, 1 - mc))
                  for s in range(nparts)]
        for cp in copies:
            cp.start()
        for cp in copies:
            cp.wait()

    got = pl.pallas_call(
        body, out_shape=jax.ShapeDtypeStruct((nparts, nrow, width), parts.dtype), in_specs=[HBM_SPEC],
        out_specs=HBM_SPEC, scratch_shapes=[pltpu.SemaphoreType.DMA((nparts,)), pltpu.SemaphoreType.DMA((nparts,))],
        name=name)(parts)
    return lax.dynamic_index_in_dim(parts, lax.axis_index("c"), 1, keepdims=False), got


def _pair_join(mine, name):
    def body(src, out, send_sems, recv_sems):
        mx, my, mc = lax.axis_index("x"), lax.axis_index("y"), lax.axis_index("c")
        cp = _remote(src, out, send_sems, recv_sems, 0, (mx, my, 1 - mc))
        cp.start()
        cp.wait()

    other = pl.pallas_call(
        body, out_shape=jax.ShapeDtypeStruct(mine.shape, mine.dtype), in_specs=[HBM_SPEC], out_specs=HBM_SPEC,
        scratch_shapes=[pltpu.SemaphoreType.DMA((1,)), pltpu.SemaphoreType.DMA((1,))], name=name)(mine)
    return jnp.where(lax.axis_index("c") == 0, jnp.stack([mine, other]), jnp.stack([other, mine]))


def _all_gather_all(buf, name):
    flips = [(fx, fy, fc) for fx in (0, 1) for fy in (0, 1) for fc in (0, 1) if fx or fy or fc]

    def body(src, out, send_sems, recv_sems):
        mx, my, mc = lax.axis_index("x"), lax.axis_index("y"), lax.axis_index("c")
        me = 4 * mx + 2 * my + mc
        peers = [(1 - mx if fx else mx, 1 - my if fy else my, 1 - mc if fc else mc) for fx, fy, fc in flips]
        sends = [_remote(src, out.at[me], send_sems, recv_sems, k, dev) for k, dev in enumerate(peers)]
        for cp in sends:
            cp.start()
        for k, (px, py, pc) in enumerate(peers):
            _remote(src, out.at[4 * px + 2 * py + pc], send_sems, recv_sems, k, (px, py, pc)).wait_recv()
        for cp in sends:
            cp.wait_send()

    out = pl.pallas_call(
        body, out_shape=jax.ShapeDtypeStruct((N_DEV,) + buf.shape, buf.dtype), in_specs=[HBM_SPEC], out_specs=HBM_SPEC,
        scratch_shapes=[pltpu.SemaphoreType.DMA((N_DEV - 1,)), pltpu.SemaphoreType.DMA((N_DEV - 1,))], name=name)(buf)
    me = 4 * lax.axis_index("x") + 2 * lax.axis_index("y") + lax.axis_index("c")
    return lax.dynamic_update_index_in_dim(out, buf, me, 0)


def _flat_tile(nrow, width, narrays):
    t = nrow
    for cand in (512, 256, 128, 64, 32, 16, 8):
        if nrow % cand == 0:
            t = cand
            if cand * width * 4 * narrays <= ROW_BLOCK_BYTES:
                break
    return t


def _sum_parts(parts, name):
    kparts, nrow, width = parts.shape
    t = _flat_tile(nrow, width, kparts + 1)

    def body(p_ref, o_ref):
        s = p_ref[0].astype(F32)
        for k in range(1, kparts):
            s = s + p_ref[k].astype(F32)
        o_ref[...] = s

    return pl.pallas_call(
        body, grid=(nrow // t,), in_specs=[pl.BlockSpec((kparts, t, width), lambda i: (0, i, 0))],
        out_specs=pl.BlockSpec((t, width), lambda i: (i, 0)), out_shape=jax.ShapeDtypeStruct((nrow, width), F32),
        compiler_params=_cparams(1), name=name)(parts)


def _add2(a, b, name):
    nparts, nrow, width = a.shape
    t = _flat_tile(nrow, width, 3)

    def body(a_ref, b_ref, o_ref):
        o_ref[...] = (a_ref[...].astype(F32) + b_ref[...].astype(F32)).astype(o_ref.dtype)

    spec = pl.BlockSpec((None, t, width), lambda p, i: (p, i, 0))
    return pl.pallas_call(body, grid=(nparts, nrow // t), in_specs=[spec, spec], out_specs=spec,
                          out_shape=jax.ShapeDtypeStruct(a.shape, a.dtype), compiler_params=_cparams(2), name=name)(a, b)


def _adamw(w, m, v, ga, gb, name):
    nrow, width = w.shape
    t = _flat_tile(nrow, width, 9)
    c1 = 1.0 - ADAM_B1 ** ADAM_STEP
    c2 = 1.0 - ADAM_B2 ** ADAM_STEP
    grads = [ga] if gb is None else [ga, gb]
    nout = 3 if gb is None else 4

    def body(w_ref, m_ref, v_ref, *refs):
        d_ref, mo_ref, vo_ref = refs[-3:]
        g = refs[0][...]
        if gb is not None:
            g = g + refs[1][...]
            refs[2][...] = g
        mn = ADAM_B1 * m_ref[...] + (1.0 - ADAM_B1) * g
        vn = ADAM_B2 * v_ref[...] + (1.0 - ADAM_B2) * jnp.square(g)
        mo_ref[...] = mn
        vo_ref[...] = vn
        d_ref[...] = -ADAM_LR * ((mn / c1) / (jnp.sqrt(vn / c2) + ADAM_EPS) + ADAM_WD * w_ref[...])

    spec = pl.BlockSpec((t, width), lambda i: (i, 0))
    return pl.pallas_call(
        body, grid=(nrow // t,), in_specs=[spec] * (3 + len(grads)), out_specs=[spec] * nout,
        out_shape=[jax.ShapeDtypeStruct((nrow, width), F32)] * nout, compiler_params=_cparams(1), name=name)(
            w, m, v, *grads)


def _pack(arrs, dtype, width):
    flat = jnp.concatenate([a.astype(dtype).reshape(-1) for a in arrs])
    quantum = PACK_ROWS * width
    pad = (-flat.shape[0]) % quantum
    if pad:
        flat = jnp.concatenate([flat, jnp.zeros((pad,), dtype)])
    return flat.reshape(-1, width)


def _unpack(buf, shapes):
    flat = buf.reshape(-1)
    out, off = [], 0
    for s in shapes:
        size = math.prod(s)
        out.append(flat[off:off + size].reshape(s))
        off += size
    return out


def _qk_norm_fwd(qkv, w, off, nh, tag):
    f = _rms_f(RMS_EPS)
    return _rowwise(lambda xv, wv: (f(xv, wv),), [(qkv, SB_HEAD_DIM, off)], [(w, False)],
                    [('row', nh * SB_HEAD_DIM, SB_HEAD_DIM, BF16)], ncb=nh, name=tag)[0]


def _qk_norm_bwd(qkv, w, off, dn, nh, tag):
    f = _rms_f(RMS_EPS)

    def fn(xv, dv, wv):
        _, vjp = jax.vjp(f, xv, wv)
        return vjp(dv)

    return _rowwise(fn, [(qkv, SB_HEAD_DIM, off), (dn, SB_HEAD_DIM, 0)], [(w, False)],
                    [('row', nh * SB_HEAD_DIM, SB_HEAD_DIM, F32), ('acc', (1, SB_HEAD_DIM), False)], ncb=nh, name=tag)


def _sb_forward(h, xin, wt, tag):
    d = xin.shape[1]
    nh = d // SB_HEAD_DIM
    qkv = _mm(h, wt['w_qkv'], "nn", name=tag + "_qkv")
    qn = _qk_norm_fwd(qkv, wt['q_norm_w'], 0, nh, tag + "_qnorm")
    kn = _qk_norm_fwd(qkv, wt['k_norm_w'], nh, nh, tag + "_knorm")
    o, rsave = _sb_attn_fwd(qn, kn, qkv, tag + "_attn")
    xm = _mm(o, wt['w_o'], "nn", extras=(xin,), epilogue=_ep_add, name=tag + "_out")
    return xm, dict(qkv=qkv, qn=qn, kn=kn, o=o, rsave=rsave)


def _sb_backward(h, dx, dxb, wt, sv, tag):
    nh = dx.shape[1] // SB_HEAD_DIM
    do = _mm(dxb, wt['w_o'], "nt", name=tag + "_do")
    g_wo = _mm(sv['o'], dxb, "tn", name=tag + "_dwo")
    dqn, dkn, dv = _sb_attn_bwd(sv['qn'], sv['kn'], sv['qkv'], do, sv['rsave'], tag + "_attn_bwd")
    dq, g_qw = _qk_norm_bwd(sv['qkv'], wt['q_norm_w'], 0, dqn, nh, tag + "_qnorm_bwd")
    dk, g_kw = _qk_norm_bwd(sv['qkv'], wt['k_norm_w'], nh, dkn, nh, tag + "_knorm_bwd")
    dqkv = jnp.concatenate([dq, dk, dv], axis=1)
    g_wqkv = _mm(h, dqkv, "tn", name=tag + "_dwqkv")
    dh = _mm(dqkv, wt['w_qkv'], "nt", name=tag + "_dh")
    return dh, dict(w_qkv=g_wqkv, w_o=g_wo, q_norm_w=g_qw, k_norm_w=g_kw)


def _ln_silu_f(x, w, b):
    mu = jnp.mean(x, axis=-1, keepdims=True)
    xc = x - mu
    return _silu(xc * lax.rsqrt(jnp.mean(xc * xc, axis=-1, keepdims=True) + LN_EPS) * w + b)


def _glu_bwd_fn(val, gate, dg):
    sg = 1.0 / (1.0 + jnp.exp(-gate))
    return (jnp.concatenate([dg * sg, dg * val * sg * (1.0 - sg)], axis=1),)


def _cf_forward(h, xin, wt, tag):
    d = xin.shape[1]
    u = _mm(h, wt['w_in'], "nn", extras=(wt['b_in'],), epilogue=_ep_add, name=tag + "_in")
    gl = _rowwise(lambda val, gate: (val / (1.0 + jnp.exp(-gate)),), [(u, d, 0), (u, d, 1)], [],
                  [('row', d, d, F32)], name=tag + "_glu")[0]
    cv = _dwconv_fwd(gl, wt['dw_w'], wt['dw_b'], tag + "_conv")
    s = _rowwise(lambda x, w, b: (_ln_silu_f(x, w, b),), [(cv, d, 0)], [(wt['ln_w'], False), (wt['ln_b'], False)],
                 [('row', d, d, BF16)], name=tag + "_ln")[0]
    xm = _mm(s, wt['w_out'], "nn", extras=(wt['b_out'], xin), epilogue=_ep_bias_add, name=tag + "_out")
    return xm, dict(u=u, gl=gl, cv=cv, s=s)


def _cf_backward(h, dx, dxb, wt, sv, tag):
    d = dx.shape[1]
    ds = _mm(dxb, wt['w_out'], "nt", name=tag + "_ds")
    g_wout = _mm(sv['s'], dxb, "tn", name=tag + "_dwout")
    g_bout = _colsum(dx, tag + "_dbout")

    def ln_bwd(x, dsv, w, b):
        _, vjp = jax.vjp(_ln_silu_f, x, w, b)
        return vjp(dsv)

    dcv, g_lnw, g_lnb = _rowwise(ln_bwd, [(sv['cv'], d, 0), (ds, d, 0)], [(wt['ln_w'], False), (wt['ln_b'], False)],
                                 [('row', d, d, F32), ('acc', (1, d), False), ('acc', (1, d), False)],
                                 name=tag + "_ln_bwd")
    dgl, g_dww, g_dwb = _dwconv_bwd(sv['gl'], dcv, wt['dw_w'], tag + "_conv_bwd")
    du = _rowwise(_glu_bwd_fn, [(sv['u'], d, 0), (sv['u'], d, 1), (dgl, d, 0)], [], [('row', 2 * d, 2 * d, F32)],
                  name=tag + "_glu_bwd")[0]
    g_bin = _colsum(du, tag + "_dbin")
    g_win = _mm(h, du, "tn", name=tag + "_dwin")
    dh = _mm(du, wt['w_in'], "nt", name=tag + "_dh")
    return dh, dict(w_in=g_win, b_in=g_bin, dw_w=g_dww, dw_b=g_dwb, ln_w=g_lnw, ln_b=g_lnb, w_out=g_wout, b_out=g_bout)


def _gated_norm_f(y, z, w):
    t = y * _silu(z)
    return t * lax.rsqrt(jnp.mean(t * t, axis=-1, keepdims=True) + M2_NORM_EPS) * w


def _m2_forward(h, xin, wt, tag):
    z = _mm(h, wt['w_z'], "nn", name=tag + "_z")
    xbc = _mm(h, wt['w_xbc'], "nn", name=tag + "_xbc")
    dtr = _mm(h, wt['w_dt'], "nn", name=tag + "_dt")
    pre = _dwconv_fwd(xbc, wt['conv_w'], wt['conv_b'], tag + "_conv")
    y, hs = _ssd_fwd(pre, dtr, wt['dt_bias'], wt['a_log'], wt['d'], tag + "_ssd")
    d_inner = y.shape[1]
    gw = d_inner // M2_GROUPS
    yn = _rowwise(lambda yv, zv, w: (_gated_norm_f(yv, zv, w),), [(y, gw, 0), (z, gw, 0)], [(wt['norm_w'], True)],
                  [('row', d_inner, gw, BF16)], ncb=M2_GROUPS, name=tag + "_gnorm")[0]
    xm = _mm(yn, wt['w_out'], "nn", extras=(xin,), epilogue=_ep_add, name=tag + "_out")
    return xm, dict(z=z, xbc=xbc, dtr=dtr, pre=pre, y=y, hs=hs, yn=yn)


def _m2_backward(h, dx, dxb, wt, sv, tag):
    dyn = _mm(dxb, wt['w_out'], "nt", name=tag + "_dyn")
    g_wout = _mm(sv['yn'], dxb, "tn", name=tag + "_dwout")
    d_inner = sv['y'].shape[1]
    gw = d_inner // M2_GROUPS

    def gn_bwd(yv, zv, dv, w):
        _, vjp = jax.vjp(_gated_norm_f, yv, zv, w)
        return vjp(dv)

    dy, dz, g_nw = _rowwise(gn_bwd, [(sv['y'], gw, 0), (sv['z'], gw, 0), (dyn, gw, 0)], [(wt['norm_w'], True)],
                            [('row', d_inner, gw, F32), ('row', d_inner, gw, F32), ('acc', (1, d_inner), True)],
                            ncb=M2_GROUPS, name=tag + "_gnorm_bwd")
    dxp, db, dc, ddt, g_dtb, g_alog, g_d = _ssd_bwd(sv['pre'], sv['dtr'], sv['hs'], dy, wt['dt_bias'], wt['a_log'],
                                                   wt['d'], tag + "_ssd_bwd")
    dpre = jnp.concatenate([dxp, db, dc], axis=1)
    dxbc, g_cw, g_cb = _dwconv_bwd(sv['xbc'], dpre, wt['conv_w'], tag + "_conv_bwd")
    g_wz = _mm(h, dz, "tn", name=tag + "_dwz")
    g_wxbc = _mm(h, dxbc, "tn", name=tag + "_dwxbc")
    g_wdt = _mm(h, ddt, "tn", name=tag + "_dwdt")
    dh = _mm(dz, wt['w_z'], "nt", name=tag + "_dh_z")
    dh = _mm(dxbc, wt['w_xbc'], "nt", extras=(dh,), epilogue=_ep_add, name=tag + "_dh_xbc")
    dh = _mm(ddt, wt['w_dt'], "nt", extras=(dh,), epilogue=_ep_add, name=tag + "_dh_dt")
    return dh, dict(w_z=g_wz, w_xbc=g_wxbc, w_dt=g_wdt, conv_w=g_cw, conv_b=g_cb, dt_bias=g_dtb, a_log=g_alog, d=g_d,
                    norm_w=g_nw, w_out=g_wout)


def _pad_lanes(v):
    return jnp.pad(v.reshape(1, -1), ((0, 0), (0, LANES - v.shape[0])))


def kernel(x, norm_mix_w, norm_mlp_w, sb_w_qkv, sb_q_norm_w, sb_k_norm_w, sb_w_o, cf_w_in, cf_b_in, cf_dw_w, cf_dw_b, cf_ln_w, cf_ln_b, cf_w_out, cf_b_out, m2_w_in, m2_conv_w, m2_conv_b, m2_dt_bias, m2_a_log, m2_d, m2_norm_w, m2_w_out, mlp_w_up, mlp_w_down, loss_target, m_norm_mix_w, m_norm_mlp_w, m_sb_w_qkv, m_sb_q_norm_w, m_sb_k_norm_w, m_sb_w_o, m_cf_w_in, m_cf_b_in, m_cf_dw_w, m_cf_dw_b, m_cf_ln_w, m_cf_ln_b, m_cf_w_out, m_cf_b_out, m_m2_w_in, m_m2_conv_w, m_m2_conv_b, m_m2_dt_bias, m_m2_a_log, m_m2_d, m_m2_norm_w, m_m2_w_out, m_mlp_w_up, m_mlp_w_down, v_norm_mix_w, v_norm_mlp_w, v_sb_w_qkv, v_sb_q_norm_w, v_sb_k_norm_w, v_sb_w_o, v_cf_w_in, v_cf_b_in, v_cf_dw_w, v_cf_dw_b, v_cf_ln_w, v_cf_ln_b, v_cf_w_out, v_cf_b_out, v_m2_w_in, v_m2_conv_w, v_m2_conv_b, v_m2_dt_bias, v_m2_a_log, v_m2_d, v_m2_norm_w, v_m2_w_out, v_mlp_w_up, v_mlp_w_down):
    given = dict(locals())
    xl = x[0]
    tgt = loss_target[0]
    d = xl.shape[1]
    depth = norm_mix_w.shape[0]
    bulk_names, small_names = list(BULK), list(SMALL)

    def halves(buf):
        return buf.reshape(2, -1, buf.shape[-1])

    gath_b = _all_gather_xy(halves(_pack([given[n] for n in bulk_names], BF16, PACK_W)), "comm_gather_bulk")
    gath_s = _all_gather_xy(halves(_pack([given[n] for n in small_names], F32, PACK_W)), "comm_gather_small")
    full = {}
    for names, gath, axes in ((bulk_names, gath_b, BULK), (small_names, gath_s, SMALL)):
        pieces = [_unpack(gath[s], [given[n].shape for n in names]) for s in range(N_XY)]
        for idx, n in enumerate(names):
            full[n] = jnp.concatenate([pieces[s][idx] for s in range(N_XY)], axis=axes[n])

    def row(v):
        return v.reshape(1, -1)

    d_inner = full['m2_w_out'].shape[1]
    conv_dim = full['m2_conv_w'].shape[2]
    nheads = m2_dt_bias.shape[1]

    def layer_weights(i):
        kind, j = i % 3, i // 3
        if kind == 0:
            return dict(w_qkv=full['sb_w_qkv'][j], w_o=full['sb_w_o'][j], q_norm_w=row(sb_q_norm_w[j]),
                        k_norm_w=row(sb_k_norm_w[j]))
        if kind == 1:
            return dict(w_in=full['cf_w_in'][j], b_in=row(cf_b_in[j]), dw_w=full['cf_dw_w'][j], dw_b=row(cf_dw_b[j]),
                        ln_w=row(cf_ln_w[j]), ln_b=row(cf_ln_b[j]), w_out=full['cf_w_out'][j], b_out=row(cf_b_out[j]))
        w_in = full['m2_w_in'][j]
        w_dt = jnp.pad(w_in[:, d_inner + conv_dim:], ((0, 0), (0, LANES - nheads)))
        return dict(w_z=w_in[:, :d_inner], w_xbc=w_in[:, d_inner:d_inner + conv_dim], w_dt=w_dt,
                    conv_w=full['m2_conv_w'][j], conv_b=row(full['m2_conv_b'][j]), dt_bias=_pad_lanes(m2_dt_bias[j]),
                    a_log=_pad_lanes(m2_a_log[j]), d=_pad_lanes(m2_d[j]), norm_w=row(full['m2_norm_w'][j]),
                    w_out=full['m2_w_out'][j])

    fwd = (_sb_forward, _cf_forward, _m2_forward)
    bwd = (_sb_backward, _cf_backward, _m2_backward)

    saved = []
    xc = xl
    for i in range(depth):
        wt = layer_weights(i)
        h = _rms_fwd(xc, row(norm_mix_w[i]), f"l{i}_norm_mix")
        xm, sv = fwd[i % 3](h, xc, wt, f"l{i}_mix")
        h2 = _rms_fwd(xm, row(norm_mlp_w[i]), f"l{i}_norm_mlp")
        u, a = _mm(h2, full['mlp_w_up'][i], "nn", epilogue=_ep_relu2, out_dtypes=(F32, BF16), name=f"l{i}_up")
        xn = _mm(a, full['mlp_w_down'][i], "nn", extras=(xm,), epilogue=_ep_add, name=f"l{i}_down")
        saved.append(dict(wt=wt, x_in=xc, h=h, mix=sv, x_mid=xm, h2=h2, u=u, a=a))
        xc = xn
    dx, dxb, loss_acc = _loss(xc, tgt, "loss")
    loss = lax.psum(loss_acc[0, 0], ("x", "y", "c"))

    grads = {n: [None] * given[n].shape[0] for n in W_NAMES}
    for i in reversed(range(depth)):
        sv = saved[i]
        kind, j = i % 3, i // 3
        du = _mm(dxb, full['mlp_w_down'][i], "nt", extras=(sv['u'],), epilogue=_ep_relu2_bwd, out_dtypes=(BF16,),
                 name=f"l{i}_du")
        grads['mlp_w_down'][i] = _mm(sv['a'], dxb, "tn", name=f"l{i}_dwdown")
        grads['mlp_w_up'][i] = _mm(sv['h2'], du, "tn", name=f"l{i}_dwup")
        dh2 = _mm(du, full['mlp_w_up'][i], "nt", name=f"l{i}_dh2")
        dxm, dxmb, g_n2 = _rms_bwd(sv['x_mid'], row(norm_mlp_w[i]), dh2, dx, f"l{i}_norm_mlp_bwd")
        grads['norm_mlp_w'][i] = g_n2[0]
        dh, gw = bwd[kind](sv['h'], dxm, dxmb, sv['wt'], sv['mix'], f"l{i}_mix")
        dx, dxb, g_n1 = _rms_bwd(sv['x_in'], row(norm_mix_w[i]), dh, dxm, f"l{i}_norm_mix_bwd")
        grads['norm_mix_w'][i] = g_n1[0]
        if kind == 0:
            grads['sb_w_qkv'][j], grads['sb_w_o'][j] = gw['w_qkv'], gw['w_o']
            grads['sb_q_norm_w'][j], grads['sb_k_norm_w'][j] = gw['q_norm_w'][0], gw['k_norm_w'][0]
        elif kind == 1:
            for n in ('w_in', 'dw_w', 'w_out'):
                grads['cf_' + n][j] = gw[n]
            for n in ('b_in', 'dw_b', 'ln_w', 'ln_b', 'b_out'):
                grads['cf_' + n][j] = gw[n][0]
        else:
            grads['m2_w_in'][j] = jnp.concatenate([gw['w_z'], gw['w_xbc'], gw['w_dt'][:, :nheads]], axis=1)
            grads['m2_conv_w'][j], grads['m2_w_out'][j] = gw['conv_w'], gw['w_out']
            grads['m2_conv_b'][j], grads['m2_norm_w'][j] = gw['conv_b'][0], gw['norm_w'][0]
            for n in ('dt_bias', 'a_log', 'd'):
                grads['m2_' + n][j] = gw[n][0, :nheads]
    grad_x = dx[None]
    gfull = {n: jnp.stack(grads[n]) for n in W_NAMES}

    outs = {}
    for names, axes, dtype, tag in ((bulk_names, BULK, BF16, "bulk"), (small_names, SMALL, F32, "small")):
        split = {n: jnp.split(gfull[n], N_XY, axis=axes[n]) for n in names}
        parts = jnp.stack([halves(_pack([split[n][s] for n in names], dtype, PACK_W)) for s in range(N_XY)])
        kept, got = _pair_split(parts, f"comm_split_{tag}")
        pair = _add2(kept, got, f"sum_pair_{tag}")
        mine = _sum_parts(_exchange_xy(pair, f"comm_exchange_{tag}"), f"sum_{tag}")
        gsum = _pair_join(mine, f"comm_join_{tag}")
        for n, g in zip(names, _unpack(gsum, [given[n].shape for n in names])):
            shape = given[n].shape
            res = _adamw(*[given[p + n].reshape(-1, shape[-1]) for p in ("", "m_", "v_")], g.reshape(-1, shape[-1]),
                         None, f"adamw_{n}")
            outs["grad", n] = g
            for kind, arr in zip(("delta", "new_m", "new_v"), res):
                outs[kind, n] = arr.reshape(shape)
    rparts = _all_gather_all(_pack([gfull[n] for n in REPL], F32, LANES), "comm_gather_repl")
    nsplit = N_DEV // 2
    res = _adamw(*[_pack([given[p + n] for n in REPL], F32, LANES) for p in ("", "m_", "v_")],
                 _sum_parts(rparts[:nsplit], "sum_repl_a"), _sum_parts(rparts[nsplit:], "sum_repl_b"), "adamw_repl")
    shapes = [given[n].shape for n in REPL]
    for kind, buf in zip(("grad", "delta", "new_m", "new_v"), res):
        for n, arr in zip(REPL, _unpack(buf, shapes)):
            outs[kind, n] = arr

    return (loss, grad_x, *[outs[kind, n] for kind in ("grad", "delta", "new_m", "new_v") for n in W_NAMES])
```

```python
import math
from typing import NamedTuple

import jax
import jax.numpy as jnp
from jax import lax
from jax.experimental import pallas as pl
from jax.experimental.pallas import tpu as pltpu

F32 = jnp.float32
BF16 = jnp.bfloat16
MESH_ID = pl.DeviceIdType.MESH
HIGHEST = lax.Precision.HIGHEST

SB_HEAD_DIM = 128
M2_HEAD_DIM = 64
M2_STATE = 128
M2_GROUPS = 8
M2_CHUNK = 128
RMS_EPS = 1e-6
LN_EPS = 1e-5
M2_NORM_EPS = 1e-5
ADAM_LR = 0.001
ADAM_B1 = 0.9
ADAM_B2 = 0.999
ADAM_EPS = 1e-08
ADAM_WD = 0.01
ADAM_STEP = 10

N_XY = 4
N_DEV = 8

V7X_VMEM_BYTES = 64 * 2**20
VMEM_LIMIT = (V7X_VMEM_BYTES * 3) // 4
LANES = 128
ROW_BLOCK_BYTES = 6 * 2**20
MM_VMEM_BUDGET = (VMEM_LIMIT * 3) // 4
PACK_W = 1024
PACK_ROWS = 1024

W_NAMES = ['norm_mix_w', 'norm_mlp_w', 'sb_w_qkv', 'sb_q_norm_w', 'sb_k_norm_w', 'sb_w_o', 'cf_w_in', 'cf_b_in',
           'cf_dw_w', 'cf_dw_b', 'cf_ln_w', 'cf_ln_b', 'cf_w_out', 'cf_b_out', 'm2_w_in', 'm2_conv_w', 'm2_conv_b',
           'm2_dt_bias', 'm2_a_log', 'm2_d', 'm2_norm_w', 'm2_w_out', 'mlp_w_up', 'mlp_w_down']
BULK = {'sb_w_qkv': 2, 'sb_w_o': 1, 'cf_w_in': 2, 'cf_w_out': 1, 'm2_w_in': 2, 'm2_w_out': 1, 'mlp_w_up': 2,
        'mlp_w_down': 1}
SMALL = {'cf_dw_w': 2, 'm2_conv_w': 2, 'm2_conv_b': 1, 'm2_norm_w': 1}
REPL = ['norm_mix_w', 'norm_mlp_w', 'sb_q_norm_w', 'sb_k_norm_w', 'cf_b_in', 'cf_dw_b', 'cf_ln_w', 'cf_ln_b',
        'cf_b_out', 'm2_dt_bias', 'm2_a_log', 'm2_d']


def _tile(n, prefs):
    for p in prefs:
        if n % p == 0:
            return p
    return n


def _cparams(n):
    return pltpu.CompilerParams(dimension_semantics=("arbitrary",) * n, vmem_limit_bytes=VMEM_LIMIT)


def _silu(v):
    return v / (1.0 + jnp.exp(-v))


def _softplus(v):
    return jnp.maximum(v, 0.0) + jnp.log(1.0 + jnp.exp(-jnp.abs(v)))


class WView(NamedTuple):
    arr: jax.Array
    layer: int
    kind: str


def _mm(a, b, mode, *, extras=(), epilogue=None, out_dtypes=(F32,), out_col_shards=1, name):
    if mode == "tn":
        kdim, m = a.shape
    else:
        m, kdim = a.shape
    view = b if isinstance(b, WView) else None
    if view is not None:
        _, _, srows, scols = view.arr.shape
        brows, bcols = (N_XY * srows, scols) if view.kind == "rows" else (srows, N_XY * scols)
        b_arr, b_item = view.arr, view.arr.dtype.itemsize
    else:
        brows, bcols = b.shape
        b_arr, b_item = b, b.dtype.itemsize
    n = brows if mode == "nt" else bcols
    assert kdim == (bcols if mode == "nt" else brows)
    n_unit, k_unit = n // out_col_shards, kdim
    if view is not None:
        if (mode == "nn") == (view.kind == "cols"):
            n_unit = n // N_XY
        else:
            k_unit = kdim // N_XY
    tm = _tile(m, (1024, 512, 256, 128))
    tn = _tile(n_unit, (1024, 512, 256, 128))
    acc_in_out = out_dtypes[0] == F32 and out_col_shards == 1
    fixed = sum(tm * tn * jnp.dtype(dt).itemsize for dt in out_dtypes)
    fixed += sum((1 if e.shape[0] == 1 else tm) * tn * e.dtype.itemsize for e in extras)

    def vmem_bytes(t):
        ab = tm * t * a.dtype.itemsize + t * tn * b_item
        casts = (tm * t * 2 if a.dtype != BF16 else 0) + (t * tn * 2 if b_item != 2 else 0)
        return 2 * (ab + fixed) + (0 if acc_in_out else tm * tn * 4) + tm * tn * 4 + casts

    tk = k_unit
    for cand in (2048, 1536, 1024, 768, 512, 256, 128):
        if k_unit % cand == 0:
            tk = cand
            if vmem_bytes(cand) <= MM_VMEM_BUDGET:
                break
    nk = kdim // tk
    dims = {"nn": ((1,), (0,)), "nt": ((1,), (1,)), "tn": ((0,), (0,))}[mode]
    if mode == "tn":
        a_spec = pl.BlockSpec((tk, tm), lambda i, j, k: (k, i))
    else:
        a_spec = pl.BlockSpec((tm, tk), lambda i, j, k: (i, k))
    if view is None:
        if mode == "nt":
            b_spec = pl.BlockSpec((tn, tk), lambda i, j, k: (j, k))
        else:
            b_spec = pl.BlockSpec((tk, tn), lambda i, j, k: (k, j))
    else:
        lay = view.layer
        rc = (lambda j, k: (j, k)) if mode == "nt" else (lambda j, k: (k, j))
        blk = (tn, tk) if mode == "nt" else (tk, tn)
        if view.kind == "rows":
            per = srows // blk[0]
            b_spec = pl.BlockSpec((None, None) + blk,
                                  lambda i, j, k: (rc(j, k)[0] // per, lay, rc(j, k)[0] % per, rc(j, k)[1]))
        else:
            per = scols // blk[1]
            b_spec = pl.BlockSpec((None, None) + blk,
                                  lambda i, j, k: (rc(j, k)[1] // per, lay, rc(j, k)[0], rc(j, k)[1] % per))
    ex_specs = [pl.BlockSpec((1, tn), lambda i, j, k: (0, j)) if e.shape[0] == 1
                else pl.BlockSpec((tm, tn), lambda i, j, k: (i, j)) for e in extras]
    n_ex, n_out = len(extras), len(out_dtypes)

    def body(*refs):
        a_ref, b_ref = refs[:2]
        ex = refs[2:2 + n_ex]
        outs = refs[2 + n_ex:2 + n_ex + n_out]
        acc = outs[0] if acc_in_out else refs[-1]
        k = pl.program_id(2)
        prod = lax.dot_general(a_ref[...].astype(BF16), b_ref[...].astype(BF16), (dims, ((), ())),
                               preferred_element_type=F32)

        def finish(res):
            vals = epilogue(res, *[e[...] for e in ex]) if epilogue is not None else (res,)
            for o, v in zip(outs, vals):
                o[...] = v.astype(o.dtype)

        if nk == 1:
            finish(prod)
        else:
            @pl.when(k == 0)
            def _():
                acc[...] = prod

            @pl.when((k > 0) & (k < nk - 1))
            def _():
                acc[...] += prod

            @pl.when(k == nk - 1)
            def _():
                finish(acc[...] + prod)

    scratch = [] if (acc_in_out or nk == 1) else [pltpu.VMEM((tm, tn), F32)]
    if out_col_shards == 1:
        out_specs = [pl.BlockSpec((tm, tn), lambda i, j, k: (i, j))] * n_out
        out_shape = [jax.ShapeDtypeStruct((m, n), d) for d in out_dtypes]
    else:
        per_out = n_unit // tn
        out_specs = [pl.BlockSpec((None, tm, tn), lambda i, j, k: (j // per_out, i, j % per_out))]
        out_shape = [jax.ShapeDtypeStruct((out_col_shards, m, n_unit), out_dtypes[0])]
    outs = pl.pallas_call(
        body, grid=(m // tm, n // tn, nk), in_specs=[a_spec, b_spec, *ex_specs], out_specs=out_specs,
        out_shape=out_shape, scratch_shapes=scratch, compiler_params=_cparams(3), name=name)(a, b_arr, *extras)
    return outs[0] if n_out == 1 else outs


def _ep_add(acc, r):
    return (acc + r,)


def _ep_bias_add(acc, b, r):
    return (acc + b + r,)


def _ep_relu2(acc):
    return acc, jnp.square(jnp.maximum(acc, 0.0))


def _ep_relu2_bwd(acc, u):
    return (acc * (2.0 * jnp.maximum(u, 0.0)),)


def _rowwise(fn, rows, consts, outs, *, ncb=1, name):
    nrow = rows[0][0].shape[0]
    row_bytes = sum(w * arr.dtype.itemsize for arr, w, _ in rows)
    row_bytes += sum(o[2] * jnp.dtype(o[3]).itemsize for o in outs if o[0] == 'row')
    t = nrow
    for cand in (1024, 512, 256, 128, 64, 32, 16):
        if nrow % cand == 0:
            t = cand
            if cand * row_bytes <= ROW_BLOCK_BYTES:
                break
    in_specs = [pl.BlockSpec((t, w), (lambda j, i, off=off: (i, off + j))) for _, w, off in rows]
    for arr, per_col in consts:
        r, wc = arr.shape
        if per_col:
            in_specs.append(pl.BlockSpec((r, wc // ncb), lambda j, i: (0, j)))
        else:
            in_specs.append(pl.BlockSpec((r, wc), lambda j, i: (0, 0)))
    out_specs, out_shape = [], []
    for o in outs:
        if o[0] == 'row':
            out_specs.append(pl.BlockSpec((t, o[2]), lambda j, i: (i, j)))
            out_shape.append(jax.ShapeDtypeStruct((nrow, o[1]), o[3]))
        else:
            r, wt = o[1]
            if o[2]:
                out_specs.append(pl.BlockSpec((r, wt // ncb), lambda j, i: (0, j)))
            else:
                out_specs.append(pl.BlockSpec((r, wt), lambda j, i: (0, 0)))
            out_shape.append(jax.ShapeDtypeStruct((r, wt), F32))
    nin = len(rows) + len(consts)

    def body(*refs):
        j, i = pl.program_id(0), pl.program_id(1)
        vals = fn(*[r[...] for r in refs[:nin]])

        def store(spec, ref, v):
            if spec[0] == 'row':
                ref[...] = v.astype(ref.dtype)
            else:
                first = (i == 0) if spec[2] else ((i == 0) & (j == 0))

                @pl.when(first)
                def _():
                    ref[...] = jnp.zeros_like(ref)

                ref[...] += v

        for spec, ref, v in zip(outs, refs[nin:], vals):
            store(spec, ref, v)

    res = pl.pallas_call(body, grid=(ncb, nrow // t), in_specs=in_specs, out_specs=out_specs, out_shape=out_shape,
                         compiler_params=_cparams(2), name=name)(*[r[0] for r in rows], *[c[0] for c in consts])
    return res


def _rms_f(eps):
    def f(x, w):
        return x * lax.rsqrt(jnp.mean(x * x, axis=-1, keepdims=True) + eps) * w
    return f


def _rms_fwd(x, w, name):
    d = x.shape[1]
    f = _rms_f(RMS_EPS)
    return _rowwise(lambda xv, wv: (f(xv, wv),), [(x, d, 0)], [(w, False)], [('row', d, d, BF16)], name=name)[0]


def _rms_bwd(x, w, dh, dres, name):
    d = x.shape[1]
    f = _rms_f(RMS_EPS)

    def fn(xv, dhv, drv, wv):
        _, vjp = jax.vjp(f, xv, wv)
        dx, dw = vjp(dhv)
        return dx + drv, dx + drv, dw

    return _rowwise(fn, [(x, d, 0), (dh, d, 0), (dres, d, 0)], [(w, False)],
                    [('row', d, d, F32), ('row', d, d, BF16), ('acc', (1, d), False)], name=name)


def _colsum(x, name):
    w = x.shape[1]
    return _rowwise(lambda v: (jnp.sum(v, axis=0, keepdims=True),), [(x, w, 0)], [], [('acc', (1, w), False)],
                    name=name)[0]


def _loss(y, tgt, name):
    d = y.shape[1]

    def fn(yv, tv):
        e = yv - tv
        s = jnp.sum(jnp.sum(e * e, axis=1, keepdims=True), axis=0, keepdims=True) * (0.5 / d)
        return e * (1.0 / d), e * (1.0 / d), s + jnp.zeros((1, LANES), F32)

    return _rowwise(fn, [(y, d, 0), (tgt, d, 0)], [],
                    [('row', d, d, F32), ('row', d, d, BF16), ('acc', (1, LANES), False)], name=name)


def _conv_tiles(x, w):
    nrow, ch = x.shape
    kw = w.shape[0]
    halo = 8 * ((kw - 1 + 7) // 8)
    t = _tile(nrow, (256, 128, 64, 32))
    tc = _tile(ch, (512, 256, 128))
    assert t % halo == 0 and nrow % t == 0
    return nrow, ch, kw, halo, t, tc


def _dwconv_fwd(x, w, b, name):
    nrow, ch, kw, halo, t, tc = _conv_tiles(x, w)
    per = t // halo

    def body(x_ref, halo_ref, w_ref, b_ref, y_ref, buf):
        i = pl.program_id(1)
        buf[0:halo, :] = jnp.where(i == 0, 0.0, halo_ref[...])
        buf[halo:halo + t, :] = x_ref[...]
        acc = jnp.zeros((t, tc), F32) + b_ref[...]
        for k in range(kw):
            s = halo - (kw - 1) + k
            acc = acc + w_ref[k:k + 1, :] * buf[s:s + t, :]
        y_ref[...] = acc

    return pl.pallas_call(
        body, grid=(ch // tc, nrow // t),
        in_specs=[pl.BlockSpec((t, tc), lambda j, i: (i, j)),
                  pl.BlockSpec((halo, tc), lambda j, i: (jnp.maximum(i * per - 1, 0), j)),
                  pl.BlockSpec((kw, tc), lambda j, i: (0, j)), pl.BlockSpec((1, tc), lambda j, i: (0, j))],
        out_specs=pl.BlockSpec((t, tc), lambda j, i: (i, j)), out_shape=jax.ShapeDtypeStruct((nrow, ch), F32),
        scratch_shapes=[pltpu.VMEM((t + halo, tc), F32)], compiler_params=_cparams(2), name=name)(x, x, w, b)


def _dwconv_bwd(x, dy, w, name):
    nrow, ch, kw, halo, t, tc = _conv_tiles(x, w)
    per = t // halo
    nt = nrow // t
    nhalo = nrow // halo

    def body(x_ref, xh_ref, dy_ref, dyh_ref, w_ref, dx_ref, dw_ref, db_ref, xbuf, dybuf):
        i = pl.program_id(1)
        xbuf[0:halo, :] = jnp.where(i == 0, 0.0, xh_ref[...])
        xbuf[halo:halo + t, :] = x_ref[...]
        dyc = dy_ref[...]
        dybuf[0:t, :] = dyc
        dybuf[t:t + halo, :] = jnp.where(i == nt - 1, 0.0, dyh_ref[...])
        acc = jnp.zeros((t, tc), F32)
        for k in range(kw):
            s = kw - 1 - k
            acc = acc + w_ref[k:k + 1, :] * dybuf[s:s + t, :]
        dx_ref[...] = acc

        @pl.when(i == 0)
        def _():
            dw_ref[...] = jnp.zeros_like(dw_ref)
            db_ref[...] = jnp.zeros_like(db_ref)

        for k in range(kw):
            s = kw - 1 - k
            dw_ref[k:k + 1, :] += jnp.sum(dyc * xbuf[halo - s:halo - s + t, :], axis=0, keepdims=True)
        db_ref[...] += jnp.sum(dyc, axis=0, keepdims=True)

    return pl.pallas_call(
        body, grid=(ch // tc, nt),
        in_specs=[pl.BlockSpec((t, tc), lambda j, i: (i, j)),
                  pl.BlockSpec((halo, tc), lambda j, i: (jnp.maximum(i * per - 1, 0), j)),
                  pl.BlockSpec((t, tc), lambda j, i: (i, j)),
                  pl.BlockSpec((halo, tc), lambda j, i: (jnp.minimum((i + 1) * per, nhalo - 1), j)),
                  pl.BlockSpec((kw, tc), lambda j, i: (0, j))],
        out_specs=[pl.BlockSpec((t, tc), lambda j, i: (i, j)), pl.BlockSpec((kw, tc), lambda j, i: (0, j)),
                   pl.BlockSpec((1, tc), lambda j, i: (0, j))],
        out_shape=[jax.ShapeDtypeStruct((nrow, ch), F32), jax.ShapeDtypeStruct((kw, ch), F32),
                   jax.ShapeDtypeStruct((1, ch), F32)],
        scratch_shapes=[pltpu.VMEM((t + halo, tc), F32), pltpu.VMEM((t + halo, tc), F32)],
        compiler_params=_cparams(2), name=name)(x, x, dy, dy, w)


SB_SUB = 128


def _dot_split2(x, u):
    hi = x.astype(BF16)
    lo = (x - hi.astype(F32)).astype(BF16)
    return jnp.dot(hi, u, preferred_element_type=F32) + jnp.dot(lo, u, preferred_element_type=F32)


def _sb_scores(q, k, strict, scale):
    z = lax.dot_general(q, k, (((1,), (1,)), ((), ())), preferred_element_type=F32) * scale
    soft = jnp.log(1.0 + jnp.exp(-jnp.abs(z)))
    ls = jnp.minimum(z, 0.0) - soft
    lk = ls - z
    if strict is not None:
        lk = jnp.where(strict, lk, 0.0)
    return ls, lk


def _sb_running(x, tri, start, reverse):
    nsub = x.shape[1] // SB_SUB
    parts = [x[:, SB_SUB * b:SB_SUB * (b + 1)] for b in range(nsub)]
    out = [None] * nsub
    run = start
    for b in (reversed(range(nsub)) if reverse else range(nsub)):
        out[b] = _dot_split2(parts[b], tri) + run
        run = run + jnp.sum(parts[b], axis=1, keepdims=True)
    return (jnp.concatenate(out, axis=1) if nsub > 1 else out[0]), run


def _sb_dims(qn):
    nrow, d = qn.shape
    hd = SB_HEAD_DIM
    t = _tile(nrow, (512, 256, 128))
    assert nrow % t == 0 and nrow // t <= LANES
    return nrow, d, hd, d // hd, t


def _sb_masks(t):
    strict = lax.broadcasted_iota(jnp.int32, (t, t), 1) < lax.broadcasted_iota(jnp.int32, (t, t), 0)
    r0 = lax.broadcasted_iota(jnp.int32, (SB_SUB, SB_SUB), 0)
    c0 = lax.broadcasted_iota(jnp.int32, (SB_SUB, SB_SUB), 1)
    return strict, (r0 > c0).astype(BF16), (r0 < c0).astype(BF16)


def _sb_attn_fwd(qn, kn, qkv, name):
    nrow, d, hd, nh, t = _sb_dims(qn)
    scale = 1.0 / math.sqrt(hd)

    def body(q_ref, k_ref, v_ref, o_ref, rs_ref, r_acc, o_acc):
        i = pl.program_id(1)
        q = q_ref[...]
        strict, after, _ = _sb_masks(t)
        lane = lax.broadcasted_iota(jnp.int32, (t, LANES), 1)
        r_acc[...] = jnp.zeros_like(r_acc)
        o_acc[...] = jnp.zeros_like(o_acc)
        rs_ref[...] = jnp.zeros_like(rs_ref)

        def tile(j, mask):
            start = pl.multiple_of(j * t, t)
            k = k_ref[pl.ds(start, t), :]
            v = v_ref[pl.ds(start, t), :].astype(BF16)
            ls, lk = _sb_scores(q, k, mask, scale)
            r = r_acc[...]
            later, r_next = _sb_running(lk, after, r, True)
            att = jnp.exp(ls + later)
            if mask is not None:
                att = jnp.where(mask, att, 0.0)
            o_acc[...] += jnp.dot(att.astype(BF16), v, preferred_element_type=F32)
            rs_ref[...] = jnp.where(lane == j, r, rs_ref[...])
            r_acc[...] = r_next

        tile(i, strict)

        def step(n, carry):
            tile(i - 1 - n, None)
            return carry

        lax.fori_loop(0, i, step, 0)
        o_ref[...] = o_acc[...].astype(o_ref.dtype)

    return pl.pallas_call(
        body, grid=(nh, nrow // t),
        in_specs=[pl.BlockSpec((t, hd), lambda h, i: (i, h)), pl.BlockSpec((nrow, hd), lambda h, i: (0, h)),
                  pl.BlockSpec((nrow, hd), lambda h, i: (0, 2 * nh + h))],
        out_specs=[pl.BlockSpec((t, hd), lambda h, i: (i, h)), pl.BlockSpec((None, t, LANES), lambda h, i: (h, i, 0))],
        out_shape=[jax.ShapeDtypeStruct((nrow, d), BF16), jax.ShapeDtypeStruct((nh, nrow, LANES), F32)],
        scratch_shapes=[pltpu.VMEM((t, LANES), F32), pltpu.VMEM((t, hd), F32)],
        compiler_params=_cparams(2), name=name)(qn, kn, qkv)


def _sb_attn_bwd(qn, kn, qkv, do, rsave, name):
    nrow, d, hd, nh, t = _sb_dims(qn)
    scale = 1.0 / math.sqrt(hd)

    def body(q_ref, k_ref, v_ref, do_ref, rs_ref, dq_ref, dk_ref, dv_ref, pg_acc):
        i = pl.program_id(1)

        @pl.when(i == 0)
        def _():
            dk_ref[...] = jnp.zeros_like(dk_ref)
            dv_ref[...] = jnp.zeros_like(dv_ref)

        pg_acc[...] = jnp.zeros_like(pg_acc)
        dq_ref[...] = jnp.zeros_like(dq_ref)
        q = q_ref[...]
        dob = do_ref[...].astype(BF16)
        strict, after, before = _sb_masks(t)
        lane = lax.broadcasted_iota(jnp.int32, (t, LANES), 1)

        def tile(j, mask):
            start = pl.multiple_of(j * t, t)
            k = k_ref[pl.ds(start, t), :]
            v = v_ref[pl.ds(start, t), :].astype(BF16)
            ls, lk = _sb_scores(q, k, mask, scale)
            rj = jnp.sum(jnp.where(lane == j, rs_ref[...], 0.0), axis=1, keepdims=True)
            later, _ = _sb_running(lk, after, jnp.broadcast_to(rj, (t, LANES)), True)
            att = jnp.exp(ls + later)
            if mask is not None:
                att = jnp.where(mask, att, 0.0)
            datt = lax.dot_general(dob, v, (((1,), (1,)), ((), ())), preferred_element_type=F32)
            g = datt * att
            dlk, pg_next = _sb_running(g, before, pg_acc[...], False)
            sig = jnp.exp(ls)
            dz = g * (1.0 - sig) - dlk * sig
            if mask is not None:
                dz = jnp.where(mask, dz, 0.0)
            dz = (dz * scale).astype(BF16)
            dq_ref[...] += jnp.dot(dz, k, preferred_element_type=F32)
            dk_ref[pl.ds(start, t), :] += lax.dot_general(dz, q, (((0,), (0,)), ((), ())), preferred_element_type=F32)
            dv_ref[pl.ds(start, t), :] += lax.dot_general(att.astype(BF16), dob, (((0,), (0,)), ((), ())),
                                                          preferred_element_type=F32)
            pg_acc[...] = pg_next

        def step(j, carry):
            tile(j, None)
            return carry

        lax.fori_loop(0, i, step, 0)
        tile(i, strict)

    return pl.pallas_call(
        body, grid=(nh, nrow // t),
        in_specs=[pl.BlockSpec((t, hd), lambda h, i: (i, h)), pl.BlockSpec((nrow, hd), lambda h, i: (0, h)),
                  pl.BlockSpec((nrow, hd), lambda h, i: (0, 2 * nh + h)), pl.BlockSpec((t, hd), lambda h, i: (i, h)),
                  pl.BlockSpec((None, t, LANES), lambda h, i: (h, i, 0))],
        out_specs=[pl.BlockSpec((t, hd), lambda h, i: (i, h)), pl.BlockSpec((nrow, hd), lambda h, i: (0, h)),
                   pl.BlockSpec((nrow, hd), lambda h, i: (0, h))],
        out_shape=[jax.ShapeDtypeStruct((nrow, d), F32)] * 3, scratch_shapes=[pltpu.VMEM((t, LANES), F32)],
        compiler_params=_cparams(2), name=name)(qn, kn, qkv, do, rsave)


def _ssd_chunk_fn(g, nheads_g, q):
    p = M2_HEAD_DIM
    npair = nheads_g // 2

    def f(xp, bp, cp, dtr, hp, dtb, alog, dsk):
        b = _silu(bp).astype(BF16)
        c = _silu(cp).astype(BF16)
        dt = _softplus(dtr + dtb)
        dta = dt * (-jnp.exp(alog))
        ri = lax.broadcasted_iota(jnp.int32, (q, q), 0)
        ci = lax.broadcasted_iota(jnp.int32, (q, q), 1)
        causal = ri >= ci
        tri = causal.astype(F32)
        acol = jnp.dot(tri, dta, precision=HIGHEST, preferred_element_type=F32)
        arow = lax.dot_general(dta, tri, (((0,), (1,)), ((), ())), precision=HIGHEST,
                               preferred_element_type=F32)
        alast = jnp.sum(dta, axis=0, keepdims=True)
        cb = lax.dot_general(c, b, (((1,), (1,)), ((), ())), preferred_element_type=F32)
        lane = lax.broadcasted_iota(jnp.int32, (1, LANES), 1)
        sub = lax.broadcasted_iota(jnp.int32, (LANES, 1), 0)
        lane_half = lane < p
        sub_half = sub < p

        def col(mat, h):
            return jnp.sum(jnp.where(lane == h, mat, 0.0), axis=1, keepdims=True)

        def row(mat, h):
            return jnp.sum(jnp.where(sub == h, mat, 0.0), axis=0, keepdims=True)

        def mix(v0, v1):
            return jnp.where(lane_half, v0, v1)

        ys, hns = [], []
        for jp in range(npair):
            h0 = g * nheads_g + 2 * jp
            h1 = h0 + 1
            x = _silu(xp[jp])
            ac0, ac1 = col(acol, h0), col(acol, h1)
            xdt = x * mix(col(dt, h0), col(dt, h1))
            xdtb = xdt.astype(BF16)
            yd = []
            for h, ac in ((h0, ac0), (h1, ac1)):
                dec = jnp.exp(jnp.where(causal, ac - row(arow, h), -jnp.inf))
                yd.append(jnp.dot((cb * dec).astype(BF16), xdtb, preferred_element_type=F32))
            hpj = hp[jp]
            yo = lax.dot_general(c, hpj.astype(BF16), (((1,), (1,)), ((), ())), preferred_element_type=F32)
            y = mix(yd[0], yd[1]) + yo * jnp.exp(mix(ac0, ac1)) + mix(col(dsk, h0), col(dsk, h1)) * x
            al0, al1 = col(alast, h0), col(alast, h1)
            dout = jnp.exp(mix(al0 - ac0, al1 - ac1))
            st = lax.dot_general((xdt * dout).astype(BF16), b, (((0,), (0,)), ((), ())), preferred_element_type=F32)
            hns.append(hpj * jnp.exp(jnp.where(sub_half, al0, al1)) + st)
            ys.append(y)
        return ys, hns

    return f


def _ssd_dims(pre, dtr):
    nrow = pre.shape[0]
    ng, n, p, q = M2_GROUPS, M2_STATE, M2_HEAD_DIM, M2_CHUNK
    d_inner = pre.shape[1] - 2 * ng * n
    gw = d_inner // ng
    nhg = gw // p
    assert nhg % 2 == 0 and gw % LANES == 0 and n == LANES and ng * nhg <= LANES and dtr.shape[1] == LANES
    return nrow, ng, n, q, d_inner, gw, nhg, nrow // q


def _ssd_fwd(pre, dtr, dtb, alog, dsk, name):
    nrow, ng, n, q, d_inner, gw, nhg, nc = _ssd_dims(pre, dtr)
    boff = d_inner // n
    npair = nhg // 2

    def body(x_ref, b_ref, c_ref, dt_ref, dtb_ref, al_ref, ds_ref, y_ref, hs_ref, state):
        ci, g = pl.program_id(0), pl.program_id(1)

        @pl.when(ci == 0)
        def _():
            state[g] = jnp.zeros((gw, n), F32)

        hs_ref[...] = state[g]
        f = _ssd_chunk_fn(g, nhg, q)
        xp = [x_ref[:, LANES * j:LANES * (j + 1)] for j in range(npair)]
        hp = [state[g, LANES * j:LANES * (j + 1), :] for j in range(npair)]
        ys, hns = f(xp, b_ref[...], c_ref[...], dt_ref[...], hp, dtb_ref[...], al_ref[...], ds_ref[...])
        for j in range(npair):
            y_ref[:, LANES * j:LANES * (j + 1)] = ys[j]
            state[g, LANES * j:LANES * (j + 1), :] = hns[j]

    par = pl.BlockSpec((1, LANES), lambda ci, g: (0, 0))
    return pl.pallas_call(
        body, grid=(nc, ng),
        in_specs=[pl.BlockSpec((q, gw), lambda ci, g: (ci, g)), pl.BlockSpec((q, n), lambda ci, g: (ci, boff + g)),
                  pl.BlockSpec((q, n), lambda ci, g: (ci, boff + ng + g)), pl.BlockSpec((q, LANES), lambda ci, g: (ci, 0)),
                  par, par, par],
        out_specs=[pl.BlockSpec((q, gw), lambda ci, g: (ci, g)),
                   pl.BlockSpec((None, None, gw, n), lambda ci, g: (ci, g, 0, 0))],
        out_shape=[jax.ShapeDtypeStruct((nrow, d_inner), F32), jax.ShapeDtypeStruct((nc, ng, gw, n), F32)],
        scratch_shapes=[pltpu.VMEM((ng, gw, n), F32)], compiler_params=_cparams(2), name=name)(
            pre, pre, pre, dtr, dtb, alog, dsk)


def _ssd_bwd(pre, dtr, hs, dy, dtb, alog, dsk, name):
    nrow, ng, n, q, d_inner, gw, nhg, nc = _ssd_dims(pre, dtr)
    boff = d_inner // n
    npair = nhg // 2

    def body(x_ref, b_ref, c_ref, dt_ref, hs_ref, dy_ref, dtb_ref, al_ref, ds_ref,
             dx_ref, db_ref, dc_ref, ddt_ref, ddtb_ref, dal_ref, dds_ref, dstate):
        ci, g = pl.program_id(0), pl.program_id(1)

        @pl.when(ci == 0)
        def _():
            dstate[g] = jnp.zeros((gw, n), F32)

        @pl.when((ci == 0) & (g == 0))
        def _():
            ddtb_ref[...] = jnp.zeros_like(ddtb_ref)
            dal_ref[...] = jnp.zeros_like(dal_ref)
            dds_ref[...] = jnp.zeros_like(dds_ref)

        @pl.when(g == 0)
        def _():
            ddt_ref[...] = jnp.zeros_like(ddt_ref)

        f = _ssd_chunk_fn(g, nhg, q)
        xp = [x_ref[:, LANES * j:LANES * (j + 1)] for j in range(npair)]
        hp = [hs_ref[LANES * j:LANES * (j + 1), :] for j in range(npair)]
        _, vjp = jax.vjp(f, xp, b_ref[...], c_ref[...], dt_ref[...], hp, dtb_ref[...], al_ref[...], ds_ref[...])
        dys = [dy_ref[:, LANES * j:LANES * (j + 1)] for j in range(npair)]
        dhn = [dstate[g, LANES * j:LANES * (j + 1), :] for j in range(npair)]
        dxp, db, dc, ddt, dhp, ddtb, dal, dds = vjp((dys, dhn))
        for j in range(npair):
            dx_ref[:, LANES * j:LANES * (j + 1)] = dxp[j]
            dstate[g, LANES * j:LANES * (j + 1), :] = dhp[j]
        db_ref[...] = db
        dc_ref[...] = dc
        ddt_ref[...] += ddt
        ddtb_ref[...] += ddtb
        dal_ref[...] += dal
        dds_ref[...] += dds

    par = pl.BlockSpec((1, LANES), lambda ci, g: (0, 0))
    last = nc - 1
    return pl.pallas_call(
        body, grid=(nc, ng),
        in_specs=[pl.BlockSpec((q, gw), lambda ci, g: (last - ci, g)),
                  pl.BlockSpec((q, n), lambda ci, g: (last - ci, boff + g)),
                  pl.BlockSpec((q, n), lambda ci, g: (last - ci, boff + ng + g)),
                  pl.BlockSpec((q, LANES), lambda ci, g: (last - ci, 0)),
                  pl.BlockSpec((None, None, gw, n), lambda ci, g: (last - ci, g, 0, 0)),
                  pl.BlockSpec((q, gw), lambda ci, g: (last - ci, g)), par, par, par],
        out_specs=[pl.BlockSpec((q, gw), lambda ci, g: (last - ci, g)), pl.BlockSpec((q, n), lambda ci, g: (last - ci, g)),
                   pl.BlockSpec((q, n), lambda ci, g: (last - ci, g)), pl.BlockSpec((q, LANES), lambda ci, g: (last - ci, 0)),
                   par, par, par],
        out_shape=[jax.ShapeDtypeStruct((nrow, d_inner), F32), jax.ShapeDtypeStruct((nrow, ng * n), F32),
                   jax.ShapeDtypeStruct((nrow, ng * n), F32), jax.ShapeDtypeStruct((nrow, LANES), F32),
                   jax.ShapeDtypeStruct((1, LANES), F32), jax.ShapeDtypeStruct((1, LANES), F32),
                   jax.ShapeDtypeStruct((1, LANES), F32)],
        scratch_shapes=[pltpu.VMEM((ng, gw, n), F32)], compiler_params=_cparams(2), name=name)(
            pre, pre, pre, dtr, hs, dy, dtb, alog, dsk)


HBM_SPEC = pl.BlockSpec(memory_space=pl.ANY)


def _xy_peers(mx, my):
    return [(1 - mx, my), (mx, 1 - my), (1 - mx, 1 - my)]


def _remote(src, dst, send_sems, recv_sems, k, dev):
    return pltpu.make_async_remote_copy(src_ref=src, dst_ref=dst, send_sem=send_sems.at[k], recv_sem=recv_sems.at[k],
                                        device_id=dev, device_id_type=MESH_ID)


def _comm_call(body, arrays, out_shapes, ncopies, name):
    nin = len(arrays)

    def wrapped(*refs):
        body(refs[:nin], refs[nin:nin + len(out_shapes)], refs[-2], refs[-1])

    return pl.pallas_call(
        wrapped, out_shape=out_shapes, in_specs=[HBM_SPEC] * nin, out_specs=[HBM_SPEC] * len(out_shapes),
        scratch_shapes=[pltpu.SemaphoreType.DMA((ncopies,)), pltpu.SemaphoreType.DMA((ncopies,))], name=name)(*arrays)


def _all_gather_xy(bufs, name):
    npeer = N_XY - 1

    def body(srcs, outs, send_sems, recv_sems):
        mx, my, mc = lax.axis_index("x"), lax.axis_index("y"), lax.axis_index("c")
        me = 2 * mx + my
        sibling = (mx, my, 1 - mc)
        peers = _xy_peers(mx, my)
        sends = []
        for a, (src, out) in enumerate(zip(srcs, outs)):
            for k, (px, py) in enumerate(peers):
                sends.append(_remote(src.at[mc], out.at[me, mc], send_sems, recv_sems, 2 * npeer * a + k, (px, py, mc)))
        for cp in sends:
            cp.start()
        for a, (src, out) in enumerate(zip(srcs, outs)):
            for k, (px, py) in enumerate(peers):
                landed = out.at[2 * px + py, mc]
                _remote(src.at[mc], landed, send_sems, recv_sems, 2 * npeer * a + k, (px, py, mc)).wait_recv()
                cp = _remote(landed, landed, send_sems, recv_sems, 2 * npeer * a + npeer + k, sibling)
                cp.start()
                sends.append(cp)
        for a, out in enumerate(outs):
            for k, (px, py) in enumerate(peers):
                passed = out.at[2 * px + py, 1 - mc]
                _remote(passed, passed, send_sems, recv_sems, 2 * npeer * a + npeer + k, sibling).wait_recv()
        for cp in sends:
            cp.wait_send()

    outs = _comm_call(body, bufs, [jax.ShapeDtypeStruct((N_XY,) + b.shape, b.dtype) for b in bufs],
                      2 * npeer * len(bufs), name)
    return [lax.dynamic_update_index_in_dim(o, b, _my_shard(), 0) for o, b in zip(outs, bufs)]


def _my_shard():
    return 2 * lax.axis_index("x") + lax.axis_index("y")


def _exchange_xy(parts, name):
    npeer = N_XY - 1

    def body(srcs, outs, send_sems, recv_sems):
        mx, my, mc = lax.axis_index("x"), lax.axis_index("y"), lax.axis_index("c")
        me = 2 * mx + my
        peers = _xy_peers(mx, my)
        sends = [_remote(src.at[2 * px + py], out.at[me], send_sems, recv_sems, npeer * a + k, (px, py, mc))
                 for k, (px, py) in enumerate(peers) for a, (src, out) in enumerate(zip(srcs, outs))]
        for cp in sends:
            cp.start()
        for k, (px, py) in enumerate(peers):
            for a, (src, out) in enumerate(zip(srcs, outs)):
                _remote(src.at[me], out.at[2 * px + py], send_sems, recv_sems, npeer * a + k, (px, py, mc)).wait_recv()
        for cp in sends:
            cp.wait_send()

    outs = _comm_call(body, parts, [jax.ShapeDtypeStruct(p.shape, p.dtype) for p in parts], npeer * len(parts), name)
    me = _my_shard()
    return [lax.dynamic_update_index_in_dim(o, lax.dynamic_index_in_dim(p, me, 0, keepdims=False), me, 0)
            for o, p in zip(outs, parts)]


def _pair_split(parts, name):
    def body(srcs, gots, send_sems, recv_sems):
        mx, my, mc = lax.axis_index("x"), lax.axis_index("y"), lax.axis_index("c")
        copies, idx = [], 0
        for src, got in zip(srcs, gots):
            for s in range(src.shape[0]):
                copies.append(_remote(src.at[s, 1 - mc], got.at[s], send_sems, recv_sems, idx, (mx, my, 1 - mc)))
                idx += 1
        for cp in copies:
            cp.start()
        for cp in copies:
            cp.wait()

    gots = _comm_call(body, parts, [jax.ShapeDtypeStruct((p.shape[0],) + p.shape[2:], p.dtype) for p in parts],
                      sum(p.shape[0] for p in parts), name)
    mc = lax.axis_index("c")
    return [lax.dynamic_index_in_dim(p, mc, 1, keepdims=False) for p in parts], gots


def _pair_join(mines, name):
    def body(srcs, outs, send_sems, recv_sems):
        mx, my, mc = lax.axis_index("x"), lax.axis_index("y"), lax.axis_index("c")
        copies = [_remote(src, out, send_sems, recv_sems, a, (mx, my, 1 - mc))
                  for a, (src, out) in enumerate(zip(srcs, outs))]
        for cp in copies:
            cp.start()
        for cp in copies:
            cp.wait()

    others = _comm_call(body, mines, [jax.ShapeDtypeStruct(m.shape, m.dtype) for m in mines], len(mines), name)
    first = lax.axis_index("c") == 0
    return [jnp.where(first, jnp.stack([m, o]), jnp.stack([o, m])) for m, o in zip(mines, others)]


def _all_gather_all(buf, name):
    flips = [(fx, fy, fc) for fx in (0, 1) for fy in (0, 1) for fc in (0, 1) if fx or fy or fc]

    def body(src, out, send_sems, recv_sems):
        mx, my, mc = lax.axis_index("x"), lax.axis_index("y"), lax.axis_index("c")
        me = 4 * mx + 2 * my + mc
        peers = [(1 - mx if fx else mx, 1 - my if fy else my, 1 - mc if fc else mc) for fx, fy, fc in flips]
        sends = [_remote(src, out.at[me], send_sems, recv_sems, k, dev) for k, dev in enumerate(peers)]
        for cp in sends:
            cp.start()
        for k, (px, py, pc) in enumerate(peers):
            _remote(src, out.at[4 * px + 2 * py + pc], send_sems, recv_sems, k, (px, py, pc)).wait_recv()
        for cp in sends:
            cp.wait_send()

    out = pl.pallas_call(
        body, out_shape=jax.ShapeDtypeStruct((N_DEV,) + buf.shape, buf.dtype), in_specs=[HBM_SPEC], out_specs=HBM_SPEC,
        scratch_shapes=[pltpu.SemaphoreType.DMA((N_DEV - 1,)), pltpu.SemaphoreType.DMA((N_DEV - 1,))], name=name)(buf)
    me = 4 * lax.axis_index("x") + 2 * lax.axis_index("y") + lax.axis_index("c")
    return lax.dynamic_update_index_in_dim(out, buf, me, 0)


def _flat_tile(nrow, width, narrays):
    t = nrow
    for cand in (512, 256, 128, 64, 32, 16, 8):
        if nrow % cand == 0:
            t = cand
            if cand * width * 4 * narrays <= ROW_BLOCK_BYTES:
                break
    return t


def _sum_parts(parts, name):
    kparts, nrow, width = parts.shape
    t = _flat_tile(nrow, width, kparts + 1)

    def body(p_ref, o_ref):
        s = p_ref[0].astype(F32)
        for k in range(1, kparts):
            s = s + p_ref[k].astype(F32)
        o_ref[...] = s

    return pl.pallas_call(
        body, grid=(nrow // t,), in_specs=[pl.BlockSpec((kparts, t, width), lambda i: (0, i, 0))],
        out_specs=pl.BlockSpec((t, width), lambda i: (i, 0)), out_shape=jax.ShapeDtypeStruct((nrow, width), F32),
        compiler_params=_cparams(1), name=name)(parts)


def _add2(a, b, name):
    nparts, nrow, width = a.shape
    t = _flat_tile(nrow, width, 3)

    def body(a_ref, b_ref, o_ref):
        o_ref[...] = (a_ref[...].astype(F32) + b_ref[...].astype(F32)).astype(o_ref.dtype)

    spec = pl.BlockSpec((None, t, width), lambda p, i: (p, i, 0))
    return pl.pallas_call(body, grid=(nparts, nrow // t), in_specs=[spec, spec], out_specs=spec,
                          out_shape=jax.ShapeDtypeStruct(a.shape, a.dtype), compiler_params=_cparams(2), name=name)(a, b)


def _adamw(w, m, v, ga, gb, name):
    nrow, width = w.shape
    t = _flat_tile(nrow, width, 9)
    c1 = 1.0 - ADAM_B1 ** ADAM_STEP
    c2 = 1.0 - ADAM_B2 ** ADAM_STEP
    grads = [ga] if gb is None else [ga, gb]
    nout = 3 if gb is None else 4

    def body(w_ref, m_ref, v_ref, *refs):
        d_ref, mo_ref, vo_ref = refs[-3:]
        g = refs[0][...]
        if gb is not None:
            g = g + refs[1][...]
            refs[2][...] = g
        mn = ADAM_B1 * m_ref[...] + (1.0 - ADAM_B1) * g
        vn = ADAM_B2 * v_ref[...] + (1.0 - ADAM_B2) * jnp.square(g)
        mo_ref[...] = mn
        vo_ref[...] = vn
        d_ref[...] = -ADAM_LR * ((mn / c1) / (jnp.sqrt(vn / c2) + ADAM_EPS) + ADAM_WD * w_ref[...])

    spec = pl.BlockSpec((t, width), lambda i: (i, 0))
    return pl.pallas_call(
        body, grid=(nrow // t,), in_specs=[spec] * (3 + len(grads)), out_specs=[spec] * nout,
        out_shape=[jax.ShapeDtypeStruct((nrow, width), F32)] * nout, compiler_params=_cparams(1), name=name)(
            w, m, v, *grads)


def _pack(arrs, dtype, width):
    flat = jnp.concatenate([a.astype(dtype).reshape(-1) for a in arrs])
    quantum = PACK_ROWS * width
    pad = (-flat.shape[0]) % quantum
    if pad:
        flat = jnp.concatenate([flat, jnp.zeros((pad,), dtype)])
    return flat.reshape(-1, width)


def _unpack(buf, shapes):
    flat = buf.reshape(-1)
    out, off = [], 0
    for s in shapes:
        size = math.prod(s)
        out.append(flat[off:off + size].reshape(s))
        off += size
    return out


def _qk_norm_fwd(qkv, w, off, nh, tag):
    f = _rms_f(RMS_EPS)
    return _rowwise(lambda xv, wv: (f(xv, wv),), [(qkv, SB_HEAD_DIM, off)], [(w, False)],
                    [('row', nh * SB_HEAD_DIM, SB_HEAD_DIM, BF16)], ncb=nh, name=tag)[0]


def _qk_norm_bwd(qkv, w, off, dn, nh, tag):
    f = _rms_f(RMS_EPS)

    def fn(xv, dv, wv):
        _, vjp = jax.vjp(f, xv, wv)
        return vjp(dv)

    return _rowwise(fn, [(qkv, SB_HEAD_DIM, off), (dn, SB_HEAD_DIM, 0)], [(w, False)],
                    [('row', nh * SB_HEAD_DIM, SB_HEAD_DIM, F32), ('acc', (1, SB_HEAD_DIM), False)], ncb=nh, name=tag)


def _sb_forward(h, xin, wt, tag):
    d = xin.shape[1]
    nh = d // SB_HEAD_DIM
    qkv = _mm(h, wt['w_qkv'], "nn", name=tag + "_qkv")
    qn = _qk_norm_fwd(qkv, wt['q_norm_w'], 0, nh, tag + "_qnorm")
    kn = _qk_norm_fwd(qkv, wt['k_norm_w'], nh, nh, tag + "_knorm")
    o, rsave = _sb_attn_fwd(qn, kn, qkv, tag + "_attn")
    xm = _mm(o, wt['w_o'], "nn", extras=(xin,), epilogue=_ep_add, name=tag + "_out")
    return xm, dict(qkv=qkv, qn=qn, kn=kn, o=o, rsave=rsave)


def _sb_backward(h, dx, dxb, wt, sv, tag):
    nh = dx.shape[1] // SB_HEAD_DIM
    do = _mm(dxb, wt['w_o'], "nt", name=tag + "_do")
    g_wo = _mm(sv['o'], dxb, "tn", out_dtypes=(BF16,), name=tag + "_dwo")
    dqn, dkn, dv = _sb_attn_bwd(sv['qn'], sv['kn'], sv['qkv'], do, sv['rsave'], tag + "_attn_bwd")
    dq, g_qw = _qk_norm_bwd(sv['qkv'], wt['q_norm_w'], 0, dqn, nh, tag + "_qnorm_bwd")
    dk, g_kw = _qk_norm_bwd(sv['qkv'], wt['k_norm_w'], nh, dkn, nh, tag + "_knorm_bwd")
    dqkv = jnp.concatenate([dq, dk, dv], axis=1)
    g_wqkv = _mm(h, dqkv, "tn", out_dtypes=(BF16,), out_col_shards=N_XY, name=tag + "_dwqkv")
    dh = _mm(dqkv, wt['w_qkv'], "nt", name=tag + "_dh")
    return dh, dict(w_qkv=g_wqkv, w_o=g_wo, q_norm_w=g_qw, k_norm_w=g_kw)


def _ln_silu_f(x, w, b):
    mu = jnp.mean(x, axis=-1, keepdims=True)
    xc = x - mu
    return _silu(xc * lax.rsqrt(jnp.mean(xc * xc, axis=-1, keepdims=True) + LN_EPS) * w + b)


def _glu_bwd_fn(val, gate, dg):
    sg = 1.0 / (1.0 + jnp.exp(-gate))
    return (jnp.concatenate([dg * sg, dg * val * sg * (1.0 - sg)], axis=1),)


def _cf_forward(h, xin, wt, tag):
    d = xin.shape[1]
    u = _mm(h, wt['w_in'], "nn", extras=(wt['b_in'],), epilogue=_ep_add, name=tag + "_in")
    gl = _rowwise(lambda val, gate: (val / (1.0 + jnp.exp(-gate)),), [(u, d, 0), (u, d, 1)], [],
                  [('row', d, d, F32)], name=tag + "_glu")[0]
    cv = _dwconv_fwd(gl, wt['dw_w'], wt['dw_b'], tag + "_conv")
    s = _rowwise(lambda x, w, b: (_ln_silu_f(x, w, b),), [(cv, d, 0)], [(wt['ln_w'], False), (wt['ln_b'], False)],
                 [('row', d, d, BF16)], name=tag + "_ln")[0]
    xm = _mm(s, wt['w_out'], "nn", extras=(wt['b_out'], xin), epilogue=_ep_bias_add, name=tag + "_out")
    return xm, dict(u=u, gl=gl, cv=cv, s=s)


def _cf_backward(h, dx, dxb, wt, sv, tag):
    d = dx.shape[1]
    ds = _mm(dxb, wt['w_out'], "nt", name=tag + "_ds")
    g_wout = _mm(sv['s'], dxb, "tn", out_dtypes=(BF16,), name=tag + "_dwout")
    g_bout = _colsum(dx, tag + "_dbout")

    def ln_bwd(x, dsv, w, b):
        _, vjp = jax.vjp(_ln_silu_f, x, w, b)
        return vjp(dsv)

    dcv, g_lnw, g_lnb = _rowwise(ln_bwd, [(sv['cv'], d, 0), (ds, d, 0)], [(wt['ln_w'], False), (wt['ln_b'], False)],
                                 [('row', d, d, F32), ('acc', (1, d), False), ('acc', (1, d), False)],
                                 name=tag + "_ln_bwd")
    dgl, g_dww, g_dwb = _dwconv_bwd(sv['gl'], dcv, wt['dw_w'], tag + "_conv_bwd")
    du = _rowwise(_glu_bwd_fn, [(sv['u'], d, 0), (sv['u'], d, 1), (dgl, d, 0)], [], [('row', 2 * d, 2 * d, F32)],
                  name=tag + "_glu_bwd")[0]
    g_bin = _colsum(du, tag + "_dbin")
    g_win = _mm(h, du, "tn", out_dtypes=(BF16,), out_col_shards=N_XY, name=tag + "_dwin")
    dh = _mm(du, wt['w_in'], "nt", name=tag + "_dh")
    return dh, dict(w_in=g_win, b_in=g_bin, dw_w=g_dww, dw_b=g_dwb, ln_w=g_lnw, ln_b=g_lnb, w_out=g_wout, b_out=g_bout)


def _gated_norm_f(y, z, w):
    t = y * _silu(z)
    return t * lax.rsqrt(jnp.mean(t * t, axis=-1, keepdims=True) + M2_NORM_EPS) * w


def _m2_forward(h, xin, wt, tag):
    z = _mm(h, wt['w_z'], "nn", name=tag + "_z")
    xbc = _mm(h, wt['w_xbc'], "nn", name=tag + "_xbc")
    dtr = _mm(h, wt['w_dt'], "nn", name=tag + "_dt")
    pre = _dwconv_fwd(xbc, wt['conv_w'], wt['conv_b'], tag + "_conv")
    y, hs = _ssd_fwd(pre, dtr, wt['dt_bias'], wt['a_log'], wt['d'], tag + "_ssd")
    d_inner = y.shape[1]
    gw = d_inner // M2_GROUPS
    yn = _rowwise(lambda yv, zv, w: (_gated_norm_f(yv, zv, w),), [(y, gw, 0), (z, gw, 0)], [(wt['norm_w'], True)],
                  [('row', d_inner, gw, BF16)], ncb=M2_GROUPS, name=tag + "_gnorm")[0]
    xm = _mm(yn, wt['w_out'], "nn", extras=(xin,), epilogue=_ep_add, name=tag + "_out")
    return xm, dict(z=z, xbc=xbc, dtr=dtr, pre=pre, y=y, hs=hs, yn=yn)


def _m2_backward(h, dx, dxb, wt, sv, tag):
    dyn = _mm(dxb, wt['w_out'], "nt", name=tag + "_dyn")
    g_wout = _mm(sv['yn'], dxb, "tn", out_dtypes=(BF16,), name=tag + "_dwout")
    d_inner = sv['y'].shape[1]
    gw = d_inner // M2_GROUPS

    def gn_bwd(yv, zv, dv, w):
        _, vjp = jax.vjp(_gated_norm_f, yv, zv, w)
        return vjp(dv)

    dy, dz, g_nw = _rowwise(gn_bwd, [(sv['y'], gw, 0), (sv['z'], gw, 0), (dyn, gw, 0)], [(wt['norm_w'], True)],
                            [('row', d_inner, gw, F32), ('row', d_inner, gw, F32), ('acc', (1, d_inner), True)],
                            ncb=M2_GROUPS, name=tag + "_gnorm_bwd")
    dxp, db, dc, ddt, g_dtb, g_alog, g_d = _ssd_bwd(sv['pre'], sv['dtr'], sv['hs'], dy, wt['dt_bias'], wt['a_log'],
                                                   wt['d'], tag + "_ssd_bwd")
    dpre = jnp.concatenate([dxp, db, dc], axis=1)
    dxbc, g_cw, g_cb = _dwconv_bwd(sv['xbc'], dpre, wt['conv_w'], tag + "_conv_bwd")
    g_wz = _mm(h, dz, "tn", name=tag + "_dwz")
    g_wxbc = _mm(h, dxbc, "tn", name=tag + "_dwxbc")
    g_wdt = _mm(h, ddt, "tn", name=tag + "_dwdt")
    dh = _mm(dz, wt['w_z'], "nt", name=tag + "_dh_z")
    dh = _mm(dxbc, wt['w_xbc'], "nt", extras=(dh,), epilogue=_ep_add, name=tag + "_dh_xbc")
    dh = _mm(ddt, wt['w_dt'], "nt", extras=(dh,), epilogue=_ep_add, name=tag + "_dh_dt")
    return dh, dict(w_z=g_wz, w_xbc=g_wxbc, w_dt=g_wdt, conv_w=g_cw, conv_b=g_cb, dt_bias=g_dtb, a_log=g_alog, d=g_d,
                    norm_w=g_nw, w_out=g_wout)


def _pad_lanes(v):
    return jnp.pad(v.reshape(1, -1), ((0, 0), (0, LANES - v.shape[0])))


def kernel(x, norm_mix_w, norm_mlp_w, sb_w_qkv, sb_q_norm_w, sb_k_norm_w, sb_w_o, cf_w_in, cf_b_in, cf_dw_w, cf_dw_b, cf_ln_w, cf_ln_b, cf_w_out, cf_b_out, m2_w_in, m2_conv_w, m2_conv_b, m2_dt_bias, m2_a_log, m2_d, m2_norm_w, m2_w_out, mlp_w_up, mlp_w_down, loss_target, m_norm_mix_w, m_norm_mlp_w, m_sb_w_qkv, m_sb_q_norm_w, m_sb_k_norm_w, m_sb_w_o, m_cf_w_in, m_cf_b_in, m_cf_dw_w, m_cf_dw_b, m_cf_ln_w, m_cf_ln_b, m_cf_w_out, m_cf_b_out, m_m2_w_in, m_m2_conv_w, m_m2_conv_b, m_m2_dt_bias, m_m2_a_log, m_m2_d, m_m2_norm_w, m_m2_w_out, m_mlp_w_up, m_mlp_w_down, v_norm_mix_w, v_norm_mlp_w, v_sb_w_qkv, v_sb_q_norm_w, v_sb_k_norm_w, v_sb_w_o, v_cf_w_in, v_cf_b_in, v_cf_dw_w, v_cf_dw_b, v_cf_ln_w, v_cf_ln_b, v_cf_w_out, v_cf_b_out, v_m2_w_in, v_m2_conv_w, v_m2_conv_b, v_m2_dt_bias, v_m2_a_log, v_m2_d, v_m2_norm_w, v_m2_w_out, v_mlp_w_up, v_mlp_w_down):
    given = dict(locals())
    xl = x[0]
    tgt = loss_target[0]
    d = xl.shape[1]
    depth = norm_mix_w.shape[0]
    bulk_names, small_names = list(BULK), list(SMALL)

    def halves(buf):
        return buf.reshape(2, -1, buf.shape[-1])

    small_pack = halves(_pack([given[n] for n in small_names], F32, PACK_W))
    gath = _all_gather_xy([halves(given[n].astype(BF16)) for n in bulk_names] + [small_pack], "comm_gather")
    gathered = {n: g.reshape((N_XY,) + given[n].shape) for n, g in zip(bulk_names, gath)}
    kinds = {n: "rows" if BULK[n] == 1 else "cols" for n in bulk_names}
    full = {}
    pieces = [_unpack(gath[-1][s], [given[n].shape for n in small_names]) for s in range(N_XY)]
    for idx, n in enumerate(small_names):
        full[n] = jnp.concatenate([pieces[s][idx] for s in range(N_XY)], axis=SMALL[n])

    def wview(n, layer):
        return WView(gathered[n], layer, kinds[n])

    def row(v):
        return v.reshape(1, -1)

    d_inner = N_XY * m2_w_out.shape[1]
    conv_dim = full['m2_conv_w'].shape[2]
    nheads = m2_dt_bias.shape[1]

    def layer_weights(i):
        kind, j = i % 3, i // 3
        if kind == 0:
            return dict(w_qkv=wview('sb_w_qkv', j), w_o=wview('sb_w_o', j), q_norm_w=row(sb_q_norm_w[j]),
                        k_norm_w=row(sb_k_norm_w[j]))
        if kind == 1:
            return dict(w_in=wview('cf_w_in', j), b_in=row(cf_b_in[j]), dw_w=full['cf_dw_w'][j], dw_b=row(cf_dw_b[j]),
                        ln_w=row(cf_ln_w[j]), ln_b=row(cf_ln_b[j]), w_out=wview('cf_w_out', j), b_out=row(cf_b_out[j]))
        shards = gathered['m2_w_in'][:, j]
        w_in = jnp.moveaxis(shards, 0, 1).reshape(shards.shape[1], -1)
        w_dt = jnp.pad(w_in[:, d_inner + conv_dim:], ((0, 0), (0, LANES - nheads)))
        return dict(w_z=w_in[:, :d_inner], w_xbc=w_in[:, d_inner:d_inner + conv_dim], w_dt=w_dt,
                    conv_w=full['m2_conv_w'][j], conv_b=row(full['m2_conv_b'][j]), dt_bias=_pad_lanes(m2_dt_bias[j]),
                    a_log=_pad_lanes(m2_a_log[j]), d=_pad_lanes(m2_d[j]), norm_w=row(full['m2_norm_w'][j]),
                    w_out=wview('m2_w_out', j))

    fwd = (_sb_forward, _cf_forward, _m2_forward)
    bwd = (_sb_backward, _cf_backward, _m2_backward)

    saved = []
    xc = xl
    for i in range(depth):
        wt = layer_weights(i)
        h = _rms_fwd(xc, row(norm_mix_w[i]), f"l{i}_norm_mix")
        xm, sv = fwd[i % 3](h, xc, wt, f"l{i}_mix")
        h2 = _rms_fwd(xm, row(norm_mlp_w[i]), f"l{i}_norm_mlp")
        u, a = _mm(h2, wview('mlp_w_up', i), "nn", epilogue=_ep_relu2, out_dtypes=(F32, BF16), name=f"l{i}_up")
        xn = _mm(a, wview('mlp_w_down', i), "nn", extras=(xm,), epilogue=_ep_add, name=f"l{i}_down")
        saved.append(dict(wt=wt, x_in=xc, h=h, mix=sv, x_mid=xm, h2=h2, u=u, a=a))
        xc = xn
    dx, dxb, loss_acc = _loss(xc, tgt, "loss")
    loss = lax.psum(loss_acc[0, 0], ("x", "y", "c"))

    grads = {n: [None] * given[n].shape[0] for n in W_NAMES}
    for i in reversed(range(depth)):
        sv = saved[i]
        kind, j = i % 3, i // 3
        du = _mm(dxb, wview('mlp_w_down', i), "nt", extras=(sv['u'],), epilogue=_ep_relu2_bwd, out_dtypes=(BF16,),
                 name=f"l{i}_du")
        grads['mlp_w_down'][i] = _mm(sv['a'], dxb, "tn", out_dtypes=(BF16,), name=f"l{i}_dwdown")
        grads['mlp_w_up'][i] = _mm(sv['h2'], du, "tn", out_dtypes=(BF16,), out_col_shards=N_XY, name=f"l{i}_dwup")
        dh2 = _mm(du, wview('mlp_w_up', i), "nt", name=f"l{i}_dh2")
        dxm, dxmb, g_n2 = _rms_bwd(sv['x_mid'], row(norm_mlp_w[i]), dh2, dx, f"l{i}_norm_mlp_bwd")
        grads['norm_mlp_w'][i] = g_n2[0]
        dh, gw = bwd[kind](sv['h'], dxm, dxmb, sv['wt'], sv['mix'], f"l{i}_mix")
        dx, dxb, g_n1 = _rms_bwd(sv['x_in'], row(norm_mix_w[i]), dh, dxm, f"l{i}_norm_mix_bwd")
        grads['norm_mix_w'][i] = g_n1[0]
        if kind == 0:
            grads['sb_w_qkv'][j], grads['sb_w_o'][j] = gw['w_qkv'], gw['w_o']
            grads['sb_q_norm_w'][j], grads['sb_k_norm_w'][j] = gw['q_norm_w'][0], gw['k_norm_w'][0]
        elif kind == 1:
            for n in ('w_in', 'dw_w', 'w_out'):
                grads['cf_' + n][j] = gw[n]
            for n in ('b_in', 'dw_b', 'ln_w', 'ln_b', 'b_out'):
                grads['cf_' + n][j] = gw[n][0]
        else:
            grads['m2_w_in'][j] = jnp.concatenate([gw['w_z'], gw['w_xbc'], gw['w_dt'][:, :nheads]], axis=1)
            grads['m2_conv_w'][j], grads['m2_w_out'][j] = gw['conv_w'], gw['w_out']
            grads['m2_conv_b'][j], grads['m2_norm_w'][j] = gw['conv_b'][0], gw['norm_w'][0]
            for n in ('dt_bias', 'a_log', 'd'):
                grads['m2_' + n][j] = gw[n][0, :nheads]
    grad_x = dx[None]
    gfull = {n: jnp.stack(grads[n]) for n in small_names + REPL}

    def shard_major(n, g):
        _, r, c = given[n].shape
        if n == 'm2_w_in':
            return jnp.moveaxis(g.reshape(r, N_XY, c), 1, 0).astype(BF16)
        return g.reshape(N_XY, r, c)

    parts = [jnp.stack([shard_major(n, g) for g in grads[n]], axis=1) for n in bulk_names]
    parts = [p.reshape(N_XY, 2, -1, p.shape[-1]) for p in parts]
    split = {n: jnp.split(gfull[n], N_XY, axis=SMALL[n]) for n in small_names}
    parts.append(jnp.stack([halves(_pack([split[n][s] for n in small_names], F32, PACK_W)) for s in range(N_XY)]))
    tags = bulk_names + ["small"]
    kept, got = _pair_split(parts, "comm_split")
    pair = [_add2(k, g, f"sum_pair_{t}") for k, g, t in zip(kept, got, tags)]
    mine = [_sum_parts(r, f"sum_{t}") for r, t in zip(_exchange_xy(pair, "comm_exchange"), tags)]
    gsum = _pair_join(mine, "comm_join")
    outs = {}
    for n, g in list(zip(bulk_names, gsum)) + list(zip(small_names, _unpack(gsum[-1], [given[n].shape for n in small_names]))):
        shape = given[n].shape
        res = _adamw(*[given[p + n].reshape(-1, shape[-1]) for p in ("", "m_", "v_")], g.reshape(-1, shape[-1]),
                     None, f"adamw_{n}")
        outs["grad", n] = g.reshape(shape)
        for kind, arr in zip(("delta", "new_m", "new_v"), res):
            outs[kind, n] = arr.reshape(shape)
    rparts = _all_gather_all(_pack([gfull[n] for n in REPL], F32, LANES), "comm_gather_repl")
    nsplit = N_DEV // 2
    res = _adamw(*[_pack([given[p + n] for n in REPL], F32, LANES) for p in ("", "m_", "v_")],
                 _sum_parts(rparts[:nsplit], "sum_repl_a"), _sum_parts(rparts[nsplit:], "sum_repl_b"), "adamw_repl")
    shapes = [given[n].shape for n in REPL]
    for kind, buf in zip(("grad", "delta", "new_m", "new_v"), res):
        for n, arr in zip(REPL, _unpack(buf, shapes)):
            outs[kind, n] = arr

    return (loss, grad_x, *[outs[kind, n] for kind in ("grad", "delta", "new_m", "new_v") for n in W_NAMES])
```

```python
import math
from typing import NamedTuple

import jax
import jax.numpy as jnp
from jax import lax
from jax.experimental import pallas as pl
from jax.experimental.pallas import tpu as pltpu

F32 = jnp.float32
BF16 = jnp.bfloat16
MESH_ID = pl.DeviceIdType.MESH
HIGHEST = lax.Precision.HIGHEST

SB_HEAD_DIM = 128
M2_HEAD_DIM = 64
M2_STATE = 128
M2_GROUPS = 8
M2_CHUNK = 128
RMS_EPS = 1e-6
LN_EPS = 1e-5
M2_NORM_EPS = 1e-5
ADAM_LR = 0.001
ADAM_B1 = 0.9
ADAM_B2 = 0.999
ADAM_EPS = 1e-08
ADAM_WD = 0.01
ADAM_STEP = 10

N_XY = 4
N_DEV = 8

V7X_VMEM_BYTES = 64 * 2**20
VMEM_LIMIT = (V7X_VMEM_BYTES * 3) // 4
LANES = 128
ROW_BLOCK_BYTES = 6 * 2**20
MM_VMEM_BUDGET = (VMEM_LIMIT * 3) // 4
PACK_W = 1024
PACK_ROWS = 1024

W_NAMES = ['norm_mix_w', 'norm_mlp_w', 'sb_w_qkv', 'sb_q_norm_w', 'sb_k_norm_w', 'sb_w_o', 'cf_w_in', 'cf_b_in',
           'cf_dw_w', 'cf_dw_b', 'cf_ln_w', 'cf_ln_b', 'cf_w_out', 'cf_b_out', 'm2_w_in', 'm2_conv_w', 'm2_conv_b',
           'm2_dt_bias', 'm2_a_log', 'm2_d', 'm2_norm_w', 'm2_w_out', 'mlp_w_up', 'mlp_w_down']
BULK = {'sb_w_qkv': 2, 'sb_w_o': 1, 'cf_w_in': 2, 'cf_w_out': 1, 'm2_w_in': 2, 'm2_w_out': 1, 'mlp_w_up': 2,
        'mlp_w_down': 1}
SMALL = {'cf_dw_w': 2, 'm2_conv_w': 2, 'm2_conv_b': 1, 'm2_norm_w': 1}
REPL = ['norm_mix_w', 'norm_mlp_w', 'sb_q_norm_w', 'sb_k_norm_w', 'cf_b_in', 'cf_dw_b', 'cf_ln_w', 'cf_ln_b',
        'cf_b_out', 'm2_dt_bias', 'm2_a_log', 'm2_d']


def _tile(n, prefs):
    for p in prefs:
        if n % p == 0:
            return p
    return n


def _cparams(n):
    return pltpu.CompilerParams(dimension_semantics=("arbitrary",) * n, vmem_limit_bytes=VMEM_LIMIT)


def _silu(v):
    return v / (1.0 + jnp.exp(-v))


def _softplus(v):
    return jnp.maximum(v, 0.0) + jnp.log(1.0 + jnp.exp(-jnp.abs(v)))


class WView(NamedTuple):
    arr: jax.Array
    layer: int
    kind: str


def _mm(a, b, mode, *, extras=(), epilogue=None, out_dtypes=(F32,), out_col_shards=1, name):
    if mode == "tn":
        kdim, m = a.shape
    else:
        m, kdim = a.shape
    view = b if isinstance(b, WView) else None
    if view is not None:
        _, _, srows, scols = view.arr.shape
        brows, bcols = (N_XY * srows, scols) if view.kind == "rows" else (srows, N_XY * scols)
        b_arr, b_item = view.arr, view.arr.dtype.itemsize
    else:
        brows, bcols = b.shape
        b_arr, b_item = b, b.dtype.itemsize
    n = brows if mode == "nt" else bcols
    assert kdim == (bcols if mode == "nt" else brows)
    n_unit, k_unit = n // out_col_shards, kdim
    if view is not None:
        if (mode == "nn") == (view.kind == "cols"):
            n_unit = n // N_XY
        else:
            k_unit = kdim // N_XY
    tm = _tile(m, (1024, 512, 256, 128))
    tn = _tile(n_unit, (1024, 512, 256, 128))
    acc_in_out = out_dtypes[0] == F32 and out_col_shards == 1
    fixed = sum(tm * tn * jnp.dtype(dt).itemsize for dt in out_dtypes)
    fixed += sum((1 if e.shape[0] == 1 else tm) * tn * e.dtype.itemsize for e in extras)

    def vmem_bytes(t):
        ab = tm * t * a.dtype.itemsize + t * tn * b_item
        casts = (tm * t * 2 if a.dtype != BF16 else 0) + (t * tn * 2 if b_item != 2 else 0)
        return 2 * (ab + fixed) + (0 if acc_in_out else tm * tn * 4) + tm * tn * 4 + casts

    tk = k_unit
    for cand in (2048, 1536, 1024, 768, 512, 256, 128):
        if k_unit % cand == 0:
            tk = cand
            if vmem_bytes(cand) <= MM_VMEM_BUDGET:
                break
    nk = kdim // tk
    dims = {"nn": ((1,), (0,)), "nt": ((1,), (1,)), "tn": ((0,), (0,))}[mode]
    if mode == "tn":
        a_spec = pl.BlockSpec((tk, tm), lambda i, j, k: (k, i))
    else:
        a_spec = pl.BlockSpec((tm, tk), lambda i, j, k: (i, k))
    if view is None:
        if mode == "nt":
            b_spec = pl.BlockSpec((tn, tk), lambda i, j, k: (j, k))
        else:
            b_spec = pl.BlockSpec((tk, tn), lambda i, j, k: (k, j))
    else:
        lay = view.layer
        rc = (lambda j, k: (j, k)) if mode == "nt" else (lambda j, k: (k, j))
        blk = (tn, tk) if mode == "nt" else (tk, tn)
        if view.kind == "rows":
            per = srows // blk[0]
            b_spec = pl.BlockSpec((None, None) + blk,
                                  lambda i, j, k: (rc(j, k)[0] // per, lay, rc(j, k)[0] % per, rc(j, k)[1]))
        else:
            per = scols // blk[1]
            b_spec = pl.BlockSpec((None, None) + blk,
                                  lambda i, j, k: (rc(j, k)[1] // per, lay, rc(j, k)[0], rc(j, k)[1] % per))
    ex_specs = [pl.BlockSpec((1, tn), lambda i, j, k: (0, j)) if e.shape[0] == 1
                else pl.BlockSpec((tm, tn), lambda i, j, k: (i, j)) for e in extras]
    n_ex, n_out = len(extras), len(out_dtypes)

    def body(*refs):
        a_ref, b_ref = refs[:2]
        ex = refs[2:2 + n_ex]
        outs = refs[2 + n_ex:2 + n_ex + n_out]
        acc = outs[0] if acc_in_out else refs[-1]
        k = pl.program_id(2)
        prod = lax.dot_general(a_ref[...].astype(BF16), b_ref[...].astype(BF16), (dims, ((), ())),
                               preferred_element_type=F32)

        def finish(res):
            vals = epilogue(res, *[e[...] for e in ex]) if epilogue is not None else (res,)
            for o, v in zip(outs, vals):
                o[...] = v.astype(o.dtype)

        if nk == 1:
            finish(prod)
        else:
            @pl.when(k == 0)
            def _():
                acc[...] = prod

            @pl.when((k > 0) & (k < nk - 1))
            def _():
                acc[...] += prod

            @pl.when(k == nk - 1)
            def _():
                finish(acc[...] + prod)

    scratch = [] if (acc_in_out or nk == 1) else [pltpu.VMEM((tm, tn), F32)]
    if out_col_shards == 1:
        out_specs = [pl.BlockSpec((tm, tn), lambda i, j, k: (i, j))] * n_out
        out_shape = [jax.ShapeDtypeStruct((m, n), d) for d in out_dtypes]
    else:
        per_out = n_unit // tn
        out_specs = [pl.BlockSpec((None, tm, tn), lambda i, j, k: (j // per_out, i, j % per_out))]
        out_shape = [jax.ShapeDtypeStruct((out_col_shards, m, n_unit), out_dtypes[0])]
    outs = pl.pallas_call(
        body, grid=(m // tm, n // tn, nk), in_specs=[a_spec, b_spec, *ex_specs], out_specs=out_specs,
        out_shape=out_shape, scratch_shapes=scratch, compiler_params=_cparams(3), name=name)(a, b_arr, *extras)
    return outs[0] if n_out == 1 else outs


def _ep_add(acc, r):
    return (acc + r,)


def _ep_bias_add(acc, b, r):
    return (acc + b + r,)


def _ep_relu2(acc):
    return acc, jnp.square(jnp.maximum(acc, 0.0))


def _ep_relu2_bwd(acc, u):
    return (acc * (2.0 * jnp.maximum(u, 0.0)),)


def _rowwise(fn, rows, consts, outs, *, ncb=1, name):
    nrow = rows[0][0].shape[0]
    row_bytes = sum(w * arr.dtype.itemsize for arr, w, _ in rows)
    row_bytes += sum(o[2] * jnp.dtype(o[3]).itemsize for o in outs if o[0] == 'row')
    t = nrow
    for cand in (1024, 512, 256, 128, 64, 32, 16):
        if nrow % cand == 0:
            t = cand
            if cand * row_bytes <= ROW_BLOCK_BYTES:
                break
    in_specs = [pl.BlockSpec((t, w), (lambda j, i, off=off: (i, off + j))) for _, w, off in rows]
    for arr, per_col in consts:
        r, wc = arr.shape
        if per_col:
            in_specs.append(pl.BlockSpec((r, wc // ncb), lambda j, i: (0, j)))
        else:
            in_specs.append(pl.BlockSpec((r, wc), lambda j, i: (0, 0)))
    out_specs, out_shape = [], []
    for o in outs:
        if o[0] == 'row':
            out_specs.append(pl.BlockSpec((t, o[2]), lambda j, i: (i, j)))
            out_shape.append(jax.ShapeDtypeStruct((nrow, o[1]), o[3]))
        else:
            r, wt = o[1]
            if o[2]:
                out_specs.append(pl.BlockSpec((r, wt // ncb), lambda j, i: (0, j)))
            else:
                out_specs.append(pl.BlockSpec((r, wt), lambda j, i: (0, 0)))
            out_shape.append(jax.ShapeDtypeStruct((r, wt), F32))
    nin = len(rows) + len(consts)

    def body(*refs):
        j, i = pl.program_id(0), pl.program_id(1)
        vals = fn(*[r[...] for r in refs[:nin]])

        def store(spec, ref, v):
            if spec[0] == 'row':
                ref[...] = v.astype(ref.dtype)
            else:
                first = (i == 0) if spec[2] else ((i == 0) & (j == 0))

                @pl.when(first)
                def _():
                    ref[...] = jnp.zeros_like(ref)

                ref[...] += v

        for spec, ref, v in zip(outs, refs[nin:], vals):
            store(spec, ref, v)

    res = pl.pallas_call(body, grid=(ncb, nrow // t), in_specs=in_specs, out_specs=out_specs, out_shape=out_shape,
                         compiler_params=_cparams(2), name=name)(*[r[0] for r in rows], *[c[0] for c in consts])
    return res


def _rms_f(eps):
    def f(x, w):
        return x * lax.rsqrt(jnp.mean(x * x, axis=-1, keepdims=True) + eps) * w
    return f


def _rms_fwd(x, w, name):
    d = x.shape[1]
    f = _rms_f(RMS_EPS)
    return _rowwise(lambda xv, wv: (f(xv, wv),), [(x, d, 0)], [(w, False)], [('row', d, d, BF16)], name=name)[0]


def _rms_bwd(x, w, dh, dres, name):
    d = x.shape[1]
    f = _rms_f(RMS_EPS)

    def fn(xv, dhv, drv, wv):
        _, vjp = jax.vjp(f, xv, wv)
        dx, dw = vjp(dhv)
        return dx + drv, dx + drv, dw

    return _rowwise(fn, [(x, d, 0), (dh, d, 0), (dres, d, 0)], [(w, False)],
                    [('row', d, d, F32), ('row', d, d, BF16), ('acc', (1, d), False)], name=name)


def _colsum(x, name):
    w = x.shape[1]
    return _rowwise(lambda v: (jnp.sum(v, axis=0, keepdims=True),), [(x, w, 0)], [], [('acc', (1, w), False)],
                    name=name)[0]


def _loss(y, tgt, name):
    d = y.shape[1]

    def fn(yv, tv):
        e = yv - tv
        s = jnp.sum(jnp.sum(e * e, axis=1, keepdims=True), axis=0, keepdims=True) * (0.5 / d)
        return e * (1.0 / d), e * (1.0 / d), s + jnp.zeros((1, LANES), F32)

    return _rowwise(fn, [(y, d, 0), (tgt, d, 0)], [],
                    [('row', d, d, F32), ('row', d, d, BF16), ('acc', (1, LANES), False)], name=name)


def _conv_tiles(x, w):
    nrow, ch = x.shape
    kw = w.shape[0]
    halo = 8 * ((kw - 1 + 7) // 8)
    t = _tile(nrow, (256, 128, 64, 32))
    tc = _tile(ch, (512, 256, 128))
    assert t % halo == 0 and nrow % t == 0
    return nrow, ch, kw, halo, t, tc


def _dwconv_fwd(x, w, b, name):
    nrow, ch, kw, halo, t, tc = _conv_tiles(x, w)
    per = t // halo

    def body(x_ref, halo_ref, w_ref, b_ref, y_ref, buf):
        i = pl.program_id(1)
        buf[0:halo, :] = jnp.where(i == 0, 0.0, halo_ref[...])
        buf[halo:halo + t, :] = x_ref[...]
        acc = jnp.zeros((t, tc), F32) + b_ref[...]
        for k in range(kw):
            s = halo - (kw - 1) + k
            acc = acc + w_ref[k:k + 1, :] * buf[s:s + t, :]
        y_ref[...] = acc

    return pl.pallas_call(
        body, grid=(ch // tc, nrow // t),
        in_specs=[pl.BlockSpec((t, tc), lambda j, i: (i, j)),
                  pl.BlockSpec((halo, tc), lambda j, i: (jnp.maximum(i * per - 1, 0), j)),
                  pl.BlockSpec((kw, tc), lambda j, i: (0, j)), pl.BlockSpec((1, tc), lambda j, i: (0, j))],
        out_specs=pl.BlockSpec((t, tc), lambda j, i: (i, j)), out_shape=jax.ShapeDtypeStruct((nrow, ch), F32),
        scratch_shapes=[pltpu.VMEM((t + halo, tc), F32)], compiler_params=_cparams(2), name=name)(x, x, w, b)


def _dwconv_bwd(x, dy, w, name):
    nrow, ch, kw, halo, t, tc = _conv_tiles(x, w)
    per = t // halo
    nt = nrow // t
    nhalo = nrow // halo

    def body(x_ref, xh_ref, dy_ref, dyh_ref, w_ref, dx_ref, dw_ref, db_ref, xbuf, dybuf):
        i = pl.program_id(1)
        xbuf[0:halo, :] = jnp.where(i == 0, 0.0, xh_ref[...])
        xbuf[halo:halo + t, :] = x_ref[...]
        dyc = dy_ref[...]
        dybuf[0:t, :] = dyc
        dybuf[t:t + halo, :] = jnp.where(i == nt - 1, 0.0, dyh_ref[...])
        acc = jnp.zeros((t, tc), F32)
        for k in range(kw):
            s = kw - 1 - k
            acc = acc + w_ref[k:k + 1, :] * dybuf[s:s + t, :]
        dx_ref[...] = acc

        @pl.when(i == 0)
        def _():
            dw_ref[...] = jnp.zeros_like(dw_ref)
            db_ref[...] = jnp.zeros_like(db_ref)

        for k in range(kw):
            s = kw - 1 - k
            dw_ref[k:k + 1, :] += jnp.sum(dyc * xbuf[halo - s:halo - s + t, :], axis=0, keepdims=True)
        db_ref[...] += jnp.sum(dyc, axis=0, keepdims=True)

    return pl.pallas_call(
        body, grid=(ch // tc, nt),
        in_specs=[pl.BlockSpec((t, tc), lambda j, i: (i, j)),
                  pl.BlockSpec((halo, tc), lambda j, i: (jnp.maximum(i * per - 1, 0), j)),
                  pl.BlockSpec((t, tc), lambda j, i: (i, j)),
                  pl.BlockSpec((halo, tc), lambda j, i: (jnp.minimum((i + 1) * per, nhalo - 1), j)),
                  pl.BlockSpec((kw, tc), lambda j, i: (0, j))],
        out_specs=[pl.BlockSpec((t, tc), lambda j, i: (i, j)), pl.BlockSpec((kw, tc), lambda j, i: (0, j)),
                   pl.BlockSpec((1, tc), lambda j, i: (0, j))],
        out_shape=[jax.ShapeDtypeStruct((nrow, ch), F32), jax.ShapeDtypeStruct((kw, ch), F32),
                   jax.ShapeDtypeStruct((1, ch), F32)],
        scratch_shapes=[pltpu.VMEM((t + halo, tc), F32), pltpu.VMEM((t + halo, tc), F32)],
        compiler_params=_cparams(2), name=name)(x, x, dy, dy, w)


SB_SUB = 128
LOG2E = 1.4426950408889634
SB_SKIP_BELOW = -256.0
SB_UNVISITED = -1e30


def _dot_split2(x, u):
    hi = x.astype(BF16)
    lo = (x - hi.astype(F32)).astype(BF16)
    return jnp.dot(hi, u, preferred_element_type=F32) + jnp.dot(lo, u, preferred_element_type=F32)


def _sb_scores(q, k, strict, scale):
    z = lax.dot_general(q, k, (((1,), (1,)), ((), ())), preferred_element_type=F32) * (scale * LOG2E)
    soft = jnp.log2(1.0 + jnp.exp2(-jnp.abs(z)))
    ls = jnp.minimum(z, 0.0) - soft
    lk = ls - z
    if strict is not None:
        lk = jnp.where(strict, lk, 0.0)
    return ls, lk


def _sb_running(x, tri, start, reverse):
    nsub = x.shape[1] // SB_SUB
    parts = [x[:, SB_SUB * b:SB_SUB * (b + 1)] for b in range(nsub)]
    out = [None] * nsub
    run = start
    for b in (reversed(range(nsub)) if reverse else range(nsub)):
        out[b] = _dot_split2(parts[b], tri) + run
        run = run + jnp.sum(parts[b], axis=1, keepdims=True)
    return (jnp.concatenate(out, axis=1) if nsub > 1 else out[0]), run


def _sb_dims(qn):
    nrow, d = qn.shape
    hd = SB_HEAD_DIM
    t = _tile(nrow, (512, 256, 128))
    assert nrow % t == 0 and nrow // t <= LANES
    return nrow, d, hd, d // hd, t


def _sb_masks(t):
    strict = lax.broadcasted_iota(jnp.int32, (t, t), 1) < lax.broadcasted_iota(jnp.int32, (t, t), 0)
    r0 = lax.broadcasted_iota(jnp.int32, (SB_SUB, SB_SUB), 0)
    c0 = lax.broadcasted_iota(jnp.int32, (SB_SUB, SB_SUB), 1)
    return strict, (r0 > c0).astype(BF16), (r0 < c0).astype(BF16)


def _sb_attn_fwd(qn, kn, qkv, name):
    nrow, d, hd, nh, t = _sb_dims(qn)
    scale = 1.0 / math.sqrt(hd)

    def body(q_ref, k_ref, v_ref, o_ref, rs_ref, r_acc, o_acc):
        i = pl.program_id(1)
        q = q_ref[...]
        strict, after, _ = _sb_masks(t)
        lane = lax.broadcasted_iota(jnp.int32, (t, LANES), 1)
        r_acc[...] = jnp.zeros_like(r_acc)
        o_acc[...] = jnp.zeros_like(o_acc)
        rs_ref[...] = jnp.full(rs_ref.shape, SB_UNVISITED, F32)

        def tile(j, mask):
            start = pl.multiple_of(j * t, t)
            k = k_ref[pl.ds(start, t), :]
            v = v_ref[pl.ds(start, t), :].astype(BF16)
            ls, lk = _sb_scores(q, k, mask, scale)
            r = r_acc[...]
            later, r_next = _sb_running(lk, after, r, True)
            att = jnp.exp2(ls + later)
            if mask is not None:
                att = jnp.where(mask, att, 0.0)
            o_acc[...] += jnp.dot(att.astype(BF16), v, preferred_element_type=F32)
            rs_ref[...] = jnp.where(lane == j, r, rs_ref[...])
            r_acc[...] = r_next

        tile(i, strict)

        def more(carry):
            n, r_max = carry
            return (n < i) & (r_max > SB_SKIP_BELOW)

        def step(carry):
            n, _ = carry
            tile(i - 1 - n, None)
            return n + 1, jnp.max(r_acc[...])

        lax.while_loop(more, step, (jnp.int32(0), jnp.max(r_acc[...])))
        o_ref[...] = o_acc[...].astype(o_ref.dtype)

    return pl.pallas_call(
        body, grid=(nh, nrow // t),
        in_specs=[pl.BlockSpec((t, hd), lambda h, i: (i, h)), pl.BlockSpec((nrow, hd), lambda h, i: (0, h)),
                  pl.BlockSpec((nrow, hd), lambda h, i: (0, 2 * nh + h))],
        out_specs=[pl.BlockSpec((t, hd), lambda h, i: (i, h)), pl.BlockSpec((None, t, LANES), lambda h, i: (h, i, 0))],
        out_shape=[jax.ShapeDtypeStruct((nrow, d), BF16), jax.ShapeDtypeStruct((nh, nrow, LANES), F32)],
        scratch_shapes=[pltpu.VMEM((t, LANES), F32), pltpu.VMEM((t, hd), F32)],
        compiler_params=_cparams(2), name=name)(qn, kn, qkv)


def _sb_attn_bwd(qn, kn, qkv, do, rsave, name):
    nrow, d, hd, nh, t = _sb_dims(qn)
    scale = 1.0 / math.sqrt(hd)

    def body(q_ref, k_ref, v_ref, do_ref, rs_ref, dq_ref, dk_ref, dv_ref, pg_acc):
        i = pl.program_id(1)

        @pl.when(i == 0)
        def _():
            dk_ref[...] = jnp.zeros_like(dk_ref)
            dv_ref[...] = jnp.zeros_like(dv_ref)

        pg_acc[...] = jnp.zeros_like(pg_acc)
        dq_ref[...] = jnp.zeros_like(dq_ref)
        q = q_ref[...]
        dob = do_ref[...].astype(BF16)
        strict, after, before = _sb_masks(t)
        lane = lax.broadcasted_iota(jnp.int32, (t, LANES), 1)

        def tile(j, mask):
            start = pl.multiple_of(j * t, t)
            k = k_ref[pl.ds(start, t), :]
            v = v_ref[pl.ds(start, t), :].astype(BF16)
            ls, lk = _sb_scores(q, k, mask, scale)
            rj = jnp.sum(jnp.where(lane == j, rs_ref[...], 0.0), axis=1, keepdims=True)
            later, _ = _sb_running(lk, after, jnp.broadcast_to(rj, (t, LANES)), True)
            att = jnp.exp2(ls + later)
            if mask is not None:
                att = jnp.where(mask, att, 0.0)
            datt = lax.dot_general(dob, v, (((1,), (1,)), ((), ())), preferred_element_type=F32)
            g = datt * att
            dlk, pg_next = _sb_running(g, before, pg_acc[...], False)
            sig = jnp.exp2(ls)
            dz = g * (1.0 - sig) - dlk * sig
            if mask is not None:
                dz = jnp.where(mask, dz, 0.0)
            dz = (dz * scale).astype(BF16)
            dq_ref[...] += jnp.dot(dz, k, preferred_element_type=F32)
            dk_ref[pl.ds(start, t), :] += lax.dot_general(dz, q, (((0,), (0,)), ((), ())), preferred_element_type=F32)
            dv_ref[pl.ds(start, t), :] += lax.dot_general(att.astype(BF16), dob, (((0,), (0,)), ((), ())),
                                                          preferred_element_type=F32)
            pg_acc[...] = pg_next

        def step(j, carry):
            tile(j, None)
            return carry

        seen = jnp.max(rs_ref[...], axis=0, keepdims=True)
        lane1 = lax.broadcasted_iota(jnp.int32, (1, LANES), 1)
        first = jnp.sum(jnp.where((lane1 < i) & (seen < 0.5 * SB_UNVISITED), 1.0, 0.0)).astype(jnp.int32)
        lax.fori_loop(first, i, step, 0)
        tile(i, strict)

    return pl.pallas_call(
        body, grid=(nh, nrow // t),
        in_specs=[pl.BlockSpec((t, hd), lambda h, i: (i, h)), pl.BlockSpec((nrow, hd), lambda h, i: (0, h)),
                  pl.BlockSpec((nrow, hd), lambda h, i: (0, 2 * nh + h)), pl.BlockSpec((t, hd), lambda h, i: (i, h)),
                  pl.BlockSpec((None, t, LANES), lambda h, i: (h, i, 0))],
        out_specs=[pl.BlockSpec((t, hd), lambda h, i: (i, h)), pl.BlockSpec((nrow, hd), lambda h, i: (0, h)),
                   pl.BlockSpec((nrow, hd), lambda h, i: (0, h))],
        out_shape=[jax.ShapeDtypeStruct((nrow, d), F32)] * 3, scratch_shapes=[pltpu.VMEM((t, LANES), F32)],
        compiler_params=_cparams(2), name=name)(qn, kn, qkv, do, rsave)


def _ssd_chunk_fn(g, nheads_g, q):
    p = M2_HEAD_DIM
    npair = nheads_g // 2

    def f(xp, bp, cp, dtr, hp, dtb, alog, dsk):
        b = _silu(bp).astype(BF16)
        c = _silu(cp).astype(BF16)
        dt = _softplus(dtr + dtb)
        dta = dt * (-jnp.exp(alog))
        ri = lax.broadcasted_iota(jnp.int32, (q, q), 0)
        ci = lax.broadcasted_iota(jnp.int32, (q, q), 1)
        causal = ri >= ci
        tri = causal.astype(F32)
        acol = jnp.dot(tri, dta, precision=HIGHEST, preferred_element_type=F32)
        arow = lax.dot_general(dta, tri, (((0,), (1,)), ((), ())), precision=HIGHEST,
                               preferred_element_type=F32)
        alast = jnp.sum(dta, axis=0, keepdims=True)
        cb = lax.dot_general(c, b, (((1,), (1,)), ((), ())), preferred_element_type=F32)
        lane = lax.broadcasted_iota(jnp.int32, (1, LANES), 1)
        sub = lax.broadcasted_iota(jnp.int32, (LANES, 1), 0)
        lane_half = lane < p
        sub_half = sub < p

        def col(mat, h):
            return jnp.sum(jnp.where(lane == h, mat, 0.0), axis=1, keepdims=True)

        def row(mat, h):
            return jnp.sum(jnp.where(sub == h, mat, 0.0), axis=0, keepdims=True)

        def mix(v0, v1):
            return jnp.where(lane_half, v0, v1)

        ys, hns = [], []
        for jp in range(npair):
            h0 = g * nheads_g + 2 * jp
            h1 = h0 + 1
            x = _silu(xp[jp])
            ac0, ac1 = col(acol, h0), col(acol, h1)
            xdt = x * mix(col(dt, h0), col(dt, h1))
            xdtb = xdt.astype(BF16)
            yd = []
            for h, ac in ((h0, ac0), (h1, ac1)):
                dec = jnp.exp(jnp.where(causal, ac - row(arow, h), -jnp.inf))
                yd.append(jnp.dot((cb * dec).astype(BF16), xdtb, preferred_element_type=F32))
            hpj = hp[jp]
            yo = lax.dot_general(c, hpj.astype(BF16), (((1,), (1,)), ((), ())), preferred_element_type=F32)
            y = mix(yd[0], yd[1]) + yo * jnp.exp(mix(ac0, ac1)) + mix(col(dsk, h0), col(dsk, h1)) * x
            al0, al1 = col(alast, h0), col(alast, h1)
            dout = jnp.exp(mix(al0 - ac0, al1 - ac1))
            st = lax.dot_general((xdt * dout).astype(BF16), b, (((0,), (0,)), ((), ())), preferred_element_type=F32)
            hns.append(hpj * jnp.exp(jnp.where(sub_half, al0, al1)) + st)
            ys.append(y)
        return ys, hns

    return f


def _ssd_dims(pre, dtr):
    nrow = pre.shape[0]
    ng, n, p, q = M2_GROUPS, M2_STATE, M2_HEAD_DIM, M2_CHUNK
    d_inner = pre.shape[1] - 2 * ng * n
    gw = d_inner // ng
    nhg = gw // p
    assert nhg % 2 == 0 and gw % LANES == 0 and n == LANES and ng * nhg <= LANES and dtr.shape[1] == LANES
    return nrow, ng, n, q, d_inner, gw, nhg, nrow // q


def _ssd_fwd(pre, dtr, dtb, alog, dsk, name):
    nrow, ng, n, q, d_inner, gw, nhg, nc = _ssd_dims(pre, dtr)
    boff = d_inner // n
    npair = nhg // 2

    def body(x_ref, b_ref, c_ref, dt_ref, dtb_ref, al_ref, ds_ref, y_ref, hs_ref, state):
        ci, g = pl.program_id(0), pl.program_id(1)

        @pl.when(ci == 0)
        def _():
            state[g] = jnp.zeros((gw, n), F32)

        hs_ref[...] = state[g]
        f = _ssd_chunk_fn(g, nhg, q)
        xp = [x_ref[:, LANES * j:LANES * (j + 1)] for j in range(npair)]
        hp = [state[g, LANES * j:LANES * (j + 1), :] for j in range(npair)]
        ys, hns = f(xp, b_ref[...], c_ref[...], dt_ref[...], hp, dtb_ref[...], al_ref[...], ds_ref[...])
        for j in range(npair):
            y_ref[:, LANES * j:LANES * (j + 1)] = ys[j]
            state[g, LANES * j:LANES * (j + 1), :] = hns[j]

    par = pl.BlockSpec((1, LANES), lambda ci, g: (0, 0))
    return pl.pallas_call(
        body, grid=(nc, ng),
        in_specs=[pl.BlockSpec((q, gw), lambda ci, g: (ci, g)), pl.BlockSpec((q, n), lambda ci, g: (ci, boff + g)),
                  pl.BlockSpec((q, n), lambda ci, g: (ci, boff + ng + g)), pl.BlockSpec((q, LANES), lambda ci, g: (ci, 0)),
                  par, par, par],
        out_specs=[pl.BlockSpec((q, gw), lambda ci, g: (ci, g)),
                   pl.BlockSpec((None, None, gw, n), lambda ci, g: (ci, g, 0, 0))],
        out_shape=[jax.ShapeDtypeStruct((nrow, d_inner), F32), jax.ShapeDtypeStruct((nc, ng, gw, n), F32)],
        scratch_shapes=[pltpu.VMEM((ng, gw, n), F32)], compiler_params=_cparams(2), name=name)(
            pre, pre, pre, dtr, dtb, alog, dsk)


def _ssd_bwd(pre, dtr, hs, dy, dtb, alog, dsk, name):
    nrow, ng, n, q, d_inner, gw, nhg, nc = _ssd_dims(pre, dtr)
    boff = d_inner // n
    npair = nhg // 2

    def body(x_ref, b_ref, c_ref, dt_ref, hs_ref, dy_ref, dtb_ref, al_ref, ds_ref,
             dx_ref, db_ref, dc_ref, ddt_ref, ddtb_ref, dal_ref, dds_ref, dstate):
        ci, g = pl.program_id(0), pl.program_id(1)

        @pl.when(ci == 0)
        def _():
            dstate[g] = jnp.zeros((gw, n), F32)

        @pl.when((ci == 0) & (g == 0))
        def _():
            ddtb_ref[...] = jnp.zeros_like(ddtb_ref)
            dal_ref[...] = jnp.zeros_like(dal_ref)
            dds_ref[...] = jnp.zeros_like(dds_ref)

        @pl.when(g == 0)
        def _():
            ddt_ref[...] = jnp.zeros_like(ddt_ref)

        f = _ssd_chunk_fn(g, nhg, q)
        xp = [x_ref[:, LANES * j:LANES * (j + 1)] for j in range(npair)]
        hp = [hs_ref[LANES * j:LANES * (j + 1), :] for j in range(npair)]
        _, vjp = jax.vjp(f, xp, b_ref[...], c_ref[...], dt_ref[...], hp, dtb_ref[...], al_ref[...], ds_ref[...])
        dys = [dy_ref[:, LANES * j:LANES * (j + 1)] for j in range(npair)]
        dhn = [dstate[g, LANES * j:LANES * (j + 1), :] for j in range(npair)]
        dxp, db, dc, ddt, dhp, ddtb, dal, dds = vjp((dys, dhn))
        for j in range(npair):
            dx_ref[:, LANES * j:LANES * (j + 1)] = dxp[j]
            dstate[g, LANES * j:LANES * (j + 1), :] = dhp[j]
        db_ref[...] = db
        dc_ref[...] = dc
        ddt_ref[...] += ddt
        ddtb_ref[...] += ddtb
        dal_ref[...] += dal
        dds_ref[...] += dds

    par = pl.BlockSpec((1, LANES), lambda ci, g: (0, 0))
    last = nc - 1
    return pl.pallas_call(
        body, grid=(nc, ng),
        in_specs=[pl.BlockSpec((q, gw), lambda ci, g: (last - ci, g)),
                  pl.BlockSpec((q, n), lambda ci, g: (last - ci, boff + g)),
                  pl.BlockSpec((q, n), lambda ci, g: (last - ci, boff + ng + g)),
                  pl.BlockSpec((q, LANES), lambda ci, g: (last - ci, 0)),
                  pl.BlockSpec((None, None, gw, n), lambda ci, g: (last - ci, g, 0, 0)),
                  pl.BlockSpec((q, gw), lambda ci, g: (last - ci, g)), par, par, par],
        out_specs=[pl.BlockSpec((q, gw), lambda ci, g: (last - ci, g)), pl.BlockSpec((q, n), lambda ci, g: (last - ci, g)),
                   pl.BlockSpec((q, n), lambda ci, g: (last - ci, g)), pl.BlockSpec((q, LANES), lambda ci, g: (last - ci, 0)),
                   par, par, par],
        out_shape=[jax.ShapeDtypeStruct((nrow, d_inner), F32), jax.ShapeDtypeStruct((nrow, ng * n), F32),
                   jax.ShapeDtypeStruct((nrow, ng * n), F32), jax.ShapeDtypeStruct((nrow, LANES), F32),
                   jax.ShapeDtypeStruct((1, LANES), F32), jax.ShapeDtypeStruct((1, LANES), F32),
                   jax.ShapeDtypeStruct((1, LANES), F32)],
        scratch_shapes=[pltpu.VMEM((ng, gw, n), F32)], compiler_params=_cparams(2), name=name)(
            pre, pre, pre, dtr, hs, dy, dtb, alog, dsk)


HBM_SPEC = pl.BlockSpec(memory_space=pl.ANY)


def _xy_peers(mx, my):
    return [(1 - mx, my), (mx, 1 - my), (1 - mx, 1 - my)]


def _remote(src, dst, send_sems, recv_sems, k, dev):
    return pltpu.make_async_remote_copy(src_ref=src, dst_ref=dst, send_sem=send_sems.at[k], recv_sem=recv_sems.at[k],
                                        device_id=dev, device_id_type=MESH_ID)


def _comm_call(body, arrays, out_shapes, ncopies, name):
    nin = len(arrays)

    def wrapped(*refs):
        body(refs[:nin], refs[nin:nin + len(out_shapes)], refs[-2], refs[-1])

    return pl.pallas_call(
        wrapped, out_shape=out_shapes, in_specs=[HBM_SPEC] * nin, out_specs=[HBM_SPEC] * len(out_shapes),
        scratch_shapes=[pltpu.SemaphoreType.DMA((ncopies,)), pltpu.SemaphoreType.DMA((ncopies,))], name=name)(*arrays)


def _all_gather_xy(bufs, name):
    npeer = N_XY - 1

    def body(srcs, outs, send_sems, recv_sems):
        mx, my, mc = lax.axis_index("x"), lax.axis_index("y"), lax.axis_index("c")
        me = 2 * mx + my
        sibling = (mx, my, 1 - mc)
        peers = _xy_peers(mx, my)
        sends = []
        for a, (src, out) in enumerate(zip(srcs, outs)):
            for k, (px, py) in enumerate(peers):
                sends.append(_remote(src.at[mc], out.at[me, mc], send_sems, recv_sems, 2 * npeer * a + k, (px, py, mc)))
        for cp in sends:
            cp.start()
        for a, (src, out) in enumerate(zip(srcs, outs)):
            for k, (px, py) in enumerate(peers):
                landed = out.at[2 * px + py, mc]
                _remote(src.at[mc], landed, send_sems, recv_sems, 2 * npeer * a + k, (px, py, mc)).wait_recv()
                cp = _remote(landed, landed, send_sems, recv_sems, 2 * npeer * a + npeer + k, sibling)
                cp.start()
                sends.append(cp)
        for a, out in enumerate(outs):
            for k, (px, py) in enumerate(peers):
                passed = out.at[2 * px + py, 1 - mc]
                _remote(passed, passed, send_sems, recv_sems, 2 * npeer * a + npeer + k, sibling).wait_recv()
        for cp in sends:
            cp.wait_send()

    outs = _comm_call(body, bufs, [jax.ShapeDtypeStruct((N_XY,) + b.shape, b.dtype) for b in bufs],
                      2 * npeer * len(bufs), name)
    return [lax.dynamic_update_index_in_dim(o, b, _my_shard(), 0) for o, b in zip(outs, bufs)]


def _my_shard():
    return 2 * lax.axis_index("x") + lax.axis_index("y")


def _exchange_xy(parts, name):
    npeer = N_XY - 1

    def body(srcs, outs, send_sems, recv_sems):
        mx, my, mc = lax.axis_index("x"), lax.axis_index("y"), lax.axis_index("c")
        me = 2 * mx + my
        peers = _xy_peers(mx, my)
        sends = [_remote(src.at[2 * px + py], out.at[me], send_sems, recv_sems, npeer * a + k, (px, py, mc))
                 for k, (px, py) in enumerate(peers) for a, (src, out) in enumerate(zip(srcs, outs))]
        for cp in sends:
            cp.start()
        for k, (px, py) in enumerate(peers):
            for a, (src, out) in enumerate(zip(srcs, outs)):
                _remote(src.at[me], out.at[2 * px + py], send_sems, recv_sems, npeer * a + k, (px, py, mc)).wait_recv()
        for cp in sends:
            cp.wait_send()

    outs = _comm_call(body, parts, [jax.ShapeDtypeStruct(p.shape, p.dtype) for p in parts], npeer * len(parts), name)
    me = _my_shard()
    return [lax.dynamic_update_index_in_dim(o, lax.dynamic_index_in_dim(p, me, 0, keepdims=False), me, 0)
            for o, p in zip(outs, parts)]


def _pair_split(parts, name):
    def body(srcs, gots, send_sems, recv_sems):
        mx, my, mc = lax.axis_index("x"), lax.axis_index("y"), lax.axis_index("c")
        copies, idx = [], 0
        for src, got in zip(srcs, gots):
            for s in range(src.shape[0]):
                copies.append(_remote(src.at[s, 1 - mc], got.at[s], send_sems, recv_sems, idx, (mx, my, 1 - mc)))
                idx += 1
        for cp in copies:
            cp.start()
        for cp in copies:
            cp.wait()

    gots = _comm_call(body, parts, [jax.ShapeDtypeStruct((p.shape[0],) + p.shape[2:], p.dtype) for p in parts],
                      sum(p.shape[0] for p in parts), name)
    mc = lax.axis_index("c")
    return [lax.dynamic_index_in_dim(p, mc, 1, keepdims=False) for p in parts], gots


def _pair_join(mines, name):
    def body(srcs, outs, send_sems, recv_sems):
        mx, my, mc = lax.axis_index("x"), lax.axis_index("y"), lax.axis_index("c")
        copies = [_remote(src, out, send_sems, recv_sems, a, (mx, my, 1 - mc))
                  for a, (src, out) in enumerate(zip(srcs, outs))]
        for cp in copies:
            cp.start()
        for cp in copies:
            cp.wait()

    others = _comm_call(body, mines, [jax.ShapeDtypeStruct(m.shape, m.dtype) for m in mines], len(mines), name)
    first = lax.axis_index("c") == 0
    return [jnp.where(first, jnp.stack([m, o]), jnp.stack([o, m])) for m, o in zip(mines, others)]


def _all_gather_all(buf, name):
    flips = [(fx, fy, fc) for fx in (0, 1) for fy in (0, 1) for fc in (0, 1) if fx or fy or fc]

    def body(src, out, send_sems, recv_sems):
        mx, my, mc = lax.axis_index("x"), lax.axis_index("y"), lax.axis_index("c")
        me = 4 * mx + 2 * my + mc
        peers = [(1 - mx if fx else mx, 1 - my if fy else my, 1 - mc if fc else mc) for fx, fy, fc in flips]
        sends = [_remote(src, out.at[me], send_sems, recv_sems, k, dev) for k, dev in enumerate(peers)]
        for cp in sends:
            cp.start()
        for k, (px, py, pc) in enumerate(peers):
            _remote(src, out.at[4 * px + 2 * py + pc], send_sems, recv_sems, k, (px, py, pc)).wait_recv()
        for cp in sends:
            cp.wait_send()

    out = pl.pallas_call(
        body, out_shape=jax.ShapeDtypeStruct((N_DEV,) + buf.shape, buf.dtype), in_specs=[HBM_SPEC], out_specs=HBM_SPEC,
        scratch_shapes=[pltpu.SemaphoreType.DMA((N_DEV - 1,)), pltpu.SemaphoreType.DMA((N_DEV - 1,))], name=name)(buf)
    me = 4 * lax.axis_index("x") + 2 * lax.axis_index("y") + lax.axis_index("c")
    return lax.dynamic_update_index_in_dim(out, buf, me, 0)


def _flat_tile(nrow, width, narrays):
    t = nrow
    for cand in (512, 256, 128, 64, 32, 16, 8):
        if nrow % cand == 0:
            t = cand
            if cand * width * 4 * narrays <= ROW_BLOCK_BYTES:
                break
    return t


def _sum_parts(parts, name):
    kparts, nrow, width = parts.shape
    t = _flat_tile(nrow, width, kparts + 1)

    def body(p_ref, o_ref):
        s = p_ref[0].astype(F32)
        for k in range(1, kparts):
            s = s + p_ref[k].astype(F32)
        o_ref[...] = s

    return pl.pallas_call(
        body, grid=(nrow // t,), in_specs=[pl.BlockSpec((kparts, t, width), lambda i: (0, i, 0))],
        out_specs=pl.BlockSpec((t, width), lambda i: (i, 0)), out_shape=jax.ShapeDtypeStruct((nrow, width), F32),
        compiler_params=_cparams(1), name=name)(parts)


def _add2(a, b, name):
    nparts, nrow, width = a.shape
    t = _flat_tile(nrow, width, 3)

    def body(a_ref, b_ref, o_ref):
        o_ref[...] = (a_ref[...].astype(F32) + b_ref[...].astype(F32)).astype(o_ref.dtype)

    spec = pl.BlockSpec((None, t, width), lambda p, i: (p, i, 0))
    return pl.pallas_call(body, grid=(nparts, nrow // t), in_specs=[spec, spec], out_specs=spec,
                          out_shape=jax.ShapeDtypeStruct(a.shape, a.dtype), compiler_params=_cparams(2), name=name)(a, b)


def _adamw(w, m, v, ga, gb, name):
    nrow, width = w.shape
    t = _flat_tile(nrow, width, 9)
    c1 = 1.0 - ADAM_B1 ** ADAM_STEP
    c2 = 1.0 - ADAM_B2 ** ADAM_STEP
    grads = [ga] if gb is None else [ga, gb]
    nout = 3 if gb is None else 4

    def body(w_ref, m_ref, v_ref, *refs):
        d_ref, mo_ref, vo_ref = refs[-3:]
        g = refs[0][...]
        if gb is not None:
            g = g + refs[1][...]
            refs[2][...] = g
        mn = ADAM_B1 * m_ref[...] + (1.0 - ADAM_B1) * g
        vn = ADAM_B2 * v_ref[...] + (1.0 - ADAM_B2) * jnp.square(g)
        mo_ref[...] = mn
        vo_ref[...] = vn
        d_ref[...] = -ADAM_LR * ((mn / c1) / (jnp.sqrt(vn / c2) + ADAM_EPS) + ADAM_WD * w_ref[...])

    spec = pl.BlockSpec((t, width), lambda i: (i, 0))
    return pl.pallas_call(
        body, grid=(nrow // t,), in_specs=[spec] * (3 + len(grads)), out_specs=[spec] * nout,
        out_shape=[jax.ShapeDtypeStruct((nrow, width), F32)] * nout, compiler_params=_cparams(1), name=name)(
            w, m, v, *grads)


def _pack(arrs, dtype, width):
    flat = jnp.concatenate([a.astype(dtype).reshape(-1) for a in arrs])
    quantum = PACK_ROWS * width
    pad = (-flat.shape[0]) % quantum
    if pad:
        flat = jnp.concatenate([flat, jnp.zeros((pad,), dtype)])
    return flat.reshape(-1, width)


def _unpack(buf, shapes):
    flat = buf.reshape(-1)
    out, off = [], 0
    for s in shapes:
        size = math.prod(s)
        out.append(flat[off:off + size].reshape(s))
        off += size
    return out


def _qk_norm_fwd(qkv, w, off, nh, tag):
    f = _rms_f(RMS_EPS)
    return _rowwise(lambda xv, wv: (f(xv, wv),), [(qkv, SB_HEAD_DIM, off)], [(w, False)],
                    [('row', nh * SB_HEAD_DIM, SB_HEAD_DIM, BF16)], ncb=nh, name=tag)[0]


def _qk_norm_bwd(qkv, w, off, dn, nh, tag):
    f = _rms_f(RMS_EPS)

    def fn(xv, dv, wv):
        _, vjp = jax.vjp(f, xv, wv)
        return vjp(dv)

    return _rowwise(fn, [(qkv, SB_HEAD_DIM, off), (dn, SB_HEAD_DIM, 0)], [(w, False)],
                    [('row', nh * SB_HEAD_DIM, SB_HEAD_DIM, F32), ('acc', (1, SB_HEAD_DIM), False)], ncb=nh, name=tag)


def _sb_forward(h, xin, wt, tag):
    d = xin.shape[1]
    nh = d // SB_HEAD_DIM
    qkv = _mm(h, wt['w_qkv'], "nn", name=tag + "_qkv")
    qn = _qk_norm_fwd(qkv, wt['q_norm_w'], 0, nh, tag + "_qnorm")
    kn = _qk_norm_fwd(qkv, wt['k_norm_w'], nh, nh, tag + "_knorm")
    o, rsave = _sb_attn_fwd(qn, kn, qkv, tag + "_attn")
    xm = _mm(o, wt['w_o'], "nn", extras=(xin,), epilogue=_ep_add, name=tag + "_out")
    return xm, dict(qkv=qkv, qn=qn, kn=kn, o=o, rsave=rsave)


def _sb_backward(h, dx, dxb, wt, sv, tag):
    nh = dx.shape[1] // SB_HEAD_DIM
    do = _mm(dxb, wt['w_o'], "nt", name=tag + "_do")
    g_wo = _mm(sv['o'], dxb, "tn", out_dtypes=(BF16,), name=tag + "_dwo")
    dqn, dkn, dv = _sb_attn_bwd(sv['qn'], sv['kn'], sv['qkv'], do, sv['rsave'], tag + "_attn_bwd")
    dq, g_qw = _qk_norm_bwd(sv['qkv'], wt['q_norm_w'], 0, dqn, nh, tag + "_qnorm_bwd")
    dk, g_kw = _qk_norm_bwd(sv['qkv'], wt['k_norm_w'], nh, dkn, nh, tag + "_knorm_bwd")
    dqkv = jnp.concatenate([dq, dk, dv], axis=1)
    g_wqkv = _mm(h, dqkv, "tn", out_dtypes=(BF16,), out_col_shards=N_XY, name=tag + "_dwqkv")
    dh = _mm(dqkv, wt['w_qkv'], "nt", name=tag + "_dh")
    return dh, dict(w_qkv=g_wqkv, w_o=g_wo, q_norm_w=g_qw, k_norm_w=g_kw)


def _ln_silu_f(x, w, b):
    mu = jnp.mean(x, axis=-1, keepdims=True)
    xc = x - mu
    return _silu(xc * lax.rsqrt(jnp.mean(xc * xc, axis=-1, keepdims=True) + LN_EPS) * w + b)


def _glu_bwd_fn(val, gate, dg):
    sg = 1.0 / (1.0 + jnp.exp(-gate))
    return (jnp.concatenate([dg * sg, dg * val * sg * (1.0 - sg)], axis=1),)


def _cf_forward(h, xin, wt, tag):
    d = xin.shape[1]
    u = _mm(h, wt['w_in'], "nn", extras=(wt['b_in'],), epilogue=_ep_add, name=tag + "_in")
    gl = _rowwise(lambda val, gate: (val / (1.0 + jnp.exp(-gate)),), [(u, d, 0), (u, d, 1)], [],
                  [('row', d, d, F32)], name=tag + "_glu")[0]
    cv = _dwconv_fwd(gl, wt['dw_w'], wt['dw_b'], tag + "_conv")
    s = _rowwise(lambda x, w, b: (_ln_silu_f(x, w, b),), [(cv, d, 0)], [(wt['ln_w'], False), (wt['ln_b'], False)],
                 [('row', d, d, BF16)], name=tag + "_ln")[0]
    xm = _mm(s, wt['w_out'], "nn", extras=(wt['b_out'], xin), epilogue=_ep_bias_add, name=tag + "_out")
    return xm, dict(u=u, gl=gl, cv=cv, s=s)


def _cf_backward(h, dx, dxb, wt, sv, tag):
    d = dx.shape[1]
    ds = _mm(dxb, wt['w_out'], "nt", name=tag + "_ds")
    g_wout = _mm(sv['s'], dxb, "tn", out_dtypes=(BF16,), name=tag + "_dwout")
    g_bout = _colsum(dx, tag + "_dbout")

    def ln_bwd(x, dsv, w, b):
        _, vjp = jax.vjp(_ln_silu_f, x, w, b)
        return vjp(dsv)

    dcv, g_lnw, g_lnb = _rowwise(ln_bwd, [(sv['cv'], d, 0), (ds, d, 0)], [(wt['ln_w'], False), (wt['ln_b'], False)],
                                 [('row', d, d, F32), ('acc', (1, d), False), ('acc', (1, d), False)],
                                 name=tag + "_ln_bwd")
    dgl, g_dww, g_dwb = _dwconv_bwd(sv['gl'], dcv, wt['dw_w'], tag + "_conv_bwd")
    du = _rowwise(_glu_bwd_fn, [(sv['u'], d, 0), (sv['u'], d, 1), (dgl, d, 0)], [], [('row', 2 * d, 2 * d, F32)],
                  name=tag + "_glu_bwd")[0]
    g_bin = _colsum(du, tag + "_dbin")
    g_win = _mm(h, du, "tn", out_dtypes=(BF16,), out_col_shards=N_XY, name=tag + "_dwin")
    dh = _mm(du, wt['w_in'], "nt", name=tag + "_dh")
    return dh, dict(w_in=g_win, b_in=g_bin, dw_w=g_dww, dw_b=g_dwb, ln_w=g_lnw, ln_b=g_lnb, w_out=g_wout, b_out=g_bout)


def _gated_norm_f(y, z, w):
    t = y * _silu(z)
    return t * lax.rsqrt(jnp.mean(t * t, axis=-1, keepdims=True) + M2_NORM_EPS) * w


def _m2_forward(h, xin, wt, tag):
    z = _mm(h, wt['w_z'], "nn", name=tag + "_z")
    xbc = _mm(h, wt['w_xbc'], "nn", name=tag + "_xbc")
    dtr = _mm(h, wt['w_dt'], "nn", name=tag + "_dt")
    pre = _dwconv_fwd(xbc, wt['conv_w'], wt['conv_b'], tag + "_conv")
    y, hs = _ssd_fwd(pre, dtr, wt['dt_bias'], wt['a_log'], wt['d'], tag + "_ssd")
    d_inner = y.shape[1]
    gw = d_inner // M2_GROUPS
    yn = _rowwise(lambda yv, zv, w: (_gated_norm_f(yv, zv, w),), [(y, gw, 0), (z, gw, 0)], [(wt['norm_w'], True)],
                  [('row', d_inner, gw, BF16)], ncb=M2_GROUPS, name=tag + "_gnorm")[0]
    xm = _mm(yn, wt['w_out'], "nn", extras=(xin,), epilogue=_ep_add, name=tag + "_out")
    return xm, dict(z=z, xbc=xbc, dtr=dtr, pre=pre, y=y, hs=hs, yn=yn)


def _m2_backward(h, dx, dxb, wt, sv, tag):
    dyn = _mm(dxb, wt['w_out'], "nt", name=tag + "_dyn")
    g_wout = _mm(sv['yn'], dxb, "tn", out_dtypes=(BF16,), name=tag + "_dwout")
    d_inner = sv['y'].shape[1]
    gw = d_inner // M2_GROUPS

    def gn_bwd(yv, zv, dv, w):
        _, vjp = jax.vjp(_gated_norm_f, yv, zv, w)
        return vjp(dv)

    dy, dz, g_nw = _rowwise(gn_bwd, [(sv['y'], gw, 0), (sv['z'], gw, 0), (dyn, gw, 0)], [(wt['norm_w'], True)],
                            [('row', d_inner, gw, F32), ('row', d_inner, gw, F32), ('acc', (1, d_inner), True)],
                            ncb=M2_GROUPS, name=tag + "_gnorm_bwd")
    dxp, db, dc, ddt, g_dtb, g_alog, g_d = _ssd_bwd(sv['pre'], sv['dtr'], sv['hs'], dy, wt['dt_bias'], wt['a_log'],
                                                   wt['d'], tag + "_ssd_bwd")
    dpre = jnp.concatenate([dxp, db, dc], axis=1)
    dxbc, g_cw, g_cb = _dwconv_bwd(sv['xbc'], dpre, wt['conv_w'], tag + "_conv_bwd")
    g_wz = _mm(h, dz, "tn", name=tag + "_dwz")
    g_wxbc = _mm(h, dxbc, "tn", name=tag + "_dwxbc")
    g_wdt = _mm(h, ddt, "tn", name=tag + "_dwdt")
    dh = _mm(dz, wt['w_z'], "nt", name=tag + "_dh_z")
    dh = _mm(dxbc, wt['w_xbc'], "nt", extras=(dh,), epilogue=_ep_add, name=tag + "_dh_xbc")
    dh = _mm(ddt, wt['w_dt'], "nt", extras=(dh,), epilogue=_ep_add, name=tag + "_dh_dt")
    return dh, dict(w_z=g_wz, w_xbc=g_wxbc, w_dt=g_wdt, conv_w=g_cw, conv_b=g_cb, dt_bias=g_dtb, a_log=g_alog, d=g_d,
                    norm_w=g_nw, w_out=g_wout)


def _pad_lanes(v):
    return jnp.pad(v.reshape(1, -1), ((0, 0), (0, LANES - v.shape[0])))


def kernel(x, norm_mix_w, norm_mlp_w, sb_w_qkv, sb_q_norm_w, sb_k_norm_w, sb_w_o, cf_w_in, cf_b_in, cf_dw_w, cf_dw_b, cf_ln_w, cf_ln_b, cf_w_out, cf_b_out, m2_w_in, m2_conv_w, m2_conv_b, m2_dt_bias, m2_a_log, m2_d, m2_norm_w, m2_w_out, mlp_w_up, mlp_w_down, loss_target, m_norm_mix_w, m_norm_mlp_w, m_sb_w_qkv, m_sb_q_norm_w, m_sb_k_norm_w, m_sb_w_o, m_cf_w_in, m_cf_b_in, m_cf_dw_w, m_cf_dw_b, m_cf_ln_w, m_cf_ln_b, m_cf_w_out, m_cf_b_out, m_m2_w_in, m_m2_conv_w, m_m2_conv_b, m_m2_dt_bias, m_m2_a_log, m_m2_d, m_m2_norm_w, m_m2_w_out, m_mlp_w_up, m_mlp_w_down, v_norm_mix_w, v_norm_mlp_w, v_sb_w_qkv, v_sb_q_norm_w, v_sb_k_norm_w, v_sb_w_o, v_cf_w_in, v_cf_b_in, v_cf_dw_w, v_cf_dw_b, v_cf_ln_w, v_cf_ln_b, v_cf_w_out, v_cf_b_out, v_m2_w_in, v_m2_conv_w, v_m2_conv_b, v_m2_dt_bias, v_m2_a_log, v_m2_d, v_m2_norm_w, v_m2_w_out, v_mlp_w_up, v_mlp_w_down):
    given = dict(locals())
    xl = x[0]
    tgt = loss_target[0]
    d = xl.shape[1]
    depth = norm_mix_w.shape[0]
    bulk_names, small_names = list(BULK), list(SMALL)

    def halves(buf):
        return buf.reshape(2, -1, buf.shape[-1])

    small_pack = halves(_pack([given[n] for n in small_names], F32, PACK_W))
    gath = _all_gather_xy([halves(given[n].astype(BF16)) for n in bulk_names] + [small_pack], "comm_gather")
    gathered = {n: g.reshape((N_XY,) + given[n].shape) for n, g in zip(bulk_names, gath)}
    kinds = {n: "rows" if BULK[n] == 1 else "cols" for n in bulk_names}
    full = {}
    pieces = [_unpack(gath[-1][s], [given[n].shape for n in small_names]) for s in range(N_XY)]
    for idx, n in enumerate(small_names):
        full[n] = jnp.concatenate([pieces[s][idx] for s in range(N_XY)], axis=SMALL[n])

    def wview(n, layer):
        return WView(gathered[n], layer, kinds[n])

    def row(v):
        return v.reshape(1, -1)

    d_inner = N_XY * m2_w_out.shape[1]
    conv_dim = full['m2_conv_w'].shape[2]
    nheads = m2_dt_bias.shape[1]

    def layer_weights(i):
        kind, j = i % 3, i // 3
        if kind == 0:
            return dict(w_qkv=wview('sb_w_qkv', j), w_o=wview('sb_w_o', j), q_norm_w=row(sb_q_norm_w[j]),
                        k_norm_w=row(sb_k_norm_w[j]))
        if kind == 1:
            return dict(w_in=wview('cf_w_in', j), b_in=row(cf_b_in[j]), dw_w=full['cf_dw_w'][j], dw_b=row(cf_dw_b[j]),
                        ln_w=row(cf_ln_w[j]), ln_b=row(cf_ln_b[j]), w_out=wview('cf_w_out', j), b_out=row(cf_b_out[j]))
        shards = gathered['m2_w_in'][:, j]
        w_in = jnp.moveaxis(shards, 0, 1).reshape(shards.shape[1], -1)
        w_dt = jnp.pad(w_in[:, d_inner + conv_dim:], ((0, 0), (0, LANES - nheads)))
        return dict(w_z=w_in[:, :d_inner], w_xbc=w_in[:, d_inner:d_inner + conv_dim], w_dt=w_dt,
                    conv_w=full['m2_conv_w'][j], conv_b=row(full['m2_conv_b'][j]), dt_bias=_pad_lanes(m2_dt_bias[j]),
                    a_log=_pad_lanes(m2_a_log[j]), d=_pad_lanes(m2_d[j]), norm_w=row(full['m2_norm_w'][j]),
                    w_out=wview('m2_w_out', j))

    fwd = (_sb_forward, _cf_forward, _m2_forward)
    bwd = (_sb_backward, _cf_backward, _m2_backward)

    saved = []
    xc = xl
    for i in range(depth):
        wt = layer_weights(i)
        h = _rms_fwd(xc, row(norm_mix_w[i]), f"l{i}_norm_mix")
        xm, sv = fwd[i % 3](h, xc, wt, f"l{i}_mix")
        h2 = _rms_fwd(xm, row(norm_mlp_w[i]), f"l{i}_norm_mlp")
        u, a = _mm(h2, wview('mlp_w_up', i), "nn", epilogue=_ep_relu2, out_dtypes=(F32, BF16), name=f"l{i}_up")
        xn = _mm(a, wview('mlp_w_down', i), "nn", extras=(xm,), epilogue=_ep_add, name=f"l{i}_down")
        saved.append(dict(wt=wt, x_in=xc, h=h, mix=sv, x_mid=xm, h2=h2, u=u, a=a))
        xc = xn
    dx, dxb, loss_acc = _loss(xc, tgt, "loss")
    loss = lax.psum(loss_acc[0, 0], ("x", "y", "c"))

    grads = {n: [None] * given[n].shape[0] for n in W_NAMES}
    for i in reversed(range(depth)):
        sv = saved[i]
        kind, j = i % 3, i // 3
        du = _mm(dxb, wview('mlp_w_down', i), "nt", extras=(sv['u'],), epilogue=_ep_relu2_bwd, out_dtypes=(BF16,),
                 name=f"l{i}_du")
        grads['mlp_w_down'][i] = _mm(sv['a'], dxb, "tn", out_dtypes=(BF16,), name=f"l{i}_dwdown")
        grads['mlp_w_up'][i] = _mm(sv['h2'], du, "tn", out_dtypes=(BF16,), out_col_shards=N_XY, name=f"l{i}_dwup")
        dh2 = _mm(du, wview('mlp_w_up', i), "nt", name=f"l{i}_dh2")
        dxm, dxmb, g_n2 = _rms_bwd(sv['x_mid'], row(norm_mlp_w[i]), dh2, dx, f"l{i}_norm_mlp_bwd")
        grads['norm_mlp_w'][i] = g_n2[0]
        dh, gw = bwd[kind](sv['h'], dxm, dxmb, sv['wt'], sv['mix'], f"l{i}_mix")
        dx, dxb, g_n1 = _rms_bwd(sv['x_in'], row(norm_mix_w[i]), dh, dxm, f"l{i}_norm_mix_bwd")
        grads['norm_mix_w'][i] = g_n1[0]
        if kind == 0:
            grads['sb_w_qkv'][j], grads['sb_w_o'][j] = gw['w_qkv'], gw['w_o']
            grads['sb_q_norm_w'][j], grads['sb_k_norm_w'][j] = gw['q_norm_w'][0], gw['k_norm_w'][0]
        elif kind == 1:
            for n in ('w_in', 'dw_w', 'w_out'):
                grads['cf_' + n][j] = gw[n]
            for n in ('b_in', 'dw_b', 'ln_w', 'ln_b', 'b_out'):
                grads['cf_' + n][j] = gw[n][0]
        else:
            grads['m2_w_in'][j] = jnp.concatenate([gw['w_z'], gw['w_xbc'], gw['w_dt'][:, :nheads]], axis=1)
            grads['m2_conv_w'][j], grads['m2_w_out'][j] = gw['conv_w'], gw['w_out']
            grads['m2_conv_b'][j], grads['m2_norm_w'][j] = gw['conv_b'][0], gw['norm_w'][0]
            for n in ('dt_bias', 'a_log', 'd'):
                grads['m2_' + n][j] = gw[n][0, :nheads]
    grad_x = dx[None]
    gfull = {n: jnp.stack(grads[n]) for n in small_names + REPL}

    def shard_major(n, g):
        _, r, c = given[n].shape
        if n == 'm2_w_in':
            return jnp.moveaxis(g.reshape(r, N_XY, c), 1, 0).astype(BF16)
        return g.reshape(N_XY, r, c)

    parts = [jnp.stack([shard_major(n, g) for g in grads[n]], axis=1) for n in bulk_names]
    parts = [p.reshape(N_XY, 2, -1, p.shape[-1]) for p in parts]
    split = {n: jnp.split(gfull[n], N_XY, axis=SMALL[n]) for n in small_names}
    parts.append(jnp.stack([halves(_pack([split[n][s] for n in small_names], F32, PACK_W)) for s in range(N_XY)]))
    tags = bulk_names + ["small"]
    kept, got = _pair_split(parts, "comm_split")
    pair = [_add2(k, g, f"sum_pair_{t}") for k, g, t in zip(kept, got, tags)]
    mine = [_sum_parts(r, f"sum_{t}") for r, t in zip(_exchange_xy(pair, "comm_exchange"), tags)]
    gsum = _pair_join(mine, "comm_join")
    outs = {}
    for n, g in list(zip(bulk_names, gsum)) + list(zip(small_names, _unpack(gsum[-1], [given[n].shape for n in small_names]))):
        shape = given[n].shape
        res = _adamw(*[given[p + n].reshape(-1, shape[-1]) for p in ("", "m_", "v_")], g.reshape(-1, shape[-1]),
                     None, f"adamw_{n}")
        outs["grad", n] = g.reshape(shape)
        for kind, arr in zip(("delta", "new_m", "new_v"), res):
            outs[kind, n] = arr.reshape(shape)
    rparts = _all_gather_all(_pack([gfull[n] for n in REPL], F32, LANES), "comm_gather_repl")
    nsplit = N_DEV // 2
    res = _adamw(*[_pack([given[p + n] for n in REPL], F32, LANES) for p in ("", "m_", "v_")],
                 _sum_parts(rparts[:nsplit], "sum_repl_a"), _sum_parts(rparts[nsplit:], "sum_repl_b"), "adamw_repl")
    shapes = [given[n].shape for n in REPL]
    for kind, buf in zip(("grad", "delta", "new_m", "new_v"), res):
        for n, arr in zip(REPL, _unpack(buf, shapes)):
            outs[kind, n] = arr

    return (loss, grad_x, *[outs[kind, n] for kind in ("grad", "delta", "new_m", "new_v") for n in W_NAMES])
```

```python
import math
from typing import NamedTuple

import jax
import jax.numpy as jnp
from jax import lax
from jax.experimental import pallas as pl
from jax.experimental.pallas import tpu as pltpu

F32 = jnp.float32
BF16 = jnp.bfloat16
MESH_ID = pl.DeviceIdType.MESH
HIGHEST = lax.Precision.HIGHEST

SB_HEAD_DIM = 128
M2_HEAD_DIM = 64
M2_STATE = 128
M2_GROUPS = 8
M2_CHUNK = 128
RMS_EPS = 1e-6
LN_EPS = 1e-5
M2_NORM_EPS = 1e-5
ADAM_LR = 0.001
ADAM_B1 = 0.9
ADAM_B2 = 0.999
ADAM_EPS = 1e-08
ADAM_WD = 0.01
ADAM_STEP = 10

N_XY = 4
N_DEV = 8

V7X_VMEM_BYTES = 64 * 2**20
VMEM_LIMIT = (V7X_VMEM_BYTES * 3) // 4
LANES = 128
ROW_BLOCK_BYTES = 6 * 2**20
MM_VMEM_BUDGET = (VMEM_LIMIT * 3) // 4
PACK_W = 1024
PACK_ROWS = 1024

W_NAMES = ['norm_mix_w', 'norm_mlp_w', 'sb_w_qkv', 'sb_q_norm_w', 'sb_k_norm_w', 'sb_w_o', 'cf_w_in', 'cf_b_in',
           'cf_dw_w', 'cf_dw_b', 'cf_ln_w', 'cf_ln_b', 'cf_w_out', 'cf_b_out', 'm2_w_in', 'm2_conv_w', 'm2_conv_b',
           'm2_dt_bias', 'm2_a_log', 'm2_d', 'm2_norm_w', 'm2_w_out', 'mlp_w_up', 'mlp_w_down']
BULK = {'sb_w_qkv': 2, 'sb_w_o': 1, 'cf_w_in': 2, 'cf_w_out': 1, 'm2_w_in': 2, 'm2_w_out': 1, 'mlp_w_up': 2,
        'mlp_w_down': 1}
SMALL = {'cf_dw_w': 2, 'm2_conv_w': 2, 'm2_conv_b': 1, 'm2_norm_w': 1}
REPL = ['norm_mix_w', 'norm_mlp_w', 'sb_q_norm_w', 'sb_k_norm_w', 'cf_b_in', 'cf_dw_b', 'cf_ln_w', 'cf_ln_b',
        'cf_b_out', 'm2_dt_bias', 'm2_a_log', 'm2_d']


def _tile(n, prefs):
    for p in prefs:
        if n % p == 0:
            return p
    return n


def _cparams(n):
    return pltpu.CompilerParams(dimension_semantics=("arbitrary",) * n, vmem_limit_bytes=VMEM_LIMIT)


def _silu(v):
    return v / (1.0 + jnp.exp(-v))


def _softplus(v):
    return jnp.maximum(v, 0.0) + jnp.log(1.0 + jnp.exp(-jnp.abs(v)))


class WView(NamedTuple):
    arr: jax.Array
    layer: int
    kind: str


def _mm(a, b, mode, *, extras=(), epilogue=None, out_dtypes=(F32,), out_col_shards=1, name):
    if mode == "tn":
        kdim, m = a.shape
    else:
        m, kdim = a.shape
    view = b if isinstance(b, WView) else None
    if view is not None:
        _, _, srows, scols = view.arr.shape
        brows, bcols = (N_XY * srows, scols) if view.kind == "rows" else (srows, N_XY * scols)
        b_arr, b_item = view.arr, view.arr.dtype.itemsize
    else:
        brows, bcols = b.shape
        b_arr, b_item = b, b.dtype.itemsize
    n = brows if mode == "nt" else bcols
    assert kdim == (bcols if mode == "nt" else brows)
    n_unit, k_unit = n // out_col_shards, kdim
    if view is not None:
        if (mode == "nn") == (view.kind == "cols"):
            n_unit = n // N_XY
        else:
            k_unit = kdim // N_XY
    tm = _tile(m, (1024, 512, 256, 128))
    tn = _tile(n_unit, (1024, 768, 512, 256, 128))
    acc_in_out = out_dtypes[0] == F32 and out_col_shards == 1
    fixed = sum(tm * tn * jnp.dtype(dt).itemsize for dt in out_dtypes)
    fixed += sum((1 if e.shape[0] == 1 else tm) * tn * e.dtype.itemsize for e in extras)

    def vmem_bytes(t):
        ab = tm * t * a.dtype.itemsize + t * tn * b_item
        casts = (tm * t * 2 if a.dtype != BF16 else 0) + (t * tn * 2 if b_item != 2 else 0)
        return 2 * (ab + fixed) + (0 if acc_in_out else tm * tn * 4) + tm * tn * 4 + casts

    tk = k_unit
    for cand in (2048, 1536, 1024, 768, 512, 256, 128):
        if k_unit % cand == 0:
            tk = cand
            if vmem_bytes(cand) <= MM_VMEM_BUDGET:
                break
    nk = kdim // tk
    dims = {"nn": ((1,), (0,)), "nt": ((1,), (1,)), "tn": ((0,), (0,))}[mode]
    if mode == "tn":
        a_spec = pl.BlockSpec((tk, tm), lambda i, j, k: (k, i))
    else:
        a_spec = pl.BlockSpec((tm, tk), lambda i, j, k: (i, k))
    if view is None:
        if mode == "nt":
            b_spec = pl.BlockSpec((tn, tk), lambda i, j, k: (j, k))
        else:
            b_spec = pl.BlockSpec((tk, tn), lambda i, j, k: (k, j))
    else:
        lay = view.layer
        rc = (lambda j, k: (j, k)) if mode == "nt" else (lambda j, k: (k, j))
        blk = (tn, tk) if mode == "nt" else (tk, tn)
        if view.kind == "rows":
            per = srows // blk[0]
            b_spec = pl.BlockSpec((None, None) + blk,
                                  lambda i, j, k: (rc(j, k)[0] // per, lay, rc(j, k)[0] % per, rc(j, k)[1]))
        else:
            per = scols // blk[1]
            b_spec = pl.BlockSpec((None, None) + blk,
                                  lambda i, j, k: (rc(j, k)[1] // per, lay, rc(j, k)[0], rc(j, k)[1] % per))
    ex_specs = [pl.BlockSpec((1, tn), lambda i, j, k: (0, j)) if e.shape[0] == 1
                else pl.BlockSpec((tm, tn), lambda i, j, k: (i, j)) for e in extras]
    n_ex, n_out = len(extras), len(out_dtypes)

    def body(*refs):
        a_ref, b_ref = refs[:2]
        ex = refs[2:2 + n_ex]
        outs = refs[2 + n_ex:2 + n_ex + n_out]
        acc = outs[0] if acc_in_out else refs[-1]
        k = pl.program_id(2)
        prod = lax.dot_general(a_ref[...].astype(BF16), b_ref[...].astype(BF16), (dims, ((), ())),
                               preferred_element_type=F32)

        def finish(res):
            vals = epilogue(res, *[e[...] for e in ex]) if epilogue is not None else (res,)
            for o, v in zip(outs, vals):
                o[...] = v.astype(o.dtype)

        if nk == 1:
            finish(prod)
        else:
            @pl.when(k == 0)
            def _():
                acc[...] = prod

            @pl.when((k > 0) & (k < nk - 1))
            def _():
                acc[...] += prod

            @pl.when(k == nk - 1)
            def _():
                finish(acc[...] + prod)

    scratch = [] if (acc_in_out or nk == 1) else [pltpu.VMEM((tm, tn), F32)]
    if out_col_shards == 1:
        out_specs = [pl.BlockSpec((tm, tn), lambda i, j, k: (i, j))] * n_out
        out_shape = [jax.ShapeDtypeStruct((m, n), d) for d in out_dtypes]
    else:
        per_out = n_unit // tn
        out_specs = [pl.BlockSpec((None, tm, tn), lambda i, j, k: (j // per_out, i, j % per_out))]
        out_shape = [jax.ShapeDtypeStruct((out_col_shards, m, n_unit), out_dtypes[0])]
    outs = pl.pallas_call(
        body, grid=(m // tm, n // tn, nk), in_specs=[a_spec, b_spec, *ex_specs], out_specs=out_specs,
        out_shape=out_shape, scratch_shapes=scratch, compiler_params=_cparams(3), name=name)(a, b_arr, *extras)
    return outs[0] if n_out == 1 else outs


def _ep_add(acc, r):
    return (acc + r,)


def _ep_bias_add(acc, b, r):
    return (acc + b + r,)


def _ep_relu2(acc):
    return acc, jnp.square(jnp.maximum(acc, 0.0))


def _ep_relu2_bwd(acc, u):
    return (acc * (2.0 * jnp.maximum(u, 0.0)),)


def _rowwise(fn, rows, consts, outs, *, ncb=1, name):
    nrow = rows[0][0].shape[0]
    row_bytes = sum(w * arr.dtype.itemsize for arr, w, _ in rows)
    row_bytes += sum(o[2] * jnp.dtype(o[3]).itemsize for o in outs if o[0] == 'row')
    t = nrow
    for cand in (1024, 512, 256, 128, 64, 32, 16):
        if nrow % cand == 0:
            t = cand
            if cand * row_bytes <= ROW_BLOCK_BYTES:
                break
    in_specs = [pl.BlockSpec((t, w), (lambda j, i, off=off: (i, off + j))) for _, w, off in rows]
    for arr, per_col in consts:
        r, wc = arr.shape
        if per_col:
            in_specs.append(pl.BlockSpec((r, wc // ncb), lambda j, i: (0, j)))
        else:
            in_specs.append(pl.BlockSpec((r, wc), lambda j, i: (0, 0)))
    out_specs, out_shape = [], []
    for o in outs:
        if o[0] == 'row':
            out_specs.append(pl.BlockSpec((t, o[2]), lambda j, i: (i, j)))
            out_shape.append(jax.ShapeDtypeStruct((nrow, o[1]), o[3]))
        else:
            r, wt = o[1]
            if o[2]:
                out_specs.append(pl.BlockSpec((r, wt // ncb), lambda j, i: (0, j)))
            else:
                out_specs.append(pl.BlockSpec((r, wt), lambda j, i: (0, 0)))
            out_shape.append(jax.ShapeDtypeStruct((r, wt), F32))
    nin = len(rows) + len(consts)

    def body(*refs):
        j, i = pl.program_id(0), pl.program_id(1)
        vals = fn(*[r[...] for r in refs[:nin]])

        def store(spec, ref, v):
            if spec[0] == 'row':
                ref[...] = v.astype(ref.dtype)
            else:
                first = (i == 0) if spec[2] else ((i == 0) & (j == 0))

                @pl.when(first)
                def _():
                    ref[...] = jnp.zeros_like(ref)

                ref[...] += v

        for spec, ref, v in zip(outs, refs[nin:], vals):
            store(spec, ref, v)

    res = pl.pallas_call(body, grid=(ncb, nrow // t), in_specs=in_specs, out_specs=out_specs, out_shape=out_shape,
                         compiler_params=_cparams(2), name=name)(*[r[0] for r in rows], *[c[0] for c in consts])
    return res


def _rms_f(eps):
    def f(x, w):
        return x * lax.rsqrt(jnp.mean(x * x, axis=-1, keepdims=True) + eps) * w
    return f


def _rms_fwd(x, w, name):
    d = x.shape[1]
    f = _rms_f(RMS_EPS)
    return _rowwise(lambda xv, wv: (f(xv, wv),), [(x, d, 0)], [(w, False)], [('row', d, d, BF16)], name=name)[0]


def _rms_bwd(x, w, dh, dres, name):
    d = x.shape[1]
    f = _rms_f(RMS_EPS)

    def fn(xv, dhv, drv, wv):
        _, vjp = jax.vjp(f, xv, wv)
        dx, dw = vjp(dhv)
        return dx + drv, dx + drv, dw

    return _rowwise(fn, [(x, d, 0), (dh, d, 0), (dres, d, 0)], [(w, False)],
                    [('row', d, d, F32), ('row', d, d, BF16), ('acc', (1, d), False)], name=name)


def _colsum(x, name):
    w = x.shape[1]
    return _rowwise(lambda v: (jnp.sum(v, axis=0, keepdims=True),), [(x, w, 0)], [], [('acc', (1, w), False)],
                    name=name)[0]


def _loss(y, tgt, name):
    d = y.shape[1]

    def fn(yv, tv):
        e = yv - tv
        s = jnp.sum(jnp.sum(e * e, axis=1, keepdims=True), axis=0, keepdims=True) * (0.5 / d)
        return e * (1.0 / d), e * (1.0 / d), s + jnp.zeros((1, LANES), F32)

    return _rowwise(fn, [(y, d, 0), (tgt, d, 0)], [],
                    [('row', d, d, F32), ('row', d, d, BF16), ('acc', (1, LANES), False)], name=name)


def _conv_tiles(x, w):
    nrow, ch = x.shape
    kw = w.shape[0]
    halo = 8 * ((kw - 1 + 7) // 8)
    few_taps = kw <= 8
    t = _tile(nrow, (512, 256, 128, 64, 32) if few_taps else (256, 128, 64, 32))
    tc = _tile(ch, (1024, 512, 256, 128) if few_taps else (512, 256, 128))
    assert t % halo == 0 and nrow % t == 0
    return nrow, ch, kw, halo, t, tc


CONV_SUB = 32


def _shifted_windows(buf, shifts, t, base):
    out = {}
    for b in range(8):
        group = [s for s in shifts if s % 8 == b]
        if group:
            rows = buf[base + b:base + max(group) + t, :]
            for s in group:
                out[s] = rows[s - b:s - b + t]
    return out


def _fold8(v):
    out = v[0:8]
    for r in range(8, v.shape[0], 8):
        out = out + v[r:r + 8]
    return out


def _dwconv_fwd(x, w, b, name):
    nrow, ch, kw, halo, t, tc = _conv_tiles(x, w)
    per = t // halo
    sub = min(CONV_SUB, t)

    def body(x_ref, halo_ref, w_ref, b_ref, y_ref, buf):
        i = pl.program_id(1)
        buf[0:halo, :] = jnp.where(i == 0, 0.0, halo_ref[...])
        buf[halo:halo + t, :] = x_ref[...]
        shifts = [halo - (kw - 1) + k for k in range(kw)]
        for r0 in range(0, t, sub):
            win = _shifted_windows(buf, shifts, sub, r0)
            acc = jnp.zeros((sub, tc), F32) + b_ref[...]
            for k in range(kw):
                acc = acc + w_ref[k:k + 1, :] * win[shifts[k]]
            y_ref[r0:r0 + sub, :] = acc

    return pl.pallas_call(
        body, grid=(ch // tc, nrow // t),
        in_specs=[pl.BlockSpec((t, tc), lambda j, i: (i, j)),
                  pl.BlockSpec((halo, tc), lambda j, i: (jnp.maximum(i * per - 1, 0), j)),
                  pl.BlockSpec((kw, tc), lambda j, i: (0, j)), pl.BlockSpec((1, tc), lambda j, i: (0, j))],
        out_specs=pl.BlockSpec((t, tc), lambda j, i: (i, j)), out_shape=jax.ShapeDtypeStruct((nrow, ch), F32),
        scratch_shapes=[pltpu.VMEM((t + halo, tc), F32)], compiler_params=_cparams(2), name=name)(x, x, w, b)


def _dwconv_bwd(x, dy, w, name):
    nrow, ch, kw, halo, t, tc = _conv_tiles(x, w)
    per = t // halo
    sub = min(CONV_SUB, t)
    nt = nrow // t
    nhalo = nrow // halo

    def body(x_ref, xh_ref, dy_ref, dyh_ref, w_ref, dx_ref, dw_ref, db_ref, xbuf, dybuf):
        i = pl.program_id(1)
        xbuf[0:halo, :] = jnp.where(i == 0, 0.0, xh_ref[...])
        xbuf[halo:halo + t, :] = x_ref[...]
        dyc = dy_ref[...]
        dybuf[0:t, :] = dyc
        dybuf[t:t + halo, :] = jnp.where(i == nt - 1, 0.0, dyh_ref[...])
        @pl.when(i == 0)
        def _():
            dw_ref[...] = jnp.zeros_like(dw_ref)
            db_ref[...] = jnp.zeros_like(db_ref)

        dw_part = [jnp.zeros((8, tc), F32) for _ in range(kw)]
        for r0 in range(0, t, sub):
            dywin = _shifted_windows(dybuf, list(range(kw)), sub, r0)
            acc = jnp.zeros((sub, tc), F32)
            for k in range(kw):
                acc = acc + w_ref[k:k + 1, :] * dywin[kw - 1 - k]
            dx_ref[r0:r0 + sub, :] = acc
            xwin = _shifted_windows(xbuf, [halo - s for s in range(kw)], sub, r0)
            for k in range(kw):
                dw_part[k] = dw_part[k] + _fold8(dywin[0] * xwin[halo - (kw - 1 - k)])
        for k in range(kw):
            dw_ref[k:k + 1, :] += jnp.sum(dw_part[k], axis=0, keepdims=True)
        db_ref[...] += jnp.sum(dyc, axis=0, keepdims=True)

    return pl.pallas_call(
        body, grid=(ch // tc, nt),
        in_specs=[pl.BlockSpec((t, tc), lambda j, i: (i, j)),
                  pl.BlockSpec((halo, tc), lambda j, i: (jnp.maximum(i * per - 1, 0), j)),
                  pl.BlockSpec((t, tc), lambda j, i: (i, j)),
                  pl.BlockSpec((halo, tc), lambda j, i: (jnp.minimum((i + 1) * per, nhalo - 1), j)),
                  pl.BlockSpec((kw, tc), lambda j, i: (0, j))],
        out_specs=[pl.BlockSpec((t, tc), lambda j, i: (i, j)), pl.BlockSpec((kw, tc), lambda j, i: (0, j)),
                   pl.BlockSpec((1, tc), lambda j, i: (0, j))],
        out_shape=[jax.ShapeDtypeStruct((nrow, ch), F32), jax.ShapeDtypeStruct((kw, ch), F32),
                   jax.ShapeDtypeStruct((1, ch), F32)],
        scratch_shapes=[pltpu.VMEM((t + halo, tc), F32), pltpu.VMEM((t + halo, tc), F32)],
        compiler_params=_cparams(2), name=name)(x, x, dy, dy, w)


SB_SUB = 128
LOG2E = 1.4426950408889634
SB_SKIP_BELOW = -256.0
SB_UNVISITED = -1e30


def _dot_split2(x, u):
    hi = x.astype(BF16)
    lo = (x - hi.astype(F32)).astype(BF16)
    return jnp.dot(hi, u, preferred_element_type=F32) + jnp.dot(lo, u, preferred_element_type=F32)


def _sb_scores(q, k, strict, scale):
    z = lax.dot_general(q, k, (((1,), (1,)), ((), ())), preferred_element_type=F32) * (scale * LOG2E)
    soft = jnp.log2(1.0 + jnp.exp2(-jnp.abs(z)))
    ls = jnp.minimum(z, 0.0) - soft
    lk = ls - z
    if strict is not None:
        lk = jnp.where(strict, lk, 0.0)
    return ls, lk


def _sb_running(x, tri, start, reverse):
    nsub = x.shape[1] // SB_SUB
    parts = [x[:, SB_SUB * b:SB_SUB * (b + 1)] for b in range(nsub)]
    out = [None] * nsub
    run = start
    for b in (reversed(range(nsub)) if reverse else range(nsub)):
        out[b] = _dot_split2(parts[b], tri) + run
        run = run + jnp.sum(parts[b], axis=1, keepdims=True)
    return (jnp.concatenate(out, axis=1) if nsub > 1 else out[0]), run


def _sb_dims(qn):
    nrow, d = qn.shape
    hd = SB_HEAD_DIM
    t = _tile(nrow, (512, 256, 128))
    assert nrow % t == 0 and nrow // t <= LANES
    return nrow, d, hd, d // hd, t


def _sb_masks(t):
    strict = lax.broadcasted_iota(jnp.int32, (t, t), 1) < lax.broadcasted_iota(jnp.int32, (t, t), 0)
    r0 = lax.broadcasted_iota(jnp.int32, (SB_SUB, SB_SUB), 0)
    c0 = lax.broadcasted_iota(jnp.int32, (SB_SUB, SB_SUB), 1)
    return strict, (r0 > c0).astype(BF16), (r0 < c0).astype(BF16)


def _sb_attn_fwd(qn, kn, qkv, name):
    nrow, d, hd, nh, t = _sb_dims(qn)
    scale = 1.0 / math.sqrt(hd)

    def body(q_ref, k_ref, v_ref, o_ref, rs_ref, r_acc, o_acc):
        i = pl.program_id(1)
        q = q_ref[...]
        strict, after, _ = _sb_masks(t)
        lane = lax.broadcasted_iota(jnp.int32, (t, LANES), 1)
        r_acc[...] = jnp.zeros_like(r_acc)
        o_acc[...] = jnp.zeros_like(o_acc)
        rs_ref[...] = jnp.full(rs_ref.shape, SB_UNVISITED, F32)

        def tile(j, mask):
            start = pl.multiple_of(j * t, t)
            k = k_ref[pl.ds(start, t), :]
            v = v_ref[pl.ds(start, t), :].astype(BF16)
            ls, lk = _sb_scores(q, k, mask, scale)
            r = r_acc[...]
            later, r_next = _sb_running(lk, after, r, True)
            att = jnp.exp2(ls + later)
            if mask is not None:
                att = jnp.where(mask, att, 0.0)
            o_acc[...] += jnp.dot(att.astype(BF16), v, preferred_element_type=F32)
            rs_ref[...] = jnp.where(lane == j, r, rs_ref[...])
            r_acc[...] = r_next

        tile(i, strict)

        def more(carry):
            n, r_max = carry
            return (n < i) & (r_max > SB_SKIP_BELOW)

        def step(carry):
            n, _ = carry
            tile(i - 1 - n, None)
            return n + 1, jnp.max(r_acc[...])

        lax.while_loop(more, step, (jnp.int32(0), jnp.max(r_acc[...])))
        o_ref[...] = o_acc[...].astype(o_ref.dtype)

    return pl.pallas_call(
        body, grid=(nh, nrow // t),
        in_specs=[pl.BlockSpec((t, hd), lambda h, i: (i, h)), pl.BlockSpec((nrow, hd), lambda h, i: (0, h)),
                  pl.BlockSpec((nrow, hd), lambda h, i: (0, 2 * nh + h))],
        out_specs=[pl.BlockSpec((t, hd), lambda h, i: (i, h)), pl.BlockSpec((None, t, LANES), lambda h, i: (h, i, 0))],
        out_shape=[jax.ShapeDtypeStruct((nrow, d), BF16), jax.ShapeDtypeStruct((nh, nrow, LANES), F32)],
        scratch_shapes=[pltpu.VMEM((t, LANES), F32), pltpu.VMEM((t, hd), F32)],
        compiler_params=_cparams(2), name=name)(qn, kn, qkv)


def _sb_attn_bwd(qn, kn, qkv, do, rsave, name):
    nrow, d, hd, nh, t = _sb_dims(qn)
    scale = 1.0 / math.sqrt(hd)

    def body(q_ref, k_ref, v_ref, do_ref, rs_ref, dq_ref, dk_ref, dv_ref, pg_acc):
        i = pl.program_id(1)

        @pl.when(i == 0)
        def _():
            dk_ref[...] = jnp.zeros_like(dk_ref)
            dv_ref[...] = jnp.zeros_like(dv_ref)

        pg_acc[...] = jnp.zeros_like(pg_acc)
        dq_ref[...] = jnp.zeros_like(dq_ref)
        q = q_ref[...]
        dob = do_ref[...].astype(BF16)
        strict, after, before = _sb_masks(t)
        lane = lax.broadcasted_iota(jnp.int32, (t, LANES), 1)

        def tile(j, mask):
            start = pl.multiple_of(j * t, t)
            k = k_ref[pl.ds(start, t), :]
            v = v_ref[pl.ds(start, t), :].astype(BF16)
            ls, lk = _sb_scores(q, k, mask, scale)
            rj = jnp.sum(jnp.where(lane == j, rs_ref[...], 0.0), axis=1, keepdims=True)
            later, _ = _sb_running(lk, after, jnp.broadcast_to(rj, (t, LANES)), True)
            att = jnp.exp2(ls + later)
            if mask is not None:
                att = jnp.where(mask, att, 0.0)
            datt = lax.dot_general(dob, v, (((1,), (1,)), ((), ())), preferred_element_type=F32)
            g = datt * att
            dlk, pg_next = _sb_running(g, before, pg_acc[...], False)
            sig = jnp.exp2(ls)
            dz = g * (1.0 - sig) - dlk * sig
            if mask is not None:
                dz = jnp.where(mask, dz, 0.0)
            dz = (dz * scale).astype(BF16)
            dq_ref[...] += jnp.dot(dz, k, preferred_element_type=F32)
            dk_ref[pl.ds(start, t), :] += lax.dot_general(dz, q, (((0,), (0,)), ((), ())), preferred_element_type=F32)
            dv_ref[pl.ds(start, t), :] += lax.dot_general(att.astype(BF16), dob, (((0,), (0,)), ((), ())),
                                                          preferred_element_type=F32)
            pg_acc[...] = pg_next

        def step(j, carry):
            tile(j, None)
            return carry

        seen = jnp.max(rs_ref[...], axis=0, keepdims=True)
        lane1 = lax.broadcasted_iota(jnp.int32, (1, LANES), 1)
        first = jnp.sum(jnp.where((lane1 < i) & (seen < 0.5 * SB_UNVISITED), 1.0, 0.0)).astype(jnp.int32)
        lax.fori_loop(first, i, step, 0)
        tile(i, strict)

    return pl.pallas_call(
        body, grid=(nh, nrow // t),
        in_specs=[pl.BlockSpec((t, hd), lambda h, i: (i, h)), pl.BlockSpec((nrow, hd), lambda h, i: (0, h)),
                  pl.BlockSpec((nrow, hd), lambda h, i: (0, 2 * nh + h)), pl.BlockSpec((t, hd), lambda h, i: (i, h)),
                  pl.BlockSpec((None, t, LANES), lambda h, i: (h, i, 0))],
        out_specs=[pl.BlockSpec((t, hd), lambda h, i: (i, h)), pl.BlockSpec((nrow, hd), lambda h, i: (0, h)),
                   pl.BlockSpec((nrow, hd), lambda h, i: (0, h))],
        out_shape=[jax.ShapeDtypeStruct((nrow, d), F32)] * 3, scratch_shapes=[pltpu.VMEM((t, LANES), F32)],
        compiler_params=_cparams(2), name=name)(qn, kn, qkv, do, rsave)


def _ssd_chunk_fn(g, nheads_g, q):
    p = M2_HEAD_DIM
    npair = nheads_g // 2

    def f(xp, bp, cp, dtr, hp, dtb, alog, dsk):
        b = _silu(bp).astype(BF16)
        c = _silu(cp).astype(BF16)
        dt = _softplus(dtr + dtb)
        dta = dt * (-jnp.exp(alog))
        ri = lax.broadcasted_iota(jnp.int32, (q, q), 0)
        ci = lax.broadcasted_iota(jnp.int32, (q, q), 1)
        causal = ri >= ci
        tri = causal.astype(F32)
        acol = jnp.dot(tri, dta, precision=HIGHEST, preferred_element_type=F32)
        arow = lax.dot_general(dta, tri, (((0,), (1,)), ((), ())), precision=HIGHEST,
                               preferred_element_type=F32)
        alast = jnp.sum(dta, axis=0, keepdims=True)
        cb = lax.dot_general(c, b, (((1,), (1,)), ((), ())), preferred_element_type=F32)
        lane = lax.broadcasted_iota(jnp.int32, (1, LANES), 1)
        sub = lax.broadcasted_iota(jnp.int32, (LANES, 1), 0)
        lane_half = lane < p
        sub_half = sub < p

        def col(mat, h):
            return jnp.sum(jnp.where(lane == h, mat, 0.0), axis=1, keepdims=True)

        def row(mat, h):
            return jnp.sum(jnp.where(sub == h, mat, 0.0), axis=0, keepdims=True)

        def mix(v0, v1):
            return jnp.where(lane_half, v0, v1)

        ys, hns = [], []
        for jp in range(npair):
            h0 = g * nheads_g + 2 * jp
            h1 = h0 + 1
            x = _silu(xp[jp])
            ac0, ac1 = col(acol, h0), col(acol, h1)
            xdt = x * mix(col(dt, h0), col(dt, h1))
            xdtb = xdt.astype(BF16)
            yd = []
            for h, ac in ((h0, ac0), (h1, ac1)):
                dec = jnp.exp(jnp.where(causal, ac - row(arow, h), -jnp.inf))
                yd.append(jnp.dot((cb * dec).astype(BF16), xdtb, preferred_element_type=F32))
            hpj = hp[jp]
            yo = lax.dot_general(c, hpj.astype(BF16), (((1,), (1,)), ((), ())), preferred_element_type=F32)
            y = mix(yd[0], yd[1]) + yo * jnp.exp(mix(ac0, ac1)) + mix(col(dsk, h0), col(dsk, h1)) * x
            al0, al1 = col(alast, h0), col(alast, h1)
            dout = jnp.exp(mix(al0 - ac0, al1 - ac1))
            st = lax.dot_general((xdt * dout).astype(BF16), b, (((0,), (0,)), ((), ())), preferred_element_type=F32)
            hns.append(hpj * jnp.exp(jnp.where(sub_half, al0, al1)) + st)
            ys.append(y)
        return ys, hns

    return f


def _ssd_dims(pre, dtr):
    nrow = pre.shape[0]
    ng, n, p, q = M2_GROUPS, M2_STATE, M2_HEAD_DIM, M2_CHUNK
    d_inner = pre.shape[1] - 2 * ng * n
    gw = d_inner // ng
    nhg = gw // p
    assert nhg % 2 == 0 and gw % LANES == 0 and n == LANES and ng * nhg <= LANES and dtr.shape[1] == LANES
    return nrow, ng, n, q, d_inner, gw, nhg, nrow // q


def _ssd_fwd(pre, dtr, dtb, alog, dsk, name):
    nrow, ng, n, q, d_inner, gw, nhg, nc = _ssd_dims(pre, dtr)
    boff = d_inner // n
    npair = nhg // 2

    def body(x_ref, b_ref, c_ref, dt_ref, dtb_ref, al_ref, ds_ref, y_ref, hs_ref, state):
        ci, g = pl.program_id(0), pl.program_id(1)

        @pl.when(ci == 0)
        def _():
            state[g] = jnp.zeros((gw, n), F32)

        hs_ref[...] = state[g]
        f = _ssd_chunk_fn(g, nhg, q)
        xp = [x_ref[:, LANES * j:LANES * (j + 1)] for j in range(npair)]
        hp = [state[g, LANES * j:LANES * (j + 1), :] for j in range(npair)]
        ys, hns = f(xp, b_ref[...], c_ref[...], dt_ref[...], hp, dtb_ref[...], al_ref[...], ds_ref[...])
        for j in range(npair):
            y_ref[:, LANES * j:LANES * (j + 1)] = ys[j]
            state[g, LANES * j:LANES * (j + 1), :] = hns[j]

    par = pl.BlockSpec((1, LANES), lambda ci, g: (0, 0))
    return pl.pallas_call(
        body, grid=(nc, ng),
        in_specs=[pl.BlockSpec((q, gw), lambda ci, g: (ci, g)), pl.BlockSpec((q, n), lambda ci, g: (ci, boff + g)),
                  pl.BlockSpec((q, n), lambda ci, g: (ci, boff + ng + g)), pl.BlockSpec((q, LANES), lambda ci, g: (ci, 0)),
                  par, par, par],
        out_specs=[pl.BlockSpec((q, gw), lambda ci, g: (ci, g)),
                   pl.BlockSpec((None, None, gw, n), lambda ci, g: (ci, g, 0, 0))],
        out_shape=[jax.ShapeDtypeStruct((nrow, d_inner), F32), jax.ShapeDtypeStruct((nc, ng, gw, n), F32)],
        scratch_shapes=[pltpu.VMEM((ng, gw, n), F32)], compiler_params=_cparams(2), name=name)(
            pre, pre, pre, dtr, dtb, alog, dsk)


def _ssd_bwd(pre, dtr, hs, dy, dtb, alog, dsk, name):
    nrow, ng, n, q, d_inner, gw, nhg, nc = _ssd_dims(pre, dtr)
    boff = d_inner // n
    npair = nhg // 2

    def body(x_ref, b_ref, c_ref, dt_ref, hs_ref, dy_ref, dtb_ref, al_ref, ds_ref,
             dx_ref, db_ref, dc_ref, ddt_ref, ddtb_ref, dal_ref, dds_ref, dstate):
        ci, g = pl.program_id(0), pl.program_id(1)

        @pl.when(ci == 0)
        def _():
            dstate[g] = jnp.zeros((gw, n), F32)

        @pl.when((ci == 0) & (g == 0))
        def _():
            ddtb_ref[...] = jnp.zeros_like(ddtb_ref)
            dal_ref[...] = jnp.zeros_like(dal_ref)
            dds_ref[...] = jnp.zeros_like(dds_ref)

        @pl.when(g == 0)
        def _():
            ddt_ref[...] = jnp.zeros_like(ddt_ref)

        f = _ssd_chunk_fn(g, nhg, q)
        xp = [x_ref[:, LANES * j:LANES * (j + 1)] for j in range(npair)]
        hp = [hs_ref[LANES * j:LANES * (j + 1), :] for j in range(npair)]
        _, vjp = jax.vjp(f, xp, b_ref[...], c_ref[...], dt_ref[...], hp, dtb_ref[...], al_ref[...], ds_ref[...])
        dys = [dy_ref[:, LANES * j:LANES * (j + 1)] for j in range(npair)]
        dhn = [dstate[g, LANES * j:LANES * (j + 1), :] for j in range(npair)]
        dxp, db, dc, ddt, dhp, ddtb, dal, dds = vjp((dys, dhn))
        for j in range(npair):
            dx_ref[:, LANES * j:LANES * (j + 1)] = dxp[j]
            dstate[g, LANES * j:LANES * (j + 1), :] = dhp[j]
        db_ref[...] = db
        dc_ref[...] = dc
        ddt_ref[...] += ddt
        ddtb_ref[...] += ddtb
        dal_ref[...] += dal
        dds_ref[...] += dds

    par = pl.BlockSpec((1, LANES), lambda ci, g: (0, 0))
    last = nc - 1
    return pl.pallas_call(
        body, grid=(nc, ng),
        in_specs=[pl.BlockSpec((q, gw), lambda ci, g: (last - ci, g)),
                  pl.BlockSpec((q, n), lambda ci, g: (last - ci, boff + g)),
                  pl.BlockSpec((q, n), lambda ci, g: (last - ci, boff + ng + g)),
                  pl.BlockSpec((q, LANES), lambda ci, g: (last - ci, 0)),
                  pl.BlockSpec((None, None, gw, n), lambda ci, g: (last - ci, g, 0, 0)),
                  pl.BlockSpec((q, gw), lambda ci, g: (last - ci, g)), par, par, par],
        out_specs=[pl.BlockSpec((q, gw), lambda ci, g: (last - ci, g)), pl.BlockSpec((q, n), lambda ci, g: (last - ci, g)),
                   pl.BlockSpec((q, n), lambda ci, g: (last - ci, g)), pl.BlockSpec((q, LANES), lambda ci, g: (last - ci, 0)),
                   par, par, par],
        out_shape=[jax.ShapeDtypeStruct((nrow, d_inner), F32), jax.ShapeDtypeStruct((nrow, ng * n), F32),
                   jax.ShapeDtypeStruct((nrow, ng * n), F32), jax.ShapeDtypeStruct((nrow, LANES), F32),
                   jax.ShapeDtypeStruct((1, LANES), F32), jax.ShapeDtypeStruct((1, LANES), F32),
                   jax.ShapeDtypeStruct((1, LANES), F32)],
        scratch_shapes=[pltpu.VMEM((ng, gw, n), F32)], compiler_params=_cparams(2), name=name)(
            pre, pre, pre, dtr, hs, dy, dtb, alog, dsk)


HBM_SPEC = pl.BlockSpec(memory_space=pl.ANY)


def _xy_peers(mx, my):
    return [(1 - mx, my), (mx, 1 - my), (1 - mx, 1 - my)]


def _remote(src, dst, send_sems, recv_sems, k, dev):
    return pltpu.make_async_remote_copy(src_ref=src, dst_ref=dst, send_sem=send_sems.at[k], recv_sem=recv_sems.at[k],
                                        device_id=dev, device_id_type=MESH_ID)


def _comm_call(body, arrays, out_shapes, ncopies, name):
    nin = len(arrays)

    def wrapped(*refs):
        body(refs[:nin], refs[nin:nin + len(out_shapes)], refs[-2], refs[-1])

    return pl.pallas_call(
        wrapped, out_shape=out_shapes, in_specs=[HBM_SPEC] * nin, out_specs=[HBM_SPEC] * len(out_shapes),
        scratch_shapes=[pltpu.SemaphoreType.DMA((ncopies,)), pltpu.SemaphoreType.DMA((ncopies,))], name=name)(*arrays)


def _all_gather_xy(bufs, name):
    npeer = N_XY - 1

    def body(srcs, outs, send_sems, recv_sems):
        mx, my, mc = lax.axis_index("x"), lax.axis_index("y"), lax.axis_index("c")
        me = 2 * mx + my
        sibling = (mx, my, 1 - mc)
        peers = _xy_peers(mx, my)
        sends = []
        for a, (src, out) in enumerate(zip(srcs, outs)):
            for k, (px, py) in enumerate(peers):
                sends.append(_remote(src.at[mc], out.at[me, mc], send_sems, recv_sems, 2 * npeer * a + k, (px, py, mc)))
        for cp in sends:
            cp.start()
        for a, (src, out) in enumerate(zip(srcs, outs)):
            for k, (px, py) in enumerate(peers):
                landed = out.at[2 * px + py, mc]
                _remote(src.at[mc], landed, send_sems, recv_sems, 2 * npeer * a + k, (px, py, mc)).wait_recv()
                cp = _remote(landed, landed, send_sems, recv_sems, 2 * npeer * a + npeer + k, sibling)
                cp.start()
                sends.append(cp)
        for a, out in enumerate(outs):
            for k, (px, py) in enumerate(peers):
                passed = out.at[2 * px + py, 1 - mc]
                _remote(passed, passed, send_sems, recv_sems, 2 * npeer * a + npeer + k, sibling).wait_recv()
        for cp in sends:
            cp.wait_send()

    outs = _comm_call(body, bufs, [jax.ShapeDtypeStruct((N_XY,) + b.shape, b.dtype) for b in bufs],
                      2 * npeer * len(bufs), name)
    return [lax.dynamic_update_index_in_dim(o, b, _my_shard(), 0) for o, b in zip(outs, bufs)]


def _my_shard():
    return 2 * lax.axis_index("x") + lax.axis_index("y")


def _exchange_xy(parts, name):
    npeer = N_XY - 1

    def body(srcs, outs, send_sems, recv_sems):
        mx, my, mc = lax.axis_index("x"), lax.axis_index("y"), lax.axis_index("c")
        me = 2 * mx + my
        peers = _xy_peers(mx, my)
        sends = [_remote(src.at[2 * px + py], out.at[me], send_sems, recv_sems, npeer * a + k, (px, py, mc))
                 for k, (px, py) in enumerate(peers) for a, (src, out) in enumerate(zip(srcs, outs))]
        for cp in sends:
            cp.start()
        for k, (px, py) in enumerate(peers):
            for a, (src, out) in enumerate(zip(srcs, outs)):
                _remote(src.at[me], out.at[2 * px + py], send_sems, recv_sems, npeer * a + k, (px, py, mc)).wait_recv()
        for cp in sends:
            cp.wait_send()

    outs = _comm_call(body, parts, [jax.ShapeDtypeStruct(p.shape, p.dtype) for p in parts], npeer * len(parts), name)
    me = _my_shard()
    return [lax.dynamic_update_index_in_dim(o, lax.dynamic_index_in_dim(p, me, 0, keepdims=False), me, 0)
            for o, p in zip(outs, parts)]


def _pair_split(parts, name):
    def body(srcs, gots, send_sems, recv_sems):
        mx, my, mc = lax.axis_index("x"), lax.axis_index("y"), lax.axis_index("c")
        copies, idx = [], 0
        for src, got in zip(srcs, gots):
            for s in range(src.shape[0]):
                copies.append(_remote(src.at[s, 1 - mc], got.at[s], send_sems, recv_sems, idx, (mx, my, 1 - mc)))
                idx += 1
        for cp in copies:
            cp.start()
        for cp in copies:
            cp.wait()

    gots = _comm_call(body, parts, [jax.ShapeDtypeStruct((p.shape[0],) + p.shape[2:], p.dtype) for p in parts],
                      sum(p.shape[0] for p in parts), name)
    mc = lax.axis_index("c")
    return [lax.dynamic_index_in_dim(p, mc, 1, keepdims=False) for p in parts], gots


def _pair_join(mines, name):
    def body(srcs, outs, send_sems, recv_sems):
        mx, my, mc = lax.axis_index("x"), lax.axis_index("y"), lax.axis_index("c")
        copies = [_remote(src, out, send_sems, recv_sems, a, (mx, my, 1 - mc))
                  for a, (src, out) in enumerate(zip(srcs, outs))]
        for cp in copies:
            cp.start()
        for cp in copies:
            cp.wait()

    others = _comm_call(body, mines, [jax.ShapeDtypeStruct(m.shape, m.dtype) for m in mines], len(mines), name)
    first = lax.axis_index("c") == 0
    return [jnp.where(first, jnp.stack([m, o]), jnp.stack([o, m])) for m, o in zip(mines, others)]


def _all_gather_all(buf, name):
    flips = [(fx, fy, fc) for fx in (0, 1) for fy in (0, 1) for fc in (0, 1) if fx or fy or fc]

    def body(src, out, send_sems, recv_sems):
        mx, my, mc = lax.axis_index("x"), lax.axis_index("y"), lax.axis_index("c")
        me = 4 * mx + 2 * my + mc
        peers = [(1 - mx if fx else mx, 1 - my if fy else my, 1 - mc if fc else mc) for fx, fy, fc in flips]
        sends = [_remote(src, out.at[me], send_sems, recv_sems, k, dev) for k, dev in enumerate(peers)]
        for cp in sends:
            cp.start()
        for k, (px, py, pc) in enumerate(peers):
            _remote(src, out.at[4 * px + 2 * py + pc], send_sems, recv_sems, k, (px, py, pc)).wait_recv()
        for cp in sends:
            cp.wait_send()

    out = pl.pallas_call(
        body, out_shape=jax.ShapeDtypeStruct((N_DEV,) + buf.shape, buf.dtype), in_specs=[HBM_SPEC], out_specs=HBM_SPEC,
        scratch_shapes=[pltpu.SemaphoreType.DMA((N_DEV - 1,)), pltpu.SemaphoreType.DMA((N_DEV - 1,))], name=name)(buf)
    me = 4 * lax.axis_index("x") + 2 * lax.axis_index("y") + lax.axis_index("c")
    return lax.dynamic_update_index_in_dim(out, buf, me, 0)


def _flat_tile(nrow, width, narrays):
    t = nrow
    for cand in (512, 256, 128, 64, 32, 16, 8):
        if nrow % cand == 0:
            t = cand
            if cand * width * 4 * narrays <= ROW_BLOCK_BYTES:
                break
    return t


def _sum_parts(parts, name):
    kparts, nrow, width = parts.shape
    t = _flat_tile(nrow, width, kparts + 1)

    def body(p_ref, o_ref):
        s = p_ref[0].astype(F32)
        for k in range(1, kparts):
            s = s + p_ref[k].astype(F32)
        o_ref[...] = s

    return pl.pallas_call(
        body, grid=(nrow // t,), in_specs=[pl.BlockSpec((kparts, t, width), lambda i: (0, i, 0))],
        out_specs=pl.BlockSpec((t, width), lambda i: (i, 0)), out_shape=jax.ShapeDtypeStruct((nrow, width), F32),
        compiler_params=_cparams(1), name=name)(parts)


def _add2(a, b, name):
    nparts, nrow, width = a.shape
    t = _flat_tile(nrow, width, 3)

    def body(a_ref, b_ref, o_ref):
        o_ref[...] = (a_ref[...].astype(F32) + b_ref[...].astype(F32)).astype(o_ref.dtype)

    spec = pl.BlockSpec((None, t, width), lambda p, i: (p, i, 0))
    return pl.pallas_call(body, grid=(nparts, nrow // t), in_specs=[spec, spec], out_specs=spec,
                          out_shape=jax.ShapeDtypeStruct(a.shape, a.dtype), compiler_params=_cparams(2), name=name)(a, b)


def _adamw(w, m, v, ga, gb, name):
    nrow, width = w.shape
    t = _flat_tile(nrow, width, 9)
    c1 = 1.0 - ADAM_B1 ** ADAM_STEP
    c2 = 1.0 - ADAM_B2 ** ADAM_STEP
    grads = [ga] if gb is None else [ga, gb]
    nout = 3 if gb is None else 4

    def body(w_ref, m_ref, v_ref, *refs):
        d_ref, mo_ref, vo_ref = refs[-3:]
        g = refs[0][...]
        if gb is not None:
            g = g + refs[1][...]
            refs[2][...] = g
        mn = ADAM_B1 * m_ref[...] + (1.0 - ADAM_B1) * g
        vn = ADAM_B2 * v_ref[...] + (1.0 - ADAM_B2) * jnp.square(g)
        mo_ref[...] = mn
        vo_ref[...] = vn
        d_ref[...] = -ADAM_LR * ((mn / c1) / (jnp.sqrt(vn / c2) + ADAM_EPS) + ADAM_WD * w_ref[...])

    spec = pl.BlockSpec((t, width), lambda i: (i, 0))
    return pl.pallas_call(
        body, grid=(nrow // t,), in_specs=[spec] * (3 + len(grads)), out_specs=[spec] * nout,
        out_shape=[jax.ShapeDtypeStruct((nrow, width), F32)] * nout, compiler_params=_cparams(1), name=name)(
            w, m, v, *grads)


def _pack(arrs, dtype, width):
    flat = jnp.concatenate([a.astype(dtype).reshape(-1) for a in arrs])
    quantum = PACK_ROWS * width
    pad = (-flat.shape[0]) % quantum
    if pad:
        flat = jnp.concatenate([flat, jnp.zeros((pad,), dtype)])
    return flat.reshape(-1, width)


def _unpack(buf, shapes):
    flat = buf.reshape(-1)
    out, off = [], 0
    for s in shapes:
        size = math.prod(s)
        out.append(flat[off:off + size].reshape(s))
        off += size
    return out


def _qk_norm_fwd(qkv, w, off, nh, tag):
    f = _rms_f(RMS_EPS)
    return _rowwise(lambda xv, wv: (f(xv, wv),), [(qkv, SB_HEAD_DIM, off)], [(w, False)],
                    [('row', nh * SB_HEAD_DIM, SB_HEAD_DIM, BF16)], ncb=nh, name=tag)[0]


def _qk_norm_bwd(qkv, w, off, dn, nh, tag):
    f = _rms_f(RMS_EPS)

    def fn(xv, dv, wv):
        _, vjp = jax.vjp(f, xv, wv)
        return vjp(dv)

    return _rowwise(fn, [(qkv, SB_HEAD_DIM, off), (dn, SB_HEAD_DIM, 0)], [(w, False)],
                    [('row', nh * SB_HEAD_DIM, SB_HEAD_DIM, F32), ('acc', (1, SB_HEAD_DIM), False)], ncb=nh, name=tag)


def _sb_forward(h, xin, wt, tag):
    d = xin.shape[1]
    nh = d // SB_HEAD_DIM
    qkv = _mm(h, wt['w_qkv'], "nn", name=tag + "_qkv")
    qn = _qk_norm_fwd(qkv, wt['q_norm_w'], 0, nh, tag + "_qnorm")
    kn = _qk_norm_fwd(qkv, wt['k_norm_w'], nh, nh, tag + "_knorm")
    o, rsave = _sb_attn_fwd(qn, kn, qkv, tag + "_attn")
    xm = _mm(o, wt['w_o'], "nn", extras=(xin,), epilogue=_ep_add, name=tag + "_out")
    return xm, dict(qkv=qkv, qn=qn, kn=kn, o=o, rsave=rsave)


def _sb_backward(h, dx, dxb, wt, sv, tag):
    nh = dx.shape[1] // SB_HEAD_DIM
    do = _mm(dxb, wt['w_o'], "nt", name=tag + "_do")
    g_wo = _mm(sv['o'], dxb, "tn", out_dtypes=(BF16,), name=tag + "_dwo")
    dqn, dkn, dv = _sb_attn_bwd(sv['qn'], sv['kn'], sv['qkv'], do, sv['rsave'], tag + "_attn_bwd")
    dq, g_qw = _qk_norm_bwd(sv['qkv'], wt['q_norm_w'], 0, dqn, nh, tag + "_qnorm_bwd")
    dk, g_kw = _qk_norm_bwd(sv['qkv'], wt['k_norm_w'], nh, dkn, nh, tag + "_knorm_bwd")
    dqkv = jnp.concatenate([dq, dk, dv], axis=1)
    g_wqkv = _mm(h, dqkv, "tn", out_dtypes=(BF16,), out_col_shards=N_XY, name=tag + "_dwqkv")
    dh = _mm(dqkv, wt['w_qkv'], "nt", name=tag + "_dh")
    return dh, dict(w_qkv=g_wqkv, w_o=g_wo, q_norm_w=g_qw, k_norm_w=g_kw)


def _ln_silu_f(x, w, b):
    mu = jnp.mean(x, axis=-1, keepdims=True)
    xc = x - mu
    return _silu(xc * lax.rsqrt(jnp.mean(xc * xc, axis=-1, keepdims=True) + LN_EPS) * w + b)


def _glu_bwd_fn(val, gate, dg):
    sg = 1.0 / (1.0 + jnp.exp(-gate))
    return (jnp.concatenate([dg * sg, dg * val * sg * (1.0 - sg)], axis=1),)


def _cf_forward(h, xin, wt, tag):
    d = xin.shape[1]
    u = _mm(h, wt['w_in'], "nn", extras=(wt['b_in'],), epilogue=_ep_add, name=tag + "_in")
    gl = _rowwise(lambda val, gate: (val / (1.0 + jnp.exp(-gate)),), [(u, d, 0), (u, d, 1)], [],
                  [('row', d, d, F32)], name=tag + "_glu")[0]
    cv = _dwconv_fwd(gl, wt['dw_w'], wt['dw_b'], tag + "_conv")
    s = _rowwise(lambda x, w, b: (_ln_silu_f(x, w, b),), [(cv, d, 0)], [(wt['ln_w'], False), (wt['ln_b'], False)],
                 [('row', d, d, BF16)], name=tag + "_ln")[0]
    xm = _mm(s, wt['w_out'], "nn", extras=(wt['b_out'], xin), epilogue=_ep_bias_add, name=tag + "_out")
    return xm, dict(u=u, gl=gl, cv=cv, s=s)


def _cf_backward(h, dx, dxb, wt, sv, tag):
    d = dx.shape[1]
    ds = _mm(dxb, wt['w_out'], "nt", name=tag + "_ds")
    g_wout = _mm(sv['s'], dxb, "tn", out_dtypes=(BF16,), name=tag + "_dwout")
    g_bout = _colsum(dx, tag + "_dbout")

    def ln_bwd(x, dsv, w, b):
        _, vjp = jax.vjp(_ln_silu_f, x, w, b)
        return vjp(dsv)

    dcv, g_lnw, g_lnb = _rowwise(ln_bwd, [(sv['cv'], d, 0), (ds, d, 0)], [(wt['ln_w'], False), (wt['ln_b'], False)],
                                 [('row', d, d, F32), ('acc', (1, d), False), ('acc', (1, d), False)],
                                 name=tag + "_ln_bwd")
    dgl, g_dww, g_dwb = _dwconv_bwd(sv['gl'], dcv, wt['dw_w'], tag + "_conv_bwd")
    du = _rowwise(_glu_bwd_fn, [(sv['u'], d, 0), (sv['u'], d, 1), (dgl, d, 0)], [], [('row', 2 * d, 2 * d, F32)],
                  name=tag + "_glu_bwd")[0]
    g_bin = _colsum(du, tag + "_dbin")
    g_win = _mm(h, du, "tn", out_dtypes=(BF16,), out_col_shards=N_XY, name=tag + "_dwin")
    dh = _mm(du, wt['w_in'], "nt", name=tag + "_dh")
    return dh, dict(w_in=g_win, b_in=g_bin, dw_w=g_dww, dw_b=g_dwb, ln_w=g_lnw, ln_b=g_lnb, w_out=g_wout, b_out=g_bout)


def _gated_norm_f(y, z, w):
    t = y * _silu(z)
    return t * lax.rsqrt(jnp.mean(t * t, axis=-1, keepdims=True) + M2_NORM_EPS) * w


def _m2_forward(h, xin, wt, tag):
    z = _mm(h, wt['w_z'], "nn", name=tag + "_z")
    xbc = _mm(h, wt['w_xbc'], "nn", name=tag + "_xbc")
    dtr = _mm(h, wt['w_dt'], "nn", name=tag + "_dt")
    pre = _dwconv_fwd(xbc, wt['conv_w'], wt['conv_b'], tag + "_conv")
    y, hs = _ssd_fwd(pre, dtr, wt['dt_bias'], wt['a_log'], wt['d'], tag + "_ssd")
    d_inner = y.shape[1]
    gw = d_inner // M2_GROUPS
    yn = _rowwise(lambda yv, zv, w: (_gated_norm_f(yv, zv, w),), [(y, gw, 0), (z, gw, 0)], [(wt['norm_w'], True)],
                  [('row', d_inner, gw, BF16)], ncb=M2_GROUPS, name=tag + "_gnorm")[0]
    xm = _mm(yn, wt['w_out'], "nn", extras=(xin,), epilogue=_ep_add, name=tag + "_out")
    return xm, dict(z=z, xbc=xbc, dtr=dtr, pre=pre, y=y, hs=hs, yn=yn)


def _m2_backward(h, dx, dxb, wt, sv, tag):
    dyn = _mm(dxb, wt['w_out'], "nt", name=tag + "_dyn")
    g_wout = _mm(sv['yn'], dxb, "tn", out_dtypes=(BF16,), name=tag + "_dwout")
    d_inner = sv['y'].shape[1]
    gw = d_inner // M2_GROUPS

    def gn_bwd(yv, zv, dv, w):
        _, vjp = jax.vjp(_gated_norm_f, yv, zv, w)
        return vjp(dv)

    dy, dz, g_nw = _rowwise(gn_bwd, [(sv['y'], gw, 0), (sv['z'], gw, 0), (dyn, gw, 0)], [(wt['norm_w'], True)],
                            [('row', d_inner, gw, F32), ('row', d_inner, gw, F32), ('acc', (1, d_inner), True)],
                            ncb=M2_GROUPS, name=tag + "_gnorm_bwd")
    dxp, db, dc, ddt, g_dtb, g_alog, g_d = _ssd_bwd(sv['pre'], sv['dtr'], sv['hs'], dy, wt['dt_bias'], wt['a_log'],
                                                   wt['d'], tag + "_ssd_bwd")
    dpre = jnp.concatenate([dxp, db, dc], axis=1)
    dxbc, g_cw, g_cb = _dwconv_bwd(sv['xbc'], dpre, wt['conv_w'], tag + "_conv_bwd")
    g_wz = _mm(h, dz, "tn", name=tag + "_dwz")
    g_wxbc = _mm(h, dxbc, "tn", name=tag + "_dwxbc")
    g_wdt = _mm(h, ddt, "tn", name=tag + "_dwdt")
    dh = _mm(dz, wt['w_z'], "nt", name=tag + "_dh_z")
    dh = _mm(dxbc, wt['w_xbc'], "nt", extras=(dh,), epilogue=_ep_add, name=tag + "_dh_xbc")
    dh = _mm(ddt, wt['w_dt'], "nt", extras=(dh,), epilogue=_ep_add, name=tag + "_dh_dt")
    return dh, dict(w_z=g_wz, w_xbc=g_wxbc, w_dt=g_wdt, conv_w=g_cw, conv_b=g_cb, dt_bias=g_dtb, a_log=g_alog, d=g_d,
                    norm_w=g_nw, w_out=g_wout)


def _pad_lanes(v):
    return jnp.pad(v.reshape(1, -1), ((0, 0), (0, LANES - v.shape[0])))


def kernel(x, norm_mix_w, norm_mlp_w, sb_w_qkv, sb_q_norm_w, sb_k_norm_w, sb_w_o, cf_w_in, cf_b_in, cf_dw_w, cf_dw_b, cf_ln_w, cf_ln_b, cf_w_out, cf_b_out, m2_w_in, m2_conv_w, m2_conv_b, m2_dt_bias, m2_a_log, m2_d, m2_norm_w, m2_w_out, mlp_w_up, mlp_w_down, loss_target, m_norm_mix_w, m_norm_mlp_w, m_sb_w_qkv, m_sb_q_norm_w, m_sb_k_norm_w, m_sb_w_o, m_cf_w_in, m_cf_b_in, m_cf_dw_w, m_cf_dw_b, m_cf_ln_w, m_cf_ln_b, m_cf_w_out, m_cf_b_out, m_m2_w_in, m_m2_conv_w, m_m2_conv_b, m_m2_dt_bias, m_m2_a_log, m_m2_d, m_m2_norm_w, m_m2_w_out, m_mlp_w_up, m_mlp_w_down, v_norm_mix_w, v_norm_mlp_w, v_sb_w_qkv, v_sb_q_norm_w, v_sb_k_norm_w, v_sb_w_o, v_cf_w_in, v_cf_b_in, v_cf_dw_w, v_cf_dw_b, v_cf_ln_w, v_cf_ln_b, v_cf_w_out, v_cf_b_out, v_m2_w_in, v_m2_conv_w, v_m2_conv_b, v_m2_dt_bias, v_m2_a_log, v_m2_d, v_m2_norm_w, v_m2_w_out, v_mlp_w_up, v_mlp_w_down):
    given = dict(locals())
    xl = x[0]
    tgt = loss_target[0]
    d = xl.shape[1]
    depth = norm_mix_w.shape[0]
    bulk_names, small_names = list(BULK), list(SMALL)

    def halves(buf):
        return buf.reshape(2, -1, buf.shape[-1])

    small_pack = halves(_pack([given[n] for n in small_names], F32, PACK_W))
    gath = _all_gather_xy([halves(given[n].astype(BF16)) for n in bulk_names] + [small_pack], "comm_gather")
    gathered = {n: g.reshape((N_XY,) + given[n].shape) for n, g in zip(bulk_names, gath)}
    kinds = {n: "rows" if BULK[n] == 1 else "cols" for n in bulk_names}
    full = {}
    pieces = [_unpack(gath[-1][s], [given[n].shape for n in small_names]) for s in range(N_XY)]
    for idx, n in enumerate(small_names):
        full[n] = jnp.concatenate([pieces[s][idx] for s in range(N_XY)], axis=SMALL[n])

    def wview(n, layer):
        return WView(gathered[n], layer, kinds[n])

    def row(v):
        return v.reshape(1, -1)

    d_inner = N_XY * m2_w_out.shape[1]
    conv_dim = full['m2_conv_w'].shape[2]
    nheads = m2_dt_bias.shape[1]

    def layer_weights(i):
        kind, j = i % 3, i // 3
        if kind == 0:
            return dict(w_qkv=wview('sb_w_qkv', j), w_o=wview('sb_w_o', j), q_norm_w=row(sb_q_norm_w[j]),
                        k_norm_w=row(sb_k_norm_w[j]))
        if kind == 1:
            return dict(w_in=wview('cf_w_in', j), b_in=row(cf_b_in[j]), dw_w=full['cf_dw_w'][j], dw_b=row(cf_dw_b[j]),
                        ln_w=row(cf_ln_w[j]), ln_b=row(cf_ln_b[j]), w_out=wview('cf_w_out', j), b_out=row(cf_b_out[j]))
        shards = gathered['m2_w_in'][:, j]
        w_in = jnp.moveaxis(shards, 0, 1).reshape(shards.shape[1], -1)
        w_dt = jnp.pad(w_in[:, d_inner + conv_dim:], ((0, 0), (0, LANES - nheads)))
        return dict(w_z=w_in[:, :d_inner], w_xbc=w_in[:, d_inner:d_inner + conv_dim], w_dt=w_dt,
                    conv_w=full['m2_conv_w'][j], conv_b=row(full['m2_conv_b'][j]), dt_bias=_pad_lanes(m2_dt_bias[j]),
                    a_log=_pad_lanes(m2_a_log[j]), d=_pad_lanes(m2_d[j]), norm_w=row(full['m2_norm_w'][j]),
                    w_out=wview('m2_w_out', j))

    fwd = (_sb_forward, _cf_forward, _m2_forward)
    bwd = (_sb_backward, _cf_backward, _m2_backward)

    saved = []
    xc = xl
    for i in range(depth):
        wt = layer_weights(i)
        h = _rms_fwd(xc, row(norm_mix_w[i]), f"l{i}_norm_mix")
        xm, sv = fwd[i % 3](h, xc, wt, f"l{i}_mix")
        h2 = _rms_fwd(xm, row(norm_mlp_w[i]), f"l{i}_norm_mlp")
        u, a = _mm(h2, wview('mlp_w_up', i), "nn", epilogue=_ep_relu2, out_dtypes=(F32, BF16), name=f"l{i}_up")
        xn = _mm(a, wview('mlp_w_down', i), "nn", extras=(xm,), epilogue=_ep_add, name=f"l{i}_down")
        saved.append(dict(wt=wt, x_in=xc, h=h, mix=sv, x_mid=xm, h2=h2, u=u, a=a))
        xc = xn
    dx, dxb, loss_acc = _loss(xc, tgt, "loss")
    loss = lax.psum(loss_acc[0, 0], ("x", "y", "c"))

    grads = {n: [None] * given[n].shape[0] for n in W_NAMES}
    for i in reversed(range(depth)):
        sv = saved[i]
        kind, j = i % 3, i // 3
        du = _mm(dxb, wview('mlp_w_down', i), "nt", extras=(sv['u'],), epilogue=_ep_relu2_bwd, out_dtypes=(BF16,),
                 name=f"l{i}_du")
        grads['mlp_w_down'][i] = _mm(sv['a'], dxb, "tn", out_dtypes=(BF16,), name=f"l{i}_dwdown")
        grads['mlp_w_up'][i] = _mm(sv['h2'], du, "tn", out_dtypes=(BF16,), out_col_shards=N_XY, name=f"l{i}_dwup")
        dh2 = _mm(du, wview('mlp_w_up', i), "nt", name=f"l{i}_dh2")
        dxm, dxmb, g_n2 = _rms_bwd(sv['x_mid'], row(norm_mlp_w[i]), dh2, dx, f"l{i}_norm_mlp_bwd")
        grads['norm_mlp_w'][i] = g_n2[0]
        dh, gw = bwd[kind](sv['h'], dxm, dxmb, sv['wt'], sv['mix'], f"l{i}_mix")
        dx, dxb, g_n1 = _rms_bwd(sv['x_in'], row(norm_mix_w[i]), dh, dxm, f"l{i}_norm_mix_bwd")
        grads['norm_mix_w'][i] = g_n1[0]
        if kind == 0:
            grads['sb_w_qkv'][j], grads['sb_w_o'][j] = gw['w_qkv'], gw['w_o']
            grads['sb_q_norm_w'][j], grads['sb_k_norm_w'][j] = gw['q_norm_w'][0], gw['k_norm_w'][0]
        elif kind == 1:
            for n in ('w_in', 'dw_w', 'w_out'):
                grads['cf_' + n][j] = gw[n]
            for n in ('b_in', 'dw_b', 'ln_w', 'ln_b', 'b_out'):
                grads['cf_' + n][j] = gw[n][0]
        else:
            grads['m2_w_in'][j] = jnp.concatenate([gw['w_z'], gw['w_xbc'], gw['w_dt'][:, :nheads]], axis=1)
            grads['m2_conv_w'][j], grads['m2_w_out'][j] = gw['conv_w'], gw['w_out']
            grads['m2_conv_b'][j], grads['m2_norm_w'][j] = gw['conv_b'][0], gw['norm_w'][0]
            for n in ('dt_bias', 'a_log', 'd'):
                grads['m2_' + n][j] = gw[n][0, :nheads]
    grad_x = dx[None]
    gfull = {n: jnp.stack(grads[n]) for n in small_names + REPL}

    def shard_major(n, g):
        _, r, c = given[n].shape
        if n == 'm2_w_in':
            return jnp.moveaxis(g.reshape(r, N_XY, c), 1, 0).astype(BF16)
        return g.reshape(N_XY, r, c)

    parts = [jnp.stack([shard_major(n, g) for g in grads[n]], axis=1) for n in bulk_names]
    parts = [p.reshape(N_XY, 2, -1, p.shape[-1]) for p in parts]
    split = {n: jnp.split(gfull[n], N_XY, axis=SMALL[n]) for n in small_names}
    parts.append(jnp.stack([halves(_pack([split[n][s] for n in small_names], F32, PACK_W)) for s in range(N_XY)]))
    tags = bulk_names + ["small"]
    kept, got = _pair_split(parts, "comm_split")
    pair = [_add2(k, g, f"sum_pair_{t}") for k, g, t in zip(kept, got, tags)]
    mine = [_sum_parts(r, f"sum_{t}") for r, t in zip(_exchange_xy(pair, "comm_exchange"), tags)]
    gsum = _pair_join(mine, "comm_join")
    outs = {}
    for n, g in list(zip(bulk_names, gsum)) + list(zip(small_names, _unpack(gsum[-1], [given[n].shape for n in small_names]))):
        shape = given[n].shape
        res = _adamw(*[given[p + n].reshape(-1, shape[-1]) for p in ("", "m_", "v_")], g.reshape(-1, shape[-1]),
                     None, f"adamw_{n}")
        outs["grad", n] = g.reshape(shape)
        for kind, arr in zip(("delta", "new_m", "new_v"), res):
            outs[kind, n] = arr.reshape(shape)
    rparts = _all_gather_all(_pack([gfull[n] for n in REPL], F32, LANES), "comm_gather_repl")
    nsplit = N_DEV // 2
    res = _adamw(*[_pack([given[p + n] for n in REPL], F32, LANES) for p in ("", "m_", "v_")],
                 _sum_parts(rparts[:nsplit], "sum_repl_a"), _sum_parts(rparts[nsplit:], "sum_repl_b"), "adamw_repl")
    shapes = [given[n].shape for n in REPL]
    for kind, buf in zip(("grad", "delta", "new_m", "new_v"), res):
        for n, arr in zip(REPL, _unpack(buf, shapes)):
            outs[kind, n] = arr

    return (loss, grad_x, *[outs[kind, n] for kind in ("grad", "delta", "new_m", "new_v") for n in W_NAMES])
```

```python
import math
from typing import NamedTuple

import jax
import jax.numpy as jnp
from jax import lax
from jax.experimental import pallas as pl
from jax.experimental.pallas import tpu as pltpu

F32 = jnp.float32
BF16 = jnp.bfloat16
MESH_ID = pl.DeviceIdType.MESH
HIGHEST = lax.Precision.HIGHEST

SB_HEAD_DIM = 128
M2_HEAD_DIM = 64
M2_STATE = 128
M2_GROUPS = 8
M2_CHUNK = 128
RMS_EPS = 1e-6
LN_EPS = 1e-5
M2_NORM_EPS = 1e-5
ADAM_LR = 0.001
ADAM_B1 = 0.9
ADAM_B2 = 0.999
ADAM_EPS = 1e-08
ADAM_WD = 0.01
ADAM_STEP = 10

N_XY = 4
N_DEV = 8

V7X_VMEM_BYTES = 64 * 2**20
VMEM_LIMIT = (V7X_VMEM_BYTES * 3) // 4
LANES = 128
ROW_BLOCK_BYTES = 6 * 2**20
MM_VMEM_BUDGET = (VMEM_LIMIT * 3) // 4
PACK_W = 1024
PACK_ROWS = 1024

W_NAMES = ['norm_mix_w', 'norm_mlp_w', 'sb_w_qkv', 'sb_q_norm_w', 'sb_k_norm_w', 'sb_w_o', 'cf_w_in', 'cf_b_in',
           'cf_dw_w', 'cf_dw_b', 'cf_ln_w', 'cf_ln_b', 'cf_w_out', 'cf_b_out', 'm2_w_in', 'm2_conv_w', 'm2_conv_b',
           'm2_dt_bias', 'm2_a_log', 'm2_d', 'm2_norm_w', 'm2_w_out', 'mlp_w_up', 'mlp_w_down']
BULK = {'sb_w_qkv': 2, 'sb_w_o': 1, 'cf_w_in': 2, 'cf_w_out': 1, 'm2_w_in': 2, 'm2_w_out': 1, 'mlp_w_up': 2,
        'mlp_w_down': 1}
SMALL = {'cf_dw_w': 2, 'm2_conv_w': 2, 'm2_conv_b': 1, 'm2_norm_w': 1}
REPL = ['norm_mix_w', 'norm_mlp_w', 'sb_q_norm_w', 'sb_k_norm_w', 'cf_b_in', 'cf_dw_b', 'cf_ln_w', 'cf_ln_b',
        'cf_b_out', 'm2_dt_bias', 'm2_a_log', 'm2_d']


def _tile(n, prefs):
    for p in prefs:
        if n % p == 0:
            return p
    return n


def _cparams(n):
    return pltpu.CompilerParams(dimension_semantics=("arbitrary",) * n, vmem_limit_bytes=VMEM_LIMIT)


def _silu(v):
    return v / (1.0 + jnp.exp(-v))


def _softplus(v):
    return jnp.maximum(v, 0.0) + jnp.log(1.0 + jnp.exp(-jnp.abs(v)))


class WView(NamedTuple):
    arr: jax.Array
    layer: int
    kind: str


def _mm(a, b, mode, *, extras=(), epilogue=None, out_dtypes=(F32,), out_col_shards=1, name):
    if mode == "tn":
        kdim, m = a.shape
    else:
        m, kdim = a.shape
    view = b if isinstance(b, WView) else None
    if view is not None:
        _, _, srows, scols = view.arr.shape
        brows, bcols = (N_XY * srows, scols) if view.kind == "rows" else (srows, N_XY * scols)
        b_arr, b_item = view.arr, view.arr.dtype.itemsize
    else:
        brows, bcols = b.shape
        b_arr, b_item = b, b.dtype.itemsize
    n = brows if mode == "nt" else bcols
    assert kdim == (bcols if mode == "nt" else brows)
    n_unit, k_unit = n // out_col_shards, kdim
    if view is not None:
        if (mode == "nn") == (view.kind == "cols"):
            n_unit = n // N_XY
        else:
            k_unit = kdim // N_XY
    tm = _tile(m, (1024, 512, 256, 128))
    tn = _tile(n_unit, (1024, 768, 512, 256, 128))
    acc_in_out = out_dtypes[0] == F32 and out_col_shards == 1
    fixed = sum(tm * tn * jnp.dtype(dt).itemsize for dt in out_dtypes)
    fixed += sum((1 if e.shape[0] == 1 else tm) * tn * e.dtype.itemsize for e in extras)

    def vmem_bytes(t):
        ab = tm * t * a.dtype.itemsize + t * tn * b_item
        casts = (tm * t * 2 if a.dtype != BF16 else 0) + (t * tn * 2 if b_item != 2 else 0)
        return 2 * (ab + fixed) + (0 if acc_in_out else tm * tn * 4) + tm * tn * 4 + casts

    tk = k_unit
    for cand in (2048, 1536, 1024, 768, 512, 256, 128):
        if k_unit % cand == 0:
            tk = cand
            if vmem_bytes(cand) <= MM_VMEM_BUDGET:
                break
    nk = kdim // tk
    dims = {"nn": ((1,), (0,)), "nt": ((1,), (1,)), "tn": ((0,), (0,))}[mode]
    if mode == "tn":
        a_spec = pl.BlockSpec((tk, tm), lambda i, j, k: (k, i))
    else:
        a_spec = pl.BlockSpec((tm, tk), lambda i, j, k: (i, k))
    if view is None:
        if mode == "nt":
            b_spec = pl.BlockSpec((tn, tk), lambda i, j, k: (j, k))
        else:
            b_spec = pl.BlockSpec((tk, tn), lambda i, j, k: (k, j))
    else:
        lay = view.layer
        rc = (lambda j, k: (j, k)) if mode == "nt" else (lambda j, k: (k, j))
        blk = (tn, tk) if mode == "nt" else (tk, tn)
        if view.kind == "rows":
            per = srows // blk[0]
            b_spec = pl.BlockSpec((None, None) + blk,
                                  lambda i, j, k: (rc(j, k)[0] // per, lay, rc(j, k)[0] % per, rc(j, k)[1]))
        else:
            per = scols // blk[1]
            b_spec = pl.BlockSpec((None, None) + blk,
                                  lambda i, j, k: (rc(j, k)[1] // per, lay, rc(j, k)[0], rc(j, k)[1] % per))
    ex_specs = [pl.BlockSpec((1, tn), lambda i, j, k: (0, j)) if e.shape[0] == 1
                else pl.BlockSpec((tm, tn), lambda i, j, k: (i, j)) for e in extras]
    n_ex, n_out = len(extras), len(out_dtypes)

    def body(*refs):
        a_ref, b_ref = refs[:2]
        ex = refs[2:2 + n_ex]
        outs = refs[2 + n_ex:2 + n_ex + n_out]
        acc = outs[0] if acc_in_out else refs[-1]
        k = pl.program_id(2)
        prod = lax.dot_general(a_ref[...].astype(BF16), b_ref[...].astype(BF16), (dims, ((), ())),
                               preferred_element_type=F32)

        def finish(res):
            vals = epilogue(res, *[e[...] for e in ex]) if epilogue is not None else (res,)
            for o, v in zip(outs, vals):
                o[...] = v.astype(o.dtype)

        if nk == 1:
            finish(prod)
        else:
            @pl.when(k == 0)
            def _():
                acc[...] = prod

            @pl.when((k > 0) & (k < nk - 1))
            def _():
                acc[...] += prod

            @pl.when(k == nk - 1)
            def _():
                finish(acc[...] + prod)

    scratch = [] if (acc_in_out or nk == 1) else [pltpu.VMEM((tm, tn), F32)]
    if out_col_shards == 1:
        out_specs = [pl.BlockSpec((tm, tn), lambda i, j, k: (i, j))] * n_out
        out_shape = [jax.ShapeDtypeStruct((m, n), d) for d in out_dtypes]
    else:
        per_out = n_unit // tn
        out_specs = [pl.BlockSpec((None, tm, tn), lambda i, j, k: (j // per_out, i, j % per_out))]
        out_shape = [jax.ShapeDtypeStruct((out_col_shards, m, n_unit), out_dtypes[0])]
    outs = pl.pallas_call(
        body, grid=(m // tm, n // tn, nk), in_specs=[a_spec, b_spec, *ex_specs], out_specs=out_specs,
        out_shape=out_shape, scratch_shapes=scratch, compiler_params=_cparams(3), name=name)(a, b_arr, *extras)
    return outs[0] if n_out == 1 else outs


def _ep_add(acc, r):
    return (acc + r,)


def _ep_bias_add(acc, b, r):
    return (acc + b + r,)


def _ep_relu2(acc):
    return acc, jnp.square(jnp.maximum(acc, 0.0))


def _ep_relu2_bwd(acc, u):
    return (acc * (2.0 * jnp.maximum(u, 0.0)),)


def _rowwise(fn, rows, consts, outs, *, ncb=1, name):
    nrow = rows[0][0].shape[0]
    row_bytes = sum(w * arr.dtype.itemsize for arr, w, _ in rows)
    row_bytes += sum(o[2] * jnp.dtype(o[3]).itemsize for o in outs if o[0] == 'row')
    t = nrow
    for cand in (1024, 512, 256, 128, 64, 32, 16):
        if nrow % cand == 0:
            t = cand
            if cand * row_bytes <= ROW_BLOCK_BYTES:
                break
    in_specs = [pl.BlockSpec((t, w), (lambda j, i, off=off: (i, off + j))) for _, w, off in rows]
    for arr, per_col in consts:
        r, wc = arr.shape
        if per_col:
            in_specs.append(pl.BlockSpec((r, wc // ncb), lambda j, i: (0, j)))
        else:
            in_specs.append(pl.BlockSpec((r, wc), lambda j, i: (0, 0)))
    out_specs, out_shape = [], []
    for o in outs:
        if o[0] == 'row':
            out_specs.append(pl.BlockSpec((t, o[2]), lambda j, i: (i, j)))
            out_shape.append(jax.ShapeDtypeStruct((nrow, o[1]), o[3]))
        else:
            r, wt = o[1]
            if o[2]:
                out_specs.append(pl.BlockSpec((r, wt // ncb), lambda j, i: (0, j)))
            else:
                out_specs.append(pl.BlockSpec((r, wt), lambda j, i: (0, 0)))
            out_shape.append(jax.ShapeDtypeStruct((r, wt), F32))
    nin = len(rows) + len(consts)

    def body(*refs):
        j, i = pl.program_id(0), pl.program_id(1)
        vals = fn(*[r[...] for r in refs[:nin]])

        def store(spec, ref, v):
            if spec[0] == 'row':
                ref[...] = v.astype(ref.dtype)
            else:
                first = (i == 0) if spec[2] else ((i == 0) & (j == 0))

                @pl.when(first)
                def _():
                    ref[...] = jnp.zeros_like(ref)

                ref[...] += v

        for spec, ref, v in zip(outs, refs[nin:], vals):
            store(spec, ref, v)

    res = pl.pallas_call(body, grid=(ncb, nrow // t), in_specs=in_specs, out_specs=out_specs, out_shape=out_shape,
                         compiler_params=_cparams(2), name=name)(*[r[0] for r in rows], *[c[0] for c in consts])
    return res


def _rms_f(eps):
    def f(x, w):
        return x * lax.rsqrt(jnp.mean(x * x, axis=-1, keepdims=True) + eps) * w
    return f


def _rms_fwd(x, w, name):
    d = x.shape[1]
    f = _rms_f(RMS_EPS)
    return _rowwise(lambda xv, wv: (f(xv, wv),), [(x, d, 0)], [(w, False)], [('row', d, d, BF16)], name=name)[0]


def _rms_bwd(x, w, dh, dres, name):
    d = x.shape[1]
    f = _rms_f(RMS_EPS)

    def fn(xv, dhv, drv, wv):
        _, vjp = jax.vjp(f, xv, wv)
        dx, dw = vjp(dhv)
        return dx + drv, dx + drv, dw

    return _rowwise(fn, [(x, d, 0), (dh, d, 0), (dres, d, 0)], [(w, False)],
                    [('row', d, d, F32), ('row', d, d, BF16), ('acc', (1, d), False)], name=name)


def _colsum(x, name):
    w = x.shape[1]
    return _rowwise(lambda v: (jnp.sum(v, axis=0, keepdims=True),), [(x, w, 0)], [], [('acc', (1, w), False)],
                    name=name)[0]


def _loss(y, tgt, name):
    d = y.shape[1]

    def fn(yv, tv):
        e = yv - tv
        s = jnp.sum(jnp.sum(e * e, axis=1, keepdims=True), axis=0, keepdims=True) * (0.5 / d)
        return e * (1.0 / d), e * (1.0 / d), s + jnp.zeros((1, LANES), F32)

    return _rowwise(fn, [(y, d, 0), (tgt, d, 0)], [],
                    [('row', d, d, F32), ('row', d, d, BF16), ('acc', (1, LANES), False)], name=name)


def _conv_tiles(x, w):
    nrow, ch = x.shape
    kw = w.shape[0]
    halo = 8 * ((kw - 1 + 7) // 8)
    few_taps = kw <= 8
    t = _tile(nrow, (512, 256, 128, 64, 32) if few_taps else (256, 128, 64, 32))
    tc = _tile(ch, (1024, 512, 256, 128) if few_taps else (512, 256, 128))
    assert t % halo == 0 and nrow % t == 0
    return nrow, ch, kw, halo, t, tc


CONV_SUB = 32


def _shifted_windows(buf, shifts, t, base):
    out = {}
    for b in range(8):
        group = [s for s in shifts if s % 8 == b]
        if group:
            rows = buf[base + b:base + max(group) + t, :]
            for s in group:
                out[s] = rows[s - b:s - b + t]
    return out


def _fold8(v):
    out = v[0:8]
    for r in range(8, v.shape[0], 8):
        out = out + v[r:r + 8]
    return out


def _dwconv_fwd(x, w, b, name):
    nrow, ch, kw, halo, t, tc = _conv_tiles(x, w)
    per = t // halo
    sub = min(CONV_SUB, t)

    def body(x_ref, halo_ref, w_ref, b_ref, y_ref, buf):
        i = pl.program_id(1)
        buf[0:halo, :] = jnp.where(i == 0, 0.0, halo_ref[...])
        buf[halo:halo + t, :] = x_ref[...]
        shifts = [halo - (kw - 1) + k for k in range(kw)]
        for r0 in range(0, t, sub):
            win = _shifted_windows(buf, shifts, sub, r0)
            acc = jnp.zeros((sub, tc), F32) + b_ref[...]
            for k in range(kw):
                acc = acc + w_ref[k:k + 1, :] * win[shifts[k]]
            y_ref[r0:r0 + sub, :] = acc

    return pl.pallas_call(
        body, grid=(ch // tc, nrow // t),
        in_specs=[pl.BlockSpec((t, tc), lambda j, i: (i, j)),
                  pl.BlockSpec((halo, tc), lambda j, i: (jnp.maximum(i * per - 1, 0), j)),
                  pl.BlockSpec((kw, tc), lambda j, i: (0, j)), pl.BlockSpec((1, tc), lambda j, i: (0, j))],
        out_specs=pl.BlockSpec((t, tc), lambda j, i: (i, j)), out_shape=jax.ShapeDtypeStruct((nrow, ch), F32),
        scratch_shapes=[pltpu.VMEM((t + halo, tc), F32)], compiler_params=_cparams(2), name=name)(x, x, w, b)


def _dwconv_bwd(x, dy, w, name, dx_dtype=F32):
    nrow, ch, kw, halo, t, tc = _conv_tiles(x, w)
    per = t // halo
    sub = min(CONV_SUB, t)
    nt = nrow // t
    nhalo = nrow // halo

    def body(x_ref, xh_ref, dy_ref, dyh_ref, w_ref, dx_ref, dw_ref, db_ref, xbuf, dybuf):
        i = pl.program_id(1)
        xbuf[0:halo, :] = jnp.where(i == 0, 0.0, xh_ref[...])
        xbuf[halo:halo + t, :] = x_ref[...]
        dyc = dy_ref[...]
        dybuf[0:t, :] = dyc
        dybuf[t:t + halo, :] = jnp.where(i == nt - 1, 0.0, dyh_ref[...])
        @pl.when(i == 0)
        def _():
            dw_ref[...] = jnp.zeros_like(dw_ref)
            db_ref[...] = jnp.zeros_like(db_ref)

        dw_part = [jnp.zeros((8, tc), F32) for _ in range(kw)]
        for r0 in range(0, t, sub):
            dywin = _shifted_windows(dybuf, list(range(kw)), sub, r0)
            acc = jnp.zeros((sub, tc), F32)
            for k in range(kw):
                acc = acc + w_ref[k:k + 1, :] * dywin[kw - 1 - k]
            dx_ref[r0:r0 + sub, :] = acc.astype(dx_ref.dtype)
            xwin = _shifted_windows(xbuf, [halo - s for s in range(kw)], sub, r0)
            for k in range(kw):
                dw_part[k] = dw_part[k] + _fold8(dywin[0] * xwin[halo - (kw - 1 - k)])
        for k in range(kw):
            dw_ref[k:k + 1, :] += jnp.sum(dw_part[k], axis=0, keepdims=True)
        db_ref[...] += jnp.sum(dyc, axis=0, keepdims=True)

    return pl.pallas_call(
        body, grid=(ch // tc, nt),
        in_specs=[pl.BlockSpec((t, tc), lambda j, i: (i, j)),
                  pl.BlockSpec((halo, tc), lambda j, i: (jnp.maximum(i * per - 1, 0), j)),
                  pl.BlockSpec((t, tc), lambda j, i: (i, j)),
                  pl.BlockSpec((halo, tc), lambda j, i: (jnp.minimum((i + 1) * per, nhalo - 1), j)),
                  pl.BlockSpec((kw, tc), lambda j, i: (0, j))],
        out_specs=[pl.BlockSpec((t, tc), lambda j, i: (i, j)), pl.BlockSpec((kw, tc), lambda j, i: (0, j)),
                   pl.BlockSpec((1, tc), lambda j, i: (0, j))],
        out_shape=[jax.ShapeDtypeStruct((nrow, ch), dx_dtype), jax.ShapeDtypeStruct((kw, ch), F32),
                   jax.ShapeDtypeStruct((1, ch), F32)],
        scratch_shapes=[pltpu.VMEM((t + halo, tc), F32), pltpu.VMEM((t + halo, tc), F32)],
        compiler_params=_cparams(2), name=name)(x, x, dy, dy, w)


SB_SUB = 128
LOG2E = 1.4426950408889634
SB_SKIP_BELOW = -256.0
SB_UNVISITED = -1e30


def _dot_split2(x, u):
    hi = x.astype(BF16)
    lo = (x - hi.astype(F32)).astype(BF16)
    return jnp.dot(hi, u, preferred_element_type=F32) + jnp.dot(lo, u, preferred_element_type=F32)


def _sb_scores(q, k, strict, scale):
    z = lax.dot_general(q, k, (((1,), (1,)), ((), ())), preferred_element_type=F32) * (scale * LOG2E)
    soft = jnp.log2(1.0 + jnp.exp2(-jnp.abs(z)))
    ls = jnp.minimum(z, 0.0) - soft
    lk = ls - z
    if strict is not None:
        lk = jnp.where(strict, lk, 0.0)
    return ls, lk


def _sb_running(x, tri, start, reverse):
    nsub = x.shape[1] // SB_SUB
    parts = [x[:, SB_SUB * b:SB_SUB * (b + 1)] for b in range(nsub)]
    out = [None] * nsub
    run = start
    for b in (reversed(range(nsub)) if reverse else range(nsub)):
        out[b] = _dot_split2(parts[b], tri) + run
        run = run + jnp.sum(parts[b], axis=1, keepdims=True)
    return (jnp.concatenate(out, axis=1) if nsub > 1 else out[0]), run


def _sb_dims(qn):
    nrow, d = qn.shape
    hd = SB_HEAD_DIM
    t = _tile(nrow, (512, 256, 128))
    assert nrow % t == 0 and nrow // t <= LANES
    return nrow, d, hd, d // hd, t


def _sb_masks(t):
    strict = lax.broadcasted_iota(jnp.int32, (t, t), 1) < lax.broadcasted_iota(jnp.int32, (t, t), 0)
    r0 = lax.broadcasted_iota(jnp.int32, (SB_SUB, SB_SUB), 0)
    c0 = lax.broadcasted_iota(jnp.int32, (SB_SUB, SB_SUB), 1)
    return strict, (r0 > c0).astype(BF16), (r0 < c0).astype(BF16)


def _sb_attn_fwd(qn, kn, qkv, name):
    nrow, d, hd, nh, t = _sb_dims(qn)
    scale = 1.0 / math.sqrt(hd)

    def body(q_ref, k_ref, v_ref, o_ref, rs_ref, r_acc, o_acc):
        i = pl.program_id(1)
        q = q_ref[...]
        strict, after, _ = _sb_masks(t)
        lane = lax.broadcasted_iota(jnp.int32, (t, LANES), 1)
        r_acc[...] = jnp.zeros_like(r_acc)
        o_acc[...] = jnp.zeros_like(o_acc)
        rs_ref[...] = jnp.full(rs_ref.shape, SB_UNVISITED, F32)

        def tile(j, mask):
            start = pl.multiple_of(j * t, t)
            k = k_ref[pl.ds(start, t), :]
            v = v_ref[pl.ds(start, t), :].astype(BF16)
            ls, lk = _sb_scores(q, k, mask, scale)
            r = r_acc[...]
            later, r_next = _sb_running(lk, after, r, True)
            att = jnp.exp2(ls + later)
            if mask is not None:
                att = jnp.where(mask, att, 0.0)
            o_acc[...] += jnp.dot(att.astype(BF16), v, preferred_element_type=F32)
            rs_ref[...] = jnp.where(lane == j, r, rs_ref[...])
            r_acc[...] = r_next

        tile(i, strict)

        def more(carry):
            n, r_max = carry
            return (n < i) & (r_max > SB_SKIP_BELOW)

        def step(carry):
            n, _ = carry
            tile(i - 1 - n, None)
            return n + 1, jnp.max(r_acc[...])

        lax.while_loop(more, step, (jnp.int32(0), jnp.max(r_acc[...])))
        o_ref[...] = o_acc[...].astype(o_ref.dtype)

    return pl.pallas_call(
        body, grid=(nh, nrow // t),
        in_specs=[pl.BlockSpec((t, hd), lambda h, i: (i, h)), pl.BlockSpec((nrow, hd), lambda h, i: (0, h)),
                  pl.BlockSpec((nrow, hd), lambda h, i: (0, 2 * nh + h))],
        out_specs=[pl.BlockSpec((t, hd), lambda h, i: (i, h)), pl.BlockSpec((None, t, LANES), lambda h, i: (h, i, 0))],
        out_shape=[jax.ShapeDtypeStruct((nrow, d), BF16), jax.ShapeDtypeStruct((nh, nrow, LANES), F32)],
        scratch_shapes=[pltpu.VMEM((t, LANES), F32), pltpu.VMEM((t, hd), F32)],
        compiler_params=_cparams(2), name=name)(qn, kn, qkv)


def _sb_attn_bwd(qn, kn, qkv, do, rsave, name):
    nrow, d, hd, nh, t = _sb_dims(qn)
    scale = 1.0 / math.sqrt(hd)

    def body(q_ref, k_ref, v_ref, do_ref, rs_ref, dq_ref, dk_ref, dv_ref, pg_acc):
        i = pl.program_id(1)

        @pl.when(i == 0)
        def _():
            dk_ref[...] = jnp.zeros_like(dk_ref)
            dv_ref[...] = jnp.zeros_like(dv_ref)

        pg_acc[...] = jnp.zeros_like(pg_acc)
        dq_ref[...] = jnp.zeros_like(dq_ref)
        q = q_ref[...]
        dob = do_ref[...].astype(BF16)
        strict, after, before = _sb_masks(t)
        lane = lax.broadcasted_iota(jnp.int32, (t, LANES), 1)

        def tile(j, mask):
            start = pl.multiple_of(j * t, t)
            k = k_ref[pl.ds(start, t), :]
            v = v_ref[pl.ds(start, t), :].astype(BF16)
            ls, lk = _sb_scores(q, k, mask, scale)
            rj = jnp.sum(jnp.where(lane == j, rs_ref[...], 0.0), axis=1, keepdims=True)
            later, _ = _sb_running(lk, after, jnp.broadcast_to(rj, (t, LANES)), True)
            att = jnp.exp2(ls + later)
            if mask is not None:
                att = jnp.where(mask, att, 0.0)
            datt = lax.dot_general(dob, v, (((1,), (1,)), ((), ())), preferred_element_type=F32)
            g = datt * att
            dlk, pg_next = _sb_running(g, before, pg_acc[...], False)
            sig = jnp.exp2(ls)
            dz = g * (1.0 - sig) - dlk * sig
            if mask is not None:
                dz = jnp.where(mask, dz, 0.0)
            dz = (dz * scale).astype(BF16)
            dq_ref[...] += jnp.dot(dz, k, preferred_element_type=F32)
            dk_ref[pl.ds(start, t), :] += lax.dot_general(dz, q, (((0,), (0,)), ((), ())), preferred_element_type=F32)
            dv_ref[pl.ds(start, t), :] += lax.dot_general(att.astype(BF16), dob, (((0,), (0,)), ((), ())),
                                                          preferred_element_type=F32)
            pg_acc[...] = pg_next

        def step(j, carry):
            tile(j, None)
            return carry

        seen = jnp.max(rs_ref[...], axis=0, keepdims=True)
        lane1 = lax.broadcasted_iota(jnp.int32, (1, LANES), 1)
        first = jnp.sum(jnp.where((lane1 < i) & (seen < 0.5 * SB_UNVISITED), 1.0, 0.0)).astype(jnp.int32)
        lax.fori_loop(first, i, step, 0)
        tile(i, strict)

    return pl.pallas_call(
        body, grid=(nh, nrow // t),
        in_specs=[pl.BlockSpec((t, hd), lambda h, i: (i, h)), pl.BlockSpec((nrow, hd), lambda h, i: (0, h)),
                  pl.BlockSpec((nrow, hd), lambda h, i: (0, 2 * nh + h)), pl.BlockSpec((t, hd), lambda h, i: (i, h)),
                  pl.BlockSpec((None, t, LANES), lambda h, i: (h, i, 0))],
        out_specs=[pl.BlockSpec((t, hd), lambda h, i: (i, h)), pl.BlockSpec((nrow, hd), lambda h, i: (0, h)),
                   pl.BlockSpec((nrow, hd), lambda h, i: (0, h))],
        out_shape=[jax.ShapeDtypeStruct((nrow, d), F32)] * 3, scratch_shapes=[pltpu.VMEM((t, LANES), F32)],
        compiler_params=_cparams(2), name=name)(qn, kn, qkv, do, rsave)


def _ssd_chunk_fn(g, nheads_g, q):
    p = M2_HEAD_DIM
    npair = nheads_g // 2

    def f(xp, bp, cp, dtr, hp, dtb, alog, dsk):
        b = _silu(bp).astype(BF16)
        c = _silu(cp).astype(BF16)
        dt = _softplus(dtr + dtb)
        dta = dt * (-jnp.exp(alog))
        ri = lax.broadcasted_iota(jnp.int32, (q, q), 0)
        ci = lax.broadcasted_iota(jnp.int32, (q, q), 1)
        causal = ri >= ci
        tri = causal.astype(F32)
        acol = jnp.dot(tri, dta, precision=HIGHEST, preferred_element_type=F32)
        arow = lax.dot_general(dta, tri, (((0,), (1,)), ((), ())), precision=HIGHEST,
                               preferred_element_type=F32)
        alast = jnp.sum(dta, axis=0, keepdims=True)
        cb = lax.dot_general(c, b, (((1,), (1,)), ((), ())), preferred_element_type=F32)
        lane = lax.broadcasted_iota(jnp.int32, (1, LANES), 1)
        sub = lax.broadcasted_iota(jnp.int32, (LANES, 1), 0)
        lane_half = lane < p
        sub_half = sub < p

        def col(mat, h):
            return jnp.sum(jnp.where(lane == h, mat, 0.0), axis=1, keepdims=True)

        def row(mat, h):
            return jnp.sum(jnp.where(sub == h, mat, 0.0), axis=0, keepdims=True)

        def mix(v0, v1):
            return jnp.where(lane_half, v0, v1)

        ys, hns = [], []
        for jp in range(npair):
            h0 = g * nheads_g + 2 * jp
            h1 = h0 + 1
            x = _silu(xp[jp])
            ac0, ac1 = col(acol, h0), col(acol, h1)
            xdt = x * mix(col(dt, h0), col(dt, h1))
            xdtb = xdt.astype(BF16)
            yd = []
            for h, ac in ((h0, ac0), (h1, ac1)):
                dec = jnp.exp(jnp.where(causal, ac - row(arow, h), -jnp.inf))
                yd.append(jnp.dot((cb * dec).astype(BF16), xdtb, preferred_element_type=F32))
            hpj = hp[jp]
            yo = lax.dot_general(c, hpj.astype(BF16), (((1,), (1,)), ((), ())), preferred_element_type=F32)
            y = mix(yd[0], yd[1]) + yo * jnp.exp(mix(ac0, ac1)) + mix(col(dsk, h0), col(dsk, h1)) * x
            al0, al1 = col(alast, h0), col(alast, h1)
            dout = jnp.exp(mix(al0 - ac0, al1 - ac1))
            st = lax.dot_general((xdt * dout).astype(BF16), b, (((0,), (0,)), ((), ())), preferred_element_type=F32)
            hns.append(hpj * jnp.exp(jnp.where(sub_half, al0, al1)) + st)
            ys.append(y)
        return ys, hns

    return f


def _ssd_dims(pre, dtr):
    nrow = pre.shape[0]
    ng, n, p, q = M2_GROUPS, M2_STATE, M2_HEAD_DIM, M2_CHUNK
    d_inner = pre.shape[1] - 2 * ng * n
    gw = d_inner // ng
    nhg = gw // p
    assert nhg % 2 == 0 and gw % LANES == 0 and n == LANES and ng * nhg <= LANES and dtr.shape[1] == LANES
    return nrow, ng, n, q, d_inner, gw, nhg, nrow // q


def _ssd_fwd(pre, dtr, dtb, alog, dsk, name):
    nrow, ng, n, q, d_inner, gw, nhg, nc = _ssd_dims(pre, dtr)
    boff = d_inner // n
    npair = nhg // 2

    def body(x_ref, b_ref, c_ref, dt_ref, dtb_ref, al_ref, ds_ref, y_ref, hs_ref, state):
        ci, g = pl.program_id(0), pl.program_id(1)

        @pl.when(ci == 0)
        def _():
            state[g] = jnp.zeros((gw, n), F32)

        hs_ref[...] = state[g]
        f = _ssd_chunk_fn(g, nhg, q)
        xp = [x_ref[:, LANES * j:LANES * (j + 1)] for j in range(npair)]
        hp = [state[g, LANES * j:LANES * (j + 1), :] for j in range(npair)]
        ys, hns = f(xp, b_ref[...], c_ref[...], dt_ref[...], hp, dtb_ref[...], al_ref[...], ds_ref[...])
        for j in range(npair):
            y_ref[:, LANES * j:LANES * (j + 1)] = ys[j]
            state[g, LANES * j:LANES * (j + 1), :] = hns[j]

    par = pl.BlockSpec((1, LANES), lambda ci, g: (0, 0))
    return pl.pallas_call(
        body, grid=(nc, ng),
        in_specs=[pl.BlockSpec((q, gw), lambda ci, g: (ci, g)), pl.BlockSpec((q, n), lambda ci, g: (ci, boff + g)),
                  pl.BlockSpec((q, n), lambda ci, g: (ci, boff + ng + g)), pl.BlockSpec((q, LANES), lambda ci, g: (ci, 0)),
                  par, par, par],
        out_specs=[pl.BlockSpec((q, gw), lambda ci, g: (ci, g)),
                   pl.BlockSpec((None, None, gw, n), lambda ci, g: (ci, g, 0, 0))],
        out_shape=[jax.ShapeDtypeStruct((nrow, d_inner), F32), jax.ShapeDtypeStruct((nc, ng, gw, n), F32)],
        scratch_shapes=[pltpu.VMEM((ng, gw, n), F32)], compiler_params=_cparams(2), name=name)(
            pre, pre, pre, dtr, dtb, alog, dsk)


def _ssd_bwd(pre, dtr, hs, dy, dtb, alog, dsk, name):
    nrow, ng, n, q, d_inner, gw, nhg, nc = _ssd_dims(pre, dtr)
    boff = d_inner // n
    npair = nhg // 2

    def body(x_ref, b_ref, c_ref, dt_ref, hs_ref, dy_ref, dtb_ref, al_ref, ds_ref,
             dx_ref, db_ref, dc_ref, ddt_ref, ddtb_ref, dal_ref, dds_ref, dstate):
        ci, g = pl.program_id(0), pl.program_id(1)

        @pl.when(ci == 0)
        def _():
            dstate[g] = jnp.zeros((gw, n), F32)

        @pl.when((ci == 0) & (g == 0))
        def _():
            ddtb_ref[...] = jnp.zeros_like(ddtb_ref)
            dal_ref[...] = jnp.zeros_like(dal_ref)
            dds_ref[...] = jnp.zeros_like(dds_ref)

        @pl.when(g == 0)
        def _():
            ddt_ref[...] = jnp.zeros_like(ddt_ref)

        f = _ssd_chunk_fn(g, nhg, q)
        xp = [x_ref[:, LANES * j:LANES * (j + 1)] for j in range(npair)]
        hp = [hs_ref[LANES * j:LANES * (j + 1), :] for j in range(npair)]
        _, vjp = jax.vjp(f, xp, b_ref[...], c_ref[...], dt_ref[...], hp, dtb_ref[...], al_ref[...], ds_ref[...])
        dys = [dy_ref[:, LANES * j:LANES * (j + 1)] for j in range(npair)]
        dhn = [dstate[g, LANES * j:LANES * (j + 1), :] for j in range(npair)]
        dxp, db, dc, ddt, dhp, ddtb, dal, dds = vjp((dys, dhn))
        for j in range(npair):
            dx_ref[:, LANES * j:LANES * (j + 1)] = dxp[j]
            dstate[g, LANES * j:LANES * (j + 1), :] = dhp[j]
        db_ref[...] = db
        dc_ref[...] = dc
        ddt_ref[...] += ddt
        ddtb_ref[...] += ddtb
        dal_ref[...] += dal
        dds_ref[...] += dds

    par = pl.BlockSpec((1, LANES), lambda ci, g: (0, 0))
    last = nc - 1
    return pl.pallas_call(
        body, grid=(nc, ng),
        in_specs=[pl.BlockSpec((q, gw), lambda ci, g: (last - ci, g)),
                  pl.BlockSpec((q, n), lambda ci, g: (last - ci, boff + g)),
                  pl.BlockSpec((q, n), lambda ci, g: (last - ci, boff + ng + g)),
                  pl.BlockSpec((q, LANES), lambda ci, g: (last - ci, 0)),
                  pl.BlockSpec((None, None, gw, n), lambda ci, g: (last - ci, g, 0, 0)),
                  pl.BlockSpec((q, gw), lambda ci, g: (last - ci, g)), par, par, par],
        out_specs=[pl.BlockSpec((q, gw), lambda ci, g: (last - ci, g)), pl.BlockSpec((q, n), lambda ci, g: (last - ci, g)),
                   pl.BlockSpec((q, n), lambda ci, g: (last - ci, g)), pl.BlockSpec((q, LANES), lambda ci, g: (last - ci, 0)),
                   par, par, par],
        out_shape=[jax.ShapeDtypeStruct((nrow, d_inner), F32), jax.ShapeDtypeStruct((nrow, ng * n), F32),
                   jax.ShapeDtypeStruct((nrow, ng * n), F32), jax.ShapeDtypeStruct((nrow, LANES), F32),
                   jax.ShapeDtypeStruct((1, LANES), F32), jax.ShapeDtypeStruct((1, LANES), F32),
                   jax.ShapeDtypeStruct((1, LANES), F32)],
        scratch_shapes=[pltpu.VMEM((ng, gw, n), F32)], compiler_params=_cparams(2), name=name)(
            pre, pre, pre, dtr, hs, dy, dtb, alog, dsk)


HBM_SPEC = pl.BlockSpec(memory_space=pl.ANY)


def _xy_peers(mx, my):
    return [(1 - mx, my), (mx, 1 - my), (1 - mx, 1 - my)]


def _remote(src, dst, send_sems, recv_sems, k, dev):
    return pltpu.make_async_remote_copy(src_ref=src, dst_ref=dst, send_sem=send_sems.at[k], recv_sem=recv_sems.at[k],
                                        device_id=dev, device_id_type=MESH_ID)


def _comm_call(body, arrays, out_shapes, ncopies, name):
    nin = len(arrays)

    def wrapped(*refs):
        body(refs[:nin], refs[nin:nin + len(out_shapes)], refs[-2], refs[-1])

    return pl.pallas_call(
        wrapped, out_shape=out_shapes, in_specs=[HBM_SPEC] * nin, out_specs=[HBM_SPEC] * len(out_shapes),
        scratch_shapes=[pltpu.SemaphoreType.DMA((ncopies,)), pltpu.SemaphoreType.DMA((ncopies,))], name=name)(*arrays)


def _all_gather_xy(bufs, name):
    npeer = N_XY - 1

    def body(srcs, outs, send_sems, recv_sems):
        mx, my, mc = lax.axis_index("x"), lax.axis_index("y"), lax.axis_index("c")
        me = 2 * mx + my
        sibling = (mx, my, 1 - mc)
        peers = _xy_peers(mx, my)
        sends = []
        for a, (src, out) in enumerate(zip(srcs, outs)):
            for k, (px, py) in enumerate(peers):
                sends.append(_remote(src.at[mc], out.at[me, mc], send_sems, recv_sems, 2 * npeer * a + k, (px, py, mc)))
        for cp in sends:
            cp.start()
        for a, (src, out) in enumerate(zip(srcs, outs)):
            for k, (px, py) in enumerate(peers):
                landed = out.at[2 * px + py, mc]
                _remote(src.at[mc], landed, send_sems, recv_sems, 2 * npeer * a + k, (px, py, mc)).wait_recv()
                cp = _remote(landed, landed, send_sems, recv_sems, 2 * npeer * a + npeer + k, sibling)
                cp.start()
                sends.append(cp)
        for a, out in enumerate(outs):
            for k, (px, py) in enumerate(peers):
                passed = out.at[2 * px + py, 1 - mc]
                _remote(passed, passed, send_sems, recv_sems, 2 * npeer * a + npeer + k, sibling).wait_recv()
        for cp in sends:
            cp.wait_send()

    outs = _comm_call(body, bufs, [jax.ShapeDtypeStruct((N_XY,) + b.shape, b.dtype) for b in bufs],
                      2 * npeer * len(bufs), name)
    return [lax.dynamic_update_index_in_dim(o, b, _my_shard(), 0) for o, b in zip(outs, bufs)]


def _my_shard():
    return 2 * lax.axis_index("x") + lax.axis_index("y")


def _exchange_xy(parts, name):
    npeer = N_XY - 1

    def body(srcs, outs, send_sems, recv_sems):
        mx, my, mc = lax.axis_index("x"), lax.axis_index("y"), lax.axis_index("c")
        me = 2 * mx + my
        peers = _xy_peers(mx, my)
        sends = [_remote(src.at[2 * px + py], out.at[me], send_sems, recv_sems, npeer * a + k, (px, py, mc))
                 for k, (px, py) in enumerate(peers) for a, (src, out) in enumerate(zip(srcs, outs))]
        for cp in sends:
            cp.start()
        for k, (px, py) in enumerate(peers):
            for a, (src, out) in enumerate(zip(srcs, outs)):
                _remote(src.at[me], out.at[2 * px + py], send_sems, recv_sems, npeer * a + k, (px, py, mc)).wait_recv()
        for cp in sends:
            cp.wait_send()

    outs = _comm_call(body, parts, [jax.ShapeDtypeStruct(p.shape, p.dtype) for p in parts], npeer * len(parts), name)
    me = _my_shard()
    return [lax.dynamic_update_index_in_dim(o, lax.dynamic_index_in_dim(p, me, 0, keepdims=False), me, 0)
            for o, p in zip(outs, parts)]


def _pair_split(parts, name):
    def body(srcs, gots, send_sems, recv_sems):
        mx, my, mc = lax.axis_index("x"), lax.axis_index("y"), lax.axis_index("c")
        copies, idx = [], 0
        for src, got in zip(srcs, gots):
            for s in range(src.shape[0]):
                copies.append(_remote(src.at[s, 1 - mc], got.at[s], send_sems, recv_sems, idx, (mx, my, 1 - mc)))
                idx += 1
        for cp in copies:
            cp.start()
        for cp in copies:
            cp.wait()

    gots = _comm_call(body, parts, [jax.ShapeDtypeStruct((p.shape[0],) + p.shape[2:], p.dtype) for p in parts],
                      sum(p.shape[0] for p in parts), name)
    mc = lax.axis_index("c")
    return [lax.dynamic_index_in_dim(p, mc, 1, keepdims=False) for p in parts], gots


def _pair_join(mines, name):
    def body(srcs, outs, send_sems, recv_sems):
        mx, my, mc = lax.axis_index("x"), lax.axis_index("y"), lax.axis_index("c")
        copies = [_remote(src, out, send_sems, recv_sems, a, (mx, my, 1 - mc))
                  for a, (src, out) in enumerate(zip(srcs, outs))]
        for cp in copies:
            cp.start()
        for cp in copies:
            cp.wait()

    others = _comm_call(body, mines, [jax.ShapeDtypeStruct(m.shape, m.dtype) for m in mines], len(mines), name)
    first = lax.axis_index("c") == 0
    return [jnp.where(first, jnp.stack([m, o]), jnp.stack([o, m])) for m, o in zip(mines, others)]


def _all_gather_all(buf, name):
    flips = [(fx, fy, fc) for fx in (0, 1) for fy in (0, 1) for fc in (0, 1) if fx or fy or fc]

    def body(src, out, send_sems, recv_sems):
        mx, my, mc = lax.axis_index("x"), lax.axis_index("y"), lax.axis_index("c")
        me = 4 * mx + 2 * my + mc
        peers = [(1 - mx if fx else mx, 1 - my if fy else my, 1 - mc if fc else mc) for fx, fy, fc in flips]
        sends = [_remote(src, out.at[me], send_sems, recv_sems, k, dev) for k, dev in enumerate(peers)]
        for cp in sends:
            cp.start()
        for k, (px, py, pc) in enumerate(peers):
            _remote(src, out.at[4 * px + 2 * py + pc], send_sems, recv_sems, k, (px, py, pc)).wait_recv()
        for cp in sends:
            cp.wait_send()

    out = pl.pallas_call(
        body, out_shape=jax.ShapeDtypeStruct((N_DEV,) + buf.shape, buf.dtype), in_specs=[HBM_SPEC], out_specs=HBM_SPEC,
        scratch_shapes=[pltpu.SemaphoreType.DMA((N_DEV - 1,)), pltpu.SemaphoreType.DMA((N_DEV - 1,))], name=name)(buf)
    me = 4 * lax.axis_index("x") + 2 * lax.axis_index("y") + lax.axis_index("c")
    return lax.dynamic_update_index_in_dim(out, buf, me, 0)


def _flat_tile(nrow, width, narrays):
    t = nrow
    for cand in (512, 256, 128, 64, 32, 16, 8):
        if nrow % cand == 0:
            t = cand
            if cand * width * 4 * narrays <= ROW_BLOCK_BYTES:
                break
    return t


def _sum_parts(parts, name):
    kparts, nrow, width = parts.shape
    t = _flat_tile(nrow, width, kparts + 1)

    def body(p_ref, o_ref):
        s = p_ref[0].astype(F32)
        for k in range(1, kparts):
            s = s + p_ref[k].astype(F32)
        o_ref[...] = s

    return pl.pallas_call(
        body, grid=(nrow // t,), in_specs=[pl.BlockSpec((kparts, t, width), lambda i: (0, i, 0))],
        out_specs=pl.BlockSpec((t, width), lambda i: (i, 0)), out_shape=jax.ShapeDtypeStruct((nrow, width), F32),
        compiler_params=_cparams(1), name=name)(parts)


def _add2(a, b, name):
    nparts, nrow, width = a.shape
    t = _flat_tile(nrow, width, 3)

    def body(a_ref, b_ref, o_ref):
        o_ref[...] = (a_ref[...].astype(F32) + b_ref[...].astype(F32)).astype(o_ref.dtype)

    spec = pl.BlockSpec((None, t, width), lambda p, i: (p, i, 0))
    return pl.pallas_call(body, grid=(nparts, nrow // t), in_specs=[spec, spec], out_specs=spec,
                          out_shape=jax.ShapeDtypeStruct(a.shape, a.dtype), compiler_params=_cparams(2), name=name)(a, b)


def _adamw(w, m, v, ga, gb, name):
    nrow, width = w.shape
    t = _flat_tile(nrow, width, 9)
    c1 = 1.0 - ADAM_B1 ** ADAM_STEP
    c2 = 1.0 - ADAM_B2 ** ADAM_STEP
    grads = [ga] if gb is None else [ga, gb]
    nout = 3 if gb is None else 4

    def body(w_ref, m_ref, v_ref, *refs):
        d_ref, mo_ref, vo_ref = refs[-3:]
        g = refs[0][...]
        if gb is not None:
            g = g + refs[1][...]
            refs[2][...] = g
        mn = ADAM_B1 * m_ref[...] + (1.0 - ADAM_B1) * g
        vn = ADAM_B2 * v_ref[...] + (1.0 - ADAM_B2) * jnp.square(g)
        mo_ref[...] = mn
        vo_ref[...] = vn
        d_ref[...] = -ADAM_LR * ((mn / c1) / (jnp.sqrt(vn / c2) + ADAM_EPS) + ADAM_WD * w_ref[...])

    spec = pl.BlockSpec((t, width), lambda i: (i, 0))
    return pl.pallas_call(
        body, grid=(nrow // t,), in_specs=[spec] * (3 + len(grads)), out_specs=[spec] * nout,
        out_shape=[jax.ShapeDtypeStruct((nrow, width), F32)] * nout, compiler_params=_cparams(1), name=name)(
            w, m, v, *grads)


def _pack(arrs, dtype, width):
    flat = jnp.concatenate([a.astype(dtype).reshape(-1) for a in arrs])
    quantum = PACK_ROWS * width
    pad = (-flat.shape[0]) % quantum
    if pad:
        flat = jnp.concatenate([flat, jnp.zeros((pad,), dtype)])
    return flat.reshape(-1, width)


def _unpack(buf, shapes):
    flat = buf.reshape(-1)
    out, off = [], 0
    for s in shapes:
        size = math.prod(s)
        out.append(flat[off:off + size].reshape(s))
        off += size
    return out


QK_NORM_HEADS = 8


def _qk_norm_fwd(qkv, w, third, nh, tag):
    f = _rms_f(RMS_EPS)
    hd = SB_HEAD_DIM
    hpb = math.gcd(nh, QK_NORM_HEADS)

    def fn(xv, wv):
        return (jnp.concatenate([f(xv[:, hd * h:hd * (h + 1)], wv) for h in range(hpb)], axis=1),)

    return _rowwise(fn, [(qkv, hpb * hd, third * (nh // hpb))], [(w, False)], [('row', nh * hd, hpb * hd, BF16)],
                    ncb=nh // hpb, name=tag)[0]


def _qk_norm_bwd(qkv, w, third, dn, nh, tag):
    f = _rms_f(RMS_EPS)
    hd = SB_HEAD_DIM
    hpb = math.gcd(nh, QK_NORM_HEADS)

    def fn(xv, dv, wv):
        dxs, dw = [], jnp.zeros_like(wv)
        for h in range(hpb):
            _, vjp = jax.vjp(f, xv[:, hd * h:hd * (h + 1)], wv)
            dxh, dwh = vjp(dv[:, hd * h:hd * (h + 1)])
            dxs.append(dxh)
            dw = dw + dwh
        return jnp.concatenate(dxs, axis=1), dw

    return _rowwise(fn, [(qkv, hpb * hd, third * (nh // hpb)), (dn, hpb * hd, 0)], [(w, False)],
                    [('row', nh * hd, hpb * hd, BF16), ('acc', (1, hd), False)], ncb=nh // hpb, name=tag)


def _sb_forward(h, xin, wt, tag):
    d = xin.shape[1]
    nh = d // SB_HEAD_DIM
    qkv = _mm(h, wt['w_qkv'], "nn", name=tag + "_qkv")
    qn = _qk_norm_fwd(qkv, wt['q_norm_w'], 0, nh, tag + "_qnorm")
    kn = _qk_norm_fwd(qkv, wt['k_norm_w'], 1, nh, tag + "_knorm")
    o, rsave = _sb_attn_fwd(qn, kn, qkv, tag + "_attn")
    xm = _mm(o, wt['w_o'], "nn", extras=(xin,), epilogue=_ep_add, name=tag + "_out")
    return xm, dict(qkv=qkv, qn=qn, kn=kn, o=o, rsave=rsave)


def _sb_backward(h, dx, dxb, wt, sv, tag):
    nh = dx.shape[1] // SB_HEAD_DIM
    do = _mm(dxb, wt['w_o'], "nt", name=tag + "_do")
    g_wo = _mm(sv['o'], dxb, "tn", out_dtypes=(BF16,), name=tag + "_dwo")
    dqn, dkn, dv = _sb_attn_bwd(sv['qn'], sv['kn'], sv['qkv'], do, sv['rsave'], tag + "_attn_bwd")
    dq, g_qw = _qk_norm_bwd(sv['qkv'], wt['q_norm_w'], 0, dqn, nh, tag + "_qnorm_bwd")
    dk, g_kw = _qk_norm_bwd(sv['qkv'], wt['k_norm_w'], 1, dkn, nh, tag + "_knorm_bwd")
    dqkv = jnp.concatenate([dq, dk, dv.astype(BF16)], axis=1)
    g_wqkv = _mm(h, dqkv, "tn", out_dtypes=(BF16,), out_col_shards=N_XY, name=tag + "_dwqkv")
    dh = _mm(dqkv, wt['w_qkv'], "nt", name=tag + "_dh")
    return dh, dict(w_qkv=g_wqkv, w_o=g_wo, q_norm_w=g_qw, k_norm_w=g_kw)


def _ln_silu_f(x, w, b):
    mu = jnp.mean(x, axis=-1, keepdims=True)
    xc = x - mu
    return _silu(xc * lax.rsqrt(jnp.mean(xc * xc, axis=-1, keepdims=True) + LN_EPS) * w + b)


def _glu_bwd_fn(val, gate, dg):
    sg = 1.0 / (1.0 + jnp.exp(-gate))
    return (jnp.concatenate([dg * sg, dg * val * sg * (1.0 - sg)], axis=1),)


def _cf_forward(h, xin, wt, tag):
    d = xin.shape[1]
    u = _mm(h, wt['w_in'], "nn", extras=(wt['b_in'],), epilogue=_ep_add, name=tag + "_in")
    gl = _rowwise(lambda val, gate: (val / (1.0 + jnp.exp(-gate)),), [(u, d, 0), (u, d, 1)], [],
                  [('row', d, d, F32)], name=tag + "_glu")[0]
    cv = _dwconv_fwd(gl, wt['dw_w'], wt['dw_b'], tag + "_conv")
    s = _rowwise(lambda x, w, b: (_ln_silu_f(x, w, b),), [(cv, d, 0)], [(wt['ln_w'], False), (wt['ln_b'], False)],
                 [('row', d, d, BF16)], name=tag + "_ln")[0]
    xm = _mm(s, wt['w_out'], "nn", extras=(wt['b_out'], xin), epilogue=_ep_bias_add, name=tag + "_out")
    return xm, dict(u=u, gl=gl, cv=cv, s=s)


def _cf_backward(h, dx, dxb, wt, sv, tag):
    d = dx.shape[1]
    ds = _mm(dxb, wt['w_out'], "nt", name=tag + "_ds")
    g_wout = _mm(sv['s'], dxb, "tn", out_dtypes=(BF16,), name=tag + "_dwout")
    g_bout = _colsum(dx, tag + "_dbout")

    def ln_bwd(x, dsv, w, b):
        _, vjp = jax.vjp(_ln_silu_f, x, w, b)
        return vjp(dsv)

    dcv, g_lnw, g_lnb = _rowwise(ln_bwd, [(sv['cv'], d, 0), (ds, d, 0)], [(wt['ln_w'], False), (wt['ln_b'], False)],
                                 [('row', d, d, F32), ('acc', (1, d), False), ('acc', (1, d), False)],
                                 name=tag + "_ln_bwd")
    dgl, g_dww, g_dwb = _dwconv_bwd(sv['gl'], dcv, wt['dw_w'], tag + "_conv_bwd")
    du = _rowwise(_glu_bwd_fn, [(sv['u'], d, 0), (sv['u'], d, 1), (dgl, d, 0)], [], [('row', 2 * d, 2 * d, F32)],
                  name=tag + "_glu_bwd")[0]
    g_bin = _colsum(du, tag + "_dbin")
    g_win = _mm(h, du, "tn", out_dtypes=(BF16,), out_col_shards=N_XY, name=tag + "_dwin")
    dh = _mm(du, wt['w_in'], "nt", name=tag + "_dh")
    return dh, dict(w_in=g_win, b_in=g_bin, dw_w=g_dww, dw_b=g_dwb, ln_w=g_lnw, ln_b=g_lnb, w_out=g_wout, b_out=g_bout)


def _gated_norm_f(y, z, w):
    t = y * _silu(z)
    return t * lax.rsqrt(jnp.mean(t * t, axis=-1, keepdims=True) + M2_NORM_EPS) * w


def _m2_forward(h, xin, wt, tag):
    z = _mm(h, wt['w_z'], "nn", name=tag + "_z")
    xbc = _mm(h, wt['w_xbc'], "nn", name=tag + "_xbc")
    dtr = _mm(h, wt['w_dt'], "nn", name=tag + "_dt")
    pre = _dwconv_fwd(xbc, wt['conv_w'], wt['conv_b'], tag + "_conv")
    y, hs = _ssd_fwd(pre, dtr, wt['dt_bias'], wt['a_log'], wt['d'], tag + "_ssd")
    d_inner = y.shape[1]
    gw = d_inner // M2_GROUPS
    yn = _rowwise(lambda yv, zv, w: (_gated_norm_f(yv, zv, w),), [(y, gw, 0), (z, gw, 0)], [(wt['norm_w'], True)],
                  [('row', d_inner, gw, BF16)], ncb=M2_GROUPS, name=tag + "_gnorm")[0]
    xm = _mm(yn, wt['w_out'], "nn", extras=(xin,), epilogue=_ep_add, name=tag + "_out")
    return xm, dict(z=z, xbc=xbc, dtr=dtr, pre=pre, y=y, hs=hs, yn=yn)


def _m2_backward(h, dx, dxb, wt, sv, tag):
    dyn = _mm(dxb, wt['w_out'], "nt", name=tag + "_dyn")
    g_wout = _mm(sv['yn'], dxb, "tn", out_dtypes=(BF16,), name=tag + "_dwout")
    d_inner = sv['y'].shape[1]
    gw = d_inner // M2_GROUPS

    def gn_bwd(yv, zv, dv, w):
        _, vjp = jax.vjp(_gated_norm_f, yv, zv, w)
        return vjp(dv)

    dy, dz, g_nw = _rowwise(gn_bwd, [(sv['y'], gw, 0), (sv['z'], gw, 0), (dyn, gw, 0)], [(wt['norm_w'], True)],
                            [('row', d_inner, gw, F32), ('row', d_inner, gw, BF16), ('acc', (1, d_inner), True)],
                            ncb=M2_GROUPS, name=tag + "_gnorm_bwd")
    dxp, db, dc, ddt, g_dtb, g_alog, g_d = _ssd_bwd(sv['pre'], sv['dtr'], sv['hs'], dy, wt['dt_bias'], wt['a_log'],
                                                   wt['d'], tag + "_ssd_bwd")
    dpre = jnp.concatenate([dxp, db, dc], axis=1)
    dxbc, g_cw, g_cb = _dwconv_bwd(sv['xbc'], dpre, wt['conv_w'], tag + "_conv_bwd", dx_dtype=BF16)
    g_wz = _mm(h, dz, "tn", name=tag + "_dwz")
    g_wxbc = _mm(h, dxbc, "tn", name=tag + "_dwxbc")
    g_wdt = _mm(h, ddt, "tn", name=tag + "_dwdt")
    dh = _mm(dz, wt['w_z'], "nt", name=tag + "_dh_z")
    dh = _mm(dxbc, wt['w_xbc'], "nt", extras=(dh,), epilogue=_ep_add, name=tag + "_dh_xbc")
    dh = _mm(ddt, wt['w_dt'], "nt", extras=(dh,), epilogue=_ep_add, name=tag + "_dh_dt")
    return dh, dict(w_z=g_wz, w_xbc=g_wxbc, w_dt=g_wdt, conv_w=g_cw, conv_b=g_cb, dt_bias=g_dtb, a_log=g_alog, d=g_d,
                    norm_w=g_nw, w_out=g_wout)


def _pad_lanes(v):
    return jnp.pad(v.reshape(1, -1), ((0, 0), (0, LANES - v.shape[0])))


def kernel(x, norm_mix_w, norm_mlp_w, sb_w_qkv, sb_q_norm_w, sb_k_norm_w, sb_w_o, cf_w_in, cf_b_in, cf_dw_w, cf_dw_b, cf_ln_w, cf_ln_b, cf_w_out, cf_b_out, m2_w_in, m2_conv_w, m2_conv_b, m2_dt_bias, m2_a_log, m2_d, m2_norm_w, m2_w_out, mlp_w_up, mlp_w_down, loss_target, m_norm_mix_w, m_norm_mlp_w, m_sb_w_qkv, m_sb_q_norm_w, m_sb_k_norm_w, m_sb_w_o, m_cf_w_in, m_cf_b_in, m_cf_dw_w, m_cf_dw_b, m_cf_ln_w, m_cf_ln_b, m_cf_w_out, m_cf_b_out, m_m2_w_in, m_m2_conv_w, m_m2_conv_b, m_m2_dt_bias, m_m2_a_log, m_m2_d, m_m2_norm_w, m_m2_w_out, m_mlp_w_up, m_mlp_w_down, v_norm_mix_w, v_norm_mlp_w, v_sb_w_qkv, v_sb_q_norm_w, v_sb_k_norm_w, v_sb_w_o, v_cf_w_in, v_cf_b_in, v_cf_dw_w, v_cf_dw_b, v_cf_ln_w, v_cf_ln_b, v_cf_w_out, v_cf_b_out, v_m2_w_in, v_m2_conv_w, v_m2_conv_b, v_m2_dt_bias, v_m2_a_log, v_m2_d, v_m2_norm_w, v_m2_w_out, v_mlp_w_up, v_mlp_w_down):
    given = dict(locals())
    xl = x[0]
    tgt = loss_target[0]
    d = xl.shape[1]
    depth = norm_mix_w.shape[0]
    bulk_names, small_names = list(BULK), list(SMALL)

    def halves(buf):
        return buf.reshape(2, -1, buf.shape[-1])

    small_pack = halves(_pack([given[n] for n in small_names], F32, PACK_W))
    gath = _all_gather_xy([halves(given[n].astype(BF16)) for n in bulk_names] + [small_pack], "comm_gather")
    gathered = {n: g.reshape((N_XY,) + given[n].shape) for n, g in zip(bulk_names, gath)}
    kinds = {n: "rows" if BULK[n] == 1 else "cols" for n in bulk_names}
    full = {}
    pieces = [_unpack(gath[-1][s], [given[n].shape for n in small_names]) for s in range(N_XY)]
    for idx, n in enumerate(small_names):
        full[n] = jnp.concatenate([pieces[s][idx] for s in range(N_XY)], axis=SMALL[n])

    def wview(n, layer):
        return WView(gathered[n], layer, kinds[n])

    def row(v):
        return v.reshape(1, -1)

    d_inner = N_XY * m2_w_out.shape[1]
    conv_dim = full['m2_conv_w'].shape[2]
    nheads = m2_dt_bias.shape[1]

    def layer_weights(i):
        kind, j = i % 3, i // 3
        if kind == 0:
            return dict(w_qkv=wview('sb_w_qkv', j), w_o=wview('sb_w_o', j), q_norm_w=row(sb_q_norm_w[j]),
                        k_norm_w=row(sb_k_norm_w[j]))
        if kind == 1:
            return dict(w_in=wview('cf_w_in', j), b_in=row(cf_b_in[j]), dw_w=full['cf_dw_w'][j], dw_b=row(cf_dw_b[j]),
                        ln_w=row(cf_ln_w[j]), ln_b=row(cf_ln_b[j]), w_out=wview('cf_w_out', j), b_out=row(cf_b_out[j]))
        shards = gathered['m2_w_in'][:, j]
        w_in = jnp.moveaxis(shards, 0, 1).reshape(shards.shape[1], -1)
        w_dt = jnp.pad(w_in[:, d_inner + conv_dim:], ((0, 0), (0, LANES - nheads)))
        return dict(w_z=w_in[:, :d_inner], w_xbc=w_in[:, d_inner:d_inner + conv_dim], w_dt=w_dt,
                    conv_w=full['m2_conv_w'][j], conv_b=row(full['m2_conv_b'][j]), dt_bias=_pad_lanes(m2_dt_bias[j]),
                    a_log=_pad_lanes(m2_a_log[j]), d=_pad_lanes(m2_d[j]), norm_w=row(full['m2_norm_w'][j]),
                    w_out=wview('m2_w_out', j))

    fwd = (_sb_forward, _cf_forward, _m2_forward)
    bwd = (_sb_backward, _cf_backward, _m2_backward)

    saved = []
    xc = xl
    for i in range(depth):
        wt = layer_weights(i)
        h = _rms_fwd(xc, row(norm_mix_w[i]), f"l{i}_norm_mix")
        xm, sv = fwd[i % 3](h, xc, wt, f"l{i}_mix")
        h2 = _rms_fwd(xm, row(norm_mlp_w[i]), f"l{i}_norm_mlp")
        u, a = _mm(h2, wview('mlp_w_up', i), "nn", epilogue=_ep_relu2, out_dtypes=(F32, BF16), name=f"l{i}_up")
        xn = _mm(a, wview('mlp_w_down', i), "nn", extras=(xm,), epilogue=_ep_add, name=f"l{i}_down")
        saved.append(dict(wt=wt, x_in=xc, h=h, mix=sv, x_mid=xm, h2=h2, u=u, a=a))
        xc = xn
    dx, dxb, loss_acc = _loss(xc, tgt, "loss")
    loss = lax.psum(loss_acc[0, 0], ("x", "y", "c"))

    grads = {n: [None] * given[n].shape[0] for n in W_NAMES}
    for i in reversed(range(depth)):
        sv = saved[i]
        kind, j = i % 3, i // 3
        du = _mm(dxb, wview('mlp_w_down', i), "nt", extras=(sv['u'],), epilogue=_ep_relu2_bwd, out_dtypes=(BF16,),
                 name=f"l{i}_du")
        grads['mlp_w_down'][i] = _mm(sv['a'], dxb, "tn", out_dtypes=(BF16,), name=f"l{i}_dwdown")
        grads['mlp_w_up'][i] = _mm(sv['h2'], du, "tn", out_dtypes=(BF16,), out_col_shards=N_XY, name=f"l{i}_dwup")
        dh2 = _mm(du, wview('mlp_w_up', i), "nt", name=f"l{i}_dh2")
        dxm, dxmb, g_n2 = _rms_bwd(sv['x_mid'], row(norm_mlp_w[i]), dh2, dx, f"l{i}_norm_mlp_bwd")
        grads['norm_mlp_w'][i] = g_n2[0]
        dh, gw = bwd[kind](sv['h'], dxm, dxmb, sv['wt'], sv['mix'], f"l{i}_mix")
        dx, dxb, g_n1 = _rms_bwd(sv['x_in'], row(norm_mix_w[i]), dh, dxm, f"l{i}_norm_mix_bwd")
        grads['norm_mix_w'][i] = g_n1[0]
        if kind == 0:
            grads['sb_w_qkv'][j], grads['sb_w_o'][j] = gw['w_qkv'], gw['w_o']
            grads['sb_q_norm_w'][j], grads['sb_k_norm_w'][j] = gw['q_norm_w'][0], gw['k_norm_w'][0]
        elif kind == 1:
            for n in ('w_in', 'dw_w', 'w_out'):
                grads['cf_' + n][j] = gw[n]
            for n in ('b_in', 'dw_b', 'ln_w', 'ln_b', 'b_out'):
                grads['cf_' + n][j] = gw[n][0]
        else:
            grads['m2_w_in'][j] = jnp.concatenate([gw['w_z'], gw['w_xbc'], gw['w_dt'][:, :nheads]], axis=1)
            grads['m2_conv_w'][j], grads['m2_w_out'][j] = gw['conv_w'], gw['w_out']
            grads['m2_conv_b'][j], grads['m2_norm_w'][j] = gw['conv_b'][0], gw['norm_w'][0]
            for n in ('dt_bias', 'a_log', 'd'):
                grads['m2_' + n][j] = gw[n][0, :nheads]
    grad_x = dx[None]
    gfull = {n: jnp.stack(grads[n]) for n in small_names + REPL}

    def shard_major(n, g):
        _, r, c = given[n].shape
        if n == 'm2_w_in':
            return jnp.moveaxis(g.reshape(r, N_XY, c), 1, 0).astype(BF16)
        return g.reshape(N_XY, r, c)

    parts = [jnp.stack([shard_major(n, g) for g in grads[n]], axis=1) for n in bulk_names]
    parts = [p.reshape(N_XY, 2, -1, p.shape[-1]) for p in parts]
    split = {n: jnp.split(gfull[n], N_XY, axis=SMALL[n]) for n in small_names}
    parts.append(jnp.stack([halves(_pack([split[n][s] for n in small_names], F32, PACK_W)) for s in range(N_XY)]))
    tags = bulk_names + ["small"]
    kept, got = _pair_split(parts, "comm_split")
    pair = [_add2(k, g, f"sum_pair_{t}") for k, g, t in zip(kept, got, tags)]
    mine = [_sum_parts(r, f"sum_{t}") for r, t in zip(_exchange_xy(pair, "comm_exchange"), tags)]
    gsum = _pair_join(mine, "comm_join")
    outs = {}
    for n, g in list(zip(bulk_names, gsum)) + list(zip(small_names, _unpack(gsum[-1], [given[n].shape for n in small_names]))):
        shape = given[n].shape
        res = _adamw(*[given[p + n].reshape(-1, shape[-1]) for p in ("", "m_", "v_")], g.reshape(-1, shape[-1]),
                     None, f"adamw_{n}")
        outs["grad", n] = g.reshape(shape)
        for kind, arr in zip(("delta", "new_m", "new_v"), res):
            outs[kind, n] = arr.reshape(shape)
    rparts = _all_gather_all(_pack([gfull[n] for n in REPL], F32, LANES), "comm_gather_repl")
    nsplit = N_DEV // 2
    res = _adamw(*[_pack([given[p + n] for n in REPL], F32, LANES) for p in ("", "m_", "v_")],
                 _sum_parts(rparts[:nsplit], "sum_repl_a"), _sum_parts(rparts[nsplit:], "sum_repl_b"), "adamw_repl")
    shapes = [given[n].shape for n in REPL]
    for kind, buf in zip(("grad", "delta", "new_m", "new_v"), res):
        for n, arr in zip(REPL, _unpack(buf, shapes)):
            outs[kind, n] = arr

    return (loss, grad_x, *[outs[kind, n] for kind in ("grad", "delta", "new_m", "new_v") for n in W_NAMES])
```

```python
import math
from typing import NamedTuple

import jax
import jax.numpy as jnp
from jax import lax
from jax.experimental import pallas as pl
from jax.experimental.pallas import tpu as pltpu

F32 = jnp.float32
BF16 = jnp.bfloat16
MESH_ID = pl.DeviceIdType.MESH
HIGHEST = lax.Precision.HIGHEST

SB_HEAD_DIM = 128
M2_HEAD_DIM = 64
M2_STATE = 128
M2_GROUPS = 8
M2_CHUNK = 128
RMS_EPS = 1e-6
LN_EPS = 1e-5
M2_NORM_EPS = 1e-5
ADAM_LR = 0.001
ADAM_B1 = 0.9
ADAM_B2 = 0.999
ADAM_EPS = 1e-08
ADAM_WD = 0.01
ADAM_STEP = 10

N_XY = 4
N_DEV = 8

V7X_VMEM_BYTES = 64 * 2**20
VMEM_LIMIT = (V7X_VMEM_BYTES * 3) // 4
LANES = 128
ROW_BLOCK_BYTES = 6 * 2**20
MM_VMEM_BUDGET = (VMEM_LIMIT * 3) // 4
PACK_W = 1024
PACK_ROWS = 1024

W_NAMES = ['norm_mix_w', 'norm_mlp_w', 'sb_w_qkv', 'sb_q_norm_w', 'sb_k_norm_w', 'sb_w_o', 'cf_w_in', 'cf_b_in',
           'cf_dw_w', 'cf_dw_b', 'cf_ln_w', 'cf_ln_b', 'cf_w_out', 'cf_b_out', 'm2_w_in', 'm2_conv_w', 'm2_conv_b',
           'm2_dt_bias', 'm2_a_log', 'm2_d', 'm2_norm_w', 'm2_w_out', 'mlp_w_up', 'mlp_w_down']
BULK = {'sb_w_qkv': 2, 'sb_w_o': 1, 'cf_w_in': 2, 'cf_w_out': 1, 'm2_w_in': 2, 'm2_w_out': 1, 'mlp_w_up': 2,
        'mlp_w_down': 1}
SMALL = {'cf_dw_w': 2, 'm2_conv_w': 2, 'm2_conv_b': 1, 'm2_norm_w': 1}
REPL = ['norm_mix_w', 'norm_mlp_w', 'sb_q_norm_w', 'sb_k_norm_w', 'cf_b_in', 'cf_dw_b', 'cf_ln_w', 'cf_ln_b',
        'cf_b_out', 'm2_dt_bias', 'm2_a_log', 'm2_d']


def _tile(n, prefs):
    for p in prefs:
        if n % p == 0:
            return p
    return n


def _cparams(n):
    return pltpu.CompilerParams(dimension_semantics=("arbitrary",) * n, vmem_limit_bytes=VMEM_LIMIT)


def _silu(v):
    return v / (1.0 + jnp.exp(-v))


def _softplus(v):
    return jnp.maximum(v, 0.0) + jnp.log(1.0 + jnp.exp(-jnp.abs(v)))


class WView(NamedTuple):
    arr: jax.Array
    layer: int
    kind: str


def _mm(a, b, mode, *, extras=(), epilogue=None, out_dtypes=(F32,), out_col_shards=1, name):
    if mode == "tn":
        kdim, m = a.shape
    else:
        m, kdim = a.shape
    view = b if isinstance(b, WView) else None
    if view is not None:
        _, _, srows, scols = view.arr.shape
        brows, bcols = (N_XY * srows, scols) if view.kind == "rows" else (srows, N_XY * scols)
        b_arr, b_item = view.arr, view.arr.dtype.itemsize
    else:
        brows, bcols = b.shape
        b_arr, b_item = b, b.dtype.itemsize
    n = brows if mode == "nt" else bcols
    assert kdim == (bcols if mode == "nt" else brows)
    n_unit, k_unit = n // out_col_shards, kdim
    if view is not None:
        if (mode == "nn") == (view.kind == "cols"):
            n_unit = n // N_XY
        else:
            k_unit = kdim // N_XY
    tm = _tile(m, (1024, 512, 256, 128))
    tn = _tile(n_unit, (1024, 768, 512, 256, 128))
    acc_in_out = out_dtypes[0] == F32 and out_col_shards == 1
    fixed = sum(tm * tn * jnp.dtype(dt).itemsize for dt in out_dtypes)
    fixed += sum((1 if e.shape[0] == 1 else tm) * tn * e.dtype.itemsize for e in extras)

    def vmem_bytes(t):
        ab = tm * t * a.dtype.itemsize + t * tn * b_item
        casts = (tm * t * 2 if a.dtype != BF16 else 0) + (t * tn * 2 if b_item != 2 else 0)
        return 2 * (ab + fixed) + (0 if acc_in_out else tm * tn * 4) + tm * tn * 4 + casts

    tk = k_unit
    for cand in (2048, 1536, 1024, 768, 512, 256, 128):
        if k_unit % cand == 0:
            tk = cand
            if vmem_bytes(cand) <= MM_VMEM_BUDGET:
                break
    nk = kdim // tk
    dims = {"nn": ((1,), (0,)), "nt": ((1,), (1,)), "tn": ((0,), (0,))}[mode]
    if mode == "tn":
        a_spec = pl.BlockSpec((tk, tm), lambda i, j, k: (k, i))
    else:
        a_spec = pl.BlockSpec((tm, tk), lambda i, j, k: (i, k))
    if view is None:
        if mode == "nt":
            b_spec = pl.BlockSpec((tn, tk), lambda i, j, k: (j, k))
        else:
            b_spec = pl.BlockSpec((tk, tn), lambda i, j, k: (k, j))
    else:
        lay = view.layer
        rc = (lambda j, k: (j, k)) if mode == "nt" else (lambda j, k: (k, j))
        blk = (tn, tk) if mode == "nt" else (tk, tn)
        if view.kind == "rows":
            per = srows // blk[0]
            b_spec = pl.BlockSpec((None, None) + blk,
                                  lambda i, j, k: (rc(j, k)[0] // per, lay, rc(j, k)[0] % per, rc(j, k)[1]))
        else:
            per = scols // blk[1]
            b_spec = pl.BlockSpec((None, None) + blk,
                                  lambda i, j, k: (rc(j, k)[1] // per, lay, rc(j, k)[0], rc(j, k)[1] % per))
    ex_specs = [pl.BlockSpec((1, tn), lambda i, j, k: (0, j)) if e.shape[0] == 1
                else pl.BlockSpec((tm, tn), lambda i, j, k: (i, j)) for e in extras]
    n_ex, n_out = len(extras), len(out_dtypes)

    def body(*refs):
        a_ref, b_ref = refs[:2]
        ex = refs[2:2 + n_ex]
        outs = refs[2 + n_ex:2 + n_ex + n_out]
        acc = outs[0] if acc_in_out else refs[-1]
        k = pl.program_id(2)
        prod = lax.dot_general(a_ref[...].astype(BF16), b_ref[...].astype(BF16), (dims, ((), ())),
                               preferred_element_type=F32)

        def finish(res):
            vals = epilogue(res, *[e[...] for e in ex]) if epilogue is not None else (res,)
            for o, v in zip(outs, vals):
                o[...] = v.astype(o.dtype)

        if nk == 1:
            finish(prod)
        else:
            @pl.when(k == 0)
            def _():
                acc[...] = prod

            @pl.when((k > 0) & (k < nk - 1))
            def _():
                acc[...] += prod

            @pl.when(k == nk - 1)
            def _():
                finish(acc[...] + prod)

    scratch = [] if (acc_in_out or nk == 1) else [pltpu.VMEM((tm, tn), F32)]
    if out_col_shards == 1:
        out_specs = [pl.BlockSpec((tm, tn), lambda i, j, k: (i, j))] * n_out
        out_shape = [jax.ShapeDtypeStruct((m, n), d) for d in out_dtypes]
    else:
        per_out = n_unit // tn
        out_specs = [pl.BlockSpec((None, tm, tn), lambda i, j, k: (j // per_out, i, j % per_out))]
        out_shape = [jax.ShapeDtypeStruct((out_col_shards, m, n_unit), out_dtypes[0])]
    outs = pl.pallas_call(
        body, grid=(m // tm, n // tn, nk), in_specs=[a_spec, b_spec, *ex_specs], out_specs=out_specs,
        out_shape=out_shape, scratch_shapes=scratch, compiler_params=_cparams(3), name=name)(a, b_arr, *extras)
    return outs[0] if n_out == 1 else outs


def _ep_add(acc, r):
    return (acc + r,)


def _ep_bias_add(acc, b, r):
    return (acc + b + r,)


def _ep_relu2(acc):
    return acc, jnp.square(jnp.maximum(acc, 0.0))


def _ep_relu2_bwd(acc, u):
    return (acc * (2.0 * jnp.maximum(u, 0.0)),)


def _rowwise(fn, rows, consts, outs, *, ncb=1, name):
    nrow = rows[0][0].shape[0]
    row_bytes = sum(w * arr.dtype.itemsize for arr, w, _ in rows)
    row_bytes += sum(o[2] * jnp.dtype(o[3]).itemsize for o in outs if o[0] == 'row')
    t = nrow
    for cand in (1024, 512, 256, 128, 64, 32, 16):
        if nrow % cand == 0:
            t = cand
            if cand * row_bytes <= ROW_BLOCK_BYTES:
                break
    in_specs = [pl.BlockSpec((t, w), (lambda j, i, off=off: (i, off + j))) for _, w, off in rows]
    for arr, per_col in consts:
        r, wc = arr.shape
        if per_col:
            in_specs.append(pl.BlockSpec((r, wc // ncb), lambda j, i: (0, j)))
        else:
            in_specs.append(pl.BlockSpec((r, wc), lambda j, i: (0, 0)))
    out_specs, out_shape = [], []
    for o in outs:
        if o[0] == 'row':
            out_specs.append(pl.BlockSpec((t, o[2]), lambda j, i: (i, j)))
            out_shape.append(jax.ShapeDtypeStruct((nrow, o[1]), o[3]))
        else:
            r, wt = o[1]
            if o[2]:
                out_specs.append(pl.BlockSpec((r, wt // ncb), lambda j, i: (0, j)))
            else:
                out_specs.append(pl.BlockSpec((r, wt), lambda j, i: (0, 0)))
            out_shape.append(jax.ShapeDtypeStruct((r, wt), F32))
    nin = len(rows) + len(consts)

    def body(*refs):
        j, i = pl.program_id(0), pl.program_id(1)
        vals = fn(*[r[...] for r in refs[:nin]])

        def store(spec, ref, v):
            if spec[0] == 'row':
                ref[...] = v.astype(ref.dtype)
            else:
                first = (i == 0) if spec[2] else ((i == 0) & (j == 0))

                @pl.when(first)
                def _():
                    ref[...] = jnp.zeros_like(ref)

                ref[...] += v

        for spec, ref, v in zip(outs, refs[nin:], vals):
            store(spec, ref, v)

    res = pl.pallas_call(body, grid=(ncb, nrow // t), in_specs=in_specs, out_specs=out_specs, out_shape=out_shape,
                         compiler_params=_cparams(2), name=name)(*[r[0] for r in rows], *[c[0] for c in consts])
    return res


def _rms_f(eps):
    def f(x, w):
        return x * lax.rsqrt(jnp.mean(x * x, axis=-1, keepdims=True) + eps) * w
    return f


def _rms_fwd(x, w, name):
    d = x.shape[1]
    f = _rms_f(RMS_EPS)
    return _rowwise(lambda xv, wv: (f(xv, wv),), [(x, d, 0)], [(w, False)], [('row', d, d, BF16)], name=name)[0]


def _rms_bwd(x, w, dh, dres, name):
    d = x.shape[1]
    f = _rms_f(RMS_EPS)

    def fn(xv, dhv, drv, wv):
        _, vjp = jax.vjp(f, xv, wv)
        dx, dw = vjp(dhv)
        return dx + drv, dx + drv, dw

    return _rowwise(fn, [(x, d, 0), (dh, d, 0), (dres, d, 0)], [(w, False)],
                    [('row', d, d, F32), ('row', d, d, BF16), ('acc', (1, d), False)], name=name)


def _colsum(x, name):
    w = x.shape[1]
    return _rowwise(lambda v: (jnp.sum(v, axis=0, keepdims=True),), [(x, w, 0)], [], [('acc', (1, w), False)],
                    name=name)[0]


def _loss(y, tgt, name):
    d = y.shape[1]

    def fn(yv, tv):
        e = yv - tv
        s = jnp.sum(jnp.sum(e * e, axis=1, keepdims=True), axis=0, keepdims=True) * (0.5 / d)
        return e * (1.0 / d), e * (1.0 / d), s + jnp.zeros((1, LANES), F32)

    return _rowwise(fn, [(y, d, 0), (tgt, d, 0)], [],
                    [('row', d, d, F32), ('row', d, d, BF16), ('acc', (1, LANES), False)], name=name)


def _conv_tiles(x, w):
    nrow, ch = x.shape
    kw = w.shape[0]
    halo = 8 * ((kw - 1 + 7) // 8)
    few_taps = kw <= 8
    t = _tile(nrow, (512, 256, 128, 64, 32) if few_taps else (256, 128, 64, 32))
    tc = _tile(ch, (1024, 512, 256, 128) if few_taps else (512, 256, 128))
    assert t % halo == 0 and nrow % t == 0
    return nrow, ch, kw, halo, t, tc


CONV_SUB = 32


def _shifted_windows(buf, shifts, t, base):
    out = {}
    for b in range(8):
        group = [s for s in shifts if s % 8 == b]
        if group:
            rows = buf[base + b:base + max(group) + t, :]
            for s in group:
                out[s] = rows[s - b:s - b + t]
    return out


def _fold8(v):
    out = v[0:8]
    for r in range(8, v.shape[0], 8):
        out = out + v[r:r + 8]
    return out


def _dwconv_fwd(x, w, b, name):
    nrow, ch, kw, halo, t, tc = _conv_tiles(x, w)
    per = t // halo
    sub = min(CONV_SUB, t)

    def body(x_ref, halo_ref, w_ref, b_ref, y_ref, buf):
        i = pl.program_id(1)
        buf[0:halo, :] = jnp.where(i == 0, 0.0, halo_ref[...])
        buf[halo:halo + t, :] = x_ref[...]
        shifts = [halo - (kw - 1) + k for k in range(kw)]
        for r0 in range(0, t, sub):
            win = _shifted_windows(buf, shifts, sub, r0)
            acc = jnp.zeros((sub, tc), F32) + b_ref[...]
            for k in range(kw):
                acc = acc + w_ref[k:k + 1, :] * win[shifts[k]]
            y_ref[r0:r0 + sub, :] = acc

    return pl.pallas_call(
        body, grid=(ch // tc, nrow // t),
        in_specs=[pl.BlockSpec((t, tc), lambda j, i: (i, j)),
                  pl.BlockSpec((halo, tc), lambda j, i: (jnp.maximum(i * per - 1, 0), j)),
                  pl.BlockSpec((kw, tc), lambda j, i: (0, j)), pl.BlockSpec((1, tc), lambda j, i: (0, j))],
        out_specs=pl.BlockSpec((t, tc), lambda j, i: (i, j)), out_shape=jax.ShapeDtypeStruct((nrow, ch), F32),
        scratch_shapes=[pltpu.VMEM((t + halo, tc), F32)], compiler_params=_cparams(2), name=name)(x, x, w, b)


def _dwconv_bwd(x, dy, w, name, dx_dtype=F32):
    nrow, ch, kw, halo, t, tc = _conv_tiles(x, w)
    per = t // halo
    sub = min(CONV_SUB, t)
    nt = nrow // t
    nhalo = nrow // halo

    def body(x_ref, xh_ref, dy_ref, dyh_ref, w_ref, dx_ref, dw_ref, db_ref, xbuf, dybuf):
        i = pl.program_id(1)
        xbuf[0:halo, :] = jnp.where(i == 0, 0.0, xh_ref[...])
        xbuf[halo:halo + t, :] = x_ref[...]
        dyc = dy_ref[...]
        dybuf[0:t, :] = dyc
        dybuf[t:t + halo, :] = jnp.where(i == nt - 1, 0.0, dyh_ref[...])
        @pl.when(i == 0)
        def _():
            dw_ref[...] = jnp.zeros_like(dw_ref)
            db_ref[...] = jnp.zeros_like(db_ref)

        dw_part = [jnp.zeros((8, tc), F32) for _ in range(kw)]
        for r0 in range(0, t, sub):
            dywin = _shifted_windows(dybuf, list(range(kw)), sub, r0)
            acc = jnp.zeros((sub, tc), F32)
            for k in range(kw):
                acc = acc + w_ref[k:k + 1, :] * dywin[kw - 1 - k]
            dx_ref[r0:r0 + sub, :] = acc.astype(dx_ref.dtype)
            xwin = _shifted_windows(xbuf, [halo - s for s in range(kw)], sub, r0)
            for k in range(kw):
                dw_part[k] = dw_part[k] + _fold8(dywin[0] * xwin[halo - (kw - 1 - k)])
        for k in range(kw):
            dw_ref[k:k + 1, :] += jnp.sum(dw_part[k], axis=0, keepdims=True)
        db_ref[...] += jnp.sum(dyc, axis=0, keepdims=True)

    return pl.pallas_call(
        body, grid=(ch // tc, nt),
        in_specs=[pl.BlockSpec((t, tc), lambda j, i: (i, j)),
                  pl.BlockSpec((halo, tc), lambda j, i: (jnp.maximum(i * per - 1, 0), j)),
                  pl.BlockSpec((t, tc), lambda j, i: (i, j)),
                  pl.BlockSpec((halo, tc), lambda j, i: (jnp.minimum((i + 1) * per, nhalo - 1), j)),
                  pl.BlockSpec((kw, tc), lambda j, i: (0, j))],
        out_specs=[pl.BlockSpec((t, tc), lambda j, i: (i, j)), pl.BlockSpec((kw, tc), lambda j, i: (0, j)),
                   pl.BlockSpec((1, tc), lambda j, i: (0, j))],
        out_shape=[jax.ShapeDtypeStruct((nrow, ch), dx_dtype), jax.ShapeDtypeStruct((kw, ch), F32),
                   jax.ShapeDtypeStruct((1, ch), F32)],
        scratch_shapes=[pltpu.VMEM((t + halo, tc), F32), pltpu.VMEM((t + halo, tc), F32)],
        compiler_params=_cparams(2), name=name)(x, x, dy, dy, w)


SB_SUB = 128
LOG2E = 1.4426950408889634
SB_SKIP_BELOW = -256.0
SB_UNVISITED = -1e30


def _dot_split2(x, u):
    hi = x.astype(BF16)
    lo = (x - hi.astype(F32)).astype(BF16)
    return jnp.dot(hi, u, preferred_element_type=F32) + jnp.dot(lo, u, preferred_element_type=F32)


def _sb_scores(q, k, strict, scale):
    z = lax.dot_general(q, k, (((1,), (1,)), ((), ())), preferred_element_type=F32) * (scale * LOG2E)
    soft = jnp.log2(1.0 + jnp.exp2(-jnp.abs(z)))
    ls = jnp.minimum(z, 0.0) - soft
    lk = ls - z
    if strict is not None:
        lk = jnp.where(strict, lk, 0.0)
    return ls, lk


def _sb_running(x, tri, start, reverse):
    nsub = x.shape[1] // SB_SUB
    parts = [x[:, SB_SUB * b:SB_SUB * (b + 1)] for b in range(nsub)]
    out = [None] * nsub
    run = start
    for b in (reversed(range(nsub)) if reverse else range(nsub)):
        out[b] = _dot_split2(parts[b], tri) + run
        run = run + jnp.sum(parts[b], axis=1, keepdims=True)
    return (jnp.concatenate(out, axis=1) if nsub > 1 else out[0]), run


def _sb_dims(qn):
    nrow, d = qn.shape
    hd = SB_HEAD_DIM
    t = _tile(nrow, (512, 256, 128))
    assert nrow % t == 0 and nrow // t <= LANES
    return nrow, d, hd, d // hd, t


def _sb_masks(t):
    strict = lax.broadcasted_iota(jnp.int32, (t, t), 1) < lax.broadcasted_iota(jnp.int32, (t, t), 0)
    r0 = lax.broadcasted_iota(jnp.int32, (SB_SUB, SB_SUB), 0)
    c0 = lax.broadcasted_iota(jnp.int32, (SB_SUB, SB_SUB), 1)
    return strict, (r0 > c0).astype(BF16), (r0 < c0).astype(BF16)


def _sb_attn_fwd(qn, kn, qkv, name):
    nrow, d, hd, nh, t = _sb_dims(qn)
    scale = 1.0 / math.sqrt(hd)

    def body(q_ref, k_ref, v_ref, o_ref, rs_ref, r_acc, o_acc):
        i = pl.program_id(1)
        q = q_ref[...]
        strict, after, _ = _sb_masks(t)
        lane = lax.broadcasted_iota(jnp.int32, (t, LANES), 1)
        r_acc[...] = jnp.zeros_like(r_acc)
        o_acc[...] = jnp.zeros_like(o_acc)
        rs_ref[...] = jnp.full(rs_ref.shape, SB_UNVISITED, F32)

        def tile(j, mask):
            start = pl.multiple_of(j * t, t)
            k = k_ref[pl.ds(start, t), :]
            v = v_ref[pl.ds(start, t), :].astype(BF16)
            ls, lk = _sb_scores(q, k, mask, scale)
            r = r_acc[...]
            later, r_next = _sb_running(lk, after, r, True)
            att = jnp.exp2(ls + later)
            if mask is not None:
                att = jnp.where(mask, att, 0.0)
            o_acc[...] += jnp.dot(att.astype(BF16), v, preferred_element_type=F32)
            rs_ref[...] = jnp.where(lane == j, r, rs_ref[...])
            r_acc[...] = r_next

        tile(i, strict)

        def more(carry):
            n, r_max = carry
            return (n < i) & (r_max > SB_SKIP_BELOW)

        def step(carry):
            n, _ = carry
            tile(i - 1 - n, None)
            return n + 1, jnp.max(r_acc[...])

        lax.while_loop(more, step, (jnp.int32(0), jnp.max(r_acc[...])))
        o_ref[...] = o_acc[...].astype(o_ref.dtype)

    return pl.pallas_call(
        body, grid=(nh, nrow // t),
        in_specs=[pl.BlockSpec((t, hd), lambda h, i: (i, h)), pl.BlockSpec((nrow, hd), lambda h, i: (0, h)),
                  pl.BlockSpec((nrow, hd), lambda h, i: (0, 2 * nh + h))],
        out_specs=[pl.BlockSpec((t, hd), lambda h, i: (i, h)), pl.BlockSpec((None, t, LANES), lambda h, i: (h, i, 0))],
        out_shape=[jax.ShapeDtypeStruct((nrow, d), BF16), jax.ShapeDtypeStruct((nh, nrow, LANES), F32)],
        scratch_shapes=[pltpu.VMEM((t, LANES), F32), pltpu.VMEM((t, hd), F32)],
        compiler_params=_cparams(2), name=name)(qn, kn, qkv)


def _sb_attn_bwd(qn, kn, qkv, do, rsave, name):
    nrow, d, hd, nh, t = _sb_dims(qn)
    scale = 1.0 / math.sqrt(hd)

    def body(q_ref, k_ref, v_ref, do_ref, rs_ref, dq_ref, dk_ref, dv_ref, pg_acc):
        i = pl.program_id(1)

        @pl.when(i == 0)
        def _():
            dk_ref[...] = jnp.zeros_like(dk_ref)
            dv_ref[...] = jnp.zeros_like(dv_ref)

        pg_acc[...] = jnp.zeros_like(pg_acc)
        dq_ref[...] = jnp.zeros_like(dq_ref)
        q = q_ref[...]
        dob = do_ref[...].astype(BF16)
        strict, after, before = _sb_masks(t)
        lane = lax.broadcasted_iota(jnp.int32, (t, LANES), 1)

        def tile(j, mask):
            start = pl.multiple_of(j * t, t)
            k = k_ref[pl.ds(start, t), :]
            v = v_ref[pl.ds(start, t), :].astype(BF16)
            ls, lk = _sb_scores(q, k, mask, scale)
            rj = jnp.sum(jnp.where(lane == j, rs_ref[...], 0.0), axis=1, keepdims=True)
            later, _ = _sb_running(lk, after, jnp.broadcast_to(rj, (t, LANES)), True)
            att = jnp.exp2(ls + later)
            if mask is not None:
                att = jnp.where(mask, att, 0.0)
            datt = lax.dot_general(dob, v, (((1,), (1,)), ((), ())), preferred_element_type=F32)
            g = datt * att
            dlk, pg_next = _sb_running(g, before, pg_acc[...], False)
            sig = jnp.exp2(ls)
            dz = g * (1.0 - sig) - dlk * sig
            if mask is not None:
                dz = jnp.where(mask, dz, 0.0)
            dz = (dz * scale).astype(BF16)
            dq_ref[...] += jnp.dot(dz, k, preferred_element_type=F32)
            dk_ref[pl.ds(start, t), :] += lax.dot_general(dz, q, (((0,), (0,)), ((), ())), preferred_element_type=F32)
            dv_ref[pl.ds(start, t), :] += lax.dot_general(att.astype(BF16), dob, (((0,), (0,)), ((), ())),
                                                          preferred_element_type=F32)
            pg_acc[...] = pg_next

        def step(j, carry):
            tile(j, None)
            return carry

        seen = jnp.max(rs_ref[...], axis=0, keepdims=True)
        lane1 = lax.broadcasted_iota(jnp.int32, (1, LANES), 1)
        first = jnp.sum(jnp.where((lane1 < i) & (seen < 0.5 * SB_UNVISITED), 1.0, 0.0)).astype(jnp.int32)
        lax.fori_loop(first, i, step, 0)
        tile(i, strict)

    return pl.pallas_call(
        body, grid=(nh, nrow // t),
        in_specs=[pl.BlockSpec((t, hd), lambda h, i: (i, h)), pl.BlockSpec((nrow, hd), lambda h, i: (0, h)),
                  pl.BlockSpec((nrow, hd), lambda h, i: (0, 2 * nh + h)), pl.BlockSpec((t, hd), lambda h, i: (i, h)),
                  pl.BlockSpec((None, t, LANES), lambda h, i: (h, i, 0))],
        out_specs=[pl.BlockSpec((t, hd), lambda h, i: (i, h)), pl.BlockSpec((nrow, hd), lambda h, i: (0, h)),
                   pl.BlockSpec((nrow, hd), lambda h, i: (0, h))],
        out_shape=[jax.ShapeDtypeStruct((nrow, d), F32)] * 3, scratch_shapes=[pltpu.VMEM((t, LANES), F32)],
        compiler_params=_cparams(2), name=name)(qn, kn, qkv, do, rsave)


def _ssd_chunk_fn(g, nheads_g, q):
    p = M2_HEAD_DIM
    npair = nheads_g // 2

    def f(xp, bp, cp, dtr, hp, dtb, alog, dsk):
        b = _silu(bp).astype(BF16)
        c = _silu(cp).astype(BF16)
        dt = _softplus(dtr + dtb)
        dta = dt * (-jnp.exp(alog))
        ri = lax.broadcasted_iota(jnp.int32, (q, q), 0)
        ci = lax.broadcasted_iota(jnp.int32, (q, q), 1)
        causal = ri >= ci
        tri = causal.astype(F32)
        acol = jnp.dot(tri, dta, precision=HIGHEST, preferred_element_type=F32)
        arow = lax.dot_general(dta, tri, (((0,), (1,)), ((), ())), precision=HIGHEST,
                               preferred_element_type=F32)
        alast = jnp.sum(dta, axis=0, keepdims=True)
        cb = lax.dot_general(c, b, (((1,), (1,)), ((), ())), preferred_element_type=F32)
        lane = lax.broadcasted_iota(jnp.int32, (1, LANES), 1)
        sub = lax.broadcasted_iota(jnp.int32, (LANES, 1), 0)
        lane_half = lane < p
        sub_half = sub < p

        def col(mat, h):
            return jnp.sum(jnp.where(lane == h, mat, 0.0), axis=1, keepdims=True)

        def row(mat, h):
            return jnp.sum(jnp.where(sub == h, mat, 0.0), axis=0, keepdims=True)

        def mix(v0, v1):
            return jnp.where(lane_half, v0, v1)

        ys, hns = [], []
        for jp in range(npair):
            h0 = g * nheads_g + 2 * jp
            h1 = h0 + 1
            x = _silu(xp[jp])
            ac0, ac1 = col(acol, h0), col(acol, h1)
            xdt = x * mix(col(dt, h0), col(dt, h1))
            xdtb = xdt.astype(BF16)
            yd = []
            for h, ac in ((h0, ac0), (h1, ac1)):
                dec = jnp.exp(jnp.where(causal, ac - row(arow, h), -jnp.inf))
                yd.append(jnp.dot((cb * dec).astype(BF16), xdtb, preferred_element_type=F32))
            hpj = hp[jp]
            yo = lax.dot_general(c, hpj.astype(BF16), (((1,), (1,)), ((), ())), preferred_element_type=F32)
            y = mix(yd[0], yd[1]) + yo * jnp.exp(mix(ac0, ac1)) + mix(col(dsk, h0), col(dsk, h1)) * x
            al0, al1 = col(alast, h0), col(alast, h1)
            dout = jnp.exp(mix(al0 - ac0, al1 - ac1))
            st = lax.dot_general((xdt * dout).astype(BF16), b, (((0,), (0,)), ((), ())), preferred_element_type=F32)
            hns.append(hpj * jnp.exp(jnp.where(sub_half, al0, al1)) + st)
            ys.append(y)
        return ys, hns

    return f


def _ssd_dims(pre, dtr):
    nrow = pre.shape[0]
    ng, n, p, q = M2_GROUPS, M2_STATE, M2_HEAD_DIM, M2_CHUNK
    d_inner = pre.shape[1] - 2 * ng * n
    gw = d_inner // ng
    nhg = gw // p
    assert nhg % 2 == 0 and gw % LANES == 0 and n == LANES and ng * nhg <= LANES and dtr.shape[1] == LANES
    return nrow, ng, n, q, d_inner, gw, nhg, nrow // q


def _ssd_fwd(pre, dtr, dtb, alog, dsk, name):
    nrow, ng, n, q, d_inner, gw, nhg, nc = _ssd_dims(pre, dtr)
    boff = d_inner // n
    npair = nhg // 2

    def body(x_ref, b_ref, c_ref, dt_ref, dtb_ref, al_ref, ds_ref, y_ref, hs_ref, state):
        ci, g = pl.program_id(0), pl.program_id(1)

        @pl.when(ci == 0)
        def _():
            state[g] = jnp.zeros((gw, n), F32)

        hs_ref[...] = state[g]
        f = _ssd_chunk_fn(g, nhg, q)
        xp = [x_ref[:, LANES * j:LANES * (j + 1)] for j in range(npair)]
        hp = [state[g, LANES * j:LANES * (j + 1), :] for j in range(npair)]
        ys, hns = f(xp, b_ref[...], c_ref[...], dt_ref[...], hp, dtb_ref[...], al_ref[...], ds_ref[...])
        for j in range(npair):
            y_ref[:, LANES * j:LANES * (j + 1)] = ys[j]
            state[g, LANES * j:LANES * (j + 1), :] = hns[j]

    par = pl.BlockSpec((1, LANES), lambda ci, g: (0, 0))
    return pl.pallas_call(
        body, grid=(nc, ng),
        in_specs=[pl.BlockSpec((q, gw), lambda ci, g: (ci, g)), pl.BlockSpec((q, n), lambda ci, g: (ci, boff + g)),
                  pl.BlockSpec((q, n), lambda ci, g: (ci, boff + ng + g)), pl.BlockSpec((q, LANES), lambda ci, g: (ci, 0)),
                  par, par, par],
        out_specs=[pl.BlockSpec((q, gw), lambda ci, g: (ci, g)),
                   pl.BlockSpec((None, None, gw, n), lambda ci, g: (ci, g, 0, 0))],
        out_shape=[jax.ShapeDtypeStruct((nrow, d_inner), F32), jax.ShapeDtypeStruct((nc, ng, gw, n), F32)],
        scratch_shapes=[pltpu.VMEM((ng, gw, n), F32)], compiler_params=_cparams(2), name=name)(
            pre, pre, pre, dtr, dtb, alog, dsk)


def _ssd_bwd(pre, dtr, hs, dy, dtb, alog, dsk, name):
    nrow, ng, n, q, d_inner, gw, nhg, nc = _ssd_dims(pre, dtr)
    boff = d_inner // n
    npair = nhg // 2

    def body(x_ref, b_ref, c_ref, dt_ref, hs_ref, dy_ref, dtb_ref, al_ref, ds_ref,
             dx_ref, db_ref, dc_ref, ddt_ref, ddtb_ref, dal_ref, dds_ref, dstate):
        ci, g = pl.program_id(0), pl.program_id(1)

        @pl.when(ci == 0)
        def _():
            dstate[g] = jnp.zeros((gw, n), F32)

        @pl.when((ci == 0) & (g == 0))
        def _():
            ddtb_ref[...] = jnp.zeros_like(ddtb_ref)
            dal_ref[...] = jnp.zeros_like(dal_ref)
            dds_ref[...] = jnp.zeros_like(dds_ref)

        @pl.when(g == 0)
        def _():
            ddt_ref[...] = jnp.zeros_like(ddt_ref)

        f = _ssd_chunk_fn(g, nhg, q)
        xp = [x_ref[:, LANES * j:LANES * (j + 1)] for j in range(npair)]
        hp = [hs_ref[LANES * j:LANES * (j + 1), :] for j in range(npair)]
        _, vjp = jax.vjp(f, xp, b_ref[...], c_ref[...], dt_ref[...], hp, dtb_ref[...], al_ref[...], ds_ref[...])
        dys = [dy_ref[:, LANES * j:LANES * (j + 1)] for j in range(npair)]
        dhn = [dstate[g, LANES * j:LANES * (j + 1), :] for j in range(npair)]
        dxp, db, dc, ddt, dhp, ddtb, dal, dds = vjp((dys, dhn))
        for j in range(npair):
            dx_ref[:, LANES * j:LANES * (j + 1)] = dxp[j]
            dstate[g, LANES * j:LANES * (j + 1), :] = dhp[j]
        db_ref[...] = db
        dc_ref[...] = dc
        ddt_ref[...] += ddt
        ddtb_ref[...] += ddtb
        dal_ref[...] += dal
        dds_ref[...] += dds

    par = pl.BlockSpec((1, LANES), lambda ci, g: (0, 0))
    last = nc - 1
    return pl.pallas_call(
        body, grid=(nc, ng),
        in_specs=[pl.BlockSpec((q, gw), lambda ci, g: (last - ci, g)),
                  pl.BlockSpec((q, n), lambda ci, g: (last - ci, boff + g)),
                  pl.BlockSpec((q, n), lambda ci, g: (last - ci, boff + ng + g)),
                  pl.BlockSpec((q, LANES), lambda ci, g: (last - ci, 0)),
                  pl.BlockSpec((None, None, gw, n), lambda ci, g: (last - ci, g, 0, 0)),
                  pl.BlockSpec((q, gw), lambda ci, g: (last - ci, g)), par, par, par],
        out_specs=[pl.BlockSpec((q, gw), lambda ci, g: (last - ci, g)), pl.BlockSpec((q, n), lambda ci, g: (last - ci, g)),
                   pl.BlockSpec((q, n), lambda ci, g: (last - ci, g)), pl.BlockSpec((q, LANES), lambda ci, g: (last - ci, 0)),
                   par, par, par],
        out_shape=[jax.ShapeDtypeStruct((nrow, d_inner), F32), jax.ShapeDtypeStruct((nrow, ng * n), F32),
                   jax.ShapeDtypeStruct((nrow, ng * n), F32), jax.ShapeDtypeStruct((nrow, LANES), F32),
                   jax.ShapeDtypeStruct((1, LANES), F32), jax.ShapeDtypeStruct((1, LANES), F32),
                   jax.ShapeDtypeStruct((1, LANES), F32)],
        scratch_shapes=[pltpu.VMEM((ng, gw, n), F32)], compiler_params=_cparams(2), name=name)(
            pre, pre, pre, dtr, hs, dy, dtb, alog, dsk)


HBM_SPEC = pl.BlockSpec(memory_space=pl.ANY)


def _xy_peers(mx, my):
    return [(1 - mx, my), (mx, 1 - my), (1 - mx, 1 - my)]


def _remote(src, dst, send_sems, recv_sems, k, dev):
    return pltpu.make_async_remote_copy(src_ref=src, dst_ref=dst, send_sem=send_sems.at[k], recv_sem=recv_sems.at[k],
                                        device_id=dev, device_id_type=MESH_ID)


def _comm_call(body, arrays, out_shapes, ncopies, name):
    nin = len(arrays)

    def wrapped(*refs):
        body(refs[:nin], refs[nin:nin + len(out_shapes)], refs[-2], refs[-1])

    return pl.pallas_call(
        wrapped, out_shape=out_shapes, in_specs=[HBM_SPEC] * nin, out_specs=[HBM_SPEC] * len(out_shapes),
        scratch_shapes=[pltpu.SemaphoreType.DMA((ncopies,)), pltpu.SemaphoreType.DMA((ncopies,))], name=name)(*arrays)


def _all_gather_xy(bufs, name):
    npeer = N_XY - 1

    def body(srcs, outs, send_sems, recv_sems):
        mx, my, mc = lax.axis_index("x"), lax.axis_index("y"), lax.axis_index("c")
        me = 2 * mx + my
        sibling = (mx, my, 1 - mc)
        peers = _xy_peers(mx, my)
        sends = []
        for a, (src, out) in enumerate(zip(srcs, outs)):
            for k, (px, py) in enumerate(peers):
                sends.append(_remote(src.at[mc], out.at[me, mc], send_sems, recv_sems, 2 * npeer * a + k, (px, py, mc)))
        for cp in sends:
            cp.start()
        for a, (src, out) in enumerate(zip(srcs, outs)):
            for k, (px, py) in enumerate(peers):
                landed = out.at[2 * px + py, mc]
                _remote(src.at[mc], landed, send_sems, recv_sems, 2 * npeer * a + k, (px, py, mc)).wait_recv()
                cp = _remote(landed, landed, send_sems, recv_sems, 2 * npeer * a + npeer + k, sibling)
                cp.start()
                sends.append(cp)
        for a, out in enumerate(outs):
            for k, (px, py) in enumerate(peers):
                passed = out.at[2 * px + py, 1 - mc]
                _remote(passed, passed, send_sems, recv_sems, 2 * npeer * a + npeer + k, sibling).wait_recv()
        for cp in sends:
            cp.wait_send()

    outs = _comm_call(body, bufs, [jax.ShapeDtypeStruct((N_XY,) + b.shape, b.dtype) for b in bufs],
                      2 * npeer * len(bufs), name)
    return [lax.dynamic_update_index_in_dim(o, b, _my_shard(), 0) for o, b in zip(outs, bufs)]


def _my_shard():
    return 2 * lax.axis_index("x") + lax.axis_index("y")


def _exchange_xy(parts, name):
    npeer = N_XY - 1

    def body(srcs, outs, send_sems, recv_sems):
        mx, my, mc = lax.axis_index("x"), lax.axis_index("y"), lax.axis_index("c")
        me = 2 * mx + my
        peers = _xy_peers(mx, my)
        sends = [_remote(src.at[2 * px + py], out.at[me], send_sems, recv_sems, npeer * a + k, (px, py, mc))
                 for k, (px, py) in enumerate(peers) for a, (src, out) in enumerate(zip(srcs, outs))]
        for cp in sends:
            cp.start()
        for k, (px, py) in enumerate(peers):
            for a, (src, out) in enumerate(zip(srcs, outs)):
                _remote(src.at[me], out.at[2 * px + py], send_sems, recv_sems, npeer * a + k, (px, py, mc)).wait_recv()
        for cp in sends:
            cp.wait_send()

    outs = _comm_call(body, parts, [jax.ShapeDtypeStruct(p.shape, p.dtype) for p in parts], npeer * len(parts), name)
    me = _my_shard()
    return [lax.dynamic_update_index_in_dim(o, lax.dynamic_index_in_dim(p, me, 0, keepdims=False), me, 0)
            for o, p in zip(outs, parts)]


def _pair_split(parts, name):
    def body(srcs, gots, send_sems, recv_sems):
        mx, my, mc = lax.axis_index("x"), lax.axis_index("y"), lax.axis_index("c")
        copies, idx = [], 0
        for src, got in zip(srcs, gots):
            for s in range(src.shape[0]):
                copies.append(_remote(src.at[s, 1 - mc], got.at[s], send_sems, recv_sems, idx, (mx, my, 1 - mc)))
                idx += 1
        for cp in copies:
            cp.start()
        for cp in copies:
            cp.wait()

    gots = _comm_call(body, parts, [jax.ShapeDtypeStruct((p.shape[0],) + p.shape[2:], p.dtype) for p in parts],
                      sum(p.shape[0] for p in parts), name)
    return gots


def _pair_join(mines, name):
    def body(srcs, outs, send_sems, recv_sems):
        mx, my, mc = lax.axis_index("x"), lax.axis_index("y"), lax.axis_index("c")
        copies = [_remote(src, out, send_sems, recv_sems, a, (mx, my, 1 - mc))
                  for a, (src, out) in enumerate(zip(srcs, outs))]
        for cp in copies:
            cp.start()
        for cp in copies:
            cp.wait()

    others = _comm_call(body, mines, [jax.ShapeDtypeStruct(m.shape, m.dtype) for m in mines], len(mines), name)
    first = lax.axis_index("c") == 0
    return [jnp.where(first, jnp.stack([m, o]), jnp.stack([o, m])) for m, o in zip(mines, others)]


def _all_gather_all(buf, name):
    flips = [(fx, fy, fc) for fx in (0, 1) for fy in (0, 1) for fc in (0, 1) if fx or fy or fc]

    def body(src, out, send_sems, recv_sems):
        mx, my, mc = lax.axis_index("x"), lax.axis_index("y"), lax.axis_index("c")
        me = 4 * mx + 2 * my + mc
        peers = [(1 - mx if fx else mx, 1 - my if fy else my, 1 - mc if fc else mc) for fx, fy, fc in flips]
        sends = [_remote(src, out.at[me], send_sems, recv_sems, k, dev) for k, dev in enumerate(peers)]
        for cp in sends:
            cp.start()
        for k, (px, py, pc) in enumerate(peers):
            _remote(src, out.at[4 * px + 2 * py + pc], send_sems, recv_sems, k, (px, py, pc)).wait_recv()
        for cp in sends:
            cp.wait_send()

    out = pl.pallas_call(
        body, out_shape=jax.ShapeDtypeStruct((N_DEV,) + buf.shape, buf.dtype), in_specs=[HBM_SPEC], out_specs=HBM_SPEC,
        scratch_shapes=[pltpu.SemaphoreType.DMA((N_DEV - 1,)), pltpu.SemaphoreType.DMA((N_DEV - 1,))], name=name)(buf)
    me = 4 * lax.axis_index("x") + 2 * lax.axis_index("y") + lax.axis_index("c")
    return lax.dynamic_update_index_in_dim(out, buf, me, 0)


def _flat_tile(nrow, width, narrays):
    t = nrow
    for cand in (512, 256, 128, 64, 32, 16, 8):
        if nrow % cand == 0:
            t = cand
            if cand * width * 4 * narrays <= ROW_BLOCK_BYTES:
                break
    return t


def _sum_parts(parts, name):
    kparts, nrow, width = parts.shape
    t = _flat_tile(nrow, width, kparts + 1)

    def body(p_ref, o_ref):
        s = p_ref[0].astype(F32)
        for k in range(1, kparts):
            s = s + p_ref[k].astype(F32)
        o_ref[...] = s

    return pl.pallas_call(
        body, grid=(nrow // t,), in_specs=[pl.BlockSpec((kparts, t, width), lambda i: (0, i, 0))],
        out_specs=pl.BlockSpec((t, width), lambda i: (i, 0)), out_shape=jax.ShapeDtypeStruct((nrow, width), F32),
        compiler_params=_cparams(1), name=name)(parts)


def _add_half(parts, other, name):
    nparts, _, nrow, width = parts.shape
    t = _flat_tile(nrow, width, 3)

    def body(c_ref, a_ref, b_ref, o_ref):
        o_ref[...] = (a_ref[...].astype(F32) + b_ref[...].astype(F32)).astype(o_ref.dtype)

    spec = pl.BlockSpec((None, t, width), lambda p, i, c_ref: (p, i, 0))
    grid_spec = pltpu.PrefetchScalarGridSpec(
        num_scalar_prefetch=1, grid=(nparts, nrow // t),
        in_specs=[pl.BlockSpec((None, None, t, width), lambda p, i, c_ref: (p, c_ref[0], i, 0)), spec], out_specs=spec)
    core = lax.axis_index("c").astype(jnp.int32).reshape(1)
    return pl.pallas_call(body, grid_spec=grid_spec, out_shape=jax.ShapeDtypeStruct(other.shape, other.dtype),
                          compiler_params=_cparams(2), name=name)(core, parts, other)


def _adamw(w, m, v, ga, gb, name):
    nrow, width = w.shape
    t = _flat_tile(nrow, width, 9)
    c1 = 1.0 - ADAM_B1 ** ADAM_STEP
    c2 = 1.0 - ADAM_B2 ** ADAM_STEP
    grads = [ga] if gb is None else [ga, gb]
    nout = 3 if gb is None else 4

    def body(w_ref, m_ref, v_ref, *refs):
        d_ref, mo_ref, vo_ref = refs[-3:]
        g = refs[0][...]
        if gb is not None:
            g = g + refs[1][...]
            refs[2][...] = g
        mn = ADAM_B1 * m_ref[...] + (1.0 - ADAM_B1) * g
        vn = ADAM_B2 * v_ref[...] + (1.0 - ADAM_B2) * jnp.square(g)
        mo_ref[...] = mn
        vo_ref[...] = vn
        d_ref[...] = -ADAM_LR * ((mn / c1) / (jnp.sqrt(vn / c2) + ADAM_EPS) + ADAM_WD * w_ref[...])

    spec = pl.BlockSpec((t, width), lambda i: (i, 0))
    return pl.pallas_call(
        body, grid=(nrow // t,), in_specs=[spec] * (3 + len(grads)), out_specs=[spec] * nout,
        out_shape=[jax.ShapeDtypeStruct((nrow, width), F32)] * nout, compiler_params=_cparams(1), name=name)(
            w, m, v, *grads)


def _pack(arrs, dtype, width):
    flat = jnp.concatenate([a.astype(dtype).reshape(-1) for a in arrs])
    quantum = PACK_ROWS * width
    pad = (-flat.shape[0]) % quantum
    if pad:
        flat = jnp.concatenate([flat, jnp.zeros((pad,), dtype)])
    return flat.reshape(-1, width)


def _unpack(buf, shapes):
    flat = buf.reshape(-1)
    out, off = [], 0
    for s in shapes:
        size = math.prod(s)
        out.append(flat[off:off + size].reshape(s))
        off += size
    return out


QK_NORM_HEADS = 8


def _qk_norm_fwd(qkv, w, third, nh, tag):
    f = _rms_f(RMS_EPS)
    hd = SB_HEAD_DIM
    hpb = math.gcd(nh, QK_NORM_HEADS)

    def fn(xv, wv):
        return (jnp.concatenate([f(xv[:, hd * h:hd * (h + 1)], wv) for h in range(hpb)], axis=1),)

    return _rowwise(fn, [(qkv, hpb * hd, third * (nh // hpb))], [(w, False)], [('row', nh * hd, hpb * hd, BF16)],
                    ncb=nh // hpb, name=tag)[0]


def _qk_norm_bwd(qkv, w, third, dn, nh, tag):
    f = _rms_f(RMS_EPS)
    hd = SB_HEAD_DIM
    hpb = math.gcd(nh, QK_NORM_HEADS)

    def fn(xv, dv, wv):
        dxs, dw = [], jnp.zeros_like(wv)
        for h in range(hpb):
            _, vjp = jax.vjp(f, xv[:, hd * h:hd * (h + 1)], wv)
            dxh, dwh = vjp(dv[:, hd * h:hd * (h + 1)])
            dxs.append(dxh)
            dw = dw + dwh
        return jnp.concatenate(dxs, axis=1), dw

    return _rowwise(fn, [(qkv, hpb * hd, third * (nh // hpb)), (dn, hpb * hd, 0)], [(w, False)],
                    [('row', nh * hd, hpb * hd, BF16), ('acc', (1, hd), False)], ncb=nh // hpb, name=tag)


def _sb_forward(h, xin, wt, tag):
    d = xin.shape[1]
    nh = d // SB_HEAD_DIM
    qkv = _mm(h, wt['w_qkv'], "nn", name=tag + "_qkv")
    qn = _qk_norm_fwd(qkv, wt['q_norm_w'], 0, nh, tag + "_qnorm")
    kn = _qk_norm_fwd(qkv, wt['k_norm_w'], 1, nh, tag + "_knorm")
    o, rsave = _sb_attn_fwd(qn, kn, qkv, tag + "_attn")
    xm = _mm(o, wt['w_o'], "nn", extras=(xin,), epilogue=_ep_add, name=tag + "_out")
    return xm, dict(qkv=qkv, qn=qn, kn=kn, o=o, rsave=rsave)


def _sb_backward(h, dx, dxb, wt, sv, tag):
    nh = dx.shape[1] // SB_HEAD_DIM
    do = _mm(dxb, wt['w_o'], "nt", name=tag + "_do")
    g_wo = _mm(sv['o'], dxb, "tn", out_dtypes=(BF16,), name=tag + "_dwo")
    dqn, dkn, dv = _sb_attn_bwd(sv['qn'], sv['kn'], sv['qkv'], do, sv['rsave'], tag + "_attn_bwd")
    dq, g_qw = _qk_norm_bwd(sv['qkv'], wt['q_norm_w'], 0, dqn, nh, tag + "_qnorm_bwd")
    dk, g_kw = _qk_norm_bwd(sv['qkv'], wt['k_norm_w'], 1, dkn, nh, tag + "_knorm_bwd")
    dqkv = jnp.concatenate([dq, dk, dv.astype(BF16)], axis=1)
    g_wqkv = _mm(h, dqkv, "tn", out_dtypes=(BF16,), out_col_shards=N_XY, name=tag + "_dwqkv")
    dh = _mm(dqkv, wt['w_qkv'], "nt", name=tag + "_dh")
    return dh, dict(w_qkv=g_wqkv, w_o=g_wo, q_norm_w=g_qw, k_norm_w=g_kw)


def _ln_silu_f(x, w, b):
    mu = jnp.mean(x, axis=-1, keepdims=True)
    xc = x - mu
    return _silu(xc * lax.rsqrt(jnp.mean(xc * xc, axis=-1, keepdims=True) + LN_EPS) * w + b)


def _glu_bwd_fn(val, gate, dg):
    sg = 1.0 / (1.0 + jnp.exp(-gate))
    return (jnp.concatenate([dg * sg, dg * val * sg * (1.0 - sg)], axis=1),)


def _cf_forward(h, xin, wt, tag):
    d = xin.shape[1]
    u = _mm(h, wt['w_in'], "nn", extras=(wt['b_in'],), epilogue=_ep_add, name=tag + "_in")
    gl = _rowwise(lambda val, gate: (val / (1.0 + jnp.exp(-gate)),), [(u, d, 0), (u, d, 1)], [],
                  [('row', d, d, F32)], name=tag + "_glu")[0]
    cv = _dwconv_fwd(gl, wt['dw_w'], wt['dw_b'], tag + "_conv")
    s = _rowwise(lambda x, w, b: (_ln_silu_f(x, w, b),), [(cv, d, 0)], [(wt['ln_w'], False), (wt['ln_b'], False)],
                 [('row', d, d, BF16)], name=tag + "_ln")[0]
    xm = _mm(s, wt['w_out'], "nn", extras=(wt['b_out'], xin), epilogue=_ep_bias_add, name=tag + "_out")
    return xm, dict(u=u, gl=gl, cv=cv, s=s)


def _cf_backward(h, dx, dxb, wt, sv, tag):
    d = dx.shape[1]
    ds = _mm(dxb, wt['w_out'], "nt", name=tag + "_ds")
    g_wout = _mm(sv['s'], dxb, "tn", out_dtypes=(BF16,), name=tag + "_dwout")
    g_bout = _colsum(dx, tag + "_dbout")

    def ln_bwd(x, dsv, w, b):
        _, vjp = jax.vjp(_ln_silu_f, x, w, b)
        return vjp(dsv)

    dcv, g_lnw, g_lnb = _rowwise(ln_bwd, [(sv['cv'], d, 0), (ds, d, 0)], [(wt['ln_w'], False), (wt['ln_b'], False)],
                                 [('row', d, d, F32), ('acc', (1, d), False), ('acc', (1, d), False)],
                                 name=tag + "_ln_bwd")
    dgl, g_dww, g_dwb = _dwconv_bwd(sv['gl'], dcv, wt['dw_w'], tag + "_conv_bwd")
    du = _rowwise(_glu_bwd_fn, [(sv['u'], d, 0), (sv['u'], d, 1), (dgl, d, 0)], [], [('row', 2 * d, 2 * d, F32)],
                  name=tag + "_glu_bwd")[0]
    g_bin = _colsum(du, tag + "_dbin")
    g_win = _mm(h, du, "tn", out_dtypes=(BF16,), out_col_shards=N_XY, name=tag + "_dwin")
    dh = _mm(du, wt['w_in'], "nt", name=tag + "_dh")
    return dh, dict(w_in=g_win, b_in=g_bin, dw_w=g_dww, dw_b=g_dwb, ln_w=g_lnw, ln_b=g_lnb, w_out=g_wout, b_out=g_bout)


def _gated_norm_f(y, z, w):
    t = y * _silu(z)
    return t * lax.rsqrt(jnp.mean(t * t, axis=-1, keepdims=True) + M2_NORM_EPS) * w


def _m2_forward(h, xin, wt, tag):
    z = _mm(h, wt['w_z'], "nn", name=tag + "_z")
    xbc = _mm(h, wt['w_xbc'], "nn", name=tag + "_xbc")
    dtr = _mm(h, wt['w_dt'], "nn", name=tag + "_dt")
    pre = _dwconv_fwd(xbc, wt['conv_w'], wt['conv_b'], tag + "_conv")
    y, hs = _ssd_fwd(pre, dtr, wt['dt_bias'], wt['a_log'], wt['d'], tag + "_ssd")
    d_inner = y.shape[1]
    gw = d_inner // M2_GROUPS
    yn = _rowwise(lambda yv, zv, w: (_gated_norm_f(yv, zv, w),), [(y, gw, 0), (z, gw, 0)], [(wt['norm_w'], True)],
                  [('row', d_inner, gw, BF16)], ncb=M2_GROUPS, name=tag + "_gnorm")[0]
    xm = _mm(yn, wt['w_out'], "nn", extras=(xin,), epilogue=_ep_add, name=tag + "_out")
    return xm, dict(z=z, xbc=xbc, dtr=dtr, pre=pre, y=y, hs=hs, yn=yn)


def _m2_backward(h, dx, dxb, wt, sv, tag):
    dyn = _mm(dxb, wt['w_out'], "nt", name=tag + "_dyn")
    g_wout = _mm(sv['yn'], dxb, "tn", out_dtypes=(BF16,), name=tag + "_dwout")
    d_inner = sv['y'].shape[1]
    gw = d_inner // M2_GROUPS

    def gn_bwd(yv, zv, dv, w):
        _, vjp = jax.vjp(_gated_norm_f, yv, zv, w)
        return vjp(dv)

    dy, dz, g_nw = _rowwise(gn_bwd, [(sv['y'], gw, 0), (sv['z'], gw, 0), (dyn, gw, 0)], [(wt['norm_w'], True)],
                            [('row', d_inner, gw, F32), ('row', d_inner, gw, BF16), ('acc', (1, d_inner), True)],
                            ncb=M2_GROUPS, name=tag + "_gnorm_bwd")
    dxp, db, dc, ddt, g_dtb, g_alog, g_d = _ssd_bwd(sv['pre'], sv['dtr'], sv['hs'], dy, wt['dt_bias'], wt['a_log'],
                                                   wt['d'], tag + "_ssd_bwd")
    dpre = jnp.concatenate([dxp, db, dc], axis=1)
    dxbc, g_cw, g_cb = _dwconv_bwd(sv['xbc'], dpre, wt['conv_w'], tag + "_conv_bwd", dx_dtype=BF16)
    g_wz = _mm(h, dz, "tn", name=tag + "_dwz")
    g_wxbc = _mm(h, dxbc, "tn", name=tag + "_dwxbc")
    g_wdt = _mm(h, ddt, "tn", name=tag + "_dwdt")
    dh = _mm(dz, wt['w_z'], "nt", name=tag + "_dh_z")
    dh = _mm(dxbc, wt['w_xbc'], "nt", extras=(dh,), epilogue=_ep_add, name=tag + "_dh_xbc")
    dh = _mm(ddt, wt['w_dt'], "nt", extras=(dh,), epilogue=_ep_add, name=tag + "_dh_dt")
    return dh, dict(w_z=g_wz, w_xbc=g_wxbc, w_dt=g_wdt, conv_w=g_cw, conv_b=g_cb, dt_bias=g_dtb, a_log=g_alog, d=g_d,
                    norm_w=g_nw, w_out=g_wout)


def _pad_lanes(v):
    return jnp.pad(v.reshape(1, -1), ((0, 0), (0, LANES - v.shape[0])))


def kernel(x, norm_mix_w, norm_mlp_w, sb_w_qkv, sb_q_norm_w, sb_k_norm_w, sb_w_o, cf_w_in, cf_b_in, cf_dw_w, cf_dw_b, cf_ln_w, cf_ln_b, cf_w_out, cf_b_out, m2_w_in, m2_conv_w, m2_conv_b, m2_dt_bias, m2_a_log, m2_d, m2_norm_w, m2_w_out, mlp_w_up, mlp_w_down, loss_target, m_norm_mix_w, m_norm_mlp_w, m_sb_w_qkv, m_sb_q_norm_w, m_sb_k_norm_w, m_sb_w_o, m_cf_w_in, m_cf_b_in, m_cf_dw_w, m_cf_dw_b, m_cf_ln_w, m_cf_ln_b, m_cf_w_out, m_cf_b_out, m_m2_w_in, m_m2_conv_w, m_m2_conv_b, m_m2_dt_bias, m_m2_a_log, m_m2_d, m_m2_norm_w, m_m2_w_out, m_mlp_w_up, m_mlp_w_down, v_norm_mix_w, v_norm_mlp_w, v_sb_w_qkv, v_sb_q_norm_w, v_sb_k_norm_w, v_sb_w_o, v_cf_w_in, v_cf_b_in, v_cf_dw_w, v_cf_dw_b, v_cf_ln_w, v_cf_ln_b, v_cf_w_out, v_cf_b_out, v_m2_w_in, v_m2_conv_w, v_m2_conv_b, v_m2_dt_bias, v_m2_a_log, v_m2_d, v_m2_norm_w, v_m2_w_out, v_mlp_w_up, v_mlp_w_down):
    given = dict(locals())
    xl = x[0]
    tgt = loss_target[0]
    d = xl.shape[1]
    depth = norm_mix_w.shape[0]
    bulk_names, small_names = list(BULK), list(SMALL)

    def halves(buf):
        return buf.reshape(2, -1, buf.shape[-1])

    small_pack = halves(_pack([given[n] for n in small_names], F32, PACK_W))
    gath = _all_gather_xy([halves(given[n].astype(BF16)) for n in bulk_names] + [small_pack], "comm_gather")
    gathered = {n: g.reshape((N_XY,) + given[n].shape) for n, g in zip(bulk_names, gath)}
    kinds = {n: "rows" if BULK[n] == 1 else "cols" for n in bulk_names}
    full = {}
    pieces = [_unpack(gath[-1][s], [given[n].shape for n in small_names]) for s in range(N_XY)]
    for idx, n in enumerate(small_names):
        full[n] = jnp.concatenate([pieces[s][idx] for s in range(N_XY)], axis=SMALL[n])

    def wview(n, layer):
        return WView(gathered[n], layer, kinds[n])

    def row(v):
        return v.reshape(1, -1)

    d_inner = N_XY * m2_w_out.shape[1]
    conv_dim = full['m2_conv_w'].shape[2]
    nheads = m2_dt_bias.shape[1]

    def layer_weights(i):
        kind, j = i % 3, i // 3
        if kind == 0:
            return dict(w_qkv=wview('sb_w_qkv', j), w_o=wview('sb_w_o', j), q_norm_w=row(sb_q_norm_w[j]),
                        k_norm_w=row(sb_k_norm_w[j]))
        if kind == 1:
            return dict(w_in=wview('cf_w_in', j), b_in=row(cf_b_in[j]), dw_w=full['cf_dw_w'][j], dw_b=row(cf_dw_b[j]),
                        ln_w=row(cf_ln_w[j]), ln_b=row(cf_ln_b[j]), w_out=wview('cf_w_out', j), b_out=row(cf_b_out[j]))
        shards = gathered['m2_w_in'][:, j]
        w_in = jnp.moveaxis(shards, 0, 1).reshape(shards.shape[1], -1)
        w_dt = jnp.pad(w_in[:, d_inner + conv_dim:], ((0, 0), (0, LANES - nheads)))
        return dict(w_z=w_in[:, :d_inner], w_xbc=w_in[:, d_inner:d_inner + conv_dim], w_dt=w_dt,
                    conv_w=full['m2_conv_w'][j], conv_b=row(full['m2_conv_b'][j]), dt_bias=_pad_lanes(m2_dt_bias[j]),
                    a_log=_pad_lanes(m2_a_log[j]), d=_pad_lanes(m2_d[j]), norm_w=row(full['m2_norm_w'][j]),
                    w_out=wview('m2_w_out', j))

    fwd = (_sb_forward, _cf_forward, _m2_forward)
    bwd = (_sb_backward, _cf_backward, _m2_backward)

    saved = []
    xc = xl
    for i in range(depth):
        wt = layer_weights(i)
        h = _rms_fwd(xc, row(norm_mix_w[i]), f"l{i}_norm_mix")
        xm, sv = fwd[i % 3](h, xc, wt, f"l{i}_mix")
        h2 = _rms_fwd(xm, row(norm_mlp_w[i]), f"l{i}_norm_mlp")
        u, a = _mm(h2, wview('mlp_w_up', i), "nn", epilogue=_ep_relu2, out_dtypes=(F32, BF16), name=f"l{i}_up")
        xn = _mm(a, wview('mlp_w_down', i), "nn", extras=(xm,), epilogue=_ep_add, name=f"l{i}_down")
        saved.append(dict(wt=wt, x_in=xc, h=h, mix=sv, x_mid=xm, h2=h2, u=u, a=a))
        xc = xn
    dx, dxb, loss_acc = _loss(xc, tgt, "loss")
    loss = lax.psum(loss_acc[0, 0], ("x", "y", "c"))

    grads = {n: [None] * given[n].shape[0] for n in W_NAMES}
    for i in reversed(range(depth)):
        sv = saved[i]
        kind, j = i % 3, i // 3
        du = _mm(dxb, wview('mlp_w_down', i), "nt", extras=(sv['u'],), epilogue=_ep_relu2_bwd, out_dtypes=(BF16,),
                 name=f"l{i}_du")
        grads['mlp_w_down'][i] = _mm(sv['a'], dxb, "tn", out_dtypes=(BF16,), name=f"l{i}_dwdown")
        grads['mlp_w_up'][i] = _mm(sv['h2'], du, "tn", out_dtypes=(BF16,), out_col_shards=N_XY, name=f"l{i}_dwup")
        dh2 = _mm(du, wview('mlp_w_up', i), "nt", name=f"l{i}_dh2")
        dxm, dxmb, g_n2 = _rms_bwd(sv['x_mid'], row(norm_mlp_w[i]), dh2, dx, f"l{i}_norm_mlp_bwd")
        grads['norm_mlp_w'][i] = g_n2[0]
        dh, gw = bwd[kind](sv['h'], dxm, dxmb, sv['wt'], sv['mix'], f"l{i}_mix")
        dx, dxb, g_n1 = _rms_bwd(sv['x_in'], row(norm_mix_w[i]), dh, dxm, f"l{i}_norm_mix_bwd")
        grads['norm_mix_w'][i] = g_n1[0]
        if kind == 0:
            grads['sb_w_qkv'][j], grads['sb_w_o'][j] = gw['w_qkv'], gw['w_o']
            grads['sb_q_norm_w'][j], grads['sb_k_norm_w'][j] = gw['q_norm_w'][0], gw['k_norm_w'][0]
        elif kind == 1:
            for n in ('w_in', 'dw_w', 'w_out'):
                grads['cf_' + n][j] = gw[n]
            for n in ('b_in', 'dw_b', 'ln_w', 'ln_b', 'b_out'):
                grads['cf_' + n][j] = gw[n][0]
        else:
            grads['m2_w_in'][j] = jnp.concatenate([gw['w_z'], gw['w_xbc'], gw['w_dt'][:, :nheads]], axis=1)
            grads['m2_conv_w'][j], grads['m2_w_out'][j] = gw['conv_w'], gw['w_out']
            grads['m2_conv_b'][j], grads['m2_norm_w'][j] = gw['conv_b'][0], gw['norm_w'][0]
            for n in ('dt_bias', 'a_log', 'd'):
                grads['m2_' + n][j] = gw[n][0, :nheads]
    grad_x = dx[None]
    gfull = {n: jnp.stack(grads[n]) for n in small_names + REPL}

    def shard_major(n, g):
        _, r, c = given[n].shape
        if n == 'm2_w_in':
            return jnp.moveaxis(g.reshape(r, N_XY, c), 1, 0).astype(BF16)
        return g.reshape(N_XY, r, c)

    parts = [jnp.stack([shard_major(n, g) for g in grads[n]], axis=1) for n in bulk_names]
    parts = [p.reshape(N_XY, 2, -1, p.shape[-1]) for p in parts]
    split = {n: jnp.split(gfull[n], N_XY, axis=SMALL[n]) for n in small_names}
    parts.append(jnp.stack([halves(_pack([split[n][s] for n in small_names], F32, PACK_W)) for s in range(N_XY)]))
    tags = bulk_names + ["small"]
    got = _pair_split(parts, "comm_split")
    pair = [_add_half(p, g, f"sum_pair_{t}") for p, g, t in zip(parts, got, tags)]
    mine = [_sum_parts(r, f"sum_{t}") for r, t in zip(_exchange_xy(pair, "comm_exchange"), tags)]
    gsum = _pair_join(mine, "comm_join")
    outs = {}
    for n, g in list(zip(bulk_names, gsum)) + list(zip(small_names, _unpack(gsum[-1], [given[n].shape for n in small_names]))):
        shape = given[n].shape
        res = _adamw(*[given[p + n].reshape(-1, shape[-1]) for p in ("", "m_", "v_")], g.reshape(-1, shape[-1]),
                     None, f"adamw_{n}")
        outs["grad", n] = g.reshape(shape)
        for kind, arr in zip(("delta", "new_m", "new_v"), res):
            outs[kind, n] = arr.reshape(shape)
    rparts = _all_gather_all(_pack([gfull[n] for n in REPL], F32, LANES), "comm_gather_repl")
    nsplit = N_DEV // 2
    res = _adamw(*[_pack([given[p + n] for n in REPL], F32, LANES) for p in ("", "m_", "v_")],
                 _sum_parts(rparts[:nsplit], "sum_repl_a"), _sum_parts(rparts[nsplit:], "sum_repl_b"), "adamw_repl")
    shapes = [given[n].shape for n in REPL]
    for kind, buf in zip(("grad", "delta", "new_m", "new_v"), res):
        for n, arr in zip(REPL, _unpack(buf, shapes)):
            outs[kind, n] = arr

    return (loss, grad_x, *[outs[kind, n] for kind in ("grad", "delta", "new_m", "new_v") for n in W_NAMES])
```

```python
import math
from typing import NamedTuple

import jax
import jax.numpy as jnp
from jax import lax
from jax.experimental import pallas as pl
from jax.experimental.pallas import tpu as pltpu

F32 = jnp.float32
BF16 = jnp.bfloat16
MESH_ID = pl.DeviceIdType.MESH
HIGHEST = lax.Precision.HIGHEST

SB_HEAD_DIM = 128
M2_HEAD_DIM = 64
M2_STATE = 128
M2_GROUPS = 8
M2_CHUNK = 128
RMS_EPS = 1e-6
LN_EPS = 1e-5
M2_NORM_EPS = 1e-5
ADAM_LR = 0.001
ADAM_B1 = 0.9
ADAM_B2 = 0.999
ADAM_EPS = 1e-08
ADAM_WD = 0.01
ADAM_STEP = 10

N_XY = 4
N_DEV = 8

V7X_VMEM_BYTES = 64 * 2**20
VMEM_LIMIT = (V7X_VMEM_BYTES * 3) // 4
LANES = 128
ROW_BLOCK_BYTES = 6 * 2**20
MM_VMEM_BUDGET = (VMEM_LIMIT * 3) // 4
PACK_W = 1024
PACK_ROWS = 1024

W_NAMES = ['norm_mix_w', 'norm_mlp_w', 'sb_w_qkv', 'sb_q_norm_w', 'sb_k_norm_w', 'sb_w_o', 'cf_w_in', 'cf_b_in',
           'cf_dw_w', 'cf_dw_b', 'cf_ln_w', 'cf_ln_b', 'cf_w_out', 'cf_b_out', 'm2_w_in', 'm2_conv_w', 'm2_conv_b',
           'm2_dt_bias', 'm2_a_log', 'm2_d', 'm2_norm_w', 'm2_w_out', 'mlp_w_up', 'mlp_w_down']
BULK = {'sb_w_qkv': 2, 'sb_w_o': 1, 'cf_w_in': 2, 'cf_w_out': 1, 'm2_w_in': 2, 'm2_w_out': 1, 'mlp_w_up': 2,
        'mlp_w_down': 1}
SMALL = {'cf_dw_w': 2, 'm2_conv_w': 2, 'm2_conv_b': 1, 'm2_norm_w': 1}
REPL = ['norm_mix_w', 'norm_mlp_w', 'sb_q_norm_w', 'sb_k_norm_w', 'cf_b_in', 'cf_dw_b', 'cf_ln_w', 'cf_ln_b',
        'cf_b_out', 'm2_dt_bias', 'm2_a_log', 'm2_d']


def _tile(n, prefs):
    for p in prefs:
        if n % p == 0:
            return p
    return n


def _cparams(n):
    return pltpu.CompilerParams(dimension_semantics=("arbitrary",) * n, vmem_limit_bytes=VMEM_LIMIT)


def _silu(v):
    return v / (1.0 + jnp.exp(-v))


def _softplus(v):
    return jnp.maximum(v, 0.0) + jnp.log(1.0 + jnp.exp(-jnp.abs(v)))


class WView(NamedTuple):
    arr: jax.Array
    layer: int
    kind: str


def _mm(a, b, mode, *, extras=(), epilogue=None, out_dtypes=(F32,), out_col_shards=1, name):
    if mode == "tn":
        kdim, m = a.shape
    else:
        m, kdim = a.shape
    view = b if isinstance(b, WView) else None
    if view is not None:
        _, _, srows, scols = view.arr.shape
        brows, bcols = (N_XY * srows, scols) if view.kind == "rows" else (srows, N_XY * scols)
        b_arr, b_item = view.arr, view.arr.dtype.itemsize
    else:
        brows, bcols = b.shape
        b_arr, b_item = b, b.dtype.itemsize
    n = brows if mode == "nt" else bcols
    assert kdim == (bcols if mode == "nt" else brows)
    n_unit, k_unit = n // out_col_shards, kdim
    if view is not None:
        if (mode == "nn") == (view.kind == "cols"):
            n_unit = n // N_XY
        else:
            k_unit = kdim // N_XY
    tm = _tile(m, (1024, 512, 256, 128))
    tn = _tile(n_unit, (1024, 768, 512, 256, 128))
    acc_in_out = out_dtypes[0] == F32 and out_col_shards == 1
    fixed = sum(tm * tn * jnp.dtype(dt).itemsize for dt in out_dtypes)
    fixed += sum((1 if e.shape[0] == 1 else tm) * tn * e.dtype.itemsize for e in extras)

    def vmem_bytes(t):
        ab = tm * t * a.dtype.itemsize + t * tn * b_item
        casts = (tm * t * 2 if a.dtype != BF16 else 0) + (t * tn * 2 if b_item != 2 else 0)
        return 2 * (ab + fixed) + (0 if acc_in_out else tm * tn * 4) + tm * tn * 4 + casts

    tk = k_unit
    for cand in (2048, 1536, 1024, 768, 512, 256, 128):
        if k_unit % cand == 0:
            tk = cand
            if vmem_bytes(cand) <= MM_VMEM_BUDGET:
                break
    nk = kdim // tk
    dims = {"nn": ((1,), (0,)), "nt": ((1,), (1,)), "tn": ((0,), (0,))}[mode]
    if mode == "tn":
        a_spec = pl.BlockSpec((tk, tm), lambda i, j, k: (k, i))
    else:
        a_spec = pl.BlockSpec((tm, tk), lambda i, j, k: (i, k))
    if view is None:
        if mode == "nt":
            b_spec = pl.BlockSpec((tn, tk), lambda i, j, k: (j, k))
        else:
            b_spec = pl.BlockSpec((tk, tn), lambda i, j, k: (k, j))
    else:
        lay = view.layer
        rc = (lambda j, k: (j, k)) if mode == "nt" else (lambda j, k: (k, j))
        blk = (tn, tk) if mode == "nt" else (tk, tn)
        if view.kind == "rows":
            per = srows // blk[0]
            b_spec = pl.BlockSpec((None, None) + blk,
                                  lambda i, j, k: (rc(j, k)[0] // per, lay, rc(j, k)[0] % per, rc(j, k)[1]))
        else:
            per = scols // blk[1]
            b_spec = pl.BlockSpec((None, None) + blk,
                                  lambda i, j, k: (rc(j, k)[1] // per, lay, rc(j, k)[0], rc(j, k)[1] % per))
    ex_specs = [pl.BlockSpec((1, tn), lambda i, j, k: (0, j)) if e.shape[0] == 1
                else pl.BlockSpec((tm, tn), lambda i, j, k: (i, j)) for e in extras]
    n_ex, n_out = len(extras), len(out_dtypes)

    def body(*refs):
        a_ref, b_ref = refs[:2]
        ex = refs[2:2 + n_ex]
        outs = refs[2 + n_ex:2 + n_ex + n_out]
        acc = outs[0] if acc_in_out else refs[-1]
        k = pl.program_id(2)
        prod = lax.dot_general(a_ref[...].astype(BF16), b_ref[...].astype(BF16), (dims, ((), ())),
                               preferred_element_type=F32)

        def finish(res):
            vals = epilogue(res, *[e[...] for e in ex]) if epilogue is not None else (res,)
            for o, v in zip(outs, vals):
                o[...] = v.astype(o.dtype)

        if nk == 1:
            finish(prod)
        else:
            @pl.when(k == 0)
            def _():
                acc[...] = prod

            @pl.when((k > 0) & (k < nk - 1))
            def _():
                acc[...] += prod

            @pl.when(k == nk - 1)
            def _():
                finish(acc[...] + prod)

    scratch = [] if (acc_in_out or nk == 1) else [pltpu.VMEM((tm, tn), F32)]
    if out_col_shards == 1:
        out_specs = [pl.BlockSpec((tm, tn), lambda i, j, k: (i, j))] * n_out
        out_shape = [jax.ShapeDtypeStruct((m, n), d) for d in out_dtypes]
    else:
        per_out = n_unit // tn
        out_specs = [pl.BlockSpec((None, tm, tn), lambda i, j, k: (j // per_out, i, j % per_out))]
        out_shape = [jax.ShapeDtypeStruct((out_col_shards, m, n_unit), out_dtypes[0])]
    outs = pl.pallas_call(
        body, grid=(m // tm, n // tn, nk), in_specs=[a_spec, b_spec, *ex_specs], out_specs=out_specs,
        out_shape=out_shape, scratch_shapes=scratch, compiler_params=_cparams(3), name=name)(a, b_arr, *extras)
    return outs[0] if n_out == 1 else outs


def _ep_add(acc, r):
    return (acc + r,)


def _ep_bias_add(acc, b, r):
    return (acc + b + r,)


def _ep_relu2(acc):
    return acc, jnp.square(jnp.maximum(acc, 0.0))


def _ep_relu2_bwd(acc, u):
    return (acc * (2.0 * jnp.maximum(u, 0.0)),)


def _rowwise(fn, rows, consts, outs, *, ncb=1, name):
    nrow = rows[0][0].shape[0]
    row_bytes = sum(w * arr.dtype.itemsize for arr, w, _ in rows)
    row_bytes += sum(o[2] * jnp.dtype(o[3]).itemsize for o in outs if o[0] == 'row')
    t = nrow
    for cand in (1024, 512, 256, 128, 64, 32, 16):
        if nrow % cand == 0:
            t = cand
            if cand * row_bytes <= ROW_BLOCK_BYTES:
                break
    in_specs = [pl.BlockSpec((t, w), (lambda j, i, off=off: (i, off + j))) for _, w, off in rows]
    for arr, per_col in consts:
        r, wc = arr.shape
        if per_col:
            in_specs.append(pl.BlockSpec((r, wc // ncb), lambda j, i: (0, j)))
        else:
            in_specs.append(pl.BlockSpec((r, wc), lambda j, i: (0, 0)))
    out_specs, out_shape = [], []
    for o in outs:
        if o[0] == 'row':
            out_specs.append(pl.BlockSpec((t, o[2]), lambda j, i: (i, j)))
            out_shape.append(jax.ShapeDtypeStruct((nrow, o[1]), o[3]))
        else:
            r, wt = o[1]
            if o[2]:
                out_specs.append(pl.BlockSpec((r, wt // ncb), lambda j, i: (0, j)))
            else:
                out_specs.append(pl.BlockSpec((r, wt), lambda j, i: (0, 0)))
            out_shape.append(jax.ShapeDtypeStruct((r, wt), F32))
    nin = len(rows) + len(consts)

    def body(*refs):
        j, i = pl.program_id(0), pl.program_id(1)
        vals = fn(*[r[...] for r in refs[:nin]])

        def store(spec, ref, v):
            if spec[0] == 'row':
                ref[...] = v.astype(ref.dtype)
            else:
                first = (i == 0) if spec[2] else ((i == 0) & (j == 0))

                @pl.when(first)
                def _():
                    ref[...] = jnp.zeros_like(ref)

                ref[...] += v

        for spec, ref, v in zip(outs, refs[nin:], vals):
            store(spec, ref, v)

    res = pl.pallas_call(body, grid=(ncb, nrow // t), in_specs=in_specs, out_specs=out_specs, out_shape=out_shape,
                         compiler_params=_cparams(2), name=name)(*[r[0] for r in rows], *[c[0] for c in consts])
    return res


def _rms_f(eps):
    def f(x, w):
        return x * lax.rsqrt(jnp.mean(x * x, axis=-1, keepdims=True) + eps) * w
    return f


def _rms_fwd(x, w, name):
    d = x.shape[1]
    f = _rms_f(RMS_EPS)
    return _rowwise(lambda xv, wv: (f(xv, wv),), [(x, d, 0)], [(w, False)], [('row', d, d, BF16)], name=name)[0]


def _rms_bwd(x, w, dh, dres, name):
    d = x.shape[1]
    f = _rms_f(RMS_EPS)

    def fn(xv, dhv, drv, wv):
        _, vjp = jax.vjp(f, xv, wv)
        dx, dw = vjp(dhv)
        return dx + drv, dx + drv, dw

    return _rowwise(fn, [(x, d, 0), (dh, d, 0), (dres, d, 0)], [(w, False)],
                    [('row', d, d, F32), ('row', d, d, BF16), ('acc', (1, d), False)], name=name)


def _colsum(x, name):
    w = x.shape[1]
    return _rowwise(lambda v: (jnp.sum(v, axis=0, keepdims=True),), [(x, w, 0)], [], [('acc', (1, w), False)],
                    name=name)[0]


def _loss(y, tgt, name):
    d = y.shape[1]

    def fn(yv, tv):
        e = yv - tv
        s = jnp.sum(jnp.sum(e * e, axis=1, keepdims=True), axis=0, keepdims=True) * (0.5 / d)
        return e * (1.0 / d), e * (1.0 / d), s + jnp.zeros((1, LANES), F32)

    return _rowwise(fn, [(y, d, 0), (tgt, d, 0)], [],
                    [('row', d, d, F32), ('row', d, d, BF16), ('acc', (1, LANES), False)], name=name)


def _conv_tiles(x, w):
    nrow, ch = x.shape
    kw = w.shape[0]
    halo = 8 * ((kw - 1 + 7) // 8)
    few_taps = kw <= 8
    t = _tile(nrow, (512, 256, 128, 64, 32) if few_taps else (256, 128, 64, 32))
    tc = _tile(ch, (1024, 512, 256, 128) if few_taps else (512, 256, 128))
    assert t % halo == 0 and nrow % t == 0
    return nrow, ch, kw, halo, t, tc


CONV_SUB = 32


def _shifted_windows(buf, shifts, t, base):
    out = {}
    for b in range(8):
        group = [s for s in shifts if s % 8 == b]
        if group:
            rows = buf[base + b:base + max(group) + t, :]
            for s in group:
                out[s] = rows[s - b:s - b + t]
    return out


def _fold8(v):
    out = v[0:8]
    for r in range(8, v.shape[0], 8):
        out = out + v[r:r + 8]
    return out


def _dwconv_fwd(x, w, b, name):
    nrow, ch, kw, halo, t, tc = _conv_tiles(x, w)
    per = t // halo
    sub = min(CONV_SUB, t)

    def body(x_ref, halo_ref, w_ref, b_ref, y_ref, buf):
        i = pl.program_id(1)
        buf[0:halo, :] = jnp.where(i == 0, 0.0, halo_ref[...])
        buf[halo:halo + t, :] = x_ref[...]
        shifts = [halo - (kw - 1) + k for k in range(kw)]
        for r0 in range(0, t, sub):
            win = _shifted_windows(buf, shifts, sub, r0)
            acc = jnp.zeros((sub, tc), F32) + b_ref[...]
            for k in range(kw):
                acc = acc + w_ref[k:k + 1, :] * win[shifts[k]]
            y_ref[r0:r0 + sub, :] = acc

    return pl.pallas_call(
        body, grid=(ch // tc, nrow // t),
        in_specs=[pl.BlockSpec((t, tc), lambda j, i: (i, j)),
                  pl.BlockSpec((halo, tc), lambda j, i: (jnp.maximum(i * per - 1, 0), j)),
                  pl.BlockSpec((kw, tc), lambda j, i: (0, j)), pl.BlockSpec((1, tc), lambda j, i: (0, j))],
        out_specs=pl.BlockSpec((t, tc), lambda j, i: (i, j)), out_shape=jax.ShapeDtypeStruct((nrow, ch), F32),
        scratch_shapes=[pltpu.VMEM((t + halo, tc), F32)], compiler_params=_cparams(2), name=name)(x, x, w, b)


def _dwconv_bwd(x, dy, w, name, dx_dtype=F32):
    nrow, ch, kw, halo, t, tc = _conv_tiles(x, w)
    per = t // halo
    sub = min(CONV_SUB, t)
    nt = nrow // t
    nhalo = nrow // halo

    def body(x_ref, xh_ref, dy_ref, dyh_ref, w_ref, dx_ref, dw_ref, db_ref, xbuf, dybuf):
        i = pl.program_id(1)
        xbuf[0:halo, :] = jnp.where(i == 0, 0.0, xh_ref[...])
        xbuf[halo:halo + t, :] = x_ref[...]
        dyc = dy_ref[...]
        dybuf[0:t, :] = dyc
        dybuf[t:t + halo, :] = jnp.where(i == nt - 1, 0.0, dyh_ref[...])
        @pl.when(i == 0)
        def _():
            dw_ref[...] = jnp.zeros_like(dw_ref)
            db_ref[...] = jnp.zeros_like(db_ref)

        dw_part = [jnp.zeros((8, tc), F32) for _ in range(kw)]
        for r0 in range(0, t, sub):
            dywin = _shifted_windows(dybuf, list(range(kw)), sub, r0)
            acc = jnp.zeros((sub, tc), F32)
            for k in range(kw):
                acc = acc + w_ref[k:k + 1, :] * dywin[kw - 1 - k]
            dx_ref[r0:r0 + sub, :] = acc.astype(dx_ref.dtype)
            xwin = _shifted_windows(xbuf, [halo - s for s in range(kw)], sub, r0)
            for k in range(kw):
                dw_part[k] = dw_part[k] + _fold8(dywin[0] * xwin[halo - (kw - 1 - k)])
        for k in range(kw):
            dw_ref[k:k + 1, :] += jnp.sum(dw_part[k], axis=0, keepdims=True)
        db_ref[...] += jnp.sum(dyc, axis=0, keepdims=True)

    return pl.pallas_call(
        body, grid=(ch // tc, nt),
        in_specs=[pl.BlockSpec((t, tc), lambda j, i: (i, j)),
                  pl.BlockSpec((halo, tc), lambda j, i: (jnp.maximum(i * per - 1, 0), j)),
                  pl.BlockSpec((t, tc), lambda j, i: (i, j)),
                  pl.BlockSpec((halo, tc), lambda j, i: (jnp.minimum((i + 1) * per, nhalo - 1), j)),
                  pl.BlockSpec((kw, tc), lambda j, i: (0, j))],
        out_specs=[pl.BlockSpec((t, tc), lambda j, i: (i, j)), pl.BlockSpec((kw, tc), lambda j, i: (0, j)),
                   pl.BlockSpec((1, tc), lambda j, i: (0, j))],
        out_shape=[jax.ShapeDtypeStruct((nrow, ch), dx_dtype), jax.ShapeDtypeStruct((kw, ch), F32),
                   jax.ShapeDtypeStruct((1, ch), F32)],
        scratch_shapes=[pltpu.VMEM((t + halo, tc), F32), pltpu.VMEM((t + halo, tc), F32)],
        compiler_params=_cparams(2), name=name)(x, x, dy, dy, w)


SB_SUB = 128
LOG2E = 1.4426950408889634
SB_SKIP_BELOW = -256.0
SB_UNVISITED = -1e30


def _dot_split2(x, u):
    hi = x.astype(BF16)
    lo = (x - hi.astype(F32)).astype(BF16)
    return jnp.dot(hi, u, preferred_element_type=F32) + jnp.dot(lo, u, preferred_element_type=F32)


def _sb_scores(q, k, strict, scale):
    z = lax.dot_general(q, k, (((1,), (1,)), ((), ())), preferred_element_type=F32) * (scale * LOG2E)
    soft = jnp.log2(1.0 + jnp.exp2(-jnp.abs(z)))
    ls = jnp.minimum(z, 0.0) - soft
    lk = ls - z
    if strict is not None:
        lk = jnp.where(strict, lk, 0.0)
    return ls, lk


def _sb_running(x, tri, start, reverse):
    nsub = x.shape[1] // SB_SUB
    parts = [x[:, SB_SUB * b:SB_SUB * (b + 1)] for b in range(nsub)]
    out = [None] * nsub
    run = start
    for b in (reversed(range(nsub)) if reverse else range(nsub)):
        out[b] = _dot_split2(parts[b], tri) + run
        run = run + jnp.sum(parts[b], axis=1, keepdims=True)
    return (jnp.concatenate(out, axis=1) if nsub > 1 else out[0]), run


def _sb_dims(qn):
    nrow, d = qn.shape
    hd = SB_HEAD_DIM
    t = _tile(nrow, (512, 256, 128))
    assert nrow % t == 0 and nrow // t <= LANES
    return nrow, d, hd, d // hd, t


def _sb_masks(t):
    strict = lax.broadcasted_iota(jnp.int32, (t, t), 1) < lax.broadcasted_iota(jnp.int32, (t, t), 0)
    r0 = lax.broadcasted_iota(jnp.int32, (SB_SUB, SB_SUB), 0)
    c0 = lax.broadcasted_iota(jnp.int32, (SB_SUB, SB_SUB), 1)
    return strict, (r0 > c0).astype(BF16), (r0 < c0).astype(BF16)


def _sb_attn_fwd(qn, kn, qkv, name):
    nrow, d, hd, nh, t = _sb_dims(qn)
    scale = 1.0 / math.sqrt(hd)

    def body(q_ref, k_ref, v_ref, o_ref, rs_ref, r_acc, o_acc):
        i = pl.program_id(1)
        q = q_ref[...]
        strict, after, _ = _sb_masks(t)
        lane = lax.broadcasted_iota(jnp.int32, (t, LANES), 1)
        r_acc[...] = jnp.zeros_like(r_acc)
        o_acc[...] = jnp.zeros_like(o_acc)
        rs_ref[...] = jnp.full(rs_ref.shape, SB_UNVISITED, F32)

        def tile(j, mask):
            start = pl.multiple_of(j * t, t)
            k = k_ref[pl.ds(start, t), :]
            v = v_ref[pl.ds(start, t), :].astype(BF16)
            ls, lk = _sb_scores(q, k, mask, scale)
            r = r_acc[...]
            later, r_next = _sb_running(lk, after, r, True)
            att = jnp.exp2(ls + later)
            if mask is not None:
                att = jnp.where(mask, att, 0.0)
            o_acc[...] += jnp.dot(att.astype(BF16), v, preferred_element_type=F32)
            rs_ref[...] = jnp.where(lane == j, r, rs_ref[...])
            r_acc[...] = r_next

        tile(i, strict)

        def more(carry):
            n, r_max = carry
            return (n < i) & (r_max > SB_SKIP_BELOW)

        def step(carry):
            n, _ = carry
            tile(i - 1 - n, None)
            return n + 1, jnp.max(r_acc[...])

        lax.while_loop(more, step, (jnp.int32(0), jnp.max(r_acc[...])))
        o_ref[...] = o_acc[...].astype(o_ref.dtype)

    return pl.pallas_call(
        body, grid=(nh, nrow // t),
        in_specs=[pl.BlockSpec((t, hd), lambda h, i: (i, h)), pl.BlockSpec((nrow, hd), lambda h, i: (0, h)),
                  pl.BlockSpec((nrow, hd), lambda h, i: (0, 2 * nh + h))],
        out_specs=[pl.BlockSpec((t, hd), lambda h, i: (i, h)), pl.BlockSpec((None, t, LANES), lambda h, i: (h, i, 0))],
        out_shape=[jax.ShapeDtypeStruct((nrow, d), BF16), jax.ShapeDtypeStruct((nh, nrow, LANES), F32)],
        scratch_shapes=[pltpu.VMEM((t, LANES), F32), pltpu.VMEM((t, hd), F32)],
        compiler_params=_cparams(2), name=name)(qn, kn, qkv)


def _sb_attn_bwd(qn, kn, qkv, do, rsave, name):
    nrow, d, hd, nh, t = _sb_dims(qn)
    scale = 1.0 / math.sqrt(hd)

    def body(q_ref, k_ref, v_ref, do_ref, rs_ref, dq_ref, dk_ref, dv_ref, pg_acc):
        i = pl.program_id(1)

        @pl.when(i == 0)
        def _():
            dk_ref[...] = jnp.zeros_like(dk_ref)
            dv_ref[...] = jnp.zeros_like(dv_ref)

        pg_acc[...] = jnp.zeros_like(pg_acc)
        dq_ref[...] = jnp.zeros_like(dq_ref)
        q = q_ref[...]
        dob = do_ref[...].astype(BF16)
        strict, after, before = _sb_masks(t)
        lane = lax.broadcasted_iota(jnp.int32, (t, LANES), 1)

        def tile(j, mask):
            start = pl.multiple_of(j * t, t)
            k = k_ref[pl.ds(start, t), :]
            v = v_ref[pl.ds(start, t), :].astype(BF16)
            ls, lk = _sb_scores(q, k, mask, scale)
            rj = jnp.sum(jnp.where(lane == j, rs_ref[...], 0.0), axis=1, keepdims=True)
            later, _ = _sb_running(lk, after, jnp.broadcast_to(rj, (t, LANES)), True)
            att = jnp.exp2(ls + later)
            if mask is not None:
                att = jnp.where(mask, att, 0.0)
            datt = lax.dot_general(dob, v, (((1,), (1,)), ((), ())), preferred_element_type=F32)
            g = datt * att
            dlk, pg_next = _sb_running(g, before, pg_acc[...], False)
            sig = jnp.exp2(ls)
            dz = g * (1.0 - sig) - dlk * sig
            if mask is not None:
                dz = jnp.where(mask, dz, 0.0)
            dz = (dz * scale).astype(BF16)
            dq_ref[...] += jnp.dot(dz, k, preferred_element_type=F32)
            dk_ref[pl.ds(start, t), :] += lax.dot_general(dz, q, (((0,), (0,)), ((), ())), preferred_element_type=F32)
            dv_ref[pl.ds(start, t), :] += lax.dot_general(att.astype(BF16), dob, (((0,), (0,)), ((), ())),
                                                          preferred_element_type=F32)
            pg_acc[...] = pg_next

        def step(j, carry):
            tile(j, None)
            return carry

        seen = jnp.max(rs_ref[...], axis=0, keepdims=True)
        lane1 = lax.broadcasted_iota(jnp.int32, (1, LANES), 1)
        first = jnp.sum(jnp.where((lane1 < i) & (seen < 0.5 * SB_UNVISITED), 1.0, 0.0)).astype(jnp.int32)
        lax.fori_loop(first, i, step, 0)
        tile(i, strict)

    return pl.pallas_call(
        body, grid=(nh, nrow // t),
        in_specs=[pl.BlockSpec((t, hd), lambda h, i: (i, h)), pl.BlockSpec((nrow, hd), lambda h, i: (0, h)),
                  pl.BlockSpec((nrow, hd), lambda h, i: (0, 2 * nh + h)), pl.BlockSpec((t, hd), lambda h, i: (i, h)),
                  pl.BlockSpec((None, t, LANES), lambda h, i: (h, i, 0))],
        out_specs=[pl.BlockSpec((t, hd), lambda h, i: (i, h)), pl.BlockSpec((nrow, hd), lambda h, i: (0, h)),
                   pl.BlockSpec((nrow, hd), lambda h, i: (0, h))],
        out_shape=[jax.ShapeDtypeStruct((nrow, d), F32)] * 3, scratch_shapes=[pltpu.VMEM((t, LANES), F32)],
        compiler_params=_cparams(2), name=name)(qn, kn, qkv, do, rsave)


def _ssd_chunk_fn(g, nheads_g, q):
    p = M2_HEAD_DIM
    npair = nheads_g // 2

    def f(xp, bp, cp, dtr, hp, dtb, alog, dsk):
        b = _silu(bp).astype(BF16)
        c = _silu(cp).astype(BF16)
        dt = _softplus(dtr + dtb)
        dta = dt * (-jnp.exp(alog))
        ri = lax.broadcasted_iota(jnp.int32, (q, q), 0)
        ci = lax.broadcasted_iota(jnp.int32, (q, q), 1)
        causal = ri >= ci
        tri = causal.astype(F32)
        acol = jnp.dot(tri, dta, precision=HIGHEST, preferred_element_type=F32)
        arow = lax.dot_general(dta, tri, (((0,), (1,)), ((), ())), precision=HIGHEST,
                               preferred_element_type=F32)
        alast = jnp.sum(dta, axis=0, keepdims=True)
        cb = lax.dot_general(c, b, (((1,), (1,)), ((), ())), preferred_element_type=F32)
        lane = lax.broadcasted_iota(jnp.int32, (1, LANES), 1)
        sub = lax.broadcasted_iota(jnp.int32, (LANES, 1), 0)
        lane_half = lane < p
        sub_half = sub < p

        def col(mat, h):
            return jnp.sum(jnp.where(lane == h, mat, 0.0), axis=1, keepdims=True)

        def row(mat, h):
            return jnp.sum(jnp.where(sub == h, mat, 0.0), axis=0, keepdims=True)

        def mix(v0, v1):
            return jnp.where(lane_half, v0, v1)

        ys, hns = [], []
        for jp in range(npair):
            h0 = g * nheads_g + 2 * jp
            h1 = h0 + 1
            x = _silu(xp[jp])
            ac0, ac1 = col(acol, h0), col(acol, h1)
            xdt = x * mix(col(dt, h0), col(dt, h1))
            xdtb = xdt.astype(BF16)
            yd = []
            for h, ac in ((h0, ac0), (h1, ac1)):
                dec = jnp.exp(jnp.where(causal, ac - row(arow, h), -jnp.inf))
                yd.append(jnp.dot((cb * dec).astype(BF16), xdtb, preferred_element_type=F32))
            hpj = hp[jp]
            yo = lax.dot_general(c, hpj.astype(BF16), (((1,), (1,)), ((), ())), preferred_element_type=F32)
            y = mix(yd[0], yd[1]) + yo * jnp.exp(mix(ac0, ac1)) + mix(col(dsk, h0), col(dsk, h1)) * x
            al0, al1 = col(alast, h0), col(alast, h1)
            dout = jnp.exp(mix(al0 - ac0, al1 - ac1))
            st = lax.dot_general((xdt * dout).astype(BF16), b, (((0,), (0,)), ((), ())), preferred_element_type=F32)
            hns.append(hpj * jnp.exp(jnp.where(sub_half, al0, al1)) + st)
            ys.append(y)
        return ys, hns

    return f


def _ssd_dims(pre, dtr):
    nrow = pre.shape[0]
    ng, n, p, q = M2_GROUPS, M2_STATE, M2_HEAD_DIM, M2_CHUNK
    d_inner = pre.shape[1] - 2 * ng * n
    gw = d_inner // ng
    nhg = gw // p
    assert nhg % 2 == 0 and gw % LANES == 0 and n == LANES and ng * nhg <= LANES and dtr.shape[1] == LANES
    return nrow, ng, n, q, d_inner, gw, nhg, nrow // q


def _ssd_fwd(pre, dtr, dtb, alog, dsk, name):
    nrow, ng, n, q, d_inner, gw, nhg, nc = _ssd_dims(pre, dtr)
    boff = d_inner // n
    npair = nhg // 2

    def body(x_ref, b_ref, c_ref, dt_ref, dtb_ref, al_ref, ds_ref, y_ref, hs_ref, state):
        ci, g = pl.program_id(0), pl.program_id(1)

        @pl.when(ci == 0)
        def _():
            state[g] = jnp.zeros((gw, n), F32)

        hs_ref[...] = state[g]
        f = _ssd_chunk_fn(g, nhg, q)
        xp = [x_ref[:, LANES * j:LANES * (j + 1)] for j in range(npair)]
        hp = [state[g, LANES * j:LANES * (j + 1), :] for j in range(npair)]
        ys, hns = f(xp, b_ref[...], c_ref[...], dt_ref[...], hp, dtb_ref[...], al_ref[...], ds_ref[...])
        for j in range(npair):
            y_ref[:, LANES * j:LANES * (j + 1)] = ys[j]
            state[g, LANES * j:LANES * (j + 1), :] = hns[j]

    par = pl.BlockSpec((1, LANES), lambda ci, g: (0, 0))
    return pl.pallas_call(
        body, grid=(nc, ng),
        in_specs=[pl.BlockSpec((q, gw), lambda ci, g: (ci, g)), pl.BlockSpec((q, n), lambda ci, g: (ci, boff + g)),
                  pl.BlockSpec((q, n), lambda ci, g: (ci, boff + ng + g)), pl.BlockSpec((q, LANES), lambda ci, g: (ci, 0)),
                  par, par, par],
        out_specs=[pl.BlockSpec((q, gw), lambda ci, g: (ci, g)),
                   pl.BlockSpec((None, None, gw, n), lambda ci, g: (ci, g, 0, 0))],
        out_shape=[jax.ShapeDtypeStruct((nrow, d_inner), F32), jax.ShapeDtypeStruct((nc, ng, gw, n), F32)],
        scratch_shapes=[pltpu.VMEM((ng, gw, n), F32)], compiler_params=_cparams(2), name=name)(
            pre, pre, pre, dtr, dtb, alog, dsk)


def _ssd_bwd(pre, dtr, hs, dy, dtb, alog, dsk, name):
    nrow, ng, n, q, d_inner, gw, nhg, nc = _ssd_dims(pre, dtr)
    boff = d_inner // n
    npair = nhg // 2

    def body(x_ref, b_ref, c_ref, dt_ref, hs_ref, dy_ref, dtb_ref, al_ref, ds_ref,
             dx_ref, db_ref, dc_ref, ddt_ref, ddtb_ref, dal_ref, dds_ref, dstate):
        ci, g = pl.program_id(0), pl.program_id(1)

        @pl.when(ci == 0)
        def _():
            dstate[g] = jnp.zeros((gw, n), F32)

        @pl.when((ci == 0) & (g == 0))
        def _():
            ddtb_ref[...] = jnp.zeros_like(ddtb_ref)
            dal_ref[...] = jnp.zeros_like(dal_ref)
            dds_ref[...] = jnp.zeros_like(dds_ref)

        @pl.when(g == 0)
        def _():
            ddt_ref[...] = jnp.zeros_like(ddt_ref)

        f = _ssd_chunk_fn(g, nhg, q)
        xp = [x_ref[:, LANES * j:LANES * (j + 1)] for j in range(npair)]
        hp = [hs_ref[LANES * j:LANES * (j + 1), :] for j in range(npair)]
        _, vjp = jax.vjp(f, xp, b_ref[...], c_ref[...], dt_ref[...], hp, dtb_ref[...], al_ref[...], ds_ref[...])
        dys = [dy_ref[:, LANES * j:LANES * (j + 1)] for j in range(npair)]
        dhn = [dstate[g, LANES * j:LANES * (j + 1), :] for j in range(npair)]
        dxp, db, dc, ddt, dhp, ddtb, dal, dds = vjp((dys, dhn))
        for j in range(npair):
            dx_ref[:, LANES * j:LANES * (j + 1)] = dxp[j]
            dstate[g, LANES * j:LANES * (j + 1), :] = dhp[j]
        db_ref[...] = db
        dc_ref[...] = dc
        ddt_ref[...] += ddt
        ddtb_ref[...] += ddtb
        dal_ref[...] += dal
        dds_ref[...] += dds

    par = pl.BlockSpec((1, LANES), lambda ci, g: (0, 0))
    last = nc - 1
    return pl.pallas_call(
        body, grid=(nc, ng),
        in_specs=[pl.BlockSpec((q, gw), lambda ci, g: (last - ci, g)),
                  pl.BlockSpec((q, n), lambda ci, g: (last - ci, boff + g)),
                  pl.BlockSpec((q, n), lambda ci, g: (last - ci, boff + ng + g)),
                  pl.BlockSpec((q, LANES), lambda ci, g: (last - ci, 0)),
                  pl.BlockSpec((None, None, gw, n), lambda ci, g: (last - ci, g, 0, 0)),
                  pl.BlockSpec((q, gw), lambda ci, g: (last - ci, g)), par, par, par],
        out_specs=[pl.BlockSpec((q, gw), lambda ci, g: (last - ci, g)), pl.BlockSpec((q, n), lambda ci, g: (last - ci, g)),
                   pl.BlockSpec((q, n), lambda ci, g: (last - ci, g)), pl.BlockSpec((q, LANES), lambda ci, g: (last - ci, 0)),
                   par, par, par],
        out_shape=[jax.ShapeDtypeStruct((nrow, d_inner), F32), jax.ShapeDtypeStruct((nrow, ng * n), F32),
                   jax.ShapeDtypeStruct((nrow, ng * n), F32), jax.ShapeDtypeStruct((nrow, LANES), F32),
                   jax.ShapeDtypeStruct((1, LANES), F32), jax.ShapeDtypeStruct((1, LANES), F32),
                   jax.ShapeDtypeStruct((1, LANES), F32)],
        scratch_shapes=[pltpu.VMEM((ng, gw, n), F32)], compiler_params=_cparams(2), name=name)(
            pre, pre, pre, dtr, hs, dy, dtb, alog, dsk)


HBM_SPEC = pl.BlockSpec(memory_space=pl.ANY)


def _xy_peers(mx, my):
    return [(1 - mx, my), (mx, 1 - my), (1 - mx, 1 - my)]


def _remote(src, dst, send_sems, recv_sems, k, dev):
    return pltpu.make_async_remote_copy(src_ref=src, dst_ref=dst, send_sem=send_sems.at[k], recv_sem=recv_sems.at[k],
                                        device_id=dev, device_id_type=MESH_ID)


def _comm_call(body, arrays, out_shapes, ncopies, name):
    nin = len(arrays)

    def wrapped(*refs):
        body(refs[:nin], refs[nin:nin + len(out_shapes)], refs[-2], refs[-1])

    return pl.pallas_call(
        wrapped, out_shape=out_shapes, in_specs=[HBM_SPEC] * nin, out_specs=[HBM_SPEC] * len(out_shapes),
        scratch_shapes=[pltpu.SemaphoreType.DMA((ncopies,)), pltpu.SemaphoreType.DMA((ncopies,))], name=name)(*arrays)


def _all_gather_xy(bufs, name):
    ncopy = 12

    def body(srcs, outs, send_sems, recv_sems):
        mx, my, mc = lax.axis_index("x"), lax.axis_index("y"), lax.axis_index("c")
        me, xn, yn, dg = 2 * mx + my, 2 * (1 - mx) + my, 2 * mx + (1 - my), 2 * (1 - mx) + (1 - my)
        xdev, ydev, sibling = (1 - mx, my, mc), (mx, 1 - my, mc), (mx, my, 1 - mc)
        started = []

        def copy(a, k, src, dst, dev):
            return _remote(src, dst, send_sems, recv_sems, ncopy * a + k, dev)

        def go(cp):
            cp.start()
            started.append(cp)

        for a, (src, out) in enumerate(zip(srcs, outs)):
            go(copy(a, 0, src.at[mc, 0], out.at[me, mc, 0], xdev))
            go(copy(a, 1, src.at[mc, 1], out.at[me, mc, 1], ydev))
        for a, (src, out) in enumerate(zip(srcs, outs)):
            from_x = out.at[xn, mc, 0]
            copy(a, 0, from_x, from_x, xdev).wait_recv()
            go(copy(a, 2, src.at[mc, 0], out.at[me, mc, 0], ydev))
            go(copy(a, 3, from_x, from_x, ydev))
            go(copy(a, 6, from_x, from_x, sibling))
            from_y = out.at[yn, mc, 1]
            copy(a, 1, from_y, from_y, ydev).wait_recv()
            go(copy(a, 4, src.at[mc, 1], out.at[me, mc, 1], xdev))
            go(copy(a, 5, from_y, from_y, xdev))
            go(copy(a, 7, from_y, from_y, sibling))
        for a, out in enumerate(outs):
            for j, (k, s, q) in enumerate(((2, yn, 0), (3, dg, 0), (4, xn, 1), (5, dg, 1))):
                landed = out.at[s, mc, q]
                copy(a, k, landed, landed, sibling).wait_recv()
                go(copy(a, 8 + j, landed, landed, sibling))
        for a, out in enumerate(outs):
            for k, s, q in ((6, xn, 0), (7, yn, 1), (8, yn, 0), (9, dg, 0), (10, xn, 1), (11, dg, 1)):
                passed = out.at[s, 1 - mc, q]
                copy(a, k, passed, passed, sibling).wait_recv()
        for cp in started:
            cp.wait_send()

    outs = _comm_call(body, bufs, [jax.ShapeDtypeStruct((N_XY,) + b.shape, b.dtype) for b in bufs],
                      ncopy * len(bufs), name)
    return [lax.dynamic_update_index_in_dim(o, b, _my_shard(), 0) for o, b in zip(outs, bufs)]


def _my_shard():
    return 2 * lax.axis_index("x") + lax.axis_index("y")


def _exchange_xy(parts, name):
    npeer = N_XY - 1

    def body(srcs, outs, send_sems, recv_sems):
        mx, my, mc = lax.axis_index("x"), lax.axis_index("y"), lax.axis_index("c")
        me = 2 * mx + my
        peers = _xy_peers(mx, my)
        sends = [_remote(src.at[2 * px + py], out.at[me], send_sems, recv_sems, npeer * a + k, (px, py, mc))
                 for k, (px, py) in enumerate(peers) for a, (src, out) in enumerate(zip(srcs, outs))]
        for cp in sends:
            cp.start()
        for k, (px, py) in enumerate(peers):
            for a, (src, out) in enumerate(zip(srcs, outs)):
                _remote(src.at[me], out.at[2 * px + py], send_sems, recv_sems, npeer * a + k, (px, py, mc)).wait_recv()
        for cp in sends:
            cp.wait_send()

    outs = _comm_call(body, parts, [jax.ShapeDtypeStruct(p.shape, p.dtype) for p in parts], npeer * len(parts), name)
    me = _my_shard()
    return [lax.dynamic_update_index_in_dim(o, lax.dynamic_index_in_dim(p, me, 0, keepdims=False), me, 0)
            for o, p in zip(outs, parts)]


def _pair_split(parts, name):
    def body(srcs, gots, send_sems, recv_sems):
        mx, my, mc = lax.axis_index("x"), lax.axis_index("y"), lax.axis_index("c")
        copies, idx = [], 0
        for src, got in zip(srcs, gots):
            for s in range(src.shape[0]):
                copies.append(_remote(src.at[s, 1 - mc], got.at[s], send_sems, recv_sems, idx, (mx, my, 1 - mc)))
                idx += 1
        for cp in copies:
            cp.start()
        for cp in copies:
            cp.wait()

    gots = _comm_call(body, parts, [jax.ShapeDtypeStruct((p.shape[0],) + p.shape[2:], p.dtype) for p in parts],
                      sum(p.shape[0] for p in parts), name)
    return gots


def _pair_join(mines, name):
    def body(srcs, outs, send_sems, recv_sems):
        mx, my, mc = lax.axis_index("x"), lax.axis_index("y"), lax.axis_index("c")
        copies = [_remote(src, out, send_sems, recv_sems, a, (mx, my, 1 - mc))
                  for a, (src, out) in enumerate(zip(srcs, outs))]
        for cp in copies:
            cp.start()
        for cp in copies:
            cp.wait()

    others = _comm_call(body, mines, [jax.ShapeDtypeStruct(m.shape, m.dtype) for m in mines], len(mines), name)
    first = lax.axis_index("c") == 0
    return [jnp.where(first, jnp.stack([m, o]), jnp.stack([o, m])) for m, o in zip(mines, others)]


def _all_gather_all(buf, name):
    flips = [(fx, fy, fc) for fx in (0, 1) for fy in (0, 1) for fc in (0, 1) if fx or fy or fc]

    def body(src, out, send_sems, recv_sems):
        mx, my, mc = lax.axis_index("x"), lax.axis_index("y"), lax.axis_index("c")
        me = 4 * mx + 2 * my + mc
        peers = [(1 - mx if fx else mx, 1 - my if fy else my, 1 - mc if fc else mc) for fx, fy, fc in flips]
        sends = [_remote(src, out.at[me], send_sems, recv_sems, k, dev) for k, dev in enumerate(peers)]
        for cp in sends:
            cp.start()
        for k, (px, py, pc) in enumerate(peers):
            _remote(src, out.at[4 * px + 2 * py + pc], send_sems, recv_sems, k, (px, py, pc)).wait_recv()
        for cp in sends:
            cp.wait_send()

    out = pl.pallas_call(
        body, out_shape=jax.ShapeDtypeStruct((N_DEV,) + buf.shape, buf.dtype), in_specs=[HBM_SPEC], out_specs=HBM_SPEC,
        scratch_shapes=[pltpu.SemaphoreType.DMA((N_DEV - 1,)), pltpu.SemaphoreType.DMA((N_DEV - 1,))], name=name)(buf)
    me = 4 * lax.axis_index("x") + 2 * lax.axis_index("y") + lax.axis_index("c")
    return lax.dynamic_update_index_in_dim(out, buf, me, 0)


def _flat_tile(nrow, width, narrays):
    t = nrow
    for cand in (512, 256, 128, 64, 32, 16, 8):
        if nrow % cand == 0:
            t = cand
            if cand * width * 4 * narrays <= ROW_BLOCK_BYTES:
                break
    return t


def _sum_parts(parts, name):
    kparts, nrow, width = parts.shape
    t = _flat_tile(nrow, width, kparts + 1)

    def body(p_ref, o_ref):
        s = p_ref[0].astype(F32)
        for k in range(1, kparts):
            s = s + p_ref[k].astype(F32)
        o_ref[...] = s

    return pl.pallas_call(
        body, grid=(nrow // t,), in_specs=[pl.BlockSpec((kparts, t, width), lambda i: (0, i, 0))],
        out_specs=pl.BlockSpec((t, width), lambda i: (i, 0)), out_shape=jax.ShapeDtypeStruct((nrow, width), F32),
        compiler_params=_cparams(1), name=name)(parts)


def _add_half(parts, other, name):
    nparts, _, nrow, width = parts.shape
    t = _flat_tile(nrow, width, 3)

    def body(c_ref, a_ref, b_ref, o_ref):
        o_ref[...] = (a_ref[...].astype(F32) + b_ref[...].astype(F32)).astype(o_ref.dtype)

    spec = pl.BlockSpec((None, t, width), lambda p, i, c_ref: (p, i, 0))
    grid_spec = pltpu.PrefetchScalarGridSpec(
        num_scalar_prefetch=1, grid=(nparts, nrow // t),
        in_specs=[pl.BlockSpec((None, None, t, width), lambda p, i, c_ref: (p, c_ref[0], i, 0)), spec], out_specs=spec)
    core = lax.axis_index("c").astype(jnp.int32).reshape(1)
    return pl.pallas_call(body, grid_spec=grid_spec, out_shape=jax.ShapeDtypeStruct(other.shape, other.dtype),
                          compiler_params=_cparams(2), name=name)(core, parts, other)


def _adamw(w, m, v, ga, gb, name):
    nrow, width = w.shape
    t = _flat_tile(nrow, width, 9)
    c1 = 1.0 - ADAM_B1 ** ADAM_STEP
    c2 = 1.0 - ADAM_B2 ** ADAM_STEP
    grads = [ga] if gb is None else [ga, gb]
    nout = 3 if gb is None else 4

    def body(w_ref, m_ref, v_ref, *refs):
        d_ref, mo_ref, vo_ref = refs[-3:]
        g = refs[0][...]
        if gb is not None:
            g = g + refs[1][...]
            refs[2][...] = g
        mn = ADAM_B1 * m_ref[...] + (1.0 - ADAM_B1) * g
        vn = ADAM_B2 * v_ref[...] + (1.0 - ADAM_B2) * jnp.square(g)
        mo_ref[...] = mn
        vo_ref[...] = vn
        d_ref[...] = -ADAM_LR * ((mn / c1) / (jnp.sqrt(vn / c2) + ADAM_EPS) + ADAM_WD * w_ref[...])

    spec = pl.BlockSpec((t, width), lambda i: (i, 0))
    return pl.pallas_call(
        body, grid=(nrow // t,), in_specs=[spec] * (3 + len(grads)), out_specs=[spec] * nout,
        out_shape=[jax.ShapeDtypeStruct((nrow, width), F32)] * nout, compiler_params=_cparams(1), name=name)(
            w, m, v, *grads)


def _pack(arrs, dtype, width):
    flat = jnp.concatenate([a.astype(dtype).reshape(-1) for a in arrs])
    quantum = PACK_ROWS * width
    pad = (-flat.shape[0]) % quantum
    if pad:
        flat = jnp.concatenate([flat, jnp.zeros((pad,), dtype)])
    return flat.reshape(-1, width)


def _unpack(buf, shapes):
    flat = buf.reshape(-1)
    out, off = [], 0
    for s in shapes:
        size = math.prod(s)
        out.append(flat[off:off + size].reshape(s))
        off += size
    return out


QK_NORM_HEADS = 8


def _qk_norm_fwd(qkv, w, third, nh, tag):
    f = _rms_f(RMS_EPS)
    hd = SB_HEAD_DIM
    hpb = math.gcd(nh, QK_NORM_HEADS)

    def fn(xv, wv):
        return (jnp.concatenate([f(xv[:, hd * h:hd * (h + 1)], wv) for h in range(hpb)], axis=1),)

    return _rowwise(fn, [(qkv, hpb * hd, third * (nh // hpb))], [(w, False)], [('row', nh * hd, hpb * hd, BF16)],
                    ncb=nh // hpb, name=tag)[0]


def _qk_norm_bwd(qkv, w, third, dn, nh, tag):
    f = _rms_f(RMS_EPS)
    hd = SB_HEAD_DIM
    hpb = math.gcd(nh, QK_NORM_HEADS)

    def fn(xv, dv, wv):
        dxs, dw = [], jnp.zeros_like(wv)
        for h in range(hpb):
            _, vjp = jax.vjp(f, xv[:, hd * h:hd * (h + 1)], wv)
            dxh, dwh = vjp(dv[:, hd * h:hd * (h + 1)])
            dxs.append(dxh)
            dw = dw + dwh
        return jnp.concatenate(dxs, axis=1), dw

    return _rowwise(fn, [(qkv, hpb * hd, third * (nh // hpb)), (dn, hpb * hd, 0)], [(w, False)],
                    [('row', nh * hd, hpb * hd, BF16), ('acc', (1, hd), False)], ncb=nh // hpb, name=tag)


def _sb_forward(h, xin, wt, tag):
    d = xin.shape[1]
    nh = d // SB_HEAD_DIM
    qkv = _mm(h, wt['w_qkv'], "nn", name=tag + "_qkv")
    qn = _qk_norm_fwd(qkv, wt['q_norm_w'], 0, nh, tag + "_qnorm")
    kn = _qk_norm_fwd(qkv, wt['k_norm_w'], 1, nh, tag + "_knorm")
    o, rsave = _sb_attn_fwd(qn, kn, qkv, tag + "_attn")
    xm = _mm(o, wt['w_o'], "nn", extras=(xin,), epilogue=_ep_add, name=tag + "_out")
    return xm, dict(qkv=qkv, qn=qn, kn=kn, o=o, rsave=rsave)


def _sb_backward(h, dx, dxb, wt, sv, tag):
    nh = dx.shape[1] // SB_HEAD_DIM
    do = _mm(dxb, wt['w_o'], "nt", name=tag + "_do")
    g_wo = _mm(sv['o'], dxb, "tn", out_dtypes=(BF16,), name=tag + "_dwo")
    dqn, dkn, dv = _sb_attn_bwd(sv['qn'], sv['kn'], sv['qkv'], do, sv['rsave'], tag + "_attn_bwd")
    dq, g_qw = _qk_norm_bwd(sv['qkv'], wt['q_norm_w'], 0, dqn, nh, tag + "_qnorm_bwd")
    dk, g_kw = _qk_norm_bwd(sv['qkv'], wt['k_norm_w'], 1, dkn, nh, tag + "_knorm_bwd")
    dqkv = jnp.concatenate([dq, dk, dv.astype(BF16)], axis=1)
    g_wqkv = _mm(h, dqkv, "tn", out_dtypes=(BF16,), out_col_shards=N_XY, name=tag + "_dwqkv")
    dh = _mm(dqkv, wt['w_qkv'], "nt", name=tag + "_dh")
    return dh, dict(w_qkv=g_wqkv, w_o=g_wo, q_norm_w=g_qw, k_norm_w=g_kw)


def _ln_silu_f(x, w, b):
    mu = jnp.mean(x, axis=-1, keepdims=True)
    xc = x - mu
    return _silu(xc * lax.rsqrt(jnp.mean(xc * xc, axis=-1, keepdims=True) + LN_EPS) * w + b)


def _glu_bwd_fn(val, gate, dg):
    sg = 1.0 / (1.0 + jnp.exp(-gate))
    return (jnp.concatenate([dg * sg, dg * val * sg * (1.0 - sg)], axis=1),)


def _cf_forward(h, xin, wt, tag):
    d = xin.shape[1]
    u = _mm(h, wt['w_in'], "nn", extras=(wt['b_in'],), epilogue=_ep_add, name=tag + "_in")
    gl = _rowwise(lambda val, gate: (val / (1.0 + jnp.exp(-gate)),), [(u, d, 0), (u, d, 1)], [],
                  [('row', d, d, F32)], name=tag + "_glu")[0]
    cv = _dwconv_fwd(gl, wt['dw_w'], wt['dw_b'], tag + "_conv")
    s = _rowwise(lambda x, w, b: (_ln_silu_f(x, w, b),), [(cv, d, 0)], [(wt['ln_w'], False), (wt['ln_b'], False)],
                 [('row', d, d, BF16)], name=tag + "_ln")[0]
    xm = _mm(s, wt['w_out'], "nn", extras=(wt['b_out'], xin), epilogue=_ep_bias_add, name=tag + "_out")
    return xm, dict(u=u, gl=gl, cv=cv, s=s)


def _cf_backward(h, dx, dxb, wt, sv, tag):
    d = dx.shape[1]
    ds = _mm(dxb, wt['w_out'], "nt", name=tag + "_ds")
    g_wout = _mm(sv['s'], dxb, "tn", out_dtypes=(BF16,), name=tag + "_dwout")
    g_bout = _colsum(dx, tag + "_dbout")

    def ln_bwd(x, dsv, w, b):
        _, vjp = jax.vjp(_ln_silu_f, x, w, b)
        return vjp(dsv)

    dcv, g_lnw, g_lnb = _rowwise(ln_bwd, [(sv['cv'], d, 0), (ds, d, 0)], [(wt['ln_w'], False), (wt['ln_b'], False)],
                                 [('row', d, d, F32), ('acc', (1, d), False), ('acc', (1, d), False)],
                                 name=tag + "_ln_bwd")
    dgl, g_dww, g_dwb = _dwconv_bwd(sv['gl'], dcv, wt['dw_w'], tag + "_conv_bwd")
    du = _rowwise(_glu_bwd_fn, [(sv['u'], d, 0), (sv['u'], d, 1), (dgl, d, 0)], [], [('row', 2 * d, 2 * d, F32)],
                  name=tag + "_glu_bwd")[0]
    g_bin = _colsum(du, tag + "_dbin")
    g_win = _mm(h, du, "tn", out_dtypes=(BF16,), out_col_shards=N_XY, name=tag + "_dwin")
    dh = _mm(du, wt['w_in'], "nt", name=tag + "_dh")
    return dh, dict(w_in=g_win, b_in=g_bin, dw_w=g_dww, dw_b=g_dwb, ln_w=g_lnw, ln_b=g_lnb, w_out=g_wout, b_out=g_bout)


def _gated_norm_f(y, z, w):
    t = y * _silu(z)
    return t * lax.rsqrt(jnp.mean(t * t, axis=-1, keepdims=True) + M2_NORM_EPS) * w


def _m2_forward(h, xin, wt, tag):
    z = _mm(h, wt['w_z'], "nn", name=tag + "_z")
    xbc = _mm(h, wt['w_xbc'], "nn", name=tag + "_xbc")
    dtr = _mm(h, wt['w_dt'], "nn", name=tag + "_dt")
    pre = _dwconv_fwd(xbc, wt['conv_w'], wt['conv_b'], tag + "_conv")
    y, hs = _ssd_fwd(pre, dtr, wt['dt_bias'], wt['a_log'], wt['d'], tag + "_ssd")
    d_inner = y.shape[1]
    gw = d_inner // M2_GROUPS
    yn = _rowwise(lambda yv, zv, w: (_gated_norm_f(yv, zv, w),), [(y, gw, 0), (z, gw, 0)], [(wt['norm_w'], True)],
                  [('row', d_inner, gw, BF16)], ncb=M2_GROUPS, name=tag + "_gnorm")[0]
    xm = _mm(yn, wt['w_out'], "nn", extras=(xin,), epilogue=_ep_add, name=tag + "_out")
    return xm, dict(z=z, xbc=xbc, dtr=dtr, pre=pre, y=y, hs=hs, yn=yn)


def _m2_backward(h, dx, dxb, wt, sv, tag):
    dyn = _mm(dxb, wt['w_out'], "nt", name=tag + "_dyn")
    g_wout = _mm(sv['yn'], dxb, "tn", out_dtypes=(BF16,), name=tag + "_dwout")
    d_inner = sv['y'].shape[1]
    gw = d_inner // M2_GROUPS

    def gn_bwd(yv, zv, dv, w):
        _, vjp = jax.vjp(_gated_norm_f, yv, zv, w)
        return vjp(dv)

    dy, dz, g_nw = _rowwise(gn_bwd, [(sv['y'], gw, 0), (sv['z'], gw, 0), (dyn, gw, 0)], [(wt['norm_w'], True)],
                            [('row', d_inner, gw, F32), ('row', d_inner, gw, BF16), ('acc', (1, d_inner), True)],
                            ncb=M2_GROUPS, name=tag + "_gnorm_bwd")
    dxp, db, dc, ddt, g_dtb, g_alog, g_d = _ssd_bwd(sv['pre'], sv['dtr'], sv['hs'], dy, wt['dt_bias'], wt['a_log'],
                                                   wt['d'], tag + "_ssd_bwd")
    dpre = jnp.concatenate([dxp, db, dc], axis=1)
    dxbc, g_cw, g_cb = _dwconv_bwd(sv['xbc'], dpre, wt['conv_w'], tag + "_conv_bwd", dx_dtype=BF16)
    g_wz = _mm(h, dz, "tn", name=tag + "_dwz")
    g_wxbc = _mm(h, dxbc, "tn", name=tag + "_dwxbc")
    g_wdt = _mm(h, ddt, "tn", name=tag + "_dwdt")
    dh = _mm(dz, wt['w_z'], "nt", name=tag + "_dh_z")
    dh = _mm(dxbc, wt['w_xbc'], "nt", extras=(dh,), epilogue=_ep_add, name=tag + "_dh_xbc")
    dh = _mm(ddt, wt['w_dt'], "nt", extras=(dh,), epilogue=_ep_add, name=tag + "_dh_dt")
    return dh, dict(w_z=g_wz, w_xbc=g_wxbc, w_dt=g_wdt, conv_w=g_cw, conv_b=g_cb, dt_bias=g_dtb, a_log=g_alog, d=g_d,
                    norm_w=g_nw, w_out=g_wout)


def _pad_lanes(v):
    return jnp.pad(v.reshape(1, -1), ((0, 0), (0, LANES - v.shape[0])))


def kernel(x, norm_mix_w, norm_mlp_w, sb_w_qkv, sb_q_norm_w, sb_k_norm_w, sb_w_o, cf_w_in, cf_b_in, cf_dw_w, cf_dw_b, cf_ln_w, cf_ln_b, cf_w_out, cf_b_out, m2_w_in, m2_conv_w, m2_conv_b, m2_dt_bias, m2_a_log, m2_d, m2_norm_w, m2_w_out, mlp_w_up, mlp_w_down, loss_target, m_norm_mix_w, m_norm_mlp_w, m_sb_w_qkv, m_sb_q_norm_w, m_sb_k_norm_w, m_sb_w_o, m_cf_w_in, m_cf_b_in, m_cf_dw_w, m_cf_dw_b, m_cf_ln_w, m_cf_ln_b, m_cf_w_out, m_cf_b_out, m_m2_w_in, m_m2_conv_w, m_m2_conv_b, m_m2_dt_bias, m_m2_a_log, m_m2_d, m_m2_norm_w, m_m2_w_out, m_mlp_w_up, m_mlp_w_down, v_norm_mix_w, v_norm_mlp_w, v_sb_w_qkv, v_sb_q_norm_w, v_sb_k_norm_w, v_sb_w_o, v_cf_w_in, v_cf_b_in, v_cf_dw_w, v_cf_dw_b, v_cf_ln_w, v_cf_ln_b, v_cf_w_out, v_cf_b_out, v_m2_w_in, v_m2_conv_w, v_m2_conv_b, v_m2_dt_bias, v_m2_a_log, v_m2_d, v_m2_norm_w, v_m2_w_out, v_mlp_w_up, v_mlp_w_down):
    given = dict(locals())
    xl = x[0]
    tgt = loss_target[0]
    d = xl.shape[1]
    depth = norm_mix_w.shape[0]
    bulk_names, small_names = list(BULK), list(SMALL)

    def halves(buf):
        return buf.reshape(2, -1, buf.shape[-1])

    def quarters(buf):
        return buf.reshape(2, 2, -1, buf.shape[-1])

    small_pack = quarters(_pack([given[n] for n in small_names], F32, PACK_W))
    gath = _all_gather_xy([quarters(given[n].astype(BF16)) for n in bulk_names] + [small_pack], "comm_gather")
    gathered = {n: g.reshape((N_XY,) + given[n].shape) for n, g in zip(bulk_names, gath)}
    kinds = {n: "rows" if BULK[n] == 1 else "cols" for n in bulk_names}
    full = {}
    pieces = [_unpack(gath[-1][s], [given[n].shape for n in small_names]) for s in range(N_XY)]
    for idx, n in enumerate(small_names):
        full[n] = jnp.concatenate([pieces[s][idx] for s in range(N_XY)], axis=SMALL[n])

    def wview(n, layer):
        return WView(gathered[n], layer, kinds[n])

    def row(v):
        return v.reshape(1, -1)

    d_inner = N_XY * m2_w_out.shape[1]
    conv_dim = full['m2_conv_w'].shape[2]
    nheads = m2_dt_bias.shape[1]

    def layer_weights(i):
        kind, j = i % 3, i // 3
        if kind == 0:
            return dict(w_qkv=wview('sb_w_qkv', j), w_o=wview('sb_w_o', j), q_norm_w=row(sb_q_norm_w[j]),
                        k_norm_w=row(sb_k_norm_w[j]))
        if kind == 1:
            return dict(w_in=wview('cf_w_in', j), b_in=row(cf_b_in[j]), dw_w=full['cf_dw_w'][j], dw_b=row(cf_dw_b[j]),
                        ln_w=row(cf_ln_w[j]), ln_b=row(cf_ln_b[j]), w_out=wview('cf_w_out', j), b_out=row(cf_b_out[j]))
        shards = gathered['m2_w_in'][:, j]
        w_in = jnp.moveaxis(shards, 0, 1).reshape(shards.shape[1], -1)
        w_dt = jnp.pad(w_in[:, d_inner + conv_dim:], ((0, 0), (0, LANES - nheads)))
        return dict(w_z=w_in[:, :d_inner], w_xbc=w_in[:, d_inner:d_inner + conv_dim], w_dt=w_dt,
                    conv_w=full['m2_conv_w'][j], conv_b=row(full['m2_conv_b'][j]), dt_bias=_pad_lanes(m2_dt_bias[j]),
                    a_log=_pad_lanes(m2_a_log[j]), d=_pad_lanes(m2_d[j]), norm_w=row(full['m2_norm_w'][j]),
                    w_out=wview('m2_w_out', j))

    fwd = (_sb_forward, _cf_forward, _m2_forward)
    bwd = (_sb_backward, _cf_backward, _m2_backward)

    saved = []
    xc = xl
    for i in range(depth):
        wt = layer_weights(i)
        h = _rms_fwd(xc, row(norm_mix_w[i]), f"l{i}_norm_mix")
        xm, sv = fwd[i % 3](h, xc, wt, f"l{i}_mix")
        h2 = _rms_fwd(xm, row(norm_mlp_w[i]), f"l{i}_norm_mlp")
        u, a = _mm(h2, wview('mlp_w_up', i), "nn", epilogue=_ep_relu2, out_dtypes=(F32, BF16), name=f"l{i}_up")
        xn = _mm(a, wview('mlp_w_down', i), "nn", extras=(xm,), epilogue=_ep_add, name=f"l{i}_down")
        saved.append(dict(wt=wt, x_in=xc, h=h, mix=sv, x_mid=xm, h2=h2, u=u, a=a))
        xc = xn
    dx, dxb, loss_acc = _loss(xc, tgt, "loss")
    loss = lax.psum(loss_acc[0, 0], ("x", "y", "c"))

    grads = {n: [None] * given[n].shape[0] for n in W_NAMES}
    for i in reversed(range(depth)):
        sv = saved[i]
        kind, j = i % 3, i // 3
        du = _mm(dxb, wview('mlp_w_down', i), "nt", extras=(sv['u'],), epilogue=_ep_relu2_bwd, out_dtypes=(BF16,),
                 name=f"l{i}_du")
        grads['mlp_w_down'][i] = _mm(sv['a'], dxb, "tn", out_dtypes=(BF16,), name=f"l{i}_dwdown")
        grads['mlp_w_up'][i] = _mm(sv['h2'], du, "tn", out_dtypes=(BF16,), out_col_shards=N_XY, name=f"l{i}_dwup")
        dh2 = _mm(du, wview('mlp_w_up', i), "nt", name=f"l{i}_dh2")
        dxm, dxmb, g_n2 = _rms_bwd(sv['x_mid'], row(norm_mlp_w[i]), dh2, dx, f"l{i}_norm_mlp_bwd")
        grads['norm_mlp_w'][i] = g_n2[0]
        dh, gw = bwd[kind](sv['h'], dxm, dxmb, sv['wt'], sv['mix'], f"l{i}_mix")
        dx, dxb, g_n1 = _rms_bwd(sv['x_in'], row(norm_mix_w[i]), dh, dxm, f"l{i}_norm_mix_bwd")
        grads['norm_mix_w'][i] = g_n1[0]
        if kind == 0:
            grads['sb_w_qkv'][j], grads['sb_w_o'][j] = gw['w_qkv'], gw['w_o']
            grads['sb_q_norm_w'][j], grads['sb_k_norm_w'][j] = gw['q_norm_w'][0], gw['k_norm_w'][0]
        elif kind == 1:
            for n in ('w_in', 'dw_w', 'w_out'):
                grads['cf_' + n][j] = gw[n]
            for n in ('b_in', 'dw_b', 'ln_w', 'ln_b', 'b_out'):
                grads['cf_' + n][j] = gw[n][0]
        else:
            grads['m2_w_in'][j] = jnp.concatenate([gw['w_z'], gw['w_xbc'], gw['w_dt'][:, :nheads]], axis=1)
            grads['m2_conv_w'][j], grads['m2_w_out'][j] = gw['conv_w'], gw['w_out']
            grads['m2_conv_b'][j], grads['m2_norm_w'][j] = gw['conv_b'][0], gw['norm_w'][0]
            for n in ('dt_bias', 'a_log', 'd'):
                grads['m2_' + n][j] = gw[n][0, :nheads]
    grad_x = dx[None]
    gfull = {n: jnp.stack(grads[n]) for n in small_names + REPL}

    def shard_major(n, g):
        _, r, c = given[n].shape
        if n == 'm2_w_in':
            return jnp.moveaxis(g.reshape(r, N_XY, c), 1, 0).astype(BF16)
        return g.reshape(N_XY, r, c)

    parts = [jnp.stack([shard_major(n, g) for g in grads[n]], axis=1) for n in bulk_names]
    parts = [p.reshape(N_XY, 2, -1, p.shape[-1]) for p in parts]
    split = {n: jnp.split(gfull[n], N_XY, axis=SMALL[n]) for n in small_names}
    parts.append(jnp.stack([halves(_pack([split[n][s] for n in small_names], F32, PACK_W)) for s in range(N_XY)]))
    tags = bulk_names + ["small"]
    got = _pair_split(parts, "comm_split")
    pair = [_add_half(p, g, f"sum_pair_{t}") for p, g, t in zip(parts, got, tags)]
    mine = [_sum_parts(r, f"sum_{t}") for r, t in zip(_exchange_xy(pair, "comm_exchange"), tags)]
    gsum = _pair_join(mine, "comm_join")
    outs = {}
    for n, g in list(zip(bulk_names, gsum)) + list(zip(small_names, _unpack(gsum[-1], [given[n].shape for n in small_names]))):
        shape = given[n].shape
        res = _adamw(*[given[p + n].reshape(-1, shape[-1]) for p in ("", "m_", "v_")], g.reshape(-1, shape[-1]),
                     None, f"adamw_{n}")
        outs["grad", n] = g.reshape(shape)
        for kind, arr in zip(("delta", "new_m", "new_v"), res):
            outs[kind, n] = arr.reshape(shape)
    rparts = _all_gather_all(_pack([gfull[n] for n in REPL], F32, LANES), "comm_gather_repl")
    nsplit = N_DEV // 2
    res = _adamw(*[_pack([given[p + n] for n in REPL], F32, LANES) for p in ("", "m_", "v_")],
                 _sum_parts(rparts[:nsplit], "sum_repl_a"), _sum_parts(rparts[nsplit:], "sum_repl_b"), "adamw_repl")
    shapes = [given[n].shape for n in REPL]
    for kind, buf in zip(("grad", "delta", "new_m", "new_v"), res):
        for n, arr in zip(REPL, _unpack(buf, shapes)):
            outs[kind, n] = arr

    return (loss, grad_x, *[outs[kind, n] for kind in ("grad", "delta", "new_m", "new_v") for n in W_NAMES])
```
